```python
import jax
import jax.numpy as jnp
from jax import lax
import numpy as np

D_MODEL = 1024
BATCH = 16
SEQ = 256
DEPTH = 2
DEC_BATCH = 8
DEC_SEQ = 1024
PAST_LEN = 512

GRID_W = 64
NA_HEADS = 8
NA_HEAD_DIM = 64
NA_KR = 8
NA_KC = 16
NA_WIDTH = NA_HEADS * NA_HEAD_DIM
NA_SCALE = NA_HEAD_DIM ** -0.5
MLA_HEADS = 8
MLA_NOPE_DIM = 64
MLA_ROPE_DIM = 32
MLA_V_DIM = 64
MLA_Q_LORA = 384
MLA_KV_LORA = 256
MLA_QK_DIM = MLA_NOPE_DIM + MLA_ROPE_DIM
MLA_WIDTH = MLA_HEADS * MLA_V_DIM
MLA_SCALE = MLA_QK_DIM ** -0.5
ROPE_BASE = 10000.0
N_GROUPS = 4
EXPERTS_PER_GROUP = 8
N_EXPERTS = N_GROUPS * EXPERTS_PER_GROUP
EXPERT_TOP_K = 2
EXPERT_FF = 256
Q_BLOCK = 128
LN_EPS = 1e-5
RMS_EPS = 1e-6
NEG_INF = -1e30
DEEPNORM_ALPHA = (2 * DEPTH) ** 0.25
DEEPNORM_BETA = (8 * DEPTH) ** -0.25
OFF_NA_Q = 0
OFF_NA_K = OFF_NA_Q + NA_WIDTH
OFF_NA_V = OFF_NA_K + NA_WIDTH
OFF_MLA_CQ = OFF_NA_V + NA_WIDTH
OFF_MLA_CKV = OFF_MLA_CQ + MLA_Q_LORA
OFF_MLA_KR = OFF_MLA_CKV + MLA_KV_LORA
OFF_GATE_NA = OFF_MLA_KR + MLA_ROPE_DIM
OFF_GATE_MLA = OFF_GATE_NA + D_MODEL
IN_PROJ_DIM = OFF_GATE_MLA + D_MODEL

kernel_name = 'hybrid_na_mla_hmoe_diffusion_step'


def layer_norm_plain(x):
    xf = x.astype(jnp.float32)
    xc = xf - jnp.mean(xf, axis=-1, keepdims=True)
    var = jnp.mean(xc * xc, axis=-1, keepdims=True)
    return xc * lax.rsqrt(var + LN_EPS)


def layer_norm(x, g, b):
    y = layer_norm_plain(x) * g.astype(jnp.float32) + b.astype(jnp.float32)
    return y.astype(x.dtype)


def rms_norm(x, g):
    xf = x.astype(jnp.float32)
    y = xf * lax.rsqrt(jnp.mean(xf * xf, axis=-1, keepdims=True) + RMS_EPS)
    return (y * g.astype(jnp.float32)).astype(x.dtype)


def modulate(x, shift, scale):
    y = layer_norm_plain(x) * (1.0 + scale.astype(jnp.float32)) + shift.astype(jnp.float32)
    return y.astype(x.dtype)


def adaln_params(cond, w_ada, b_ada):
    ada = jax.nn.silu(cond) @ w_ada + b_ada
    return jnp.split(ada, 6, axis=-1)


def axial_rope_tables(n_tokens, dim):
    n_freq = dim // 4
    inv_freq = ROPE_BASE ** (-jnp.arange(n_freq, dtype=jnp.float32) / n_freq)
    t = jnp.arange(n_tokens)
    row = (t // GRID_W).astype(jnp.float32)
    col = (t % GRID_W).astype(jnp.float32)
    ang = jnp.concatenate([row[:, None] * inv_freq, col[:, None] * inv_freq], axis=-1)
    return jnp.cos(ang), jnp.sin(ang)


def apply_axial_rope(x, cos, sin):
    B, T, G, dim = x.shape
    n_freq = dim // 4
    xf = x.astype(jnp.float32).reshape(B, T, G, 2, 2, n_freq)
    x1 = xf[..., 0, :]
    x2 = xf[..., 1, :]
    c = cos.reshape(T, 1, 2, n_freq)
    s = sin.reshape(T, 1, 2, n_freq)
    out = jnp.stack([x1 * c - x2 * s, x1 * s + x2 * c], axis=-2)
    return out.reshape(B, T, G, dim).astype(x.dtype)


def dense_attention(q, k, v, scale):
    B, S, H, dq = q.shape
    dv = v.shape[-1]
    blk = min(Q_BLOCK, S)
    nb = S // blk
    q_blocks = jnp.moveaxis(q.reshape(B, nb, blk, H, dq), 1, 0)

    def attend(qb):
        s = jnp.einsum('bqhd,bkhd->bhqk', qb, k).astype(jnp.float32) * scale
        p = jax.nn.softmax(s, axis=-1).astype(v.dtype)
        return jnp.einsum('bhqk,bkhd->bqhd', p, v)

    out = lax.map(attend, q_blocks)
    return jnp.moveaxis(out, 0, 1).reshape(B, S, H, dv)


def neighborhood_attention(q, k, v, k_ctx, v_ctx, rpb):
    B, T, H, dh = q.shape
    rows = T // GRID_W
    kr = min(NA_KR, rows)
    ncb = GRID_W // NA_KC
    band = 2 * NA_KC
    r = jnp.arange(rows)
    r_start = jnp.clip(r - kr // 2, 0, rows - kr)
    ridx = r_start[:, None] + jnp.arange(kr)
    cidx = jnp.arange(GRID_W)
    c_start = jnp.clip(cidx - NA_KC // 2, 0, GRID_W - NA_KC)
    b_start = jnp.clip(c_start[::NA_KC], 0, GRID_W - band)
    bidx = b_start[:, None] + jnp.arange(band)
    kg = k.reshape(B, rows, GRID_W, H, dh)[:, ridx[:, :, None, None], bidx[None, None, :, :]]
    vg = v.reshape(B, rows, GRID_W, H, dh)[:, ridx[:, :, None, None], bidx[None, None, :, :]]
    qg = q.reshape(B, rows, ncb, NA_KC, H, dh)
    s_loc = jnp.einsum('brnqhd,brknjhd->bhrnqkj', qg, kg).astype(jnp.float32) * NA_SCALE
    qcol = cidx.reshape(ncb, NA_KC)
    cs = c_start.reshape(ncb, NA_KC)
    kcol = bidx[:, None, :]
    valid = (kcol >= cs[:, :, None]) & (kcol < cs[:, :, None] + NA_KC)
    roff = ridx - r[:, None] + (NA_KR - 1)
    coff = jnp.clip(kcol - qcol[:, :, None] + (NA_KC - 1), 0, 2 * NA_KC - 2)
    bias = rpb[:, roff[:, None, None, :, None], coff[None, :, :, None, :]]
    s_loc = s_loc + bias[None].astype(jnp.float32)
    s_loc = jnp.where(valid[:, :, None, :], s_loc, NEG_INF)
    s_loc = s_loc.reshape(B, H, rows, ncb, NA_KC, kr * band)
    s_ctx = jnp.einsum('brnqhd,blhd->bhrnql', qg, k_ctx).astype(jnp.float32) * NA_SCALE
    n_loc = kr * band
    p = jax.nn.softmax(jnp.concatenate([s_loc, s_ctx], axis=-1), axis=-1).astype(v.dtype)
    p_loc = p[..., :n_loc].reshape(B, H, rows, ncb, NA_KC, kr, band)
    p_ctx = p[..., n_loc:]
    out = (jnp.einsum('bhrnqkj,brknjhd->brnqhd', p_loc, vg)
           + jnp.einsum('bhrnql,blhd->brnqhd', p_ctx, v_ctx))
    return out.reshape(B, T, H, dh)


def mixer_inputs(h, w_in, q_norm, kv_norm, w_uq):
    B, S, _ = h.shape
    proj = h @ w_in
    q_na = proj[..., OFF_NA_Q:OFF_NA_K].reshape(B, S, NA_HEADS, NA_HEAD_DIM)
    k_na = proj[..., OFF_NA_K:OFF_NA_V].reshape(B, S, NA_HEADS, NA_HEAD_DIM)
    v_na = proj[..., OFF_NA_V:OFF_MLA_CQ].reshape(B, S, NA_HEADS, NA_HEAD_DIM)
    c_q = rms_norm(proj[..., OFF_MLA_CQ:OFF_MLA_CKV], q_norm)
    c_kv = rms_norm(proj[..., OFF_MLA_CKV:OFF_MLA_KR], kv_norm)
    k_r = proj[..., OFF_MLA_KR:OFF_GATE_NA]
    q_mla = (c_q @ w_uq).reshape(B, S, MLA_HEADS, MLA_QK_DIM)
    gate_na = jax.nn.sigmoid(proj[..., OFF_GATE_NA:OFF_GATE_MLA])
    gate_mla = jax.nn.sigmoid(proj[..., OFF_GATE_MLA:])
    return q_na, k_na, v_na, q_mla, c_kv, k_r, gate_na, gate_mla


def mla_keys_values(c_kv, k_r, w_uk, w_uv):
    B, S, _ = c_kv.shape
    k_nope = (c_kv @ w_uk).reshape(B, S, MLA_HEADS, MLA_NOPE_DIM)
    v = (c_kv @ w_uv).reshape(B, S, MLA_HEADS, MLA_V_DIM)
    k_rope = jnp.broadcast_to(k_r[:, :, None, :], (B, S, MLA_HEADS, MLA_ROPE_DIM))
    return jnp.concatenate([k_nope, k_rope], axis=-1), v


def merge_branches(o_na, o_mla, gate_na, gate_mla, w_o_na, w_o_mla, w_out):
    B, S = o_na.shape[:2]
    y_na = o_na.reshape(B, S, NA_WIDTH) @ w_o_na
    y_mla = o_mla.reshape(B, S, MLA_WIDTH) @ w_o_mla
    return (gate_na * y_na + gate_mla * y_mla) @ w_out


def hierarchical_moe(h, rg_w, rg_b, re_w, re_b, w_gate, w_up, w_down):
    g_logits = (h @ rg_w + rg_b).astype(jnp.float32)
    g_prob = jax.nn.softmax(g_logits, axis=-1)
    g_idx = jnp.argmax(g_logits, axis=-1)
    g_w = jnp.max(g_prob, axis=-1, keepdims=True)
    e_logits = (h @ re_w + re_b).astype(jnp.float32)
    e_logits = e_logits.reshape(h.shape[0], h.shape[1], N_GROUPS, EXPERTS_PER_GROUP)
    g_onehot = jax.nn.one_hot(g_idx, N_GROUPS, dtype=jnp.float32)
    e_sel = jnp.sum(e_logits * g_onehot[..., None], axis=-2)
    e_prob = jax.nn.softmax(e_sel, axis=-1)
    top_p, top_i = lax.top_k(e_prob, EXPERT_TOP_K)
    top_w = g_w * top_p / jnp.sum(top_p, axis=-1, keepdims=True)
    expert_id = g_idx[..., None] * EXPERTS_PER_GROUP + top_i
    combine = jnp.sum(jax.nn.one_hot(expert_id, N_EXPERTS, dtype=jnp.float32) * top_w[..., None], axis=-2)
    hid = jax.nn.silu(jnp.einsum('bsd,edf->bsef', h, w_gate)) * jnp.einsum('bsd,edf->bsef', h, w_up)
    hid = hid * combine[..., None].astype(h.dtype)
    return jnp.einsum('bsef,efd->bsd', hid, w_down)


def setup_inputs(seed: int = 0) -> dict:
    key = jax.random.key(seed)
    ks = jax.random.split(key, 32)

    def nrm(k, shape, s):
        return jax.random.normal(k, shape, jnp.float32) * s

    D = D_MODEL
    return {
        'x_prompt': nrm(ks[0], (BATCH, SEQ, D), 1.0),
        'x_sample': nrm(ks[1], (DEC_BATCH, DEC_SEQ, D), 1.0),
        'cache_na_k': nrm(ks[2], (DEC_BATCH, DEPTH, PAST_LEN, NA_HEADS, NA_HEAD_DIM), 1.0),
        'cache_na_v': nrm(ks[3], (DEC_BATCH, DEPTH, PAST_LEN, NA_HEADS, NA_HEAD_DIM), 1.0),
        'cache_mla_ckv': nrm(ks[4], (DEC_BATCH, DEPTH, PAST_LEN, MLA_KV_LORA), 1.0),
        'cache_mla_kr': nrm(ks[5], (DEC_BATCH, DEPTH, PAST_LEN, MLA_ROPE_DIM), 1.0),
        'c': nrm(ks[6], (DEC_BATCH, D), 1.0),
        'c_ctx': nrm(ks[7], (D,), 1.0),
        'w_in': nrm(ks[8], (DEPTH, D, IN_PROJ_DIM), D ** -0.5),
        'mla_q_norm': 1.0 + nrm(ks[9], (DEPTH, MLA_Q_LORA), 0.02),
        'mla_kv_norm': 1.0 + nrm(ks[10], (DEPTH, MLA_KV_LORA), 0.02),
        'w_uq': nrm(ks[11], (DEPTH, MLA_Q_LORA, MLA_HEADS * MLA_QK_DIM), MLA_Q_LORA ** -0.5),
        'w_uk': nrm(ks[12], (DEPTH, MLA_KV_LORA, MLA_HEADS * MLA_NOPE_DIM), MLA_KV_LORA ** -0.5),
        'w_uv': nrm(ks[13], (DEPTH, MLA_KV_LORA, MLA_HEADS * MLA_V_DIM), MLA_KV_LORA ** -0.5),
        'na_rpb': nrm(ks[14], (DEPTH, NA_HEADS, 2 * NA_KR - 1, 2 * NA_KC - 1), 0.1),
        'w_o_na': nrm(ks[15], (DEPTH, NA_WIDTH, D), NA_WIDTH ** -0.5 * DEEPNORM_BETA),
        'w_o_mla': nrm(ks[16], (DEPTH, MLA_WIDTH, D), MLA_WIDTH ** -0.5 * DEEPNORM_BETA),
        'w_out': nrm(ks[17], (DEPTH, D, D), D ** -0.5 * DEEPNORM_BETA),
        'w_ada': nrm(ks[18], (DEPTH, D, 6 * D), D ** -0.5),
        'b_ada': nrm(ks[19], (DEPTH, 6 * D), 0.02),
        'ln1_g': 1.0 + nrm(ks[20], (DEPTH, D), 0.02),
        'ln1_b': nrm(ks[21], (DEPTH, D), 0.02),
        'ln2_g': 1.0 + nrm(ks[22], (DEPTH, D), 0.02),
        'ln2_b': nrm(ks[23], (DEPTH, D), 0.02),
        'router_group_w': nrm(ks[24], (DEPTH, D, N_GROUPS), D ** -0.5),
        'router_group_b': nrm(ks[25], (DEPTH, N_GROUPS), 0.01),
        'router_expert_w': nrm(ks[26], (DEPTH, D, N_EXPERTS), D ** -0.5),
        'router_expert_b': nrm(ks[27], (DEPTH, N_EXPERTS), 0.01),
        'expert_w_gate': nrm(ks[28], (DEPTH, N_EXPERTS, D, EXPERT_FF), D ** -0.5),
        'expert_w_up': nrm(ks[29], (DEPTH, N_EXPERTS, D, EXPERT_FF), D ** -0.5),
        'expert_w_down': nrm(ks[30], (DEPTH, N_EXPERTS, EXPERT_FF, D), EXPERT_FF ** -0.5 * DEEPNORM_BETA),
    }


def reference(x_prompt, x_sample, cache_na_k, cache_na_v, cache_mla_ckv, cache_mla_kr, c, c_ctx,
              w_in, mla_q_norm, mla_kv_norm, w_uq, w_uk, w_uv, na_rpb, w_o_na, w_o_mla, w_out,
              w_ada, b_ada, ln1_g, ln1_b, ln2_g, ln2_b,
              router_group_w, router_group_b, router_expert_w, router_expert_b,
              expert_w_gate, expert_w_up, expert_w_down):
    cond_ctx = c_ctx[None, None, :]
    cond_lat = c[:, None, :]
    rope_cos, rope_sin = axial_rope_tables(x_sample.shape[1], MLA_ROPE_DIM)
    xp = x_prompt
    xs = x_sample
    new_na_k, new_na_v, new_ckv, new_kr = [], [], [], []
    for l in range(DEPTH):
        proj_w = (w_in[l], mla_q_norm[l], mla_kv_norm[l], w_uq[l])
        out_w = (w_o_na[l], w_o_mla[l], w_out[l])
        moe_w = (router_group_w[l], router_group_b[l], router_expert_w[l], router_expert_b[l],
                 expert_w_gate[l], expert_w_up[l], expert_w_down[l])

        sh1, sc1, g1, sh2, sc2, g2 = adaln_params(cond_ctx, w_ada[l], b_ada[l])
        q_na, k_na, v_na, q_mla, c_kv, k_r, gate_na, gate_mla = mixer_inputs(modulate(xp, sh1, sc1), *proj_w)
        k_mla, v_mla = mla_keys_values(c_kv, k_r, w_uk[l], w_uv[l])
        o_na = dense_attention(q_na, k_na, v_na, NA_SCALE)
        o_mla = dense_attention(q_mla, k_mla, v_mla, MLA_SCALE)
        y = merge_branches(o_na, o_mla, gate_na, gate_mla, *out_w)
        xp = layer_norm(DEEPNORM_ALPHA * xp + g1 * y, ln1_g[l], ln1_b[l])
        y = hierarchical_moe(modulate(xp, sh2, sc2), *moe_w)
        xp = layer_norm(DEEPNORM_ALPHA * xp + g2 * y, ln2_g[l], ln2_b[l])
        new_na_k.append(k_na)
        new_na_v.append(v_na)
        new_ckv.append(c_kv)
        new_kr.append(k_r)

        sh1, sc1, g1, sh2, sc2, g2 = adaln_params(cond_lat, w_ada[l], b_ada[l])
        q_na, k_na, v_na, q_mla, c_kv, k_r, gate_na, gate_mla = mixer_inputs(modulate(xs, sh1, sc1), *proj_w)
        q_mla = jnp.concatenate(
            [q_mla[..., :MLA_NOPE_DIM], apply_axial_rope(q_mla[..., MLA_NOPE_DIM:], rope_cos, rope_sin)], axis=-1)
        k_r = apply_axial_rope(k_r[:, :, None, :], rope_cos, rope_sin)[:, :, 0, :]
        k_lat, v_lat = mla_keys_values(c_kv, k_r, w_uk[l], w_uv[l])
        k_ctx, v_ctx = mla_keys_values(cache_mla_ckv[:, l], cache_mla_kr[:, l], w_uk[l], w_uv[l])
        o_mla = dense_attention(q_mla, jnp.concatenate([k_ctx, k_lat], axis=1),
                                jnp.concatenate([v_ctx, v_lat], axis=1), MLA_SCALE)
        o_na = neighborhood_attention(q_na, k_na, v_na, cache_na_k[:, l], cache_na_v[:, l], na_rpb[l])
        y = merge_branches(o_na, o_mla, gate_na, gate_mla, *out_w)
        xs = layer_norm(DEEPNORM_ALPHA * xs + g1 * y, ln1_g[l], ln1_b[l])
        y = hierarchical_moe(modulate(xs, sh2, sc2), *moe_w)
        xs = layer_norm(DEEPNORM_ALPHA * xs + g2 * y, ln2_g[l], ln2_b[l])

    return (xp, xs, jnp.stack(new_na_k, axis=1), jnp.stack(new_na_v, axis=1),
            jnp.stack(new_ckv, axis=1), jnp.stack(new_kr, axis=1))
```

```python
import functools

import numpy as np
import jax
import jax.numpy as jnp
from jax import lax
from jax.experimental import pallas as pl
from jax.experimental.pallas import tpu as pltpu

F32 = jnp.float32
BF16 = jnp.bfloat16

D_MODEL = 1024
DEPTH = 2
GRID_W = 64
NA_HEADS = 8
NA_HEAD_DIM = 64
NA_KR = 8
NA_KC = 16
NA_WIDTH = NA_HEADS * NA_HEAD_DIM
NA_SCALE = NA_HEAD_DIM ** -0.5
MLA_HEADS = 8
MLA_NOPE_DIM = 64
MLA_ROPE_DIM = 32
MLA_V_DIM = 64
MLA_Q_LORA = 384
MLA_KV_LORA = 256
MLA_QK_DIM = MLA_NOPE_DIM + MLA_ROPE_DIM
MLA_SCALE = MLA_QK_DIM ** -0.5
ROPE_BASE = 10000.0
N_GROUPS = 4
EXPERTS_PER_GROUP = 8
N_EXPERTS = N_GROUPS * EXPERTS_PER_GROUP
EXPERT_FF = 256
LN_EPS = 1e-5
RMS_EPS = 1e-6
NEG_INF = -1e30
DEEPNORM_ALPHA = (2 * DEPTH) ** 0.25

LANES = 128
MLA_HEAD_PAD = LANES
MLA_PAD_WIDTH = MLA_HEADS * MLA_HEAD_PAD
C_Q = 0
C_K = C_Q + NA_WIDTH
C_V = C_K + NA_WIDTH
C_CQ = C_V + NA_WIDTH
C_CKV = C_CQ + MLA_Q_LORA
C_KR = C_CKV + MLA_KV_LORA
C_GATE = C_KR + LANES
C_END = C_GATE + 2 * D_MODEL
R_EXP0 = N_GROUPS

VMEM_LIMIT = 48 * 1024 * 1024


def _rope_partner():
    j = np.arange(MLA_ROPE_DIM)
    n_freq = MLA_ROPE_DIM // 4
    axis, within = j // (2 * n_freq), j % (2 * n_freq)
    half, f = within // n_freq, within % n_freq
    return axis * 2 * n_freq + (1 - half) * n_freq + f, half


def _rope_tables(n_tokens):
    n_freq = MLA_ROPE_DIM // 4
    inv_freq = ROPE_BASE ** (-jnp.arange(n_freq, dtype=F32) / n_freq)
    t = jnp.arange(n_tokens)
    row = (t // GRID_W).astype(F32)
    col = (t % GRID_W).astype(F32)
    ang = jnp.concatenate([row[:, None] * inv_freq, col[:, None] * inv_freq], axis=-1)
    cos, sin = jnp.cos(ang), jnp.sin(ang)
    cos32 = jnp.concatenate([cos[:, :n_freq], cos[:, :n_freq], cos[:, n_freq:], cos[:, n_freq:]], axis=-1)
    sin32 = jnp.concatenate([-sin[:, :n_freq], sin[:, :n_freq], -sin[:, n_freq:], sin[:, n_freq:]], axis=-1)
    return cos32, sin32


def _ln_plain(x):
    mu = jnp.mean(x, axis=-1, keepdims=True)
    xc = x - mu
    var = jnp.mean(xc * xc, axis=-1, keepdims=True)
    return xc * lax.rsqrt(var + LN_EPS)


def _rms(x, g):
    return x * lax.rsqrt(jnp.mean(x * x, axis=-1, keepdims=True) + RMS_EPS) * g


def _sigmoid(x):
    return 1.0 / (1.0 + jnp.exp(-x))


def _dot(a, b):
    return jnp.dot(a, b, preferred_element_type=F32)


def _params(sem):
    return pltpu.CompilerParams(dimension_semantics=sem, vmem_limit_bytes=VMEM_LIMIT)


def _ada_body(c_ref, w_ref, b_ref, o_ref):
    c = c_ref[...]
    s = (c * _sigmoid(c)).astype(BF16)
    o_ref[0] = _dot(s, w_ref[0].astype(BF16)) + b_ref[0]


def _ada_call(cond, w_ada, b_ada):
    n_rows = cond.shape[0]
    tn = 1024
    return pl.pallas_call(
        _ada_body,
        grid=(DEPTH, 6 * D_MODEL // tn),
        in_specs=[
            pl.BlockSpec((n_rows, D_MODEL), lambda l, j: (0, 0)),
            pl.BlockSpec((1, D_MODEL, tn), lambda l, j: (l, 0, j)),
            pl.BlockSpec((1, 1, tn), lambda l, j: (l, 0, j)),
        ],
        out_specs=pl.BlockSpec((1, n_rows, tn), lambda l, j: (l, 0, j)),
        out_shape=jax.ShapeDtypeStruct((DEPTH, n_rows, 6 * D_MODEL), F32),
        compiler_params=_params(("arbitrary", "arbitrary")),
        name="ada",
    )(cond, w_ada, b_ada)


def _proj_body(*refs, rope, emit_cache):
    it = iter(refs)
    x_ref, ada_ref, w_ref, qn_ref, kvn_ref, wuq_ref = (next(it) for _ in range(6))
    wuqs_ref = next(it) if rope else None
    wuk_ref, wuv_ref, e_ref = next(it), next(it), next(it)
    if rope:
        cq_ref, sq_ref, tk_ref = next(it), next(it), next(it)
    oq_ref, ok_ref, ov_ref, oqm_ref, okm_ref, ovm_ref, og_ref = (next(it) for _ in range(7))
    if emit_cache:
        ockv_ref, okr_ref = next(it), next(it)

    x = x_ref[...]
    sh = ada_ref[0:1, :]
    sc = ada_ref[1:2, :]
    h = (_ln_plain(x) * (1.0 + sc) + sh).astype(BF16)

    oq_ref[...] = _dot(h, w_ref[:, C_Q:C_K]).astype(oq_ref.dtype)
    ok_ref[...] = _dot(h, w_ref[:, C_K:C_V]).astype(ok_ref.dtype)
    ov_ref[...] = _dot(h, w_ref[:, C_V:C_CQ]).astype(ov_ref.dtype)

    c_q = _rms(_dot(h, w_ref[:, C_CQ:C_CKV]), qn_ref[...]).astype(BF16)
    q = _dot(c_q, wuq_ref[...])
    if rope:
        q = q * cq_ref[...] + _dot(c_q, wuqs_ref[...]) * sq_ref[...]
    else:
        q = q * MLA_SCALE
    oqm_ref[...] = q.astype(BF16)

    c_kv = _rms(_dot(h, w_ref[:, C_CKV:C_KR]), kvn_ref[...])
    slab = _dot(h, w_ref[:, C_KR:C_GATE])
    if emit_cache:
        ockv_ref[...] = c_kv
        okr_ref[...] = slab[:, :MLA_ROPE_DIM]
    if rope:
        slab = slab * tk_ref[...]
    c_kv = c_kv.astype(BF16)
    okm_ref[...] = (_dot(c_kv, wuk_ref[...]) + _dot(slab.astype(BF16), e_ref[...])).astype(BF16)
    ovm_ref[...] = _dot(c_kv, wuv_ref[...]).astype(BF16)

    og_ref[...] = _sigmoid(_dot(h, w_ref[:, C_GATE:C_END])).astype(BF16)


def _proj_call(x, ada, lw, rope_tabs, kv_dtype, emit_cache, tm):
    bsz, seq, _ = x.shape
    nj = seq // tm
    rope = rope_tabs is not None

    def tok(width):
        return pl.BlockSpec((None, tm, width), lambda b, j: (b, j, 0))

    def full(arr):
        return pl.BlockSpec(arr.shape, lambda b, j: (0,) * arr.ndim)

    def tab(width):
        return pl.BlockSpec((tm, width), lambda b, j: (j, 0))

    args = [x, ada, lw["w_in"], lw["q_norm"], lw["kv_norm"], lw["w_uq"]]
    specs = [tok(D_MODEL), pl.BlockSpec((None, 8, D_MODEL), lambda b, j: (b, 0, 0)),
             full(lw["w_in"]), full(lw["q_norm"]), full(lw["kv_norm"]), full(lw["w_uq"])]
    if rope:
        args.append(lw["w_uq_sw"])
        specs.append(full(lw["w_uq_sw"]))
    e_mat = lw["e_rope"] if rope else lw["e_plain"]
    args += [lw["w_uk"], lw["w_uv"], e_mat]
    specs += [full(lw["w_uk"]), full(lw["w_uv"]), full(e_mat)]
    if rope:
        args += list(rope_tabs)
        specs += [tab(MLA_PAD_WIDTH), tab(MLA_PAD_WIDTH), tab(LANES)]

    def shp(width, dt):
        return jax.ShapeDtypeStruct((bsz, seq, width), dt)

    out_shape = [shp(NA_WIDTH, BF16), shp(NA_WIDTH, kv_dtype), shp(NA_WIDTH, kv_dtype),
                 shp(MLA_PAD_WIDTH, BF16), shp(MLA_PAD_WIDTH, BF16), shp(NA_WIDTH, BF16),
                 shp(2 * D_MODEL, BF16)]
    out_specs = [tok(NA_WIDTH), tok(NA_WIDTH), tok(NA_WIDTH), tok(MLA_PAD_WIDTH), tok(MLA_PAD_WIDTH),
                 tok(NA_WIDTH), tok(2 * D_MODEL)]
    if emit_cache:
        out_shape += [shp(MLA_KV_LORA, F32), shp(MLA_ROPE_DIM, F32)]
        out_specs += [tok(MLA_KV_LORA), tok(MLA_ROPE_DIM)]
    return pl.pallas_call(
        functools.partial(_proj_body, rope=rope, emit_cache=emit_cache),
        grid=(bsz, nj),
        in_specs=specs,
        out_specs=out_specs,
        out_shape=out_shape,
        compiler_params=_params(("arbitrary", "arbitrary")),
        name="proj_lat" if rope else "proj_ctx",
    )(*args)


def _ctxkv_body(ckv_ref, kr_ref, wuk_ref, wuv_ref, e_ref, okm_ref, ovm_ref):
    c_kv = ckv_ref[...].astype(BF16)
    okm_ref[...] = (_dot(c_kv, wuk_ref[...]) + _dot(kr_ref[...].astype(BF16), e_ref[...])).astype(BF16)
    ovm_ref[...] = _dot(c_kv, wuv_ref[...]).astype(BF16)


def _ctxkv_call(cache_ckv, cache_kr_pad, layer, lw):
    bsz, _, past, _ = cache_ckv.shape

    def full(arr):
        return pl.BlockSpec(arr.shape, lambda b: (0,) * arr.ndim)

    return pl.pallas_call(
        _ctxkv_body,
        grid=(bsz,),
        in_specs=[
            pl.BlockSpec((None, None, past, MLA_KV_LORA), lambda b: (b, layer, 0, 0)),
            pl.BlockSpec((None, None, past, LANES), lambda b: (b, layer, 0, 0)),
            full(lw["w_uk"]), full(lw["w_uv"]), full(lw["e_plain"]),
        ],
        out_specs=[pl.BlockSpec((None, past, MLA_PAD_WIDTH), lambda b: (b, 0, 0)),
                   pl.BlockSpec((None, past, NA_WIDTH), lambda b: (b, 0, 0))],
        out_shape=[jax.ShapeDtypeStruct((bsz, past, MLA_PAD_WIDTH), BF16),
                   jax.ShapeDtypeStruct((bsz, past, NA_WIDTH), BF16)],
        compiler_params=_params(("arbitrary",)),
        name="ctxkv",
    )(cache_ckv, cache_kr_pad, lw["w_uk"], lw["w_uv"], lw["e_plain"])


def _attn_body(*refs, n_seg, dq, has_bias):
    q_ref = refs[0]
    seg_refs = [(refs[1 + 2 * i], refs[2 + 2 * i]) for i in range(n_seg)]
    bias_ref = refs[1 + 2 * n_seg] if has_bias else None
    o_ref = refs[-1]
    outs = []
    for hh in range(2):
        qh = q_ref[:, hh * dq:(hh + 1) * dq]
        scores = []
        for i, (k_ref, _) in enumerate(seg_refs):
            kh = k_ref[:, hh * dq:(hh + 1) * dq].astype(BF16)
            s = lax.dot_general(qh, kh, (((1,), (1,)), ((), ())), preferred_element_type=F32)
            if has_bias and i == n_seg - 1:
                s = s + bias_ref[hh]
            scores.append(s)
        m = functools.reduce(jnp.maximum, [jnp.max(s, axis=-1, keepdims=True) for s in scores])
        den = None
        acc = None
        for s, (_, v_ref) in zip(scores, seg_refs):
            e = jnp.exp(s - m)
            vh = v_ref[:, hh * MLA_V_DIM:(hh + 1) * MLA_V_DIM].astype(BF16)
            d = jnp.sum(e, axis=-1, keepdims=True)
            a = _dot(e.astype(BF16), vh)
            den = d if den is None else den + d
            acc = a if acc is None else acc + a
        outs.append(acc / den)
    o_ref[...] = jnp.concatenate(outs, axis=-1).astype(o_ref.dtype)


def _attn_call(q, segs, bias, dq, tq, name):
    bsz, seq, _ = q.shape
    args = [q]
    specs = [pl.BlockSpec((None, tq, 2 * dq), lambda hp, qt, b: (b, qt, hp))]
    for k, v, ks, vs in segs:
        args += [k, v]
        specs += [ks, vs]
    if bias is not None:
        args.append(bias)
        specs.append(pl.BlockSpec((2, tq, bias.shape[-1]), lambda hp, qt, b: (hp, qt, 0)))
    return pl.pallas_call(
        functools.partial(_attn_body, n_seg=len(segs), dq=dq, has_bias=bias is not None),
        grid=(NA_HEADS // 2, seq // tq, bsz),
        in_specs=specs,
        out_specs=pl.BlockSpec((None, tq, 2 * MLA_V_DIM), lambda hp, qt, b: (b, qt, hp)),
        out_shape=jax.ShapeDtypeStruct((bsz, seq, NA_WIDTH), BF16),
        compiler_params=_params(("arbitrary", "arbitrary", "arbitrary")),
        name=name,
    )(*args)


def _seg3(arr, width):
    n = arr.shape[1]
    return pl.BlockSpec((None, n, width), lambda hp, qt, b: (b, 0, hp))


def _seg4(arr, layer, width):
    n = arr.shape[2]
    return pl.BlockSpec((None, None, n, width), lambda hp, qt, b: (b, layer, 0, hp))


def _route(logits):
    lane = lax.broadcasted_iota(jnp.int32, logits.shape, 1)
    lane_f = lane.astype(F32)
    far = float(LANES)
    is_g = lane < N_GROUPS
    gl = jnp.where(is_g, logits, NEG_INF)
    gmax = jnp.max(gl, axis=-1, keepdims=True)
    gsum = jnp.sum(jnp.exp(gl - gmax), axis=-1, keepdims=True)
    g_w = 1.0 / gsum
    g_idx = jnp.min(jnp.where(gl == gmax, lane_f, far), axis=-1, keepdims=True)
    e_group = ((lane - R_EXP0) >> 3).astype(F32)
    is_e = (lane >= R_EXP0) & (lane < R_EXP0 + N_EXPERTS) & (e_group == g_idx)
    el = jnp.where(is_e, logits, NEG_INF)
    m1 = jnp.max(el, axis=-1, keepdims=True)
    i1 = jnp.min(jnp.where(el == m1, lane_f, far), axis=-1, keepdims=True)
    el2 = jnp.where(lane_f == i1, NEG_INF, el)
    m2 = jnp.max(el2, axis=-1, keepdims=True)
    i2 = jnp.min(jnp.where(el2 == m2, lane_f, far), axis=-1, keepdims=True)
    r = jnp.exp(m2 - m1)
    w1 = g_w / (1.0 + r)
    w2 = g_w * r / (1.0 + r)
    return lane_f, i1, i2, w1, w2


def _merge_body(ona_ref, omla_ref, g_ref, x_ref, ada_ref, wna_ref, wmla_ref, wout_ref, lng_ref, lnb_ref,
                wr_ref, br_ref, ox_ref, oh_ref, oc_ref):
    y_na = _dot(ona_ref[...], wna_ref[...])
    y_mla = _dot(omla_ref[...], wmla_ref[...])
    g_na = g_ref[:, :D_MODEL].astype(F32)
    g_mla = g_ref[:, D_MODEL:].astype(F32)
    y = _dot((g_na * y_na + g_mla * y_mla).astype(BF16), wout_ref[...])
    g1 = ada_ref[2:3, :]
    x1 = _ln_plain(DEEPNORM_ALPHA * x_ref[...] + g1 * y) * lng_ref[...] + lnb_ref[...]
    ox_ref[...] = x1
    h2 = _ln_plain(x1) * (1.0 + ada_ref[4:5, :]) + ada_ref[3:4, :]
    oh_ref[...] = h2.astype(BF16)
    logits = jnp.dot(h2, wr_ref[...], preferred_element_type=F32, precision=lax.Precision.HIGHEST) + br_ref[...]
    lane_f, i1, i2, w1, w2 = _route(logits)
    oc_ref[...] = jnp.where(lane_f == i1, w1, 0.0) + jnp.where(lane_f == i2, w2, 0.0)


def _merge_call(o_na, o_mla, gates, x, ada, lw, tm, name):
    bsz, seq, _ = x.shape

    def tok(width):
        return pl.BlockSpec((None, tm, width), lambda b, j: (b, j, 0))

    def full(arr):
        return pl.BlockSpec(arr.shape, lambda b, j: (0,) * arr.ndim)

    ws = [lw["w_o_na"], lw["w_o_mla"], lw["w_out"], lw["ln1_g"], lw["ln1_b"], lw["w_r"], lw["b_r"]]
    return pl.pallas_call(
        _merge_body,
        grid=(bsz, seq // tm),
        in_specs=[tok(NA_WIDTH), tok(NA_WIDTH), tok(2 * D_MODEL), tok(D_MODEL),
                  pl.BlockSpec((None, 8, D_MODEL), lambda b, j: (b, 0, 0))] + [full(w) for w in ws],
        out_specs=[tok(D_MODEL), tok(D_MODEL), tok(LANES)],
        out_shape=[jax.ShapeDtypeStruct((bsz, seq, D_MODEL), F32),
                   jax.ShapeDtypeStruct((bsz, seq, D_MODEL), BF16),
                   jax.ShapeDtypeStruct((bsz, seq, LANES), F32)],
        compiler_params=_params(("arbitrary", "arbitrary")),
        name=name,
    )(o_na, o_mla, gates, x, ada, *ws)


def _moe_body(h_ref, c_ref, x_ref, ada_ref, wgu_ref, wd_ref, lng_ref, lnb_ref, o_ref, acc_ref):
    e = pl.program_id(2)

    @pl.when(e == 0)
    def _():
        acc_ref[...] = jnp.zeros_like(acc_ref)

    h = h_ref[...]
    gu = _dot(h, wgu_ref[0])
    gate = gu[:, :EXPERT_FF]
    up = gu[:, EXPERT_FF:]
    comb = c_ref[...]
    lane = lax.broadcasted_iota(jnp.int32, comb.shape, 1)
    c_e = jnp.sum(jnp.where(lane == e + R_EXP0, comb, 0.0), axis=-1, keepdims=True)
    hid = (gate * _sigmoid(gate) * up * c_e).astype(BF16)
    acc_ref[...] += _dot(hid, wd_ref[0])

    @pl.when(e == N_EXPERTS - 1)
    def _():
        g2 = ada_ref[5:6, :]
        o_ref[...] = _ln_plain(DEEPNORM_ALPHA * x_ref[...] + g2 * acc_ref[...]) * lng_ref[...] + lnb_ref[...]


def _moe_call(h2, comb, x1, ada, lw, tm, name):
    bsz, seq, _ = x1.shape
    nj = seq // tm

    def tok(width):
        return pl.BlockSpec((None, tm, width), lambda b, j, e: (b, j, 0))

    def full(arr):
        return pl.BlockSpec(arr.shape, lambda b, j, e: (0,) * arr.ndim)

    return pl.pallas_call(
        _moe_body,
        grid=(bsz, nj, N_EXPERTS),
        in_specs=[tok(D_MODEL), tok(LANES), tok(D_MODEL),
                  pl.BlockSpec((None, 8, D_MODEL), lambda b, j, e: (b, 0, 0)),
                  pl.BlockSpec((1, D_MODEL, 2 * EXPERT_FF), lambda b, j, e: (e, 0, 0)),
                  pl.BlockSpec((1, EXPERT_FF, D_MODEL), lambda b, j, e: (e, 0, 0)),
                  full(lw["ln2_g"]), full(lw["ln2_b"])],
        out_specs=tok(D_MODEL),
        out_shape=jax.ShapeDtypeStruct((bsz, seq, D_MODEL), F32),
        scratch_shapes=[pltpu.VMEM((tm, D_MODEL), F32)],
        compiler_params=_params(("arbitrary", "arbitrary", "arbitrary")),
        name=name,
    )(h2, comb, x1, ada, lw["w_gu"], lw["w_d"], lw["ln2_g"], lw["ln2_b"])


def _placement(rows_used):
    e = np.zeros((LANES, MLA_PAD_WIDTH), np.float32)
    for r in range(rows_used):
        for hd in range(MLA_HEADS):
            e[r, hd * MLA_HEAD_PAD + MLA_NOPE_DIM + (r % MLA_ROPE_DIM)] = 1.0
    return jnp.asarray(e, BF16)


def _layer_weights(l, p):
    partner, _ = _rope_partner()
    w = p["w_in"][l]
    k_r = w[:, 2176:2208]
    w_in = jnp.concatenate(
        [w[:, 0:512] * NA_SCALE, w[:, 512:2176], k_r, k_r[:, partner],
         jnp.zeros((D_MODEL, LANES - 2 * MLA_ROPE_DIM), F32), w[:, 2208:]], axis=1).astype(BF16)
    uq = p["w_uq"][l].reshape(MLA_Q_LORA, MLA_HEADS, MLA_QK_DIM)
    pad = MLA_HEAD_PAD - MLA_QK_DIM
    w_uq = jnp.pad(uq, ((0, 0), (0, 0), (0, pad))).reshape(MLA_Q_LORA, MLA_PAD_WIDTH).astype(BF16)
    uq_sw = jnp.pad(uq[:, :, MLA_NOPE_DIM + partner], ((0, 0), (0, 0), (MLA_NOPE_DIM, pad)))
    w_uq_sw = uq_sw.reshape(MLA_Q_LORA, MLA_PAD_WIDTH).astype(BF16)
    uk = p["w_uk"][l].reshape(MLA_KV_LORA, MLA_HEADS, MLA_NOPE_DIM)
    w_uk = jnp.pad(uk, ((0, 0), (0, 0), (0, MLA_HEAD_PAD - MLA_NOPE_DIM))).reshape(
        MLA_KV_LORA, MLA_PAD_WIDTH).astype(BF16)
    w_r = jnp.concatenate([p["router_group_w"][l], p["router_expert_w"][l],
                           jnp.zeros((D_MODEL, LANES - N_GROUPS - N_EXPERTS), F32)], axis=1)
    b_r = jnp.concatenate([p["router_group_b"][l], p["router_expert_b"][l],
                           jnp.zeros((LANES - N_GROUPS - N_EXPERTS,), F32)])[None, :]
    return dict(
        w_in=w_in, w_uq=w_uq, w_uq_sw=w_uq_sw, w_uk=w_uk, w_uv=p["w_uv"][l].astype(BF16),
        q_norm=p["mla_q_norm"][l][None, :], kv_norm=p["mla_kv_norm"][l][None, :],
        e_plain=_placement(MLA_ROPE_DIM), e_rope=_placement(2 * MLA_ROPE_DIM),
        w_o_na=p["w_o_na"][l].astype(BF16), w_o_mla=p["w_o_mla"][l].astype(BF16),
        w_out=p["w_out"][l].astype(BF16),
        ln1_g=p["ln1_g"][l][None, :], ln1_b=p["ln1_b"][l][None, :],
        ln2_g=p["ln2_g"][l][None, :], ln2_b=p["ln2_b"][l][None, :],
        w_r=w_r, b_r=b_r,
        w_gu=jnp.concatenate([p["expert_w_gate"][l], p["expert_w_up"][l]], axis=-1).astype(BF16),
        w_d=p["expert_w_down"][l].astype(BF16),
    )


def _na_bias(rpb, rows):
    kr = min(NA_KR, rows)
    r = np.arange(rows)
    rs = np.clip(r - kr // 2, 0, rows - kr)
    vrow = (r[None, :] >= rs[:, None]) & (r[None, :] < rs[:, None] + kr)
    roff = np.clip(r[None, :] - r[:, None] + NA_KR - 1, 0, 2 * NA_KR - 2)
    c = np.arange(GRID_W)
    cs = np.clip(c - NA_KC // 2, 0, GRID_W - NA_KC)
    vcol = (c[None, :] >= cs[:, None]) & (c[None, :] < cs[:, None] + NA_KC)
    coff = np.clip(c[None, :] - c[:, None] + NA_KC - 1, 0, 2 * NA_KC - 2)
    b = rpb[:, roff[:, None, :, None], coff[None, :, None, :]]
    valid = vrow[:, None, :, None] & vcol[None, :, None, :]
    t = rows * GRID_W
    return jnp.where(valid[None], b, NEG_INF).reshape(rpb.shape[0], t, t).astype(F32)


def kernel(x_prompt, x_sample, cache_na_k, cache_na_v, cache_mla_ckv, cache_mla_kr, c, c_ctx, w_in, mla_q_norm, mla_kv_norm, w_uq, w_uk, w_uv, na_rpb, w_o_na, w_o_mla, w_out, w_ada, b_ada, ln1_g, ln1_b, ln2_g, ln2_b, router_group_w, router_group_b, router_expert_w, router_expert_b, expert_w_gate, expert_w_up, expert_w_down):
    p = dict(w_in=w_in, mla_q_norm=mla_q_norm, mla_kv_norm=mla_kv_norm, w_uq=w_uq, w_uk=w_uk, w_uv=w_uv,
             w_o_na=w_o_na, w_o_mla=w_o_mla, w_out=w_out, ln1_g=ln1_g, ln1_b=ln1_b, ln2_g=ln2_g, ln2_b=ln2_b,
             router_group_w=router_group_w, router_group_b=router_group_b,
             router_expert_w=router_expert_w, router_expert_b=router_expert_b,
             expert_w_gate=expert_w_gate, expert_w_up=expert_w_up, expert_w_down=expert_w_down)
    pb, ps, _ = x_prompt.shape
    sb, ss, _ = x_sample.shape
    past = cache_na_k.shape[2]

    n_cond = 16
    cond = jnp.zeros((n_cond, D_MODEL), F32).at[0].set(c_ctx).at[1:1 + sb].set(c)
    ada_all = _ada_call(cond, w_ada, b_ada[:, None, :]).reshape(DEPTH, n_cond, 6, D_MODEL)
    ada_all = jnp.pad(ada_all, ((0, 0), (0, 0), (0, 2), (0, 0)))

    cos32, sin32 = _rope_tables(ss)
    head_c = jnp.concatenate([jnp.ones((ss, MLA_NOPE_DIM), F32), cos32,
                              jnp.zeros((ss, MLA_HEAD_PAD - MLA_QK_DIM), F32)], axis=-1)
    head_s = jnp.concatenate([jnp.zeros((ss, MLA_NOPE_DIM), F32), sin32,
                              jnp.zeros((ss, MLA_HEAD_PAD - MLA_QK_DIM), F32)], axis=-1)
    rope_tabs = (jnp.tile(head_c, (1, MLA_HEADS)) * MLA_SCALE, jnp.tile(head_s, (1, MLA_HEADS)) * MLA_SCALE,
                 jnp.concatenate([cos32, sin32, jnp.zeros((ss, LANES - 2 * MLA_ROPE_DIM), F32)], axis=-1))

    cna_k = cache_na_k.reshape(sb, DEPTH, past, NA_WIDTH)
    cna_v = cache_na_v.reshape(sb, DEPTH, past, NA_WIDTH)
    ckr_pad = jnp.pad(cache_mla_kr, ((0, 0), (0, 0), (0, 0), (0, LANES - MLA_ROPE_DIM)))

    xp, xs = x_prompt, x_sample
    new_k, new_v, new_ckv, new_kr = [], [], [], []
    for l in range(DEPTH):
        lw = _layer_weights(l, p)
        ada_p = jnp.broadcast_to(ada_all[l, 0], (pb, 8, D_MODEL))
        ada_s = ada_all[l, 1:1 + sb]

        qn, kn, vn, qm, km, vm, gates, ckv, k_r = _proj_call(xp, ada_p, lw, None, F32, True, ps)
        o_na = _attn_call(qn, [(kn, vn, _seg3(kn, 128), _seg3(vn, 128))], None, NA_HEAD_DIM, ps, "attn_ctx_na")
        o_mla = _attn_call(qm, [(km, vm, _seg3(km, 256), _seg3(vm, 128))], None, MLA_HEAD_PAD, ps, "attn_ctx_mla")
        x1, h2, comb = _merge_call(o_na, o_mla, gates, xp, ada_p, lw, ps, "merge_ctx")
        xp = _moe_call(h2, comb, x1, ada_p, lw, ps, "moe_ctx")
        new_k.append(kn)
        new_v.append(vn)
        new_ckv.append(ckv)
        new_kr.append(k_r)

        qn, kn, vn, qm, km, vm, gates = _proj_call(xs, ada_s, lw, rope_tabs, BF16, False, 256)
        km_ctx, vm_ctx = _ctxkv_call(cache_mla_ckv, ckr_pad, l, lw)
        bias = _na_bias(na_rpb[l], ss // GRID_W)
        tq = 512
        o_na = _attn_call(
            qn, [(cna_k, cna_v, _seg4(cna_k, l, 128), _seg4(cna_v, l, 128)),
                 (kn, vn, _seg3(kn, 128), _seg3(vn, 128))], bias, NA_HEAD_DIM, tq, "attn_lat_na")
        o_mla = _attn_call(
            qm, [(km_ctx, vm_ctx, _seg3(km_ctx, 256), _seg3(vm_ctx, 128)),
                 (km, vm, _seg3(km, 256), _seg3(vm, 128))], None, MLA_HEAD_PAD, tq, "attn_lat_mla")
        x1, h2, comb = _merge_call(o_na, o_mla, gates, xs, ada_s, lw, 256, "merge_lat")
        xs = _moe_call(h2, comb, x1, ada_s, lw, 512, "moe_lat")

    def stack(parts, tail):
        return jnp.stack(parts, axis=1).reshape((pb, DEPTH, ps) + tail)

    return (xp, xs, stack(new_k, (NA_HEADS, NA_HEAD_DIM)), stack(new_v, (NA_HEADS, NA_HEAD_DIM)),
            stack(new_ckv, (MLA_KV_LORA,)), stack(new_kr, (MLA_ROPE_DIM,)))
```

```python
import functools

import numpy as np
import jax
import jax.numpy as jnp
from jax import lax
from jax.experimental import pallas as pl
from jax.experimental.pallas import tpu as pltpu

F32 = jnp.float32
BF16 = jnp.bfloat16

D_MODEL = 1024
DEPTH = 2
GRID_W = 64
NA_HEADS = 8
NA_HEAD_DIM = 64
NA_KR = 8
NA_KC = 16
NA_WIDTH = NA_HEADS * NA_HEAD_DIM
NA_SCALE = NA_HEAD_DIM ** -0.5
MLA_HEADS = 8
MLA_NOPE_DIM = 64
MLA_ROPE_DIM = 32
MLA_V_DIM = 64
MLA_Q_LORA = 384
MLA_KV_LORA = 256
MLA_QK_DIM = MLA_NOPE_DIM + MLA_ROPE_DIM
MLA_SCALE = MLA_QK_DIM ** -0.5
ROPE_BASE = 10000.0
N_GROUPS = 4
EXPERTS_PER_GROUP = 8
N_EXPERTS = N_GROUPS * EXPERTS_PER_GROUP
EXPERT_FF = 256
LN_EPS = 1e-5
RMS_EPS = 1e-6
NEG_INF = -1e30
DEEPNORM_ALPHA = (2 * DEPTH) ** 0.25

LANES = 128
MLA_HEAD_PAD = LANES
MLA_PAD_WIDTH = MLA_HEADS * MLA_HEAD_PAD
C_Q = 0
C_K = C_Q + NA_WIDTH
C_V = C_K + NA_WIDTH
C_CQ = C_V + NA_WIDTH
C_CKV = C_CQ + MLA_Q_LORA
C_KR = C_CKV + MLA_KV_LORA
C_GATE = C_KR + LANES
C_END = C_GATE + 2 * D_MODEL
R_EXP0 = N_GROUPS

VMEM_LIMIT = 48 * 1024 * 1024


def _rope_partner():
    j = np.arange(MLA_ROPE_DIM)
    n_freq = MLA_ROPE_DIM // 4
    axis, within = j // (2 * n_freq), j % (2 * n_freq)
    half, f = within // n_freq, within % n_freq
    return axis * 2 * n_freq + (1 - half) * n_freq + f, half


def _rope_tables(n_tokens):
    n_freq = MLA_ROPE_DIM // 4
    inv_freq = ROPE_BASE ** (-jnp.arange(n_freq, dtype=F32) / n_freq)
    t = jnp.arange(n_tokens)
    row = (t // GRID_W).astype(F32)
    col = (t % GRID_W).astype(F32)
    ang = jnp.concatenate([row[:, None] * inv_freq, col[:, None] * inv_freq], axis=-1)
    cos, sin = jnp.cos(ang), jnp.sin(ang)
    cos32 = jnp.concatenate([cos[:, :n_freq], cos[:, :n_freq], cos[:, n_freq:], cos[:, n_freq:]], axis=-1)
    sin32 = jnp.concatenate([-sin[:, :n_freq], sin[:, :n_freq], -sin[:, n_freq:], sin[:, n_freq:]], axis=-1)
    return cos32, sin32


def _ln_plain(x):
    mu = jnp.mean(x, axis=-1, keepdims=True)
    xc = x - mu
    var = jnp.mean(xc * xc, axis=-1, keepdims=True)
    return xc * lax.rsqrt(var + LN_EPS)


def _rms(x, g):
    return x * lax.rsqrt(jnp.mean(x * x, axis=-1, keepdims=True) + RMS_EPS) * g


def _sigmoid(x):
    return 1.0 / (1.0 + jnp.exp(-x))


def _dot(a, b):
    return jnp.dot(a, b, preferred_element_type=F32)


def _params(sem):
    return pltpu.CompilerParams(dimension_semantics=sem, vmem_limit_bytes=VMEM_LIMIT)


def _ada_body(c_ref, w_ref, b_ref, o_ref):
    c = c_ref[...]
    s = (c * _sigmoid(c)).astype(BF16)
    o_ref[0] = _dot(s, w_ref[0].astype(BF16)) + b_ref[0]


def _ada_call(cond, w_ada, b_ada):
    n_rows = cond.shape[0]
    tn = 1024
    return pl.pallas_call(
        _ada_body,
        grid=(DEPTH, 6 * D_MODEL // tn),
        in_specs=[
            pl.BlockSpec((n_rows, D_MODEL), lambda l, j: (0, 0)),
            pl.BlockSpec((1, D_MODEL, tn), lambda l, j: (l, 0, j)),
            pl.BlockSpec((1, 1, tn), lambda l, j: (l, 0, j)),
        ],
        out_specs=pl.BlockSpec((1, n_rows, tn), lambda l, j: (l, 0, j)),
        out_shape=jax.ShapeDtypeStruct((DEPTH, n_rows, 6 * D_MODEL), F32),
        compiler_params=_params(("arbitrary", "arbitrary")),
        name="ada",
    )(cond, w_ada, b_ada)


def _proj_body(*refs, rope, emit_cache):
    it = iter(refs)
    x_ref, ada_ref, w_ref, qn_ref, kvn_ref, wuq_ref = (next(it) for _ in range(6))
    wuqs_ref = next(it) if rope else None
    wuk_ref, wuv_ref, e_ref = next(it), next(it), next(it)
    if rope:
        cq_ref, sq_ref, tk_ref = next(it), next(it), next(it)
    oq_ref, ok_ref, ov_ref, oqm_ref, okm_ref, ovm_ref, og_ref = (next(it) for _ in range(7))
    if emit_cache:
        ockv_ref, okr_ref = next(it), next(it)

    x = x_ref[...]
    sh = ada_ref[0:1, :]
    sc = ada_ref[1:2, :]
    h = (_ln_plain(x) * (1.0 + sc) + sh).astype(BF16)

    oq_ref[...] = _dot(h, w_ref[:, C_Q:C_K]).astype(oq_ref.dtype)
    ok_ref[...] = _dot(h, w_ref[:, C_K:C_V]).astype(ok_ref.dtype)
    ov_ref[...] = _dot(h, w_ref[:, C_V:C_CQ]).astype(ov_ref.dtype)

    c_q = _rms(_dot(h, w_ref[:, C_CQ:C_CKV]), qn_ref[...]).astype(BF16)
    q = _dot(c_q, wuq_ref[...])
    if rope:
        q = q * cq_ref[...] + _dot(c_q, wuqs_ref[...]) * sq_ref[...]
    else:
        q = q * MLA_SCALE
    oqm_ref[...] = q.astype(BF16)

    c_kv = _rms(_dot(h, w_ref[:, C_CKV:C_KR]), kvn_ref[...])
    slab = _dot(h, w_ref[:, C_KR:C_GATE])
    if emit_cache:
        ockv_ref[...] = c_kv
        okr_ref[...] = slab[:, :MLA_ROPE_DIM]
    if rope:
        slab = slab * tk_ref[...]
    c_kv = c_kv.astype(BF16)
    okm_ref[...] = (_dot(c_kv, wuk_ref[...]) + _dot(slab.astype(BF16), e_ref[...])).astype(BF16)
    ovm_ref[...] = _dot(c_kv, wuv_ref[...]).astype(BF16)

    og_ref[...] = _sigmoid(_dot(h, w_ref[:, C_GATE:C_END])).astype(BF16)


def _proj_call(x, ada, lw, rope_tabs, kv_dtype, emit_cache, tm):
    bsz, seq, _ = x.shape
    nj = seq // tm
    rope = rope_tabs is not None

    def tok(width):
        return pl.BlockSpec((None, tm, width), lambda b, j: (b, j, 0))

    def full(arr):
        return pl.BlockSpec(arr.shape, lambda b, j: (0,) * arr.ndim)

    def tab(width):
        return pl.BlockSpec((tm, width), lambda b, j: (j, 0))

    args = [x, ada, lw["w_in"], lw["q_norm"], lw["kv_norm"], lw["w_uq"]]
    specs = [tok(D_MODEL), pl.BlockSpec((None, 8, D_MODEL), lambda b, j: (b, 0, 0)),
             full(lw["w_in"]), full(lw["q_norm"]), full(lw["kv_norm"]), full(lw["w_uq"])]
    if rope:
        args.append(lw["w_uq_sw"])
        specs.append(full(lw["w_uq_sw"]))
    e_mat = lw["e_rope"] if rope else lw["e_plain"]
    args += [lw["w_uk"], lw["w_uv"], e_mat]
    specs += [full(lw["w_uk"]), full(lw["w_uv"]), full(e_mat)]
    if rope:
        args += list(rope_tabs)
        specs += [tab(MLA_PAD_WIDTH), tab(MLA_PAD_WIDTH), tab(LANES)]

    def shp(width, dt):
        return jax.ShapeDtypeStruct((bsz, seq, width), dt)

    out_shape = [shp(NA_WIDTH, BF16), shp(NA_WIDTH, kv_dtype), shp(NA_WIDTH, kv_dtype),
                 shp(MLA_PAD_WIDTH, BF16), shp(MLA_PAD_WIDTH, BF16), shp(NA_WIDTH, BF16),
                 shp(2 * D_MODEL, BF16)]
    out_specs = [tok(NA_WIDTH), tok(NA_WIDTH), tok(NA_WIDTH), tok(MLA_PAD_WIDTH), tok(MLA_PAD_WIDTH),
                 tok(NA_WIDTH), tok(2 * D_MODEL)]
    if emit_cache:
        out_shape += [shp(MLA_KV_LORA, F32), shp(MLA_ROPE_DIM, F32)]
        out_specs += [tok(MLA_KV_LORA), tok(MLA_ROPE_DIM)]
    return pl.pallas_call(
        functools.partial(_proj_body, rope=rope, emit_cache=emit_cache),
        grid=(bsz, nj),
        in_specs=specs,
        out_specs=out_specs,
        out_shape=out_shape,
        compiler_params=_params(("arbitrary", "arbitrary")),
        name="proj_lat" if rope else "proj_ctx",
    )(*args)


def _ctxkv_body(ckv_ref, kr_ref, wuk_ref, wuv_ref, e_ref, okm_ref, ovm_ref):
    c_kv = ckv_ref[...].astype(BF16)
    okm_ref[...] = (_dot(c_kv, wuk_ref[...]) + _dot(kr_ref[...].astype(BF16), e_ref[...])).astype(BF16)
    ovm_ref[...] = _dot(c_kv, wuv_ref[...]).astype(BF16)


def _ctxkv_call(cache_ckv, cache_kr_pad, layer, lw):
    bsz, _, past, _ = cache_ckv.shape

    def full(arr):
        return pl.BlockSpec(arr.shape, lambda b: (0,) * arr.ndim)

    return pl.pallas_call(
        _ctxkv_body,
        grid=(bsz,),
        in_specs=[
            pl.BlockSpec((None, None, past, MLA_KV_LORA), lambda b: (b, layer, 0, 0)),
            pl.BlockSpec((None, None, past, LANES), lambda b: (b, layer, 0, 0)),
            full(lw["w_uk"]), full(lw["w_uv"]), full(lw["e_plain"]),
        ],
        out_specs=[pl.BlockSpec((None, past, MLA_PAD_WIDTH), lambda b: (b, 0, 0)),
                   pl.BlockSpec((None, past, NA_WIDTH), lambda b: (b, 0, 0))],
        out_shape=[jax.ShapeDtypeStruct((bsz, past, MLA_PAD_WIDTH), BF16),
                   jax.ShapeDtypeStruct((bsz, past, NA_WIDTH), BF16)],
        compiler_params=_params(("arbitrary",)),
        name="ctxkv",
    )(cache_ckv, cache_kr_pad, lw["w_uk"], lw["w_uv"], lw["e_plain"])


def _attn_body(*refs, n_seg, dq, has_bias):
    q_ref = refs[0]
    seg_refs = [(refs[1 + 2 * i], refs[2 + 2 * i]) for i in range(n_seg)]
    bias_ref = refs[1 + 2 * n_seg] if has_bias else None
    o_ref = refs[-1]
    outs = []
    for hh in range(2):
        qh = q_ref[:, hh * dq:(hh + 1) * dq]
        scores = []
        for i, (k_ref, _) in enumerate(seg_refs):
            kh = k_ref[:, hh * dq:(hh + 1) * dq].astype(BF16)
            s = lax.dot_general(qh, kh, (((1,), (1,)), ((), ())), preferred_element_type=F32)
            if has_bias and i == n_seg - 1:
                s = s + bias_ref[hh]
            scores.append(s)
        m = functools.reduce(jnp.maximum, [jnp.max(s, axis=-1, keepdims=True) for s in scores])
        den = None
        acc = None
        for s, (_, v_ref) in zip(scores, seg_refs):
            e = jnp.exp(s - m)
            vh = v_ref[:, hh * MLA_V_DIM:(hh + 1) * MLA_V_DIM].astype(BF16)
            d = jnp.sum(e, axis=-1, keepdims=True)
            a = _dot(e.astype(BF16), vh)
            den = d if den is None else den + d
            acc = a if acc is None else acc + a
        outs.append(acc / den)
    o_ref[...] = jnp.concatenate(outs, axis=-1).astype(o_ref.dtype)


def _attn_call(q, segs, bias, dq, tq, name):
    bsz, seq, _ = q.shape
    args = [q]
    specs = [pl.BlockSpec((None, tq, 2 * dq), lambda hp, qt, b: (b, qt, hp))]
    for k, v, ks, vs in segs:
        args += [k, v]
        specs += [ks, vs]
    if bias is not None:
        args.append(bias)
        specs.append(pl.BlockSpec((2, tq, bias.shape[-1]), lambda hp, qt, b: (hp, qt, 0)))
    return pl.pallas_call(
        functools.partial(_attn_body, n_seg=len(segs), dq=dq, has_bias=bias is not None),
        grid=(NA_HEADS // 2, seq // tq, bsz),
        in_specs=specs,
        out_specs=pl.BlockSpec((None, tq, 2 * MLA_V_DIM), lambda hp, qt, b: (b, qt, hp)),
        out_shape=jax.ShapeDtypeStruct((bsz, seq, NA_WIDTH), BF16),
        compiler_params=_params(("arbitrary", "arbitrary", "arbitrary")),
        name=name,
    )(*args)


def _seg3(arr, width):
    n = arr.shape[1]
    return pl.BlockSpec((None, n, width), lambda hp, qt, b: (b, 0, hp))


def _seg4(arr, layer, width):
    n = arr.shape[2]
    return pl.BlockSpec((None, None, n, width), lambda hp, qt, b: (b, layer, 0, hp))


def _route(logits):
    lane = lax.broadcasted_iota(jnp.int32, logits.shape, 1)
    lane_f = lane.astype(F32)
    far = float(LANES)
    is_g = lane < N_GROUPS
    gl = jnp.where(is_g, logits, NEG_INF)
    gmax = jnp.max(gl, axis=-1, keepdims=True)
    gsum = jnp.sum(jnp.exp(gl - gmax), axis=-1, keepdims=True)
    g_w = 1.0 / gsum
    g_idx = jnp.min(jnp.where(gl == gmax, lane_f, far), axis=-1, keepdims=True)
    e_group = ((lane - R_EXP0) >> 3).astype(F32)
    is_e = (lane >= R_EXP0) & (lane < R_EXP0 + N_EXPERTS) & (e_group == g_idx)
    el = jnp.where(is_e, logits, NEG_INF)
    m1 = jnp.max(el, axis=-1, keepdims=True)
    i1 = jnp.min(jnp.where(el == m1, lane_f, far), axis=-1, keepdims=True)
    el2 = jnp.where(lane_f == i1, NEG_INF, el)
    m2 = jnp.max(el2, axis=-1, keepdims=True)
    i2 = jnp.min(jnp.where(el2 == m2, lane_f, far), axis=-1, keepdims=True)
    r = jnp.exp(m2 - m1)
    w1 = g_w / (1.0 + r)
    w2 = g_w * r / (1.0 + r)
    return lane_f, i1, i2, w1, w2


def _merge_body(ona_ref, omla_ref, g_ref, x_ref, ada_ref, wna_ref, wmla_ref, wout_ref, lng_ref, lnb_ref,
                wr_ref, br_ref, ox_ref, oh_ref, oc_ref):
    y_na = _dot(ona_ref[...], wna_ref[...])
    y_mla = _dot(omla_ref[...], wmla_ref[...])
    g_na = g_ref[:, :D_MODEL].astype(F32)
    g_mla = g_ref[:, D_MODEL:].astype(F32)
    y = _dot((g_na * y_na + g_mla * y_mla).astype(BF16), wout_ref[...])
    g1 = ada_ref[2:3, :]
    x1 = _ln_plain(DEEPNORM_ALPHA * x_ref[...] + g1 * y) * lng_ref[...] + lnb_ref[...]
    ox_ref[...] = x1
    h2 = _ln_plain(x1) * (1.0 + ada_ref[4:5, :]) + ada_ref[3:4, :]
    oh_ref[...] = h2.astype(BF16)
    logits = jnp.dot(h2, wr_ref[...], preferred_element_type=F32, precision=lax.Precision.HIGHEST) + br_ref[...]
    lane_f, i1, i2, w1, w2 = _route(logits)
    oc_ref[...] = jnp.where(lane_f == i1, w1, 0.0) + jnp.where(lane_f == i2, w2, 0.0)


def _merge_call(o_na, o_mla, gates, x, ada, lw, tm, name):
    bsz, seq, _ = x.shape

    def tok(width):
        return pl.BlockSpec((None, tm, width), lambda b, j: (b, j, 0))

    def full(arr):
        return pl.BlockSpec(arr.shape, lambda b, j: (0,) * arr.ndim)

    ws = [lw["w_o_na"], lw["w_o_mla"], lw["w_out"], lw["ln1_g"], lw["ln1_b"], lw["w_r"], lw["b_r"]]
    return pl.pallas_call(
        _merge_body,
        grid=(bsz, seq // tm),
        in_specs=[tok(NA_WIDTH), tok(NA_WIDTH), tok(2 * D_MODEL), tok(D_MODEL),
                  pl.BlockSpec((None, 8, D_MODEL), lambda b, j: (b, 0, 0))] + [full(w) for w in ws],
        out_specs=[tok(D_MODEL), tok(D_MODEL), tok(LANES)],
        out_shape=[jax.ShapeDtypeStruct((bsz, seq, D_MODEL), F32),
                   jax.ShapeDtypeStruct((bsz, seq, D_MODEL), BF16),
                   jax.ShapeDtypeStruct((bsz, seq, LANES), F32)],
        compiler_params=_params(("arbitrary", "arbitrary")),
        name=name,
    )(o_na, o_mla, gates, x, ada, *ws)


def _moe_body(h_ref, c_ref, x_ref, ada_ref, wgu_ref, wd_ref, lng_ref, lnb_ref, o_ref, acc_ref):
    e = pl.program_id(2)

    @pl.when(e == 0)
    def _():
        acc_ref[...] = jnp.zeros_like(acc_ref)

    h = h_ref[...]
    gu = _dot(h, wgu_ref[0])
    gate = gu[:, :EXPERT_FF]
    up = gu[:, EXPERT_FF:]
    comb = c_ref[...]
    lane = lax.broadcasted_iota(jnp.int32, comb.shape, 1)
    c_e = jnp.sum(jnp.where(lane == e + R_EXP0, comb, 0.0), axis=-1, keepdims=True)
    hid = (gate * _sigmoid(gate) * up * c_e).astype(BF16)
    acc_ref[...] += _dot(hid, wd_ref[0])

    @pl.when(e == N_EXPERTS - 1)
    def _():
        g2 = ada_ref[5:6, :]
        o_ref[...] = _ln_plain(DEEPNORM_ALPHA * x_ref[...] + g2 * acc_ref[...]) * lng_ref[...] + lnb_ref[...]


def _moe_call(h2, comb, x1, ada, lw, tm, name):
    bsz, seq, _ = x1.shape
    nj = seq // tm

    def tok(width):
        return pl.BlockSpec((None, tm, width), lambda b, j, e: (b, j, 0))

    def full(arr):
        return pl.BlockSpec(arr.shape, lambda b, j, e: (0,) * arr.ndim)

    return pl.pallas_call(
        _moe_body,
        grid=(bsz, nj, N_EXPERTS),
        in_specs=[tok(D_MODEL), tok(LANES), tok(D_MODEL),
                  pl.BlockSpec((None, 8, D_MODEL), lambda b, j, e: (b, 0, 0)),
                  pl.BlockSpec((1, D_MODEL, 2 * EXPERT_FF), lambda b, j, e: (e, 0, 0)),
                  pl.BlockSpec((1, EXPERT_FF, D_MODEL), lambda b, j, e: (e, 0, 0)),
                  full(lw["ln2_g"]), full(lw["ln2_b"])],
        out_specs=tok(D_MODEL),
        out_shape=jax.ShapeDtypeStruct((bsz, seq, D_MODEL), F32),
        scratch_shapes=[pltpu.VMEM((tm, D_MODEL), F32)],
        compiler_params=_params(("arbitrary", "arbitrary", "arbitrary")),
        name=name,
    )(h2, comb, x1, ada, lw["w_gu"], lw["w_d"], lw["ln2_g"], lw["ln2_b"])


def _placement(rows_used):
    e = np.zeros((LANES, MLA_PAD_WIDTH), np.float32)
    for r in range(rows_used):
        for hd in range(MLA_HEADS):
            e[r, hd * MLA_HEAD_PAD + MLA_NOPE_DIM + (r % MLA_ROPE_DIM)] = 1.0
    return jnp.asarray(e, BF16)


def _layer_weights(l, p):
    partner, _ = _rope_partner()
    w = p["w_in"][l]
    k_r = w[:, 2176:2208]
    w_in = jnp.concatenate(
        [w[:, 0:512] * NA_SCALE, w[:, 512:2176], k_r, k_r[:, partner],
         jnp.zeros((D_MODEL, LANES - 2 * MLA_ROPE_DIM), F32), w[:, 2208:]], axis=1).astype(BF16)
    uq = p["w_uq"][l].reshape(MLA_Q_LORA, MLA_HEADS, MLA_QK_DIM)
    pad = MLA_HEAD_PAD - MLA_QK_DIM
    w_uq = jnp.pad(uq, ((0, 0), (0, 0), (0, pad))).reshape(MLA_Q_LORA, MLA_PAD_WIDTH).astype(BF16)
    uq_sw = jnp.pad(uq[:, :, MLA_NOPE_DIM + partner], ((0, 0), (0, 0), (MLA_NOPE_DIM, pad)))
    w_uq_sw = uq_sw.reshape(MLA_Q_LORA, MLA_PAD_WIDTH).astype(BF16)
    uk = p["w_uk"][l].reshape(MLA_KV_LORA, MLA_HEADS, MLA_NOPE_DIM)
    w_uk = jnp.pad(uk, ((0, 0), (0, 0), (0, MLA_HEAD_PAD - MLA_NOPE_DIM))).reshape(
        MLA_KV_LORA, MLA_PAD_WIDTH).astype(BF16)
    w_r = jnp.concatenate([p["router_group_w"][l], p["router_expert_w"][l],
                           jnp.zeros((D_MODEL, LANES - N_GROUPS - N_EXPERTS), F32)], axis=1)
    b_r = jnp.concatenate([p["router_group_b"][l], p["router_expert_b"][l],
                           jnp.zeros((LANES - N_GROUPS - N_EXPERTS,), F32)])[None, :]
    return dict(
        w_in=w_in, w_uq=w_uq, w_uq_sw=w_uq_sw, w_uk=w_uk, w_uv=p["w_uv"][l].astype(BF16),
        q_norm=p["mla_q_norm"][l][None, :], kv_norm=p["mla_kv_norm"][l][None, :],
        e_plain=_placement(MLA_ROPE_DIM), e_rope=_placement(2 * MLA_ROPE_DIM),
        w_o_na=p["w_o_na"][l].astype(BF16), w_o_mla=p["w_o_mla"][l].astype(BF16),
        w_out=p["w_out"][l].astype(BF16),
        ln1_g=p["ln1_g"][l][None, :], ln1_b=p["ln1_b"][l][None, :],
        ln2_g=p["ln2_g"][l][None, :], ln2_b=p["ln2_b"][l][None, :],
        w_r=w_r, b_r=b_r,
        w_gu=jnp.concatenate([p["expert_w_gate"][l], p["expert_w_up"][l]], axis=-1).astype(BF16),
        w_d=p["expert_w_down"][l].astype(BF16),
    )


def _na_bias(rpb, rows):
    kr = min(NA_KR, rows)
    r = np.arange(rows)
    rs = np.clip(r - kr // 2, 0, rows - kr)
    vrow = (r[None, :] >= rs[:, None]) & (r[None, :] < rs[:, None] + kr)
    c = np.arange(GRID_W)
    cs = np.clip(c - NA_KC // 2, 0, GRID_W - NA_KC)
    vcol = (c[None, :] >= cs[:, None]) & (c[None, :] < cs[:, None] + NA_KC)
    coff = c[None, :] - c[:, None] + NA_KC - 1
    onehot = ((coff[None] == np.arange(2 * NA_KC - 1)[:, None, None]) & vcol[None]).astype(np.float32)
    blocks = jnp.einsum("hdj,jqk->hdqk", rpb, jnp.asarray(onehot), precision=lax.Precision.HIGHEST)
    blocks = jnp.where(jnp.asarray(vcol), blocks, NEG_INF)
    n_heads = rpb.shape[0]
    outside = jnp.full((n_heads, GRID_W, GRID_W), NEG_INF, F32)
    q_rows = []
    for qr in range(rows):
        q_rows.append(jnp.concatenate(
            [blocks[:, k - qr + NA_KR - 1] if vrow[qr, k] else outside for k in range(rows)], axis=-1))
    t = rows * GRID_W
    return jnp.stack(q_rows, axis=1).reshape(n_heads, t, t)


def kernel(x_prompt, x_sample, cache_na_k, cache_na_v, cache_mla_ckv, cache_mla_kr, c, c_ctx, w_in, mla_q_norm, mla_kv_norm, w_uq, w_uk, w_uv, na_rpb, w_o_na, w_o_mla, w_out, w_ada, b_ada, ln1_g, ln1_b, ln2_g, ln2_b, router_group_w, router_group_b, router_expert_w, router_expert_b, expert_w_gate, expert_w_up, expert_w_down):
    p = dict(w_in=w_in, mla_q_norm=mla_q_norm, mla_kv_norm=mla_kv_norm, w_uq=w_uq, w_uk=w_uk, w_uv=w_uv,
             w_o_na=w_o_na, w_o_mla=w_o_mla, w_out=w_out, ln1_g=ln1_g, ln1_b=ln1_b, ln2_g=ln2_g, ln2_b=ln2_b,
             router_group_w=router_group_w, router_group_b=router_group_b,
             router_expert_w=router_expert_w, router_expert_b=router_expert_b,
             expert_w_gate=expert_w_gate, expert_w_up=expert_w_up, expert_w_down=expert_w_down)
    pb, ps, _ = x_prompt.shape
    sb, ss, _ = x_sample.shape
    past = cache_na_k.shape[2]

    n_cond = 16
    cond = jnp.zeros((n_cond, D_MODEL), F32).at[0].set(c_ctx).at[1:1 + sb].set(c)
    ada_all = _ada_call(cond, w_ada, b_ada[:, None, :]).reshape(DEPTH, n_cond, 6, D_MODEL)
    ada_all = jnp.pad(ada_all, ((0, 0), (0, 0), (0, 2), (0, 0)))

    cos32, sin32 = _rope_tables(ss)
    head_c = jnp.concatenate([jnp.ones((ss, MLA_NOPE_DIM), F32), cos32,
                              jnp.zeros((ss, MLA_HEAD_PAD - MLA_QK_DIM), F32)], axis=-1)
    head_s = jnp.concatenate([jnp.zeros((ss, MLA_NOPE_DIM), F32), sin32,
                              jnp.zeros((ss, MLA_HEAD_PAD - MLA_QK_DIM), F32)], axis=-1)
    rope_tabs = (jnp.tile(head_c, (1, MLA_HEADS)) * MLA_SCALE, jnp.tile(head_s, (1, MLA_HEADS)) * MLA_SCALE,
                 jnp.concatenate([cos32, sin32, jnp.zeros((ss, LANES - 2 * MLA_ROPE_DIM), F32)], axis=-1))

    cna_k = cache_na_k.reshape(sb, DEPTH, past, NA_WIDTH)
    cna_v = cache_na_v.reshape(sb, DEPTH, past, NA_WIDTH)
    ckr_pad = jnp.pad(cache_mla_kr, ((0, 0), (0, 0), (0, 0), (0, LANES - MLA_ROPE_DIM)))

    xp, xs = x_prompt, x_sample
    new_k, new_v, new_ckv, new_kr = [], [], [], []
    for l in range(DEPTH):
        lw = _layer_weights(l, p)
        ada_p = jnp.broadcast_to(ada_all[l, 0], (pb, 8, D_MODEL))
        ada_s = ada_all[l, 1:1 + sb]

        qn, kn, vn, qm, km, vm, gates, ckv, k_r = _proj_call(xp, ada_p, lw, None, F32, True, ps)
        o_na = _attn_call(qn, [(kn, vn, _seg3(kn, 128), _seg3(vn, 128))], None, NA_HEAD_DIM, ps, "attn_ctx_na")
        o_mla = _attn_call(qm, [(km, vm, _seg3(km, 256), _seg3(vm, 128))], None, MLA_HEAD_PAD, ps, "attn_ctx_mla")
        x1, h2, comb = _merge_call(o_na, o_mla, gates, xp, ada_p, lw, ps, "merge_ctx")
        xp = _moe_call(h2, comb, x1, ada_p, lw, ps, "moe_ctx")
        new_k.append(kn)
        new_v.append(vn)
        new_ckv.append(ckv)
        new_kr.append(k_r)

        qn, kn, vn, qm, km, vm, gates = _proj_call(xs, ada_s, lw, rope_tabs, BF16, False, 256)
        km_ctx, vm_ctx = _ctxkv_call(cache_mla_ckv, ckr_pad, l, lw)
        bias = _na_bias(na_rpb[l], ss // GRID_W)
        tq = 512
        o_na = _attn_call(
            qn, [(cna_k, cna_v, _seg4(cna_k, l, 128), _seg4(cna_v, l, 128)),
                 (kn, vn, _seg3(kn, 128), _seg3(vn, 128))], bias, NA_HEAD_DIM, tq, "attn_lat_na")
        o_mla = _attn_call(
            qm, [(km_ctx, vm_ctx, _seg3(km_ctx, 256), _seg3(vm_ctx, 128)),
                 (km, vm, _seg3(km, 256), _seg3(vm, 128))], None, MLA_HEAD_PAD, tq, "attn_lat_mla")
        x1, h2, comb = _merge_call(o_na, o_mla, gates, xs, ada_s, lw, 256, "merge_lat")
        xs = _moe_call(h2, comb, x1, ada_s, lw, 512, "moe_lat")

    def stack(parts, tail):
        return jnp.stack(parts, axis=1).reshape((pb, DEPTH, ps) + tail)

    return (xp, xs, stack(new_k, (NA_HEADS, NA_HEAD_DIM)), stack(new_v, (NA_HEADS, NA_HEAD_DIM)),
            stack(new_ckv, (MLA_KV_LORA,)), stack(new_kr, (MLA_ROPE_DIM,)))
```

```python
import functools

import numpy as np
import jax
import jax.numpy as jnp
from jax import lax
from jax.experimental import pallas as pl
from jax.experimental.pallas import tpu as pltpu

F32 = jnp.float32
BF16 = jnp.bfloat16

D_MODEL = 1024
DEPTH = 2
GRID_W = 64
NA_HEADS = 8
NA_HEAD_DIM = 64
NA_KR = 8
NA_KC = 16
NA_WIDTH = NA_HEADS * NA_HEAD_DIM
NA_SCALE = NA_HEAD_DIM ** -0.5
MLA_HEADS = 8
MLA_NOPE_DIM = 64
MLA_ROPE_DIM = 32
MLA_V_DIM = 64
MLA_Q_LORA = 384
MLA_KV_LORA = 256
MLA_QK_DIM = MLA_NOPE_DIM + MLA_ROPE_DIM
MLA_SCALE = MLA_QK_DIM ** -0.5
ROPE_BASE = 10000.0
N_GROUPS = 4
EXPERTS_PER_GROUP = 8
N_EXPERTS = N_GROUPS * EXPERTS_PER_GROUP
EXPERT_FF = 256
LN_EPS = 1e-5
RMS_EPS = 1e-6
NEG_INF = -1e30
DEEPNORM_ALPHA = (2 * DEPTH) ** 0.25

LANES = 128
MLA_HEAD_PAD = LANES
MLA_PAD_WIDTH = MLA_HEADS * MLA_HEAD_PAD
C_Q = 0
C_K = C_Q + NA_WIDTH
C_V = C_K + NA_WIDTH
C_CQ = C_V + NA_WIDTH
C_CKV = C_CQ + MLA_Q_LORA
C_KR = C_CKV + MLA_KV_LORA
C_GATE = C_KR + LANES
C_END = C_GATE + 2 * D_MODEL
R_EXP0 = N_GROUPS

VMEM_LIMIT = 48 * 1024 * 1024


def _rope_partner():
    j = np.arange(MLA_ROPE_DIM)
    n_freq = MLA_ROPE_DIM // 4
    axis, within = j // (2 * n_freq), j % (2 * n_freq)
    half, f = within // n_freq, within % n_freq
    return axis * 2 * n_freq + (1 - half) * n_freq + f, half


def _rope_tables(n_tokens):
    n_freq = MLA_ROPE_DIM // 4
    inv_freq = ROPE_BASE ** (-jnp.arange(n_freq, dtype=F32) / n_freq)
    t = jnp.arange(n_tokens)
    row = (t // GRID_W).astype(F32)
    col = (t % GRID_W).astype(F32)
    ang = jnp.concatenate([row[:, None] * inv_freq, col[:, None] * inv_freq], axis=-1)
    cos, sin = jnp.cos(ang), jnp.sin(ang)
    cos32 = jnp.concatenate([cos[:, :n_freq], cos[:, :n_freq], cos[:, n_freq:], cos[:, n_freq:]], axis=-1)
    sin32 = jnp.concatenate([-sin[:, :n_freq], sin[:, :n_freq], -sin[:, n_freq:], sin[:, n_freq:]], axis=-1)
    return cos32, sin32


def _ln_plain(x):
    mu = jnp.mean(x, axis=-1, keepdims=True)
    xc = x - mu
    var = jnp.mean(xc * xc, axis=-1, keepdims=True)
    return xc * lax.rsqrt(var + LN_EPS)


def _rms(x, g):
    return x * lax.rsqrt(jnp.mean(x * x, axis=-1, keepdims=True) + RMS_EPS) * g


def _sigmoid(x):
    return 1.0 / (1.0 + jnp.exp(-x))


def _dot(a, b):
    return jnp.dot(a, b, preferred_element_type=F32)


def _params(sem):
    return pltpu.CompilerParams(dimension_semantics=sem, vmem_limit_bytes=VMEM_LIMIT)


def _ada_body(c_ref, w_ref, b_ref, o_ref):
    c = c_ref[...]
    s = (c * _sigmoid(c)).astype(BF16)
    o_ref[0] = _dot(s, w_ref[0].astype(BF16)) + b_ref[0]


def _ada_call(cond, w_ada, b_ada):
    n_rows = cond.shape[0]
    tn = 1024
    return pl.pallas_call(
        _ada_body,
        grid=(DEPTH, 6 * D_MODEL // tn),
        in_specs=[
            pl.BlockSpec((n_rows, D_MODEL), lambda l, j: (0, 0)),
            pl.BlockSpec((1, D_MODEL, tn), lambda l, j: (l, 0, j)),
            pl.BlockSpec((1, 1, tn), lambda l, j: (l, 0, j)),
        ],
        out_specs=pl.BlockSpec((1, n_rows, tn), lambda l, j: (l, 0, j)),
        out_shape=jax.ShapeDtypeStruct((DEPTH, n_rows, 6 * D_MODEL), F32),
        compiler_params=_params(("arbitrary", "arbitrary")),
        name="ada",
    )(cond, w_ada, b_ada)


def _proj_body(*refs, rope, emit_cache):
    it = iter(refs)
    x_ref, ada_ref, w_ref, qn_ref, kvn_ref, wuq_ref = (next(it) for _ in range(6))
    wuqs_ref = next(it) if rope else None
    wuk_ref, wuv_ref, e_ref = next(it), next(it), next(it)
    if rope:
        cq_ref, sq_ref, tk_ref = next(it), next(it), next(it)
    oq_ref, ok_ref, ov_ref, oqm_ref, okm_ref, ovm_ref, og_ref = (next(it) for _ in range(7))
    if emit_cache:
        ockv_ref, okr_ref = next(it), next(it)

    x = x_ref[...]
    sh = ada_ref[0:1, :]
    sc = ada_ref[1:2, :]
    h = (_ln_plain(x) * (1.0 + sc) + sh).astype(BF16)

    oq_ref[...] = _dot(h, w_ref[:, C_Q:C_K]).astype(oq_ref.dtype)
    ok_ref[...] = _dot(h, w_ref[:, C_K:C_V]).astype(ok_ref.dtype)
    ov_ref[...] = _dot(h, w_ref[:, C_V:C_CQ]).astype(ov_ref.dtype)

    c_q = _rms(_dot(h, w_ref[:, C_CQ:C_CKV]), qn_ref[...]).astype(BF16)
    q = _dot(c_q, wuq_ref[...])
    if rope:
        q = q * cq_ref[...] + _dot(c_q, wuqs_ref[...]) * sq_ref[...]
    else:
        q = q * MLA_SCALE
    oqm_ref[...] = q.astype(BF16)

    c_kv = _rms(_dot(h, w_ref[:, C_CKV:C_KR]), kvn_ref[...])
    slab = _dot(h, w_ref[:, C_KR:C_GATE])
    if emit_cache:
        ockv_ref[...] = c_kv
        okr_ref[...] = slab[:, :MLA_ROPE_DIM]
    if rope:
        slab = slab * tk_ref[...]
    c_kv = c_kv.astype(BF16)
    okm_ref[...] = (_dot(c_kv, wuk_ref[...]) + _dot(slab.astype(BF16), e_ref[...])).astype(BF16)
    ovm_ref[...] = _dot(c_kv, wuv_ref[...]).astype(BF16)

    og_ref[...] = _sigmoid(_dot(h, w_ref[:, C_GATE:C_END])).astype(BF16)


def _proj_call(x, ada, lw, rope_tabs, kv_dtype, emit_cache, tm):
    bsz, seq, _ = x.shape
    nj = seq // tm
    rope = rope_tabs is not None

    def tok(width):
        return pl.BlockSpec((None, tm, width), lambda b, j: (b, j, 0))

    def full(arr):
        return pl.BlockSpec(arr.shape, lambda b, j: (0,) * arr.ndim)

    def tab(width):
        return pl.BlockSpec((tm, width), lambda b, j: (j, 0))

    args = [x, ada, lw["w_in"], lw["q_norm"], lw["kv_norm"], lw["w_uq"]]
    specs = [tok(D_MODEL), pl.BlockSpec((None, 8, D_MODEL), lambda b, j: (b, 0, 0)),
             full(lw["w_in"]), full(lw["q_norm"]), full(lw["kv_norm"]), full(lw["w_uq"])]
    if rope:
        args.append(lw["w_uq_sw"])
        specs.append(full(lw["w_uq_sw"]))
    e_mat = lw["e_rope"] if rope else lw["e_plain"]
    args += [lw["w_uk"], lw["w_uv"], e_mat]
    specs += [full(lw["w_uk"]), full(lw["w_uv"]), full(e_mat)]
    if rope:
        args += list(rope_tabs)
        specs += [tab(MLA_PAD_WIDTH), tab(MLA_PAD_WIDTH), tab(LANES)]

    def shp(width, dt):
        return jax.ShapeDtypeStruct((bsz, seq, width), dt)

    out_shape = [shp(NA_WIDTH, BF16), shp(NA_WIDTH, kv_dtype), shp(NA_WIDTH, kv_dtype),
                 shp(MLA_PAD_WIDTH, BF16), shp(MLA_PAD_WIDTH, BF16), shp(NA_WIDTH, BF16),
                 shp(2 * D_MODEL, BF16)]
    out_specs = [tok(NA_WIDTH), tok(NA_WIDTH), tok(NA_WIDTH), tok(MLA_PAD_WIDTH), tok(MLA_PAD_WIDTH),
                 tok(NA_WIDTH), tok(2 * D_MODEL)]
    if emit_cache:
        out_shape += [shp(MLA_KV_LORA, F32), shp(MLA_ROPE_DIM, F32)]
        out_specs += [tok(MLA_KV_LORA), tok(MLA_ROPE_DIM)]
    return pl.pallas_call(
        functools.partial(_proj_body, rope=rope, emit_cache=emit_cache),
        grid=(bsz, nj),
        in_specs=specs,
        out_specs=out_specs,
        out_shape=out_shape,
        compiler_params=_params(("arbitrary", "arbitrary")),
        name="proj_lat" if rope else "proj_ctx",
    )(*args)


def _ctxkv_body(ckv_ref, kr_ref, wuk_ref, wuv_ref, e_ref, okm_ref, ovm_ref):
    c_kv = ckv_ref[...].astype(BF16)
    okm_ref[...] = (_dot(c_kv, wuk_ref[...]) + _dot(kr_ref[...].astype(BF16), e_ref[...])).astype(BF16)
    ovm_ref[...] = _dot(c_kv, wuv_ref[...]).astype(BF16)


def _ctxkv_call(cache_ckv, cache_kr_pad, layer, lw):
    bsz, _, past, _ = cache_ckv.shape

    def full(arr):
        return pl.BlockSpec(arr.shape, lambda b: (0,) * arr.ndim)

    return pl.pallas_call(
        _ctxkv_body,
        grid=(bsz,),
        in_specs=[
            pl.BlockSpec((None, None, past, MLA_KV_LORA), lambda b: (b, layer, 0, 0)),
            pl.BlockSpec((None, None, past, LANES), lambda b: (b, layer, 0, 0)),
            full(lw["w_uk"]), full(lw["w_uv"]), full(lw["e_plain"]),
        ],
        out_specs=[pl.BlockSpec((None, past, MLA_PAD_WIDTH), lambda b: (b, 0, 0)),
                   pl.BlockSpec((None, past, NA_WIDTH), lambda b: (b, 0, 0))],
        out_shape=[jax.ShapeDtypeStruct((bsz, past, MLA_PAD_WIDTH), BF16),
                   jax.ShapeDtypeStruct((bsz, past, NA_WIDTH), BF16)],
        compiler_params=_params(("arbitrary",)),
        name="ctxkv",
    )(cache_ckv, cache_kr_pad, lw["w_uk"], lw["w_uv"], lw["e_plain"])


def _attn_body(*refs, n_seg, dq, has_bias):
    q_ref = refs[0]
    seg_refs = [(refs[1 + 2 * i], refs[2 + 2 * i]) for i in range(n_seg)]
    bias_ref = refs[1 + 2 * n_seg] if has_bias else None
    o_ref = refs[-1]
    outs = []
    for hh in range(2):
        qh = q_ref[:, hh * dq:(hh + 1) * dq]
        scores = []
        for i, (k_ref, _) in enumerate(seg_refs):
            kh = k_ref[:, hh * dq:(hh + 1) * dq].astype(BF16)
            s = lax.dot_general(qh, kh, (((1,), (1,)), ((), ())), preferred_element_type=F32)
            if has_bias and i == n_seg - 1:
                s = s + bias_ref[hh]
            scores.append(s)
        m = functools.reduce(jnp.maximum, [jnp.max(s, axis=-1, keepdims=True) for s in scores])
        den = None
        acc = None
        for s, (_, v_ref) in zip(scores, seg_refs):
            e = jnp.exp(s - m)
            vh = v_ref[:, hh * MLA_V_DIM:(hh + 1) * MLA_V_DIM].astype(BF16)
            d = jnp.sum(e, axis=-1, keepdims=True)
            a = _dot(e.astype(BF16), vh)
            den = d if den is None else den + d
            acc = a if acc is None else acc + a
        outs.append(acc / den)
    o_ref[...] = jnp.concatenate(outs, axis=-1).astype(o_ref.dtype)


def _attn_call(q, segs, bias, dq, tq, name):
    bsz, seq, _ = q.shape
    args = [q]
    specs = [pl.BlockSpec((None, tq, 2 * dq), lambda hp, qt, b: (b, qt, hp))]
    for k, v, ks, vs in segs:
        args += [k, v]
        specs += [ks, vs]
    if bias is not None:
        args.append(bias)
        specs.append(pl.BlockSpec((2, tq, bias.shape[-1]), lambda hp, qt, b: (hp, qt, 0)))
    return pl.pallas_call(
        functools.partial(_attn_body, n_seg=len(segs), dq=dq, has_bias=bias is not None),
        grid=(NA_HEADS // 2, seq // tq, bsz),
        in_specs=specs,
        out_specs=pl.BlockSpec((None, tq, 2 * MLA_V_DIM), lambda hp, qt, b: (b, qt, hp)),
        out_shape=jax.ShapeDtypeStruct((bsz, seq, NA_WIDTH), BF16),
        compiler_params=_params(("arbitrary", "arbitrary", "arbitrary")),
        name=name,
    )(*args)


def _seg3(arr, width):
    n = arr.shape[1]
    return pl.BlockSpec((None, n, width), lambda hp, qt, b: (b, 0, hp))


def _seg4(arr, layer, width):
    n = arr.shape[2]
    return pl.BlockSpec((None, None, n, width), lambda hp, qt, b: (b, layer, 0, hp))


def _route(logits):
    lane = lax.broadcasted_iota(jnp.int32, logits.shape, 1)
    lane_f = lane.astype(F32)
    far = float(LANES)
    is_g = lane < N_GROUPS
    gl = jnp.where(is_g, logits, NEG_INF)
    gmax = jnp.max(gl, axis=-1, keepdims=True)
    gsum = jnp.sum(jnp.exp(gl - gmax), axis=-1, keepdims=True)
    g_w = 1.0 / gsum
    g_idx = jnp.min(jnp.where(gl == gmax, lane_f, far), axis=-1, keepdims=True)
    e_group = ((lane - R_EXP0) >> 3).astype(F32)
    is_e = (lane >= R_EXP0) & (lane < R_EXP0 + N_EXPERTS) & (e_group == g_idx)
    el = jnp.where(is_e, logits, NEG_INF)
    m1 = jnp.max(el, axis=-1, keepdims=True)
    i1 = jnp.min(jnp.where(el == m1, lane_f, far), axis=-1, keepdims=True)
    el2 = jnp.where(lane_f == i1, NEG_INF, el)
    m2 = jnp.max(el2, axis=-1, keepdims=True)
    i2 = jnp.min(jnp.where(el2 == m2, lane_f, far), axis=-1, keepdims=True)
    r = jnp.exp(m2 - m1)
    w1 = g_w / (1.0 + r)
    w2 = g_w * r / (1.0 + r)
    return lane_f, i1, i2, w1, w2


def _merge_body(ona_ref, omla_ref, g_ref, x_ref, ada_ref, wna_ref, wmla_ref, wout_ref, lng_ref, lnb_ref,
                wr_ref, br_ref, ox_ref, oh_ref, or_ref, ocnt_ref, carry_ref):
    first = (pl.program_id(0) == 0) & (pl.program_id(1) == 0)

    @pl.when(first)
    def _():
        carry_ref[...] = jnp.zeros_like(carry_ref)

    y_na = _dot(ona_ref[...], wna_ref[...])
    y_mla = _dot(omla_ref[...], wmla_ref[...])
    g_na = g_ref[:, :D_MODEL].astype(F32)
    g_mla = g_ref[:, D_MODEL:].astype(F32)
    y = _dot((g_na * y_na + g_mla * y_mla).astype(BF16), wout_ref[...])
    g1 = ada_ref[2:3, :]
    x1 = _ln_plain(DEEPNORM_ALPHA * x_ref[...] + g1 * y) * lng_ref[...] + lnb_ref[...]
    ox_ref[...] = x1
    h2 = _ln_plain(x1) * (1.0 + ada_ref[4:5, :]) + ada_ref[3:4, :]
    oh_ref[...] = h2
    logits = jnp.dot(h2, wr_ref[...], preferred_element_type=F32, precision=lax.Precision.HIGHEST) + br_ref[...]
    lane_f, i1, i2, w1, w2 = _route(logits)

    tm = logits.shape[0]
    sel1 = lane_f == i1
    sel2 = lane_f == i2
    used = (jnp.where(sel1, 1.0, 0.0) + jnp.where(sel2, 1.0, 0.0)).astype(BF16)
    row = lax.broadcasted_iota(jnp.int32, (tm, tm), 0)
    col = lax.broadcasted_iota(jnp.int32, (tm, tm), 1)
    earlier = jnp.where(row > col, 1.0, 0.0).astype(BF16)
    before = _dot(earlier, used) + carry_ref[0:1, :]
    rank1 = jnp.sum(jnp.where(sel1, before, 0.0), axis=-1, keepdims=True)
    rank2 = jnp.sum(jnp.where(sel2, before, 0.0), axis=-1, keepdims=True)
    carry_ref[...] += _dot(jnp.ones((8, tm), BF16), used)
    ocnt_ref[...] = carry_ref[...]

    fields = (i1 - R_EXP0, i2 - R_EXP0, w1, w2, rank1, rank2)
    route = jnp.zeros_like(logits)
    for k, val in enumerate(fields):
        route = jnp.where(lane_f == float(k), val, route)
    or_ref[...] = route


def _merge_call(o_na, o_mla, gates, x, ada, lw, tm, name):
    bsz, seq, _ = x.shape

    def tok(width):
        return pl.BlockSpec((None, tm, width), lambda b, j: (b, j, 0))

    def full(arr):
        return pl.BlockSpec(arr.shape, lambda b, j: (0,) * arr.ndim)

    ws = [lw["w_o_na"], lw["w_o_mla"], lw["w_out"], lw["ln1_g"], lw["ln1_b"], lw["w_r"], lw["b_r"]]
    return pl.pallas_call(
        _merge_body,
        grid=(bsz, seq // tm),
        in_specs=[tok(NA_WIDTH), tok(NA_WIDTH), tok(2 * D_MODEL), tok(D_MODEL),
                  pl.BlockSpec((None, 8, D_MODEL), lambda b, j: (b, 0, 0))] + [full(w) for w in ws],
        out_specs=[tok(D_MODEL), tok(D_MODEL), tok(LANES), pl.BlockSpec((8, LANES), lambda b, j: (0, 0))],
        out_shape=[jax.ShapeDtypeStruct((bsz, seq, D_MODEL), F32),
                   jax.ShapeDtypeStruct((bsz, seq, D_MODEL), F32),
                   jax.ShapeDtypeStruct((bsz, seq, LANES), F32),
                   jax.ShapeDtypeStruct((8, LANES), F32)],
        scratch_shapes=[pltpu.VMEM((8, LANES), F32)],
        compiler_params=_params(("arbitrary", "arbitrary")),
        name=name,
    )(o_na, o_mla, gates, x, ada, *ws)


POS_BITS = 16
POS_MASK = (1 << POS_BITS) - 1


def _plan(route, counts, n_tok, tr):
    r = route.reshape(n_tok, LANES)
    e1, e2 = r[:, 0].astype(jnp.int32), r[:, 1].astype(jnp.int32)
    rk1, rk2 = r[:, 4].astype(jnp.int32), r[:, 5].astype(jnp.int32)
    cnt = counts[0, R_EXP0:R_EXP0 + N_EXPERTS].astype(jnp.int32)
    tiles_e = (cnt + tr - 1) // tr
    tile_end = jnp.cumsum(tiles_e)
    row_off = (tile_end - tiles_e) * tr
    eid = jnp.arange(N_EXPERTS, dtype=jnp.int32)
    p1 = jnp.sum(jnp.where(e1[:, None] == eid, row_off, 0), axis=-1) + rk1
    p2 = jnp.sum(jnp.where(e2[:, None] == eid, row_off, 0), axis=-1) + rk2
    n_tiles = 2 * n_tok // tr + N_EXPERTS
    n_used = tile_end[-1]
    ti = jnp.arange(n_tiles, dtype=jnp.int32)
    tile_expert = jnp.sum(jnp.minimum(ti, n_used - 1)[:, None] >= tile_end[None, :], axis=-1).astype(jnp.int32)
    return p1 | (p2 << POS_BITS), tile_expert, jnp.stack([n_used, n_used]).astype(jnp.int32), n_tiles


def _expert_body(pos_ref, te_ref, meta_ref, h_hbm, wg_ref, wu_ref, wd_ref, o_ref,
                 src_ref, hbuf, sem, wg_b, wu_b, wd_b, *, n_tok, tr, n_rows):
    i = pl.program_id(0)
    n_used = meta_ref[0]

    def row_copy(tile, slot, r):
        t = src_ref[tile * tr + r]
        return pltpu.make_async_copy(h_hbm.at[pl.ds(t, 1)], hbuf.at[slot, pl.ds(r, 1)], sem.at[slot])

    def issue(tile, slot):
        def body(r, carry):
            row_copy(tile, slot, r).start()
            return carry
        lax.fori_loop(0, tr, body, 0, unroll=8)

    @pl.when(i == 0)
    def _():
        def clear(p, carry):
            src_ref[p] = 0
            return carry
        lax.fori_loop(0, n_rows, clear, 0, unroll=8)

        def place(t, carry):
            packed = pos_ref[t]
            src_ref[packed & POS_MASK] = t
            src_ref[packed >> POS_BITS] = t
            return carry
        lax.fori_loop(0, n_tok, place, 0, unroll=8)
        issue(0, 0)

    @pl.when(i + 1 < n_used)
    def _():
        issue(i + 1, (i + 1) % 2)

    @pl.when(i >= n_used)
    def _():
        o_ref[...] = jnp.zeros_like(o_ref)

    @pl.when(i < n_used)
    def _():
        slot = i % 2

        def wait(r, carry):
            row_copy(i, slot, r).wait()
            return carry
        lax.fori_loop(0, tr, wait, 0, unroll=8)

        @pl.when((i == 0) | (te_ref[i] != te_ref[jnp.maximum(i - 1, 0)]))
        def _():
            wg_b[...] = wg_ref[...].astype(BF16)
            wu_b[...] = wu_ref[...].astype(BF16)
            wd_b[...] = wd_ref[...].astype(BF16)

        h = hbuf[slot].astype(BF16)
        gate = _dot(h, wg_b[...])
        up = _dot(h, wu_b[...])
        hid = (gate * _sigmoid(gate) * up).astype(BF16)
        o_ref[...] = _dot(hid, wd_b[...])


def _expert_call(h2, plan, w_gate, w_up, w_down, layer, tr, name):
    packed, tile_expert, meta, n_tiles = plan
    n_tok = h2.shape[0]
    n_rows = n_tiles * tr
    return pl.pallas_call(
        functools.partial(_expert_body, n_tok=n_tok, tr=tr, n_rows=n_rows),
        grid_spec=pltpu.PrefetchScalarGridSpec(
            num_scalar_prefetch=3,
            grid=(n_tiles,),
            in_specs=[
                pl.BlockSpec(memory_space=pl.ANY),
                pl.BlockSpec((None, None, D_MODEL, EXPERT_FF), lambda i, pos, te, meta: (layer, te[i], 0, 0)),
                pl.BlockSpec((None, None, D_MODEL, EXPERT_FF), lambda i, pos, te, meta: (layer, te[i], 0, 0)),
                pl.BlockSpec((None, None, EXPERT_FF, D_MODEL), lambda i, pos, te, meta: (layer, te[i], 0, 0)),
            ],
            out_specs=pl.BlockSpec((tr, D_MODEL), lambda i, pos, te, meta: (i, 0)),
            scratch_shapes=[
                pltpu.SMEM((n_rows,), jnp.int32),
                pltpu.VMEM((2, tr, D_MODEL), F32),
                pltpu.SemaphoreType.DMA((2,)),
                pltpu.VMEM((D_MODEL, EXPERT_FF), BF16),
                pltpu.VMEM((D_MODEL, EXPERT_FF), BF16),
                pltpu.VMEM((EXPERT_FF, D_MODEL), BF16),
            ],
        ),
        out_shape=jax.ShapeDtypeStruct((n_rows, D_MODEL), F32),
        compiler_params=_params(("arbitrary",)),
        name=name,
    )(packed, tile_expert, meta, h2, w_gate, w_up, w_down)


def _combine_body(pos_ref, y_hbm, r_ref, x_ref, ada_ref, lng_ref, lnb_ref, o_ref, ybuf, sem, *, tm):
    i = pl.program_id(0)
    n = pl.num_programs(0)

    def row_copies(tile, slot, r):
        packed = pos_ref[tile * tm + r]
        return (pltpu.make_async_copy(y_hbm.at[pl.ds(packed & POS_MASK, 1)], ybuf.at[slot, 0, pl.ds(r, 1)],
                                      sem.at[slot]),
                pltpu.make_async_copy(y_hbm.at[pl.ds(packed >> POS_BITS, 1)], ybuf.at[slot, 1, pl.ds(r, 1)],
                                      sem.at[slot]))

    def issue(tile, slot):
        def body(r, carry):
            for cp in row_copies(tile, slot, r):
                cp.start()
            return carry
        lax.fori_loop(0, tm, body, 0, unroll=8)

    @pl.when(i == 0)
    def _():
        issue(0, 0)

    @pl.when(i + 1 < n)
    def _():
        issue(i + 1, (i + 1) % 2)

    slot = i % 2

    def wait(r, carry):
        for cp in row_copies(i, slot, r):
            cp.wait()
        return carry
    lax.fori_loop(0, tm, wait, 0, unroll=8)

    route = r_ref[...]
    lane = lax.broadcasted_iota(jnp.int32, route.shape, 1)
    w1 = jnp.sum(jnp.where(lane == 2, route, 0.0), axis=-1, keepdims=True)
    w2 = jnp.sum(jnp.where(lane == 3, route, 0.0), axis=-1, keepdims=True)
    y = w1 * ybuf[slot, 0] + w2 * ybuf[slot, 1]
    g2 = ada_ref[5:6, :]
    o_ref[...] = _ln_plain(DEEPNORM_ALPHA * x_ref[...] + g2 * y) * lng_ref[...] + lnb_ref[...]


def _combine_call(ys, packed, route, x1, ada, lw, seq, tm, name):
    n_tok = x1.shape[0]
    per_batch = seq // tm

    def tok(width):
        return pl.BlockSpec((tm, width), lambda i, pos: (i, 0))

    def full(arr):
        return pl.BlockSpec(arr.shape, lambda i, pos: (0,) * arr.ndim)

    return pl.pallas_call(
        functools.partial(_combine_body, tm=tm),
        grid_spec=pltpu.PrefetchScalarGridSpec(
            num_scalar_prefetch=1,
            grid=(n_tok // tm,),
            in_specs=[pl.BlockSpec(memory_space=pl.ANY), tok(LANES), tok(D_MODEL),
                      pl.BlockSpec((None, 8, D_MODEL), lambda i, pos: (i // per_batch, 0, 0)),
                      full(lw["ln2_g"]), full(lw["ln2_b"])],
            out_specs=tok(D_MODEL),
            scratch_shapes=[pltpu.VMEM((2, 2, tm, D_MODEL), F32), pltpu.SemaphoreType.DMA((2,))],
        ),
        out_shape=jax.ShapeDtypeStruct((n_tok, D_MODEL), F32),
        compiler_params=_params(("arbitrary",)),
        name=name,
    )(packed, ys, route, x1, ada, lw["ln2_g"], lw["ln2_b"])


def _moe(h2, route, counts, x1, ada, lw, experts, layer, tr, tm, name):
    bsz, seq, _ = x1.shape
    n_tok = bsz * seq
    plan = _plan(route, counts, n_tok, tr)
    ys = _expert_call(h2.reshape(n_tok, D_MODEL), plan, *experts, layer, tr, "expert_" + name)
    out = _combine_call(ys, plan[0], route.reshape(n_tok, LANES), x1.reshape(n_tok, D_MODEL), ada, lw,
                        seq, tm, "combine_" + name)
    return out.reshape(bsz, seq, D_MODEL)


def _placement(rows_used):
    e = np.zeros((LANES, MLA_PAD_WIDTH), np.float32)
    for r in range(rows_used):
        for hd in range(MLA_HEADS):
            e[r, hd * MLA_HEAD_PAD + MLA_NOPE_DIM + (r % MLA_ROPE_DIM)] = 1.0
    return jnp.asarray(e, BF16)


def _layer_weights(l, p):
    partner, _ = _rope_partner()
    w = p["w_in"][l]
    k_r = w[:, 2176:2208]
    w_in = jnp.concatenate(
        [w[:, 0:512] * NA_SCALE, w[:, 512:2176], k_r, k_r[:, partner],
         jnp.zeros((D_MODEL, LANES - 2 * MLA_ROPE_DIM), F32), w[:, 2208:]], axis=1).astype(BF16)
    uq = p["w_uq"][l].reshape(MLA_Q_LORA, MLA_HEADS, MLA_QK_DIM)
    pad = MLA_HEAD_PAD - MLA_QK_DIM
    w_uq = jnp.pad(uq, ((0, 0), (0, 0), (0, pad))).reshape(MLA_Q_LORA, MLA_PAD_WIDTH).astype(BF16)
    uq_sw = jnp.pad(uq[:, :, MLA_NOPE_DIM + partner], ((0, 0), (0, 0), (MLA_NOPE_DIM, pad)))
    w_uq_sw = uq_sw.reshape(MLA_Q_LORA, MLA_PAD_WIDTH).astype(BF16)
    uk = p["w_uk"][l].reshape(MLA_KV_LORA, MLA_HEADS, MLA_NOPE_DIM)
    w_uk = jnp.pad(uk, ((0, 0), (0, 0), (0, MLA_HEAD_PAD - MLA_NOPE_DIM))).reshape(
        MLA_KV_LORA, MLA_PAD_WIDTH).astype(BF16)
    w_r = jnp.concatenate([p["router_group_w"][l], p["router_expert_w"][l],
                           jnp.zeros((D_MODEL, LANES - N_GROUPS - N_EXPERTS), F32)], axis=1)
    b_r = jnp.concatenate([p["router_group_b"][l], p["router_expert_b"][l],
                           jnp.zeros((LANES - N_GROUPS - N_EXPERTS,), F32)])[None, :]
    return dict(
        w_in=w_in, w_uq=w_uq, w_uq_sw=w_uq_sw, w_uk=w_uk, w_uv=p["w_uv"][l].astype(BF16),
        q_norm=p["mla_q_norm"][l][None, :], kv_norm=p["mla_kv_norm"][l][None, :],
        e_plain=_placement(MLA_ROPE_DIM), e_rope=_placement(2 * MLA_ROPE_DIM),
        w_o_na=p["w_o_na"][l].astype(BF16), w_o_mla=p["w_o_mla"][l].astype(BF16),
        w_out=p["w_out"][l].astype(BF16),
        ln1_g=p["ln1_g"][l][None, :], ln1_b=p["ln1_b"][l][None, :],
        ln2_g=p["ln2_g"][l][None, :], ln2_b=p["ln2_b"][l][None, :],
        w_r=w_r, b_r=b_r,
    )


def _na_bias(rpb, rows):
    kr = min(NA_KR, rows)
    r = np.arange(rows)
    rs = np.clip(r - kr // 2, 0, rows - kr)
    vrow = (r[None, :] >= rs[:, None]) & (r[None, :] < rs[:, None] + kr)
    c = np.arange(GRID_W)
    cs = np.clip(c - NA_KC // 2, 0, GRID_W - NA_KC)
    vcol = (c[None, :] >= cs[:, None]) & (c[None, :] < cs[:, None] + NA_KC)
    coff = c[None, :] - c[:, None] + NA_KC - 1
    onehot = ((coff[None] == np.arange(2 * NA_KC - 1)[:, None, None]) & vcol[None]).astype(np.float32)
    blocks = jnp.einsum("hdj,jqk->hdqk", rpb, jnp.asarray(onehot), precision=lax.Precision.HIGHEST)
    blocks = jnp.where(jnp.asarray(vcol), blocks, NEG_INF)
    n_heads = rpb.shape[0]
    outside = jnp.full((n_heads, GRID_W, GRID_W), NEG_INF, F32)
    q_rows = []
    for qr in range(rows):
        q_rows.append(jnp.concatenate(
            [blocks[:, k - qr + NA_KR - 1] if vrow[qr, k] else outside for k in range(rows)], axis=-1))
    t = rows * GRID_W
    return jnp.stack(q_rows, axis=1).reshape(n_heads, t, t)


def kernel(x_prompt, x_sample, cache_na_k, cache_na_v, cache_mla_ckv, cache_mla_kr, c, c_ctx, w_in, mla_q_norm, mla_kv_norm, w_uq, w_uk, w_uv, na_rpb, w_o_na, w_o_mla, w_out, w_ada, b_ada, ln1_g, ln1_b, ln2_g, ln2_b, router_group_w, router_group_b, router_expert_w, router_expert_b, expert_w_gate, expert_w_up, expert_w_down):
    p = dict(w_in=w_in, mla_q_norm=mla_q_norm, mla_kv_norm=mla_kv_norm, w_uq=w_uq, w_uk=w_uk, w_uv=w_uv,
             w_o_na=w_o_na, w_o_mla=w_o_mla, w_out=w_out, ln1_g=ln1_g, ln1_b=ln1_b, ln2_g=ln2_g, ln2_b=ln2_b,
             router_group_w=router_group_w, router_group_b=router_group_b,
             router_expert_w=router_expert_w, router_expert_b=router_expert_b,
             expert_w_gate=expert_w_gate, expert_w_up=expert_w_up, expert_w_down=expert_w_down)
    pb, ps, _ = x_prompt.shape
    sb, ss, _ = x_sample.shape
    past = cache_na_k.shape[2]

    n_cond = 16
    cond = jnp.zeros((n_cond, D_MODEL), F32).at[0].set(c_ctx).at[1:1 + sb].set(c)
    ada_all = _ada_call(cond, w_ada, b_ada[:, None, :]).reshape(DEPTH, n_cond, 6, D_MODEL)
    ada_all = jnp.pad(ada_all, ((0, 0), (0, 0), (0, 2), (0, 0)))

    cos32, sin32 = _rope_tables(ss)
    head_c = jnp.concatenate([jnp.ones((ss, MLA_NOPE_DIM), F32), cos32,
                              jnp.zeros((ss, MLA_HEAD_PAD - MLA_QK_DIM), F32)], axis=-1)
    head_s = jnp.concatenate([jnp.zeros((ss, MLA_NOPE_DIM), F32), sin32,
                              jnp.zeros((ss, MLA_HEAD_PAD - MLA_QK_DIM), F32)], axis=-1)
    rope_tabs = (jnp.tile(head_c, (1, MLA_HEADS)) * MLA_SCALE, jnp.tile(head_s, (1, MLA_HEADS)) * MLA_SCALE,
                 jnp.concatenate([cos32, sin32, jnp.zeros((ss, LANES - 2 * MLA_ROPE_DIM), F32)], axis=-1))

    cna_k = cache_na_k.reshape(sb, DEPTH, past, NA_WIDTH)
    cna_v = cache_na_v.reshape(sb, DEPTH, past, NA_WIDTH)
    ckr_pad = jnp.pad(cache_mla_kr, ((0, 0), (0, 0), (0, 0), (0, LANES - MLA_ROPE_DIM)))

    experts = (expert_w_gate, expert_w_up, expert_w_down)
    xp, xs = x_prompt, x_sample
    new_k, new_v, new_ckv, new_kr = [], [], [], []
    for l in range(DEPTH):
        lw = _layer_weights(l, p)
        ada_p = jnp.broadcast_to(ada_all[l, 0], (pb, 8, D_MODEL))
        ada_s = ada_all[l, 1:1 + sb]

        qn, kn, vn, qm, km, vm, gates, ckv, k_r = _proj_call(xp, ada_p, lw, None, F32, True, ps)
        o_na = _attn_call(qn, [(kn, vn, _seg3(kn, 128), _seg3(vn, 128))], None, NA_HEAD_DIM, ps, "attn_ctx_na")
        o_mla = _attn_call(qm, [(km, vm, _seg3(km, 256), _seg3(vm, 128))], None, MLA_HEAD_PAD, ps, "attn_ctx_mla")
        x1, h2, route, counts = _merge_call(o_na, o_mla, gates, xp, ada_p, lw, ps, "merge_ctx")
        xp = _moe(h2, route, counts, x1, ada_p, lw, experts, l, 128, ps, "ctx")
        new_k.append(kn)
        new_v.append(vn)
        new_ckv.append(ckv)
        new_kr.append(k_r)

        qn, kn, vn, qm, km, vm, gates = _proj_call(xs, ada_s, lw, rope_tabs, BF16, False, 256)
        km_ctx, vm_ctx = _ctxkv_call(cache_mla_ckv, ckr_pad, l, lw)
        bias = _na_bias(na_rpb[l], ss // GRID_W)
        tq = 512
        o_na = _attn_call(
            qn, [(cna_k, cna_v, _seg4(cna_k, l, 128), _seg4(cna_v, l, 128)),
                 (kn, vn, _seg3(kn, 128), _seg3(vn, 128))], bias, NA_HEAD_DIM, tq, "attn_lat_na")
        o_mla = _attn_call(
            qm, [(km_ctx, vm_ctx, _seg3(km_ctx, 256), _seg3(vm_ctx, 128)),
                 (km, vm, _seg3(km, 256), _seg3(vm, 128))], None, MLA_HEAD_PAD, tq, "attn_lat_mla")
        x1, h2, route, counts = _merge_call(o_na, o_mla, gates, xs, ada_s, lw, 256, "merge_lat")
        xs = _moe(h2, route, counts, x1, ada_s, lw, experts, l, 256, 256, "lat")

    def stack(parts, tail):
        return jnp.stack(parts, axis=1).reshape((pb, DEPTH, ps) + tail)

    return (xp, xs, stack(new_k, (NA_HEADS, NA_HEAD_DIM)), stack(new_v, (NA_HEADS, NA_HEAD_DIM)),
            stack(new_ckv, (MLA_KV_LORA,)), stack(new_kr, (MLA_ROPE_DIM,)))
```

```python
import functools

import numpy as np
import jax
import jax.numpy as jnp
from jax import lax
from jax.experimental import pallas as pl
from jax.experimental.pallas import tpu as pltpu

F32 = jnp.float32
BF16 = jnp.bfloat16

D_MODEL = 1024
DEPTH = 2
GRID_W = 64
NA_HEADS = 8
NA_HEAD_DIM = 64
NA_KR = 8
NA_KC = 16
NA_WIDTH = NA_HEADS * NA_HEAD_DIM
NA_SCALE = NA_HEAD_DIM ** -0.5
MLA_HEADS = 8
MLA_NOPE_DIM = 64
MLA_ROPE_DIM = 32
MLA_V_DIM = 64
MLA_Q_LORA = 384
MLA_KV_LORA = 256
MLA_QK_DIM = MLA_NOPE_DIM + MLA_ROPE_DIM
MLA_SCALE = MLA_QK_DIM ** -0.5
ROPE_BASE = 10000.0
N_GROUPS = 4
EXPERTS_PER_GROUP = 8
N_EXPERTS = N_GROUPS * EXPERTS_PER_GROUP
EXPERT_FF = 256
LN_EPS = 1e-5
RMS_EPS = 1e-6
NEG_INF = -1e30
DEEPNORM_ALPHA = (2 * DEPTH) ** 0.25

LANES = 128
MLA_HEAD_PAD = LANES
MLA_PAD_WIDTH = MLA_HEADS * MLA_HEAD_PAD
C_Q = 0
C_K = C_Q + NA_WIDTH
C_V = C_K + NA_WIDTH
C_CQ = C_V + NA_WIDTH
C_CKV = C_CQ + MLA_Q_LORA
C_KR = C_CKV + MLA_KV_LORA
C_GATE = C_KR + LANES
C_END = C_GATE + 2 * D_MODEL
R_EXP0 = N_GROUPS

VMEM_LIMIT = 48 * 1024 * 1024


def _rope_partner():
    j = np.arange(MLA_ROPE_DIM)
    n_freq = MLA_ROPE_DIM // 4
    axis, within = j // (2 * n_freq), j % (2 * n_freq)
    half, f = within // n_freq, within % n_freq
    return axis * 2 * n_freq + (1 - half) * n_freq + f, half


def _rope_tables(n_tokens):
    n_freq = MLA_ROPE_DIM // 4
    inv_freq = ROPE_BASE ** (-jnp.arange(n_freq, dtype=F32) / n_freq)
    t = jnp.arange(n_tokens)
    row = (t // GRID_W).astype(F32)
    col = (t % GRID_W).astype(F32)
    ang = jnp.concatenate([row[:, None] * inv_freq, col[:, None] * inv_freq], axis=-1)
    cos, sin = jnp.cos(ang), jnp.sin(ang)
    cos32 = jnp.concatenate([cos[:, :n_freq], cos[:, :n_freq], cos[:, n_freq:], cos[:, n_freq:]], axis=-1)
    sin32 = jnp.concatenate([-sin[:, :n_freq], sin[:, :n_freq], -sin[:, n_freq:], sin[:, n_freq:]], axis=-1)
    return cos32, sin32


def _ln_plain(x):
    mu = jnp.mean(x, axis=-1, keepdims=True)
    xc = x - mu
    var = jnp.mean(xc * xc, axis=-1, keepdims=True)
    return xc * lax.rsqrt(var + LN_EPS)


def _rms(x, g):
    return x * lax.rsqrt(jnp.mean(x * x, axis=-1, keepdims=True) + RMS_EPS) * g


def _sigmoid(x):
    return 1.0 / (1.0 + jnp.exp(-x))


def _dot(a, b):
    return jnp.dot(a, b, preferred_element_type=F32)


def _params(sem):
    return pltpu.CompilerParams(dimension_semantics=sem, vmem_limit_bytes=VMEM_LIMIT)


TOKEN_TILE = (D_MODEL // LANES, LANES)


def _store_token_tiles(ref, val):
    for k in range(TOKEN_TILE[0]):
        ref[:, k, :] = val[:, k * LANES:(k + 1) * LANES]


def _load_token_tiles(ref):
    return jnp.concatenate([ref[:, k, :] for k in range(TOKEN_TILE[0])], axis=-1)


def _ada_body(c_ref, w_ref, b_ref, o_ref):
    c = c_ref[...]
    s = (c * _sigmoid(c)).astype(BF16)
    o_ref[0] = _dot(s, w_ref[0].astype(BF16)) + b_ref[0]


def _ada_call(cond, w_ada, b_ada):
    n_rows = cond.shape[0]
    tn = 1024
    return pl.pallas_call(
        _ada_body,
        grid=(DEPTH, 6 * D_MODEL // tn),
        in_specs=[
            pl.BlockSpec((n_rows, D_MODEL), lambda l, j: (0, 0)),
            pl.BlockSpec((1, D_MODEL, tn), lambda l, j: (l, 0, j)),
            pl.BlockSpec((1, 1, tn), lambda l, j: (l, 0, j)),
        ],
        out_specs=pl.BlockSpec((1, n_rows, tn), lambda l, j: (l, 0, j)),
        out_shape=jax.ShapeDtypeStruct((DEPTH, n_rows, 6 * D_MODEL), F32),
        compiler_params=_params(("arbitrary", "arbitrary")),
        name="ada",
    )(cond, w_ada, b_ada)


def _proj_body(*refs, rope, emit_cache):
    it = iter(refs)
    x_ref, ada_ref, w_ref, qn_ref, kvn_ref, wuq_ref = (next(it) for _ in range(6))
    wuqs_ref = next(it) if rope else None
    wuk_ref, wuv_ref, e_ref = next(it), next(it), next(it)
    if rope:
        cq_ref, sq_ref, tk_ref = next(it), next(it), next(it)
    oq_ref, ok_ref, ov_ref, oqm_ref, okm_ref, ovm_ref, og_ref = (next(it) for _ in range(7))
    if emit_cache:
        ockv_ref, okr_ref = next(it), next(it)

    x = x_ref[...]
    sh = ada_ref[0:1, :]
    sc = ada_ref[1:2, :]
    h = (_ln_plain(x) * (1.0 + sc) + sh).astype(BF16)

    oq_ref[...] = _dot(h, w_ref[:, C_Q:C_K]).astype(oq_ref.dtype)
    ok_ref[...] = _dot(h, w_ref[:, C_K:C_V]).astype(ok_ref.dtype)
    ov_ref[...] = _dot(h, w_ref[:, C_V:C_CQ]).astype(ov_ref.dtype)

    c_q = _rms(_dot(h, w_ref[:, C_CQ:C_CKV]), qn_ref[...]).astype(BF16)
    q = _dot(c_q, wuq_ref[...])
    if rope:
        q = q * cq_ref[...] + _dot(c_q, wuqs_ref[...]) * sq_ref[...]
    else:
        q = q * MLA_SCALE
    oqm_ref[...] = q.astype(BF16)

    c_kv = _rms(_dot(h, w_ref[:, C_CKV:C_KR]), kvn_ref[...])
    slab = _dot(h, w_ref[:, C_KR:C_GATE])
    if emit_cache:
        ockv_ref[...] = c_kv
        okr_ref[...] = slab[:, :MLA_ROPE_DIM]
    if rope:
        slab = slab * tk_ref[...]
    c_kv = c_kv.astype(BF16)
    okm_ref[...] = (_dot(c_kv, wuk_ref[...]) + _dot(slab.astype(BF16), e_ref[...])).astype(BF16)
    ovm_ref[...] = _dot(c_kv, wuv_ref[...]).astype(BF16)

    og_ref[...] = _sigmoid(_dot(h, w_ref[:, C_GATE:C_END])).astype(BF16)


def _proj_call(x, ada, lw, rope_tabs, kv_dtype, emit_cache, tm):
    bsz, seq, _ = x.shape
    nj = seq // tm
    rope = rope_tabs is not None

    def tok(width):
        return pl.BlockSpec((None, tm, width), lambda b, j: (b, j, 0))

    def full(arr):
        return pl.BlockSpec(arr.shape, lambda b, j: (0,) * arr.ndim)

    def tab(width):
        return pl.BlockSpec((tm, width), lambda b, j: (j, 0))

    args = [x, ada, lw["w_in"], lw["q_norm"], lw["kv_norm"], lw["w_uq"]]
    specs = [tok(D_MODEL), pl.BlockSpec((None, 8, D_MODEL), lambda b, j: (b, 0, 0)),
             full(lw["w_in"]), full(lw["q_norm"]), full(lw["kv_norm"]), full(lw["w_uq"])]
    if rope:
        args.append(lw["w_uq_sw"])
        specs.append(full(lw["w_uq_sw"]))
    e_mat = lw["e_rope"] if rope else lw["e_plain"]
    args += [lw["w_uk"], lw["w_uv"], e_mat]
    specs += [full(lw["w_uk"]), full(lw["w_uv"]), full(e_mat)]
    if rope:
        args += list(rope_tabs)
        specs += [tab(MLA_PAD_WIDTH), tab(MLA_PAD_WIDTH), tab(LANES)]

    def shp(width, dt):
        return jax.ShapeDtypeStruct((bsz, seq, width), dt)

    out_shape = [shp(NA_WIDTH, BF16), shp(NA_WIDTH, kv_dtype), shp(NA_WIDTH, kv_dtype),
                 shp(MLA_PAD_WIDTH, BF16), shp(MLA_PAD_WIDTH, BF16), shp(NA_WIDTH, BF16),
                 shp(2 * D_MODEL, BF16)]
    out_specs = [tok(NA_WIDTH), tok(NA_WIDTH), tok(NA_WIDTH), tok(MLA_PAD_WIDTH), tok(MLA_PAD_WIDTH),
                 tok(NA_WIDTH), tok(2 * D_MODEL)]
    if emit_cache:
        out_shape += [shp(MLA_KV_LORA, F32), shp(MLA_ROPE_DIM, F32)]
        out_specs += [tok(MLA_KV_LORA), tok(MLA_ROPE_DIM)]
    return pl.pallas_call(
        functools.partial(_proj_body, rope=rope, emit_cache=emit_cache),
        grid=(bsz, nj),
        in_specs=specs,
        out_specs=out_specs,
        out_shape=out_shape,
        compiler_params=_params(("arbitrary", "arbitrary")),
        name="proj_lat" if rope else "proj_ctx",
    )(*args)


def _ctxkv_body(ckv_ref, kr_ref, wuk_ref, wuv_ref, e_ref, okm_ref, ovm_ref):
    c_kv = ckv_ref[...].astype(BF16)
    okm_ref[...] = (_dot(c_kv, wuk_ref[...]) + _dot(kr_ref[...].astype(BF16), e_ref[...])).astype(BF16)
    ovm_ref[...] = _dot(c_kv, wuv_ref[...]).astype(BF16)


def _ctxkv_call(cache_ckv, cache_kr_pad, layer, lw):
    bsz, _, past, _ = cache_ckv.shape

    def full(arr):
        return pl.BlockSpec(arr.shape, lambda b: (0,) * arr.ndim)

    return pl.pallas_call(
        _ctxkv_body,
        grid=(bsz,),
        in_specs=[
            pl.BlockSpec((None, None, past, MLA_KV_LORA), lambda b: (b, layer, 0, 0)),
            pl.BlockSpec((None, None, past, LANES), lambda b: (b, layer, 0, 0)),
            full(lw["w_uk"]), full(lw["w_uv"]), full(lw["e_plain"]),
        ],
        out_specs=[pl.BlockSpec((None, past, MLA_PAD_WIDTH), lambda b: (b, 0, 0)),
                   pl.BlockSpec((None, past, NA_WIDTH), lambda b: (b, 0, 0))],
        out_shape=[jax.ShapeDtypeStruct((bsz, past, MLA_PAD_WIDTH), BF16),
                   jax.ShapeDtypeStruct((bsz, past, NA_WIDTH), BF16)],
        compiler_params=_params(("arbitrary",)),
        name="ctxkv",
    )(cache_ckv, cache_kr_pad, lw["w_uk"], lw["w_uv"], lw["e_plain"])


def _attn_body(*refs, n_seg, dq, has_bias, heads):
    q_ref = refs[0]
    seg_refs = [(refs[1 + 2 * i], refs[2 + 2 * i]) for i in range(n_seg)]
    bias_ref = refs[1 + 2 * n_seg] if has_bias else None
    o_ref = refs[-1]
    outs = []
    for hh in range(heads):
        qh = q_ref[:, hh * dq:(hh + 1) * dq]
        scores = []
        for i, (k_ref, _) in enumerate(seg_refs):
            kh = k_ref[:, hh * dq:(hh + 1) * dq].astype(BF16)
            s = lax.dot_general(qh, kh, (((1,), (1,)), ((), ())), preferred_element_type=F32)
            if has_bias and i == n_seg - 1:
                s = s + bias_ref[hh]
            scores.append(s)
        m = functools.reduce(jnp.maximum, [jnp.max(s, axis=-1, keepdims=True) for s in scores])
        den = None
        acc = None
        for s, (_, v_ref) in zip(scores, seg_refs):
            e = jnp.exp(s - m)
            vh = v_ref[:, hh * MLA_V_DIM:(hh + 1) * MLA_V_DIM].astype(BF16)
            d = jnp.sum(e, axis=-1, keepdims=True)
            a = _dot(e.astype(BF16), vh)
            den = d if den is None else den + d
            acc = a if acc is None else acc + a
        outs.append(acc / den)
    o_ref[...] = jnp.concatenate(outs, axis=-1).astype(o_ref.dtype)


def _attn_call(q, segs, bias, dq, tq, heads, name):
    bsz, seq, _ = q.shape
    args = [q]
    specs = [pl.BlockSpec((None, tq, heads * dq), lambda hp, qt, b: (b, qt, hp))]
    for k, v, ks, vs in segs:
        args += [k, v]
        specs += [ks, vs]
    if bias is not None:
        args.append(bias)
        specs.append(pl.BlockSpec((heads, tq, bias.shape[-1]), lambda hp, qt, b: (hp, qt, 0)))
    return pl.pallas_call(
        functools.partial(_attn_body, n_seg=len(segs), dq=dq, has_bias=bias is not None, heads=heads),
        grid=(NA_HEADS // heads, seq // tq, bsz),
        in_specs=specs,
        out_specs=pl.BlockSpec((None, tq, heads * MLA_V_DIM), lambda hp, qt, b: (b, qt, hp)),
        out_shape=jax.ShapeDtypeStruct((bsz, seq, NA_WIDTH), BF16),
        compiler_params=_params(("arbitrary", "arbitrary", "arbitrary")),
        name=name,
    )(*args)


def _seg3(arr, width):
    n = arr.shape[1]
    return pl.BlockSpec((None, n, width), lambda hp, qt, b: (b, 0, hp))


def _seg4(arr, layer, width):
    n = arr.shape[2]
    return pl.BlockSpec((None, None, n, width), lambda hp, qt, b: (b, layer, 0, hp))


def _route(logits):
    lane = lax.broadcasted_iota(jnp.int32, logits.shape, 1)
    lane_f = lane.astype(F32)
    far = float(LANES)
    is_g = lane < N_GROUPS
    gl = jnp.where(is_g, logits, NEG_INF)
    gmax = jnp.max(gl, axis=-1, keepdims=True)
    gsum = jnp.sum(jnp.exp(gl - gmax), axis=-1, keepdims=True)
    g_w = 1.0 / gsum
    g_idx = jnp.min(jnp.where(gl == gmax, lane_f, far), axis=-1, keepdims=True)
    e_group = ((lane - R_EXP0) >> 3).astype(F32)
    is_e = (lane >= R_EXP0) & (lane < R_EXP0 + N_EXPERTS) & (e_group == g_idx)
    el = jnp.where(is_e, logits, NEG_INF)
    m1 = jnp.max(el, axis=-1, keepdims=True)
    i1 = jnp.min(jnp.where(el == m1, lane_f, far), axis=-1, keepdims=True)
    el2 = jnp.where(lane_f == i1, NEG_INF, el)
    m2 = jnp.max(el2, axis=-1, keepdims=True)
    i2 = jnp.min(jnp.where(el2 == m2, lane_f, far), axis=-1, keepdims=True)
    r = jnp.exp(m2 - m1)
    w1 = g_w / (1.0 + r)
    w2 = g_w * r / (1.0 + r)
    return lane_f, i1, i2, w1, w2


def _merge_body(ona_ref, omla_ref, g_ref, x_ref, ada_ref, wna_ref, wmla_ref, wout_ref, lng_ref, lnb_ref,
                wr_ref, wrl_ref, br_ref, ox_ref, oh_ref, or_ref, ocnt_ref, carry_ref):
    first = (pl.program_id(0) == 0) & (pl.program_id(1) == 0)

    @pl.when(first)
    def _():
        carry_ref[...] = jnp.zeros_like(carry_ref)

    y_na = _dot(ona_ref[...], wna_ref[...])
    y_mla = _dot(omla_ref[...], wmla_ref[...])
    g_na = g_ref[:, :D_MODEL].astype(F32)
    g_mla = g_ref[:, D_MODEL:].astype(F32)
    y = _dot((g_na * y_na + g_mla * y_mla).astype(BF16), wout_ref[...])
    g1 = ada_ref[2:3, :]
    x1 = _ln_plain(DEEPNORM_ALPHA * x_ref[...] + g1 * y) * lng_ref[...] + lnb_ref[...]
    ox_ref[...] = x1
    h2 = _ln_plain(x1) * (1.0 + ada_ref[4:5, :]) + ada_ref[3:4, :]
    _store_token_tiles(oh_ref, h2)
    h_hi = h2.astype(BF16)
    h_lo = (h2 - h_hi.astype(F32)).astype(BF16)
    logits = (_dot(h_hi, wr_ref[...]) + (_dot(h_lo, wr_ref[...]) + _dot(h_hi, wrl_ref[...]))) + br_ref[...]
    lane_f, i1, i2, w1, w2 = _route(logits)

    tm = logits.shape[0]
    sel1 = lane_f == i1
    sel2 = lane_f == i2
    used = (jnp.where(sel1, 1.0, 0.0) + jnp.where(sel2, 1.0, 0.0)).astype(BF16)
    row = lax.broadcasted_iota(jnp.int32, (tm, tm), 0)
    col = lax.broadcasted_iota(jnp.int32, (tm, tm), 1)
    earlier = jnp.where(row > col, 1.0, 0.0).astype(BF16)
    before = _dot(earlier, used) + carry_ref[0:1, :]
    rank1 = jnp.sum(jnp.where(sel1, before, 0.0), axis=-1, keepdims=True)
    rank2 = jnp.sum(jnp.where(sel2, before, 0.0), axis=-1, keepdims=True)
    carry_ref[...] += _dot(jnp.ones((8, tm), BF16), used)
    ocnt_ref[...] = carry_ref[...]

    fields = (i1 - R_EXP0, i2 - R_EXP0, w1, w2, rank1, rank2)
    route = jnp.zeros_like(logits)
    for k, val in enumerate(fields):
        route = jnp.where(lane_f == float(k), val, route)
    or_ref[...] = route


def _merge_call(o_na, o_mla, gates, x, ada, lw, tm, name):
    bsz, seq, _ = x.shape

    def tok(width):
        return pl.BlockSpec((None, tm, width), lambda b, j: (b, j, 0))

    def full(arr):
        return pl.BlockSpec(arr.shape, lambda b, j: (0,) * arr.ndim)

    ws = [lw["w_o_na"], lw["w_o_mla"], lw["w_out"], lw["ln1_g"], lw["ln1_b"], lw["w_r"], lw["w_r_lo"], lw["b_r"]]
    return pl.pallas_call(
        _merge_body,
        grid=(bsz, seq // tm),
        in_specs=[tok(NA_WIDTH), tok(NA_WIDTH), tok(2 * D_MODEL), tok(D_MODEL),
                  pl.BlockSpec((None, 8, D_MODEL), lambda b, j: (b, 0, 0))] + [full(w) for w in ws],
        out_specs=[tok(D_MODEL), pl.BlockSpec((None, tm) + TOKEN_TILE, lambda b, j: (b, j, 0, 0)), tok(LANES),
                   pl.BlockSpec((8, LANES), lambda b, j: (0, 0))],
        out_shape=[jax.ShapeDtypeStruct((bsz, seq, D_MODEL), F32),
                   jax.ShapeDtypeStruct((bsz, seq) + TOKEN_TILE, F32),
                   jax.ShapeDtypeStruct((bsz, seq, LANES), F32),
                   jax.ShapeDtypeStruct((8, LANES), F32)],
        scratch_shapes=[pltpu.VMEM((8, LANES), F32)],
        compiler_params=_params(("arbitrary", "arbitrary")),
        name=name,
    )(o_na, o_mla, gates, x, ada, *ws)


POS_BITS = 16
POS_MASK = (1 << POS_BITS) - 1


def _plan(route, counts, n_tok, tr):
    r = route.reshape(n_tok, LANES)
    e1, e2 = r[:, 0].astype(jnp.int32), r[:, 1].astype(jnp.int32)
    rk1, rk2 = r[:, 4].astype(jnp.int32), r[:, 5].astype(jnp.int32)
    cnt = counts[0, R_EXP0:R_EXP0 + N_EXPERTS].astype(jnp.int32)
    tiles_e = (cnt + tr - 1) // tr
    tile_end = jnp.cumsum(tiles_e)
    row_off = (tile_end - tiles_e) * tr
    eid = jnp.arange(N_EXPERTS, dtype=jnp.int32)
    p1 = jnp.sum(jnp.where(e1[:, None] == eid, row_off, 0), axis=-1) + rk1
    p2 = jnp.sum(jnp.where(e2[:, None] == eid, row_off, 0), axis=-1) + rk2
    n_tiles = 2 * n_tok // tr + N_EXPERTS
    n_used = tile_end[-1]
    ti = jnp.arange(n_tiles, dtype=jnp.int32)
    tile_expert = jnp.sum(jnp.minimum(ti, n_used - 1)[:, None] >= tile_end[None, :], axis=-1).astype(jnp.int32)
    own = tile_expert[:, None] == eid
    seg_end = jnp.sum(jnp.where(own, row_off + cnt, 0), axis=-1)
    tile_rows = jnp.clip(seg_end - ti * tr, 1, tr).astype(jnp.int32)
    return p1 | (p2 << POS_BITS), tile_expert, tile_rows, jnp.stack([n_used, n_used]).astype(jnp.int32), n_tiles


ROW_GROUP = 8


def _expert_body(pos_ref, te_ref, rows_ref, meta_ref, h_hbm, wg_ref, wu_ref, wd_ref, o_ref,
                 src_ref, hbuf, sem, wg_b, wu_b, wd_b, *, n_tok, tr):
    i = pl.program_id(0)
    n_used = meta_ref[0]

    def row_copy(tile, slot, r):
        t = src_ref[tile * tr + jnp.minimum(r, rows_ref[tile] - 1)]
        return pltpu.make_async_copy(h_hbm.at[t], hbuf.at[slot, r], sem.at[slot])

    def for_row_groups(tile, fn):
        def body(g, carry):
            for k in range(ROW_GROUP):
                fn(g * ROW_GROUP + k)
            return carry
        lax.fori_loop(0, (rows_ref[tile] + ROW_GROUP - 1) // ROW_GROUP, body, 0)

    def issue(tile, slot):
        for_row_groups(tile, lambda r: row_copy(tile, slot, r).start())

    @pl.when(i == 0)
    def _():
        hbuf[...] = jnp.zeros_like(hbuf)

        def place(t, carry):
            packed = pos_ref[t]
            src_ref[packed & POS_MASK] = t
            src_ref[packed >> POS_BITS] = t
            return carry
        lax.fori_loop(0, n_tok, place, 0, unroll=8)
        issue(0, 0)

    @pl.when(i + 1 < n_used)
    def _():
        issue(i + 1, (i + 1) % 2)

    @pl.when(i >= n_used)
    def _():
        o_ref[...] = jnp.zeros_like(o_ref)

    @pl.when(i < n_used)
    def _():
        slot = i % 2
        for_row_groups(i, lambda r: row_copy(i, slot, r).wait())

        @pl.when((i == 0) | (te_ref[i] != te_ref[jnp.maximum(i - 1, 0)]))
        def _():
            wg_b[...] = wg_ref[...].astype(BF16)
            wu_b[...] = wu_ref[...].astype(BF16)
            wd_b[...] = wd_ref[...].astype(BF16)

        h = _load_token_tiles(hbuf.at[slot]).astype(BF16)
        gate = _dot(h, wg_b[...])
        up = _dot(h, wu_b[...])
        hid = (gate * _sigmoid(gate) * up).astype(BF16)
        _store_token_tiles(o_ref, _dot(hid, wd_b[...]))


def _expert_call(h2, plan, w_gate, w_up, w_down, layer, tr, name):
    packed, tile_expert, tile_rows, meta, n_tiles = plan
    n_tok = h2.shape[0]
    n_rows = n_tiles * tr
    return pl.pallas_call(
        functools.partial(_expert_body, n_tok=n_tok, tr=tr),
        grid_spec=pltpu.PrefetchScalarGridSpec(
            num_scalar_prefetch=4,
            grid=(n_tiles,),
            in_specs=[
                pl.BlockSpec(memory_space=pl.ANY),
                pl.BlockSpec((None, None, D_MODEL, EXPERT_FF), lambda i, pos, te, nr, meta: (layer, te[i], 0, 0)),
                pl.BlockSpec((None, None, D_MODEL, EXPERT_FF), lambda i, pos, te, nr, meta: (layer, te[i], 0, 0)),
                pl.BlockSpec((None, None, EXPERT_FF, D_MODEL), lambda i, pos, te, nr, meta: (layer, te[i], 0, 0)),
            ],
            out_specs=pl.BlockSpec((tr,) + TOKEN_TILE, lambda i, pos, te, nr, meta: (i, 0, 0)),
            scratch_shapes=[
                pltpu.SMEM((n_rows,), jnp.int32),
                pltpu.VMEM((2, tr) + TOKEN_TILE, F32),
                pltpu.SemaphoreType.DMA((2,)),
                pltpu.VMEM((D_MODEL, EXPERT_FF), BF16),
                pltpu.VMEM((D_MODEL, EXPERT_FF), BF16),
                pltpu.VMEM((EXPERT_FF, D_MODEL), BF16),
            ],
        ),
        out_shape=jax.ShapeDtypeStruct((n_rows,) + TOKEN_TILE, F32),
        compiler_params=_params(("arbitrary",)),
        name=name,
    )(packed, tile_expert, tile_rows, meta, h2, w_gate, w_up, w_down)


def _combine_body(pos_ref, y_hbm, r_ref, x_ref, ada_ref, lng_ref, lnb_ref, o_ref, ybuf, sem, *, tm):
    i = pl.program_id(0)
    n = pl.num_programs(0)

    def row_copies(tile, slot, r):
        packed = pos_ref[tile * tm + r]
        return (pltpu.make_async_copy(y_hbm.at[packed & POS_MASK], ybuf.at[slot, 0, r], sem.at[slot]),
                pltpu.make_async_copy(y_hbm.at[packed >> POS_BITS], ybuf.at[slot, 1, r], sem.at[slot]))

    def issue(tile, slot):
        def body(r, carry):
            for cp in row_copies(tile, slot, r):
                cp.start()
            return carry
        lax.fori_loop(0, tm, body, 0, unroll=8)

    @pl.when(i == 0)
    def _():
        issue(0, 0)

    @pl.when(i + 1 < n)
    def _():
        issue(i + 1, (i + 1) % 2)

    slot = i % 2

    def wait(r, carry):
        for cp in row_copies(i, slot, r):
            cp.wait()
        return carry
    lax.fori_loop(0, tm, wait, 0, unroll=8)

    route = r_ref[...]
    lane = lax.broadcasted_iota(jnp.int32, route.shape, 1)
    w1 = jnp.sum(jnp.where(lane == 2, route, 0.0), axis=-1, keepdims=True)
    w2 = jnp.sum(jnp.where(lane == 3, route, 0.0), axis=-1, keepdims=True)
    y = w1 * _load_token_tiles(ybuf.at[slot, 0]) + w2 * _load_token_tiles(ybuf.at[slot, 1])
    g2 = ada_ref[5:6, :]
    o_ref[...] = _ln_plain(DEEPNORM_ALPHA * x_ref[...] + g2 * y) * lng_ref[...] + lnb_ref[...]


def _combine_call(ys, packed, route, x1, ada, lw, seq, tm, name):
    n_tok = x1.shape[0]
    per_batch = seq // tm

    def tok(width):
        return pl.BlockSpec((tm, width), lambda i, pos: (i, 0))

    def full(arr):
        return pl.BlockSpec(arr.shape, lambda i, pos: (0,) * arr.ndim)

    return pl.pallas_call(
        functools.partial(_combine_body, tm=tm),
        grid_spec=pltpu.PrefetchScalarGridSpec(
            num_scalar_prefetch=1,
            grid=(n_tok // tm,),
            in_specs=[pl.BlockSpec(memory_space=pl.ANY), tok(LANES), tok(D_MODEL),
                      pl.BlockSpec((None, 8, D_MODEL), lambda i, pos: (i // per_batch, 0, 0)),
                      full(lw["ln2_g"]), full(lw["ln2_b"])],
            out_specs=tok(D_MODEL),
            scratch_shapes=[pltpu.VMEM((2, 2, tm) + TOKEN_TILE, F32), pltpu.SemaphoreType.DMA((2,))],
        ),
        out_shape=jax.ShapeDtypeStruct((n_tok, D_MODEL), F32),
        compiler_params=_params(("arbitrary",)),
        name=name,
    )(packed, ys, route, x1, ada, lw["ln2_g"], lw["ln2_b"])


def _moe(h2, route, counts, x1, ada, lw, experts, layer, tr, tm, name):
    bsz, seq, _ = x1.shape
    n_tok = bsz * seq
    plan = _plan(route, counts, n_tok, tr)
    ys = _expert_call(h2.reshape((n_tok,) + TOKEN_TILE), plan, *experts, layer, tr, "expert_" + name)
    out = _combine_call(ys, plan[0], route.reshape(n_tok, LANES), x1.reshape(n_tok, D_MODEL), ada, lw,
                        seq, tm, "combine_" + name)
    return out.reshape(bsz, seq, D_MODEL)


def _placement(rows_used):
    e = np.zeros((LANES, MLA_PAD_WIDTH), np.float32)
    for r in range(rows_used):
        for hd in range(MLA_HEADS):
            e[r, hd * MLA_HEAD_PAD + MLA_NOPE_DIM + (r % MLA_ROPE_DIM)] = 1.0
    return jnp.asarray(e, BF16)


def _layer_weights(l, p):
    partner, _ = _rope_partner()
    w = p["w_in"][l]
    k_r = w[:, 2176:2208]
    w_in = jnp.concatenate(
        [w[:, 0:512] * NA_SCALE, w[:, 512:2176], k_r, k_r[:, partner],
         jnp.zeros((D_MODEL, LANES - 2 * MLA_ROPE_DIM), F32), w[:, 2208:]], axis=1).astype(BF16)
    uq = p["w_uq"][l].reshape(MLA_Q_LORA, MLA_HEADS, MLA_QK_DIM)
    pad = MLA_HEAD_PAD - MLA_QK_DIM
    w_uq = jnp.pad(uq, ((0, 0), (0, 0), (0, pad))).reshape(MLA_Q_LORA, MLA_PAD_WIDTH).astype(BF16)
    uq_sw = jnp.pad(uq[:, :, MLA_NOPE_DIM + partner], ((0, 0), (0, 0), (MLA_NOPE_DIM, pad)))
    w_uq_sw = uq_sw.reshape(MLA_Q_LORA, MLA_PAD_WIDTH).astype(BF16)
    uk = p["w_uk"][l].reshape(MLA_KV_LORA, MLA_HEADS, MLA_NOPE_DIM)
    w_uk = jnp.pad(uk, ((0, 0), (0, 0), (0, MLA_HEAD_PAD - MLA_NOPE_DIM))).reshape(
        MLA_KV_LORA, MLA_PAD_WIDTH).astype(BF16)
    w_r = jnp.concatenate([p["router_group_w"][l], p["router_expert_w"][l],
                           jnp.zeros((D_MODEL, LANES - N_GROUPS - N_EXPERTS), F32)], axis=1)
    b_r = jnp.concatenate([p["router_group_b"][l], p["router_expert_b"][l],
                           jnp.zeros((LANES - N_GROUPS - N_EXPERTS,), F32)])[None, :]
    return dict(
        w_in=w_in, w_uq=w_uq, w_uq_sw=w_uq_sw, w_uk=w_uk, w_uv=p["w_uv"][l].astype(BF16),
        q_norm=p["mla_q_norm"][l][None, :], kv_norm=p["mla_kv_norm"][l][None, :],
        e_plain=_placement(MLA_ROPE_DIM), e_rope=_placement(2 * MLA_ROPE_DIM),
        w_o_na=p["w_o_na"][l].astype(BF16), w_o_mla=p["w_o_mla"][l].astype(BF16),
        w_out=p["w_out"][l].astype(BF16),
        ln1_g=p["ln1_g"][l][None, :], ln1_b=p["ln1_b"][l][None, :],
        ln2_g=p["ln2_g"][l][None, :], ln2_b=p["ln2_b"][l][None, :],
        w_r=w_r.astype(BF16), w_r_lo=(w_r - w_r.astype(BF16).astype(F32)).astype(BF16), b_r=b_r,
    )


def _na_bias(rpb, rows):
    kr = min(NA_KR, rows)
    r = np.arange(rows)
    rs = np.clip(r - kr // 2, 0, rows - kr)
    vrow = (r[None, :] >= rs[:, None]) & (r[None, :] < rs[:, None] + kr)
    c = np.arange(GRID_W)
    cs = np.clip(c - NA_KC // 2, 0, GRID_W - NA_KC)
    vcol = (c[None, :] >= cs[:, None]) & (c[None, :] < cs[:, None] + NA_KC)
    coff = c[None, :] - c[:, None] + NA_KC - 1
    onehot = ((coff[None] == np.arange(2 * NA_KC - 1)[:, None, None]) & vcol[None]).astype(np.float32)
    blocks = jnp.einsum("hdj,jqk->hdqk", rpb, jnp.asarray(onehot), precision=lax.Precision.HIGHEST)
    blocks = jnp.where(jnp.asarray(vcol), blocks, NEG_INF)
    n_heads = rpb.shape[0]
    outside = jnp.full((n_heads, GRID_W, GRID_W), NEG_INF, F32)
    q_rows = []
    for qr in range(rows):
        q_rows.append(jnp.concatenate(
            [blocks[:, k - qr + NA_KR - 1] if vrow[qr, k] else outside for k in range(rows)], axis=-1))
    t = rows * GRID_W
    return jnp.stack(q_rows, axis=1).reshape(n_heads, t, t)


def kernel(x_prompt, x_sample, cache_na_k, cache_na_v, cache_mla_ckv, cache_mla_kr, c, c_ctx, w_in, mla_q_norm, mla_kv_norm, w_uq, w_uk, w_uv, na_rpb, w_o_na, w_o_mla, w_out, w_ada, b_ada, ln1_g, ln1_b, ln2_g, ln2_b, router_group_w, router_group_b, router_expert_w, router_expert_b, expert_w_gate, expert_w_up, expert_w_down):
    p = dict(w_in=w_in, mla_q_norm=mla_q_norm, mla_kv_norm=mla_kv_norm, w_uq=w_uq, w_uk=w_uk, w_uv=w_uv,
             w_o_na=w_o_na, w_o_mla=w_o_mla, w_out=w_out, ln1_g=ln1_g, ln1_b=ln1_b, ln2_g=ln2_g, ln2_b=ln2_b,
             router_group_w=router_group_w, router_group_b=router_group_b,
             router_expert_w=router_expert_w, router_expert_b=router_expert_b,
             expert_w_gate=expert_w_gate, expert_w_up=expert_w_up, expert_w_down=expert_w_down)
    pb, ps, _ = x_prompt.shape
    sb, ss, _ = x_sample.shape
    past = cache_na_k.shape[2]

    n_cond = 16
    cond = jnp.zeros((n_cond, D_MODEL), F32).at[0].set(c_ctx).at[1:1 + sb].set(c)
    ada_all = _ada_call(cond, w_ada, b_ada[:, None, :]).reshape(DEPTH, n_cond, 6, D_MODEL)
    ada_all = jnp.pad(ada_all, ((0, 0), (0, 0), (0, 2), (0, 0)))

    cos32, sin32 = _rope_tables(ss)
    head_c = jnp.concatenate([jnp.ones((ss, MLA_NOPE_DIM), F32), cos32,
                              jnp.zeros((ss, MLA_HEAD_PAD - MLA_QK_DIM), F32)], axis=-1)
    head_s = jnp.concatenate([jnp.zeros((ss, MLA_NOPE_DIM), F32), sin32,
                              jnp.zeros((ss, MLA_HEAD_PAD - MLA_QK_DIM), F32)], axis=-1)
    rope_tabs = (jnp.tile(head_c, (1, MLA_HEADS)) * MLA_SCALE, jnp.tile(head_s, (1, MLA_HEADS)) * MLA_SCALE,
                 jnp.concatenate([cos32, sin32, jnp.zeros((ss, LANES - 2 * MLA_ROPE_DIM), F32)], axis=-1))

    cna_k = cache_na_k.reshape(sb, DEPTH, past, NA_WIDTH)
    cna_v = cache_na_v.reshape(sb, DEPTH, past, NA_WIDTH)
    ckr_pad = jnp.pad(cache_mla_kr, ((0, 0), (0, 0), (0, 0), (0, LANES - MLA_ROPE_DIM)))

    experts = (expert_w_gate, expert_w_up, expert_w_down)
    xp, xs = x_prompt, x_sample
    new_k, new_v, new_ckv, new_kr = [], [], [], []
    for l in range(DEPTH):
        lw = _layer_weights(l, p)
        ada_p = jnp.broadcast_to(ada_all[l, 0], (pb, 8, D_MODEL))
        ada_s = ada_all[l, 1:1 + sb]

        qn, kn, vn, qm, km, vm, gates, ckv, k_r = _proj_call(xp, ada_p, lw, None, F32, True, ps)
        o_na = _attn_call(qn, [(kn, vn, _seg3(kn, NA_WIDTH), _seg3(vn, NA_WIDTH))], None, NA_HEAD_DIM, ps,
                          NA_HEADS, "attn_ctx_na")
        o_mla = _attn_call(qm, [(km, vm, _seg3(km, MLA_PAD_WIDTH), _seg3(vm, NA_WIDTH))], None, MLA_HEAD_PAD, ps,
                           MLA_HEADS, "attn_ctx_mla")
        x1, h2, route, counts = _merge_call(o_na, o_mla, gates, xp, ada_p, lw, ps, "merge_ctx")
        xp = _moe(h2, route, counts, x1, ada_p, lw, experts, l, 128, ps, "ctx")
        new_k.append(kn)
        new_v.append(vn)
        new_ckv.append(ckv)
        new_kr.append(k_r)

        qn, kn, vn, qm, km, vm, gates = _proj_call(xs, ada_s, lw, rope_tabs, BF16, False, 256)
        km_ctx, vm_ctx = _ctxkv_call(cache_mla_ckv, ckr_pad, l, lw)
        bias = _na_bias(na_rpb[l], ss // GRID_W)
        tq = 512
        o_na = _attn_call(
            qn, [(cna_k, cna_v, _seg4(cna_k, l, 128), _seg4(cna_v, l, 128)),
                 (kn, vn, _seg3(kn, 128), _seg3(vn, 128))], bias, NA_HEAD_DIM, tq, 2, "attn_lat_na")
        o_mla = _attn_call(
            qm, [(km_ctx, vm_ctx, _seg3(km_ctx, 256), _seg3(vm_ctx, 128)),
                 (km, vm, _seg3(km, 256), _seg3(vm, 128))], None, MLA_HEAD_PAD, tq, 2, "attn_lat_mla")
        x1, h2, route, counts = _merge_call(o_na, o_mla, gates, xs, ada_s, lw, 256, "merge_lat")
        xs = _moe(h2, route, counts, x1, ada_s, lw, experts, l, 256, 256, "lat")

    def stack(parts, tail):
        return jnp.stack(parts, axis=1).reshape((pb, DEPTH, ps) + tail)

    return (xp, xs, stack(new_k, (NA_HEADS, NA_HEAD_DIM)), stack(new_v, (NA_HEADS, NA_HEAD_DIM)),
            stack(new_ckv, (MLA_KV_LORA,)), stack(new_kr, (MLA_ROPE_DIM,)))
```

```python
import functools

import numpy as np
import jax
import jax.numpy as jnp
from jax import lax
from jax.experimental import pallas as pl
from jax.experimental.pallas import tpu as pltpu

F32 = jnp.float32
BF16 = jnp.bfloat16

D_MODEL = 1024
DEPTH = 2
GRID_W = 64
NA_HEADS = 8
NA_HEAD_DIM = 64
NA_KR = 8
NA_KC = 16
NA_WIDTH = NA_HEADS * NA_HEAD_DIM
NA_SCALE = NA_HEAD_DIM ** -0.5
MLA_HEADS = 8
MLA_NOPE_DIM = 64
MLA_ROPE_DIM = 32
MLA_V_DIM = 64
MLA_Q_LORA = 384
MLA_KV_LORA = 256
MLA_QK_DIM = MLA_NOPE_DIM + MLA_ROPE_DIM
MLA_SCALE = MLA_QK_DIM ** -0.5
ROPE_BASE = 10000.0
N_GROUPS = 4
EXPERTS_PER_GROUP = 8
N_EXPERTS = N_GROUPS * EXPERTS_PER_GROUP
EXPERT_FF = 256
LN_EPS = 1e-5
RMS_EPS = 1e-6
NEG_INF = -1e30
DEEPNORM_ALPHA = (2 * DEPTH) ** 0.25

LANES = 128
MLA_HEAD_PAD = LANES
MLA_PAD_WIDTH = MLA_HEADS * MLA_HEAD_PAD
C_Q = 0
C_K = C_Q + NA_WIDTH
C_V = C_K + NA_WIDTH
C_CQ = C_V + NA_WIDTH
C_CKV = C_CQ + MLA_Q_LORA
C_KR = C_CKV + MLA_KV_LORA
C_GATE = C_KR + LANES
C_END = C_GATE + 2 * D_MODEL
R_EXP0 = N_GROUPS

VMEM_LIMIT = 48 * 1024 * 1024


def _rope_partner():
    j = np.arange(MLA_ROPE_DIM)
    n_freq = MLA_ROPE_DIM // 4
    axis, within = j // (2 * n_freq), j % (2 * n_freq)
    half, f = within // n_freq, within % n_freq
    return axis * 2 * n_freq + (1 - half) * n_freq + f, half


def _rope_tables(n_tokens):
    n_freq = MLA_ROPE_DIM // 4
    inv_freq = ROPE_BASE ** (-jnp.arange(n_freq, dtype=F32) / n_freq)
    t = jnp.arange(n_tokens)
    row = (t // GRID_W).astype(F32)
    col = (t % GRID_W).astype(F32)
    ang = jnp.concatenate([row[:, None] * inv_freq, col[:, None] * inv_freq], axis=-1)
    cos, sin = jnp.cos(ang), jnp.sin(ang)
    cos32 = jnp.concatenate([cos[:, :n_freq], cos[:, :n_freq], cos[:, n_freq:], cos[:, n_freq:]], axis=-1)
    sin32 = jnp.concatenate([-sin[:, :n_freq], sin[:, :n_freq], -sin[:, n_freq:], sin[:, n_freq:]], axis=-1)
    return cos32, sin32


def _ln_plain(x):
    mu = jnp.mean(x, axis=-1, keepdims=True)
    xc = x - mu
    var = jnp.mean(xc * xc, axis=-1, keepdims=True)
    return xc * lax.rsqrt(var + LN_EPS)


def _rms(x, g):
    return x * lax.rsqrt(jnp.mean(x * x, axis=-1, keepdims=True) + RMS_EPS) * g


def _sigmoid(x):
    return 1.0 / (1.0 + jnp.exp(-x))


def _dot(a, b):
    return jnp.dot(a, b, preferred_element_type=F32)


def _params(sem):
    return pltpu.CompilerParams(dimension_semantics=sem, vmem_limit_bytes=VMEM_LIMIT)


TOKEN_TILE = (D_MODEL // LANES, LANES)


def _store_token_tiles(ref, val):
    for k in range(TOKEN_TILE[0]):
        ref[:, k, :] = val[:, k * LANES:(k + 1) * LANES]


def _load_token_tiles(ref):
    return jnp.concatenate([ref[:, k, :] for k in range(TOKEN_TILE[0])], axis=-1)


def _ada_body(c_ref, w_ref, b_ref, o_ref):
    c = c_ref[...]
    s = (c * _sigmoid(c)).astype(BF16)
    o_ref[0] = _dot(s, w_ref[0].astype(BF16)) + b_ref[0]


def _ada_call(cond, w_ada, b_ada):
    n_rows = cond.shape[0]
    tn = 1024
    return pl.pallas_call(
        _ada_body,
        grid=(DEPTH, 6 * D_MODEL // tn),
        in_specs=[
            pl.BlockSpec((n_rows, D_MODEL), lambda l, j: (0, 0)),
            pl.BlockSpec((1, D_MODEL, tn), lambda l, j: (l, 0, j)),
            pl.BlockSpec((1, 1, tn), lambda l, j: (l, 0, j)),
        ],
        out_specs=pl.BlockSpec((1, n_rows, tn), lambda l, j: (l, 0, j)),
        out_shape=jax.ShapeDtypeStruct((DEPTH, n_rows, 6 * D_MODEL), F32),
        compiler_params=_params(("arbitrary", "arbitrary")),
        name="ada",
    )(cond, w_ada, b_ada)


def _proj_body(*refs, rope, emit_cache):
    it = iter(refs)
    x_ref, ada_ref, w_ref, qn_ref, kvn_ref, wuq_ref = (next(it) for _ in range(6))
    wuqs_ref = next(it) if rope else None
    wuk_ref, wuv_ref, e_ref = next(it), next(it), next(it)
    if rope:
        cq_ref, sq_ref, tk_ref = next(it), next(it), next(it)
    oq_ref, ok_ref, ov_ref, oqm_ref, okm_ref, ovm_ref, og_ref = (next(it) for _ in range(7))
    if emit_cache:
        ockv_ref, okr_ref = next(it), next(it)

    x = x_ref[...]
    sh = ada_ref[0:1, :]
    sc = ada_ref[1:2, :]
    h = (_ln_plain(x) * (1.0 + sc) + sh).astype(BF16)

    oq_ref[...] = _dot(h, w_ref[:, C_Q:C_K]).astype(oq_ref.dtype)
    ok_ref[...] = _dot(h, w_ref[:, C_K:C_V]).astype(ok_ref.dtype)
    ov_ref[...] = _dot(h, w_ref[:, C_V:C_CQ]).astype(ov_ref.dtype)

    c_q = _rms(_dot(h, w_ref[:, C_CQ:C_CKV]), qn_ref[...]).astype(BF16)
    q = _dot(c_q, wuq_ref[...])
    if rope:
        q = q * cq_ref[...] + _dot(c_q, wuqs_ref[...]) * sq_ref[...]
    else:
        q = q * MLA_SCALE
    oqm_ref[...] = q.astype(BF16)

    c_kv = _rms(_dot(h, w_ref[:, C_CKV:C_KR]), kvn_ref[...])
    slab = _dot(h, w_ref[:, C_KR:C_GATE])
    if emit_cache:
        ockv_ref[...] = c_kv
        okr_ref[...] = slab[:, :MLA_ROPE_DIM]
    if rope:
        slab = slab * tk_ref[...]
    c_kv = c_kv.astype(BF16)
    okm_ref[...] = (_dot(c_kv, wuk_ref[...]) + _dot(slab.astype(BF16), e_ref[...])).astype(BF16)
    ovm_ref[...] = _dot(c_kv, wuv_ref[...]).astype(BF16)

    og_ref[...] = _sigmoid(_dot(h, w_ref[:, C_GATE:C_END])).astype(BF16)


def _proj_call(x, ada, lw, rope_tabs, kv_dtype, emit_cache, tm):
    bsz, seq, _ = x.shape
    nj = seq // tm
    rope = rope_tabs is not None

    def tok(width):
        return pl.BlockSpec((None, tm, width), lambda b, j: (b, j, 0))

    def full(arr):
        return pl.BlockSpec(arr.shape, lambda b, j: (0,) * arr.ndim)

    def tab(width):
        return pl.BlockSpec((tm, width), lambda b, j: (j, 0))

    args = [x, ada, lw["w_in"], lw["q_norm"], lw["kv_norm"], lw["w_uq"]]
    specs = [tok(D_MODEL), pl.BlockSpec((None, 8, D_MODEL), lambda b, j: (b, 0, 0)),
             full(lw["w_in"]), full(lw["q_norm"]), full(lw["kv_norm"]), full(lw["w_uq"])]
    if rope:
        args.append(lw["w_uq_sw"])
        specs.append(full(lw["w_uq_sw"]))
    e_mat = lw["e_rope"] if rope else lw["e_plain"]
    args += [lw["w_uk"], lw["w_uv"], e_mat]
    specs += [full(lw["w_uk"]), full(lw["w_uv"]), full(e_mat)]
    if rope:
        args += list(rope_tabs)
        specs += [tab(MLA_PAD_WIDTH), tab(MLA_PAD_WIDTH), tab(LANES)]

    def shp(width, dt):
        return jax.ShapeDtypeStruct((bsz, seq, width), dt)

    out_shape = [shp(NA_WIDTH, BF16), shp(NA_WIDTH, kv_dtype), shp(NA_WIDTH, kv_dtype),
                 shp(MLA_PAD_WIDTH, BF16), shp(MLA_PAD_WIDTH, BF16), shp(NA_WIDTH, BF16),
                 shp(2 * D_MODEL, BF16)]
    out_specs = [tok(NA_WIDTH), tok(NA_WIDTH), tok(NA_WIDTH), tok(MLA_PAD_WIDTH), tok(MLA_PAD_WIDTH),
                 tok(NA_WIDTH), tok(2 * D_MODEL)]
    if emit_cache:
        out_shape += [shp(MLA_KV_LORA, F32), shp(MLA_ROPE_DIM, F32)]
        out_specs += [tok(MLA_KV_LORA), tok(MLA_ROPE_DIM)]
    return pl.pallas_call(
        functools.partial(_proj_body, rope=rope, emit_cache=emit_cache),
        grid=(bsz, nj),
        in_specs=specs,
        out_specs=out_specs,
        out_shape=out_shape,
        compiler_params=_params(("arbitrary", "arbitrary")),
        name="proj_lat" if rope else "proj_ctx",
    )(*args)


def _ctxkv_body(ckv_ref, kr_ref, wuk_ref, wuv_ref, e_ref, okm_ref, ovm_ref):
    c_kv = ckv_ref[...].astype(BF16)
    okm_ref[...] = (_dot(c_kv, wuk_ref[...]) + _dot(kr_ref[...].astype(BF16), e_ref[...])).astype(BF16)
    ovm_ref[...] = _dot(c_kv, wuv_ref[...]).astype(BF16)


def _ctxkv_call(cache_ckv, cache_kr_pad, layer, lw):
    bsz, _, past, _ = cache_ckv.shape

    def full(arr):
        return pl.BlockSpec(arr.shape, lambda b: (0,) * arr.ndim)

    return pl.pallas_call(
        _ctxkv_body,
        grid=(bsz,),
        in_specs=[
            pl.BlockSpec((None, None, past, MLA_KV_LORA), lambda b: (b, layer, 0, 0)),
            pl.BlockSpec((None, None, past, LANES), lambda b: (b, layer, 0, 0)),
            full(lw["w_uk"]), full(lw["w_uv"]), full(lw["e_plain"]),
        ],
        out_specs=[pl.BlockSpec((None, past, MLA_PAD_WIDTH), lambda b: (b, 0, 0)),
                   pl.BlockSpec((None, past, NA_WIDTH), lambda b: (b, 0, 0))],
        out_shape=[jax.ShapeDtypeStruct((bsz, past, MLA_PAD_WIDTH), BF16),
                   jax.ShapeDtypeStruct((bsz, past, NA_WIDTH), BF16)],
        compiler_params=_params(("arbitrary",)),
        name="ctxkv",
    )(cache_ckv, cache_kr_pad, lw["w_uk"], lw["w_uv"], lw["e_plain"])


def _local_bias(pb_ref, rm_ref, hh, q0, k0, q_rows, k_rows, grid_rows):
    rows = []
    for qr in range(q_rows):
        blocks = []
        for p in range(k_rows // 2):
            pair = k0 // 2 + p
            e = 2 * pair - (q0 + qr) + grid_rows
            blocks.append(pb_ref[hh, e] + rm_ref[pl.ds((q0 + qr) * (grid_rows // 2) + pair, 1), :])
        rows.append(jnp.concatenate(blocks, axis=-1))
    return jnp.concatenate(rows, axis=0)


def _attn_body(*refs, n_seg, dq, local, heads):
    q_ref = refs[0]
    seg_refs = [(refs[1 + 2 * i], refs[2 + 2 * i]) for i in range(n_seg)]
    o_ref = refs[-1]
    if local is not None:
        pb_ref, rm_ref = refs[1 + 2 * n_seg], refs[2 + 2 * n_seg]
        q_rows, k_rows, grid_rows = local
        q0 = pl.program_id(1) * q_rows
        k0 = jnp.minimum(jnp.clip(q0 - NA_KR // 2, 0, grid_rows - NA_KR), grid_rows - k_rows)
        keys = pl.ds(pl.multiple_of(k0 * GRID_W, GRID_W), k_rows * GRID_W)
    outs = []
    for hh in range(heads):
        qh = q_ref[:, hh * dq:(hh + 1) * dq]
        scores = []
        values = []
        for i, (k_ref, v_ref) in enumerate(seg_refs):
            windowed = local is not None and i == n_seg - 1
            rows = keys if windowed else slice(None)
            kh = k_ref[rows, hh * dq:(hh + 1) * dq].astype(BF16)
            values.append(v_ref[rows, hh * MLA_V_DIM:(hh + 1) * MLA_V_DIM].astype(BF16))
            s = lax.dot_general(qh, kh, (((1,), (1,)), ((), ())), preferred_element_type=F32)
            if windowed:
                s = s + _local_bias(pb_ref, rm_ref, hh, q0, k0, q_rows, k_rows, grid_rows)
            scores.append(s)
        m = functools.reduce(jnp.maximum, [jnp.max(s, axis=-1, keepdims=True) for s in scores])
        den = None
        acc = None
        for s, vh in zip(scores, values):
            e = jnp.exp(s - m)
            d = jnp.sum(e, axis=-1, keepdims=True)
            a = _dot(e.astype(BF16), vh)
            den = d if den is None else den + d
            acc = a if acc is None else acc + a
        outs.append(acc / den)
    o_ref[...] = jnp.concatenate(outs, axis=-1).astype(o_ref.dtype)


def _attn_call(q, segs, local_bias, dq, tq, heads, name):
    bsz, seq, _ = q.shape
    args = [q]
    specs = [pl.BlockSpec((None, tq, heads * dq), lambda hp, qt, b: (b, qt, hp))]
    for k, v, ks, vs in segs:
        args += [k, v]
        specs += [ks, vs]
    local = None
    if local_bias is not None:
        pair_blocks, row_masks, k_rows = local_bias
        args += [pair_blocks, row_masks]
        specs += [pl.BlockSpec((heads,) + pair_blocks.shape[1:], lambda hp, qt, b: (hp, 0, 0, 0)),
                  pl.BlockSpec(row_masks.shape, lambda hp, qt, b: (0, 0))]
        local = (tq // GRID_W, k_rows, seq // GRID_W)
    return pl.pallas_call(
        functools.partial(_attn_body, n_seg=len(segs), dq=dq, local=local, heads=heads),
        grid=(NA_HEADS // heads, seq // tq, bsz),
        in_specs=specs,
        out_specs=pl.BlockSpec((None, tq, heads * MLA_V_DIM), lambda hp, qt, b: (b, qt, hp)),
        out_shape=jax.ShapeDtypeStruct((bsz, seq, NA_WIDTH), BF16),
        compiler_params=_params(("arbitrary", "arbitrary", "arbitrary")),
        name=name,
    )(*args)


def _seg3(arr, width):
    n = arr.shape[1]
    return pl.BlockSpec((None, n, width), lambda hp, qt, b: (b, 0, hp))


def _seg4(arr, layer, width):
    n = arr.shape[2]
    return pl.BlockSpec((None, None, n, width), lambda hp, qt, b: (b, layer, 0, hp))


def _route(logits):
    lane = lax.broadcasted_iota(jnp.int32, logits.shape, 1)
    lane_f = lane.astype(F32)
    far = float(LANES)
    is_g = lane < N_GROUPS
    gl = jnp.where(is_g, logits, NEG_INF)
    gmax = jnp.max(gl, axis=-1, keepdims=True)
    gsum = jnp.sum(jnp.exp(gl - gmax), axis=-1, keepdims=True)
    g_w = 1.0 / gsum
    g_idx = jnp.min(jnp.where(gl == gmax, lane_f, far), axis=-1, keepdims=True)
    e_group = ((lane - R_EXP0) >> 3).astype(F32)
    is_e = (lane >= R_EXP0) & (lane < R_EXP0 + N_EXPERTS) & (e_group == g_idx)
    el = jnp.where(is_e, logits, NEG_INF)
    m1 = jnp.max(el, axis=-1, keepdims=True)
    i1 = jnp.min(jnp.where(el == m1, lane_f, far), axis=-1, keepdims=True)
    el2 = jnp.where(lane_f == i1, NEG_INF, el)
    m2 = jnp.max(el2, axis=-1, keepdims=True)
    i2 = jnp.min(jnp.where(el2 == m2, lane_f, far), axis=-1, keepdims=True)
    r = jnp.exp(m2 - m1)
    w1 = g_w / (1.0 + r)
    w2 = g_w * r / (1.0 + r)
    return lane_f, i1, i2, w1, w2


def _merge_body(ona_ref, omla_ref, g_ref, x_ref, ada_ref, wna_ref, wmla_ref, wout_ref, lng_ref, lnb_ref,
                wr_ref, wrl_ref, br_ref, ox_ref, oh_ref, or_ref, ocnt_ref, carry_ref):
    first = (pl.program_id(0) == 0) & (pl.program_id(1) == 0)

    @pl.when(first)
    def _():
        carry_ref[...] = jnp.zeros_like(carry_ref)

    y_na = _dot(ona_ref[...], wna_ref[...])
    y_mla = _dot(omla_ref[...], wmla_ref[...])
    g_na = g_ref[:, :D_MODEL].astype(F32)
    g_mla = g_ref[:, D_MODEL:].astype(F32)
    y = _dot((g_na * y_na + g_mla * y_mla).astype(BF16), wout_ref[...])
    g1 = ada_ref[2:3, :]
    x1 = _ln_plain(DEEPNORM_ALPHA * x_ref[...] + g1 * y) * lng_ref[...] + lnb_ref[...]
    ox_ref[...] = x1
    h2 = _ln_plain(x1) * (1.0 + ada_ref[4:5, :]) + ada_ref[3:4, :]
    _store_token_tiles(oh_ref, h2)
    h_hi = h2.astype(BF16)
    h_lo = (h2 - h_hi.astype(F32)).astype(BF16)
    logits = (_dot(h_hi, wr_ref[...]) + (_dot(h_lo, wr_ref[...]) + _dot(h_hi, wrl_ref[...]))) + br_ref[...]
    lane_f, i1, i2, w1, w2 = _route(logits)

    tm = logits.shape[0]
    sel1 = lane_f == i1
    sel2 = lane_f == i2
    used = (jnp.where(sel1, 1.0, 0.0) + jnp.where(sel2, 1.0, 0.0)).astype(BF16)
    row = lax.broadcasted_iota(jnp.int32, (tm, tm), 0)
    col = lax.broadcasted_iota(jnp.int32, (tm, tm), 1)
    earlier = jnp.where(row > col, 1.0, 0.0).astype(BF16)
    before = _dot(earlier, used) + carry_ref[0:1, :]
    rank1 = jnp.sum(jnp.where(sel1, before, 0.0), axis=-1, keepdims=True)
    rank2 = jnp.sum(jnp.where(sel2, before, 0.0), axis=-1, keepdims=True)
    carry_ref[...] += _dot(jnp.ones((8, tm), BF16), used)
    ocnt_ref[...] = carry_ref[...]

    fields = (i1 - R_EXP0, i2 - R_EXP0, w1, w2, rank1, rank2)
    route = jnp.zeros_like(logits)
    for k, val in enumerate(fields):
        route = jnp.where(lane_f == float(k), val, route)
    or_ref[...] = route


def _merge_call(o_na, o_mla, gates, x, ada, lw, tm, name):
    bsz, seq, _ = x.shape

    def tok(width):
        return pl.BlockSpec((None, tm, width), lambda b, j: (b, j, 0))

    def full(arr):
        return pl.BlockSpec(arr.shape, lambda b, j: (0,) * arr.ndim)

    ws = [lw["w_o_na"], lw["w_o_mla"], lw["w_out"], lw["ln1_g"], lw["ln1_b"], lw["w_r"], lw["w_r_lo"], lw["b_r"]]
    return pl.pallas_call(
        _merge_body,
        grid=(bsz, seq // tm),
        in_specs=[tok(NA_WIDTH), tok(NA_WIDTH), tok(2 * D_MODEL), tok(D_MODEL),
                  pl.BlockSpec((None, 8, D_MODEL), lambda b, j: (b, 0, 0))] + [full(w) for w in ws],
        out_specs=[tok(D_MODEL), pl.BlockSpec((None, tm) + TOKEN_TILE, lambda b, j: (b, j, 0, 0)), tok(LANES),
                   pl.BlockSpec((8, LANES), lambda b, j: (0, 0))],
        out_shape=[jax.ShapeDtypeStruct((bsz, seq, D_MODEL), F32),
                   jax.ShapeDtypeStruct((bsz, seq) + TOKEN_TILE, F32),
                   jax.ShapeDtypeStruct((bsz, seq, LANES), F32),
                   jax.ShapeDtypeStruct((8, LANES), F32)],
        scratch_shapes=[pltpu.VMEM((8, LANES), F32)],
        compiler_params=_params(("arbitrary", "arbitrary")),
        name=name,
    )(o_na, o_mla, gates, x, ada, *ws)


POS_BITS = 16
POS_MASK = (1 << POS_BITS) - 1


def _plan(route, counts, n_tok, tr):
    r = route.reshape(n_tok, LANES)
    e1, e2 = r[:, 0].astype(jnp.int32), r[:, 1].astype(jnp.int32)
    rk1, rk2 = r[:, 4].astype(jnp.int32), r[:, 5].astype(jnp.int32)
    cnt = counts[0, R_EXP0:R_EXP0 + N_EXPERTS].astype(jnp.int32)
    tiles_e = (cnt + tr - 1) // tr
    tile_end = jnp.cumsum(tiles_e)
    row_off = (tile_end - tiles_e) * tr
    eid = jnp.arange(N_EXPERTS, dtype=jnp.int32)
    p1 = jnp.sum(jnp.where(e1[:, None] == eid, row_off, 0), axis=-1) + rk1
    p2 = jnp.sum(jnp.where(e2[:, None] == eid, row_off, 0), axis=-1) + rk2
    n_tiles = 2 * n_tok // tr + N_EXPERTS
    n_used = tile_end[-1]
    ti = jnp.arange(n_tiles, dtype=jnp.int32)
    tile_expert = jnp.sum(jnp.minimum(ti, n_used - 1)[:, None] >= tile_end[None, :], axis=-1).astype(jnp.int32)
    own = tile_expert[:, None] == eid
    seg_end = jnp.sum(jnp.where(own, row_off + cnt, 0), axis=-1)
    tile_rows = jnp.clip(seg_end - ti * tr, 1, tr).astype(jnp.int32)
    return p1 | (p2 << POS_BITS), tile_expert, tile_rows, jnp.stack([n_used, n_used]).astype(jnp.int32), n_tiles


ROW_GROUP = 8


def _expert_body(pos_ref, te_ref, rows_ref, meta_ref, h_hbm, wg_ref, wu_ref, wd_ref, o_ref,
                 src_ref, hbuf, sem, wg_b, wu_b, wd_b, *, n_tok, tr):
    i = pl.program_id(0)
    n_used = meta_ref[0]

    def row_copy(tile, slot, r):
        t = src_ref[tile * tr + jnp.minimum(r, rows_ref[tile] - 1)]
        return pltpu.make_async_copy(h_hbm.at[t], hbuf.at[slot, r], sem.at[slot])

    def for_row_groups(tile, fn):
        def body(g, carry):
            for k in range(ROW_GROUP):
                fn(g * ROW_GROUP + k, k)
            return carry
        lax.fori_loop(0, (rows_ref[tile] + ROW_GROUP - 1) // ROW_GROUP, body, 0)

    def issue(tile, slot):
        for_row_groups(tile, lambda r, k: row_copy(tile, slot, r).start(priority=k % 2))

    @pl.when(i == 0)
    def _():
        hbuf[...] = jnp.zeros_like(hbuf)

        def place(t, carry):
            packed = pos_ref[t]
            src_ref[packed & POS_MASK] = t
            src_ref[packed >> POS_BITS] = t
            return carry
        lax.fori_loop(0, n_tok, place, 0, unroll=8)
        issue(0, 0)

    @pl.when(i + 1 < n_used)
    def _():
        issue(i + 1, (i + 1) % 2)

    @pl.when(i >= n_used)
    def _():
        o_ref[...] = jnp.zeros_like(o_ref)

    @pl.when(i < n_used)
    def _():
        slot = i % 2
        for_row_groups(i, lambda r, k: row_copy(i, slot, r).wait())

        @pl.when((i == 0) | (te_ref[i] != te_ref[jnp.maximum(i - 1, 0)]))
        def _():
            wg_b[...] = wg_ref[...].astype(BF16)
            wu_b[...] = wu_ref[...].astype(BF16)
            wd_b[...] = wd_ref[...].astype(BF16)

        h = _load_token_tiles(hbuf.at[slot]).astype(BF16)
        gate = _dot(h, wg_b[...])
        up = _dot(h, wu_b[...])
        hid = (gate * _sigmoid(gate) * up).astype(BF16)
        _store_token_tiles(o_ref, _dot(hid, wd_b[...]))


def _expert_call(h2, plan, w_gate, w_up, w_down, layer, tr, name):
    packed, tile_expert, tile_rows, meta, n_tiles = plan
    n_tok = h2.shape[0]
    n_rows = n_tiles * tr
    return pl.pallas_call(
        functools.partial(_expert_body, n_tok=n_tok, tr=tr),
        grid_spec=pltpu.PrefetchScalarGridSpec(
            num_scalar_prefetch=4,
            grid=(n_tiles,),
            in_specs=[
                pl.BlockSpec(memory_space=pl.ANY),
                pl.BlockSpec((None, None, D_MODEL, EXPERT_FF), lambda i, pos, te, nr, meta: (layer, te[i], 0, 0)),
                pl.BlockSpec((None, None, D_MODEL, EXPERT_FF), lambda i, pos, te, nr, meta: (layer, te[i], 0, 0)),
                pl.BlockSpec((None, None, EXPERT_FF, D_MODEL), lambda i, pos, te, nr, meta: (layer, te[i], 0, 0)),
            ],
            out_specs=pl.BlockSpec((tr,) + TOKEN_TILE, lambda i, pos, te, nr, meta: (i, 0, 0)),
            scratch_shapes=[
                pltpu.SMEM((n_rows,), jnp.int32),
                pltpu.VMEM((2, tr) + TOKEN_TILE, F32),
                pltpu.SemaphoreType.DMA((2,)),
                pltpu.VMEM((D_MODEL, EXPERT_FF), BF16),
                pltpu.VMEM((D_MODEL, EXPERT_FF), BF16),
                pltpu.VMEM((EXPERT_FF, D_MODEL), BF16),
            ],
        ),
        out_shape=jax.ShapeDtypeStruct((n_rows,) + TOKEN_TILE, F32),
        compiler_params=_params(("arbitrary",)),
        name=name,
    )(packed, tile_expert, tile_rows, meta, h2, w_gate, w_up, w_down)


def _combine_body(pos_ref, y_hbm, r_ref, x_ref, ada_ref, lng_ref, lnb_ref, o_ref, ybuf, sem, *, tm):
    i = pl.program_id(0)
    n = pl.num_programs(0)

    def row_copies(tile, slot, r):
        packed = pos_ref[tile * tm + r]
        return (pltpu.make_async_copy(y_hbm.at[packed & POS_MASK], ybuf.at[slot, 0, r], sem.at[slot]),
                pltpu.make_async_copy(y_hbm.at[packed >> POS_BITS], ybuf.at[slot, 1, r], sem.at[slot]))

    def issue(tile, slot):
        def body(r, carry):
            for prio, cp in enumerate(row_copies(tile, slot, r)):
                cp.start(priority=prio)
            return carry
        lax.fori_loop(0, tm, body, 0, unroll=8)

    @pl.when(i == 0)
    def _():
        issue(0, 0)

    @pl.when(i + 1 < n)
    def _():
        issue(i + 1, (i + 1) % 2)

    slot = i % 2

    def wait(r, carry):
        for cp in row_copies(i, slot, r):
            cp.wait()
        return carry
    lax.fori_loop(0, tm, wait, 0, unroll=8)

    route = r_ref[...]
    lane = lax.broadcasted_iota(jnp.int32, route.shape, 1)
    w1 = jnp.sum(jnp.where(lane == 2, route, 0.0), axis=-1, keepdims=True)
    w2 = jnp.sum(jnp.where(lane == 3, route, 0.0), axis=-1, keepdims=True)
    y = w1 * _load_token_tiles(ybuf.at[slot, 0]) + w2 * _load_token_tiles(ybuf.at[slot, 1])
    g2 = ada_ref[5:6, :]
    o_ref[...] = _ln_plain(DEEPNORM_ALPHA * x_ref[...] + g2 * y) * lng_ref[...] + lnb_ref[...]


def _combine_call(ys, packed, route, x1, ada, lw, seq, tm, name):
    n_tok = x1.shape[0]
    per_batch = seq // tm

    def tok(width):
        return pl.BlockSpec((tm, width), lambda i, pos: (i, 0))

    def full(arr):
        return pl.BlockSpec(arr.shape, lambda i, pos: (0,) * arr.ndim)

    return pl.pallas_call(
        functools.partial(_combine_body, tm=tm),
        grid_spec=pltpu.PrefetchScalarGridSpec(
            num_scalar_prefetch=1,
            grid=(n_tok // tm,),
            in_specs=[pl.BlockSpec(memory_space=pl.ANY), tok(LANES), tok(D_MODEL),
                      pl.BlockSpec((None, 8, D_MODEL), lambda i, pos: (i // per_batch, 0, 0)),
                      full(lw["ln2_g"]), full(lw["ln2_b"])],
            out_specs=tok(D_MODEL),
            scratch_shapes=[pltpu.VMEM((2, 2, tm) + TOKEN_TILE, F32), pltpu.SemaphoreType.DMA((2,))],
        ),
        out_shape=jax.ShapeDtypeStruct((n_tok, D_MODEL), F32),
        compiler_params=_params(("arbitrary",)),
        name=name,
    )(packed, ys, route, x1, ada, lw["ln2_g"], lw["ln2_b"])


def _moe(h2, route, counts, x1, ada, lw, experts, layer, tr, tm, name):
    bsz, seq, _ = x1.shape
    n_tok = bsz * seq
    plan = _plan(route, counts, n_tok, tr)
    ys = _expert_call(h2.reshape((n_tok,) + TOKEN_TILE), plan, *experts, layer, tr, "expert_" + name)
    out = _combine_call(ys, plan[0], route.reshape(n_tok, LANES), x1.reshape(n_tok, D_MODEL), ada, lw,
                        seq, tm, "combine_" + name)
    return out.reshape(bsz, seq, D_MODEL)


def _placement(rows_used):
    e = np.zeros((LANES, MLA_PAD_WIDTH), np.float32)
    for r in range(rows_used):
        for hd in range(MLA_HEADS):
            e[r, hd * MLA_HEAD_PAD + MLA_NOPE_DIM + (r % MLA_ROPE_DIM)] = 1.0
    return jnp.asarray(e, BF16)


def _layer_weights(l, p):
    partner, _ = _rope_partner()
    w = p["w_in"][l]
    k_r = w[:, 2176:2208]
    w_in = jnp.concatenate(
        [w[:, 0:512] * NA_SCALE, w[:, 512:2176], k_r, k_r[:, partner],
         jnp.zeros((D_MODEL, LANES - 2 * MLA_ROPE_DIM), F32), w[:, 2208:]], axis=1).astype(BF16)
    uq = p["w_uq"][l].reshape(MLA_Q_LORA, MLA_HEADS, MLA_QK_DIM)
    pad = MLA_HEAD_PAD - MLA_QK_DIM
    w_uq = jnp.pad(uq, ((0, 0), (0, 0), (0, pad))).reshape(MLA_Q_LORA, MLA_PAD_WIDTH).astype(BF16)
    uq_sw = jnp.pad(uq[:, :, MLA_NOPE_DIM + partner], ((0, 0), (0, 0), (MLA_NOPE_DIM, pad)))
    w_uq_sw = uq_sw.reshape(MLA_Q_LORA, MLA_PAD_WIDTH).astype(BF16)
    uk = p["w_uk"][l].reshape(MLA_KV_LORA, MLA_HEADS, MLA_NOPE_DIM)
    w_uk = jnp.pad(uk, ((0, 0), (0, 0), (0, MLA_HEAD_PAD - MLA_NOPE_DIM))).reshape(
        MLA_KV_LORA, MLA_PAD_WIDTH).astype(BF16)
    w_r = jnp.concatenate([p["router_group_w"][l], p["router_expert_w"][l],
                           jnp.zeros((D_MODEL, LANES - N_GROUPS - N_EXPERTS), F32)], axis=1)
    b_r = jnp.concatenate([p["router_group_b"][l], p["router_expert_b"][l],
                           jnp.zeros((LANES - N_GROUPS - N_EXPERTS,), F32)])[None, :]
    return dict(
        w_in=w_in, w_uq=w_uq, w_uq_sw=w_uq_sw, w_uk=w_uk, w_uv=p["w_uv"][l].astype(BF16),
        q_norm=p["mla_q_norm"][l][None, :], kv_norm=p["mla_kv_norm"][l][None, :],
        e_plain=_placement(MLA_ROPE_DIM), e_rope=_placement(2 * MLA_ROPE_DIM),
        w_o_na=p["w_o_na"][l].astype(BF16), w_o_mla=p["w_o_mla"][l].astype(BF16),
        w_out=p["w_out"][l].astype(BF16),
        ln1_g=p["ln1_g"][l][None, :], ln1_b=p["ln1_b"][l][None, :],
        ln2_g=p["ln2_g"][l][None, :], ln2_b=p["ln2_b"][l][None, :],
        w_r=w_r.astype(BF16), w_r_lo=(w_r - w_r.astype(BF16).astype(F32)).astype(BF16), b_r=b_r,
    )


def _window_rows(rows, q_rows):
    firsts = np.arange(0, rows, q_rows)
    lo = np.clip(firsts - NA_KR // 2, 0, rows - NA_KR)
    hi = np.clip(firsts + q_rows - 1 - NA_KR // 2, 0, rows - NA_KR) + NA_KR
    k_rows = min(rows, int(np.max(hi - lo)) + int(np.max(hi - lo)) % 2)
    k0 = np.minimum(lo, rows - k_rows)
    assert np.all(k0 % 2 == 0) and np.all(k0 + k_rows >= hi), (k0, k_rows)
    return k_rows, k0


def _na_bias_tables(rpb, rows):
    assert rows >= NA_KR and rows % 2 == 0
    r = np.arange(rows)
    rs = np.clip(r - NA_KR // 2, 0, rows - NA_KR)
    vrow = (r[None, :] >= rs[:, None]) & (r[None, :] < rs[:, None] + NA_KR)
    c = np.arange(GRID_W)
    cs = np.clip(c - NA_KC // 2, 0, GRID_W - NA_KC)
    vcol = (c[None, :] >= cs[:, None]) & (c[None, :] < cs[:, None] + NA_KC)
    coff = c[None, :] - c[:, None] + NA_KC - 1
    onehot = ((coff[None] == np.arange(2 * NA_KC - 1)[:, None, None]) & vcol[None]).astype(np.float32)
    blocks = jnp.einsum("hdj,jqk->hdqk", rpb, jnp.asarray(onehot), precision=lax.Precision.HIGHEST)
    blocks = jnp.where(jnp.asarray(vcol), blocks, NEG_INF)
    n_heads = rpb.shape[0]
    reach = NA_KR - 1
    by_offset = jnp.concatenate(
        [jnp.full((n_heads, rows - reach, GRID_W, GRID_W), NEG_INF, F32), blocks,
         jnp.full((n_heads, rows - reach - 1, GRID_W, GRID_W), NEG_INF, F32)], axis=1)
    pair_blocks = jnp.concatenate([by_offset[:, :-1], by_offset[:, 1:]], axis=-1)
    in_window = np.repeat(vrow.reshape(rows, rows // 2, 2), GRID_W, axis=-1)
    row_masks = np.where(in_window, 0.0, NEG_INF).astype(np.float32).reshape(rows * rows // 2, 2 * GRID_W)
    return pair_blocks, jnp.asarray(row_masks)


def kernel(x_prompt, x_sample, cache_na_k, cache_na_v, cache_mla_ckv, cache_mla_kr, c, c_ctx, w_in, mla_q_norm, mla_kv_norm, w_uq, w_uk, w_uv, na_rpb, w_o_na, w_o_mla, w_out, w_ada, b_ada, ln1_g, ln1_b, ln2_g, ln2_b, router_group_w, router_group_b, router_expert_w, router_expert_b, expert_w_gate, expert_w_up, expert_w_down):
    p = dict(w_in=w_in, mla_q_norm=mla_q_norm, mla_kv_norm=mla_kv_norm, w_uq=w_uq, w_uk=w_uk, w_uv=w_uv,
             w_o_na=w_o_na, w_o_mla=w_o_mla, w_out=w_out, ln1_g=ln1_g, ln1_b=ln1_b, ln2_g=ln2_g, ln2_b=ln2_b,
             router_group_w=router_group_w, router_group_b=router_group_b,
             router_expert_w=router_expert_w, router_expert_b=router_expert_b,
             expert_w_gate=expert_w_gate, expert_w_up=expert_w_up, expert_w_down=expert_w_down)
    pb, ps, _ = x_prompt.shape
    sb, ss, _ = x_sample.shape
    past = cache_na_k.shape[2]

    n_cond = 16
    cond = jnp.zeros((n_cond, D_MODEL), F32).at[0].set(c_ctx).at[1:1 + sb].set(c)
    ada_all = _ada_call(cond, w_ada, b_ada[:, None, :]).reshape(DEPTH, n_cond, 6, D_MODEL)
    ada_all = jnp.pad(ada_all, ((0, 0), (0, 0), (0, 2), (0, 0)))

    cos32, sin32 = _rope_tables(ss)
    head_c = jnp.concatenate([jnp.ones((ss, MLA_NOPE_DIM), F32), cos32,
                              jnp.zeros((ss, MLA_HEAD_PAD - MLA_QK_DIM), F32)], axis=-1)
    head_s = jnp.concatenate([jnp.zeros((ss, MLA_NOPE_DIM), F32), sin32,
                              jnp.zeros((ss, MLA_HEAD_PAD - MLA_QK_DIM), F32)], axis=-1)
    rope_tabs = (jnp.tile(head_c, (1, MLA_HEADS)) * MLA_SCALE, jnp.tile(head_s, (1, MLA_HEADS)) * MLA_SCALE,
                 jnp.concatenate([cos32, sin32, jnp.zeros((ss, LANES - 2 * MLA_ROPE_DIM), F32)], axis=-1))

    cna_k = cache_na_k.reshape(sb, DEPTH, past, NA_WIDTH)
    cna_v = cache_na_v.reshape(sb, DEPTH, past, NA_WIDTH)
    ckr_pad = jnp.pad(cache_mla_kr, ((0, 0), (0, 0), (0, 0), (0, LANES - MLA_ROPE_DIM)))

    experts = (expert_w_gate, expert_w_up, expert_w_down)
    xp, xs = x_prompt, x_sample
    new_k, new_v, new_ckv, new_kr = [], [], [], []
    for l in range(DEPTH):
        lw = _layer_weights(l, p)
        ada_p = jnp.broadcast_to(ada_all[l, 0], (pb, 8, D_MODEL))
        ada_s = ada_all[l, 1:1 + sb]

        qn, kn, vn, qm, km, vm, gates, ckv, k_r = _proj_call(xp, ada_p, lw, None, F32, True, ps)
        o_na = _attn_call(qn, [(kn, vn, _seg3(kn, NA_WIDTH), _seg3(vn, NA_WIDTH))], None, NA_HEAD_DIM, ps,
                          NA_HEADS, "attn_ctx_na")
        o_mla = _attn_call(qm, [(km, vm, _seg3(km, MLA_PAD_WIDTH), _seg3(vm, NA_WIDTH))], None, MLA_HEAD_PAD, ps,
                           MLA_HEADS, "attn_ctx_mla")
        x1, h2, route, counts = _merge_call(o_na, o_mla, gates, xp, ada_p, lw, ps, "merge_ctx")
        xp = _moe(h2, route, counts, x1, ada_p, lw, experts, l, 128, ps, "ctx")
        new_k.append(kn)
        new_v.append(vn)
        new_ckv.append(ckv)
        new_kr.append(k_r)

        qn, kn, vn, qm, km, vm, gates = _proj_call(xs, ada_s, lw, rope_tabs, BF16, False, 256)
        km_ctx, vm_ctx = _ctxkv_call(cache_mla_ckv, ckr_pad, l, lw)
        tq = 512
        grid_rows = ss // GRID_W
        local_bias = _na_bias_tables(na_rpb[l], grid_rows) + (_window_rows(grid_rows, tq // GRID_W)[0],)
        o_na = _attn_call(
            qn, [(cna_k, cna_v, _seg4(cna_k, l, 128), _seg4(cna_v, l, 128)),
                 (kn, vn, _seg3(kn, 128), _seg3(vn, 128))], local_bias, NA_HEAD_DIM, tq, 2, "attn_lat_na")
        o_mla = _attn_call(
            qm, [(km_ctx, vm_ctx, _seg3(km_ctx, 256), _seg3(vm_ctx, 128)),
                 (km, vm, _seg3(km, 256), _seg3(vm, 128))], None, MLA_HEAD_PAD, tq, 2, "attn_lat_mla")
        x1, h2, route, counts = _merge_call(o_na, o_mla, gates, xs, ada_s, lw, 256, "merge_lat")
        xs = _moe(h2, route, counts, x1, ada_s, lw, experts, l, 256, 256, "lat")

    def stack(parts, tail):
        return jnp.stack(parts, axis=1).reshape((pb, DEPTH, ps) + tail)

    return (xp, xs, stack(new_k, (NA_HEADS, NA_HEAD_DIM)), stack(new_v, (NA_HEADS, NA_HEAD_DIM)),
            stack(new_ckv, (MLA_KV_LORA,)), stack(new_kr, (MLA_ROPE_DIM,)))
```

```python
import functools

import numpy as np
import jax
import jax.numpy as jnp
from jax import lax
from jax.experimental import pallas as pl
from jax.experimental.pallas import tpu as pltpu

F32 = jnp.float32
BF16 = jnp.bfloat16

D_MODEL = 1024
DEPTH = 2
GRID_W = 64
NA_HEADS = 8
NA_HEAD_DIM = 64
NA_KR = 8
NA_KC = 16
NA_WIDTH = NA_HEADS * NA_HEAD_DIM
NA_SCALE = NA_HEAD_DIM ** -0.5
MLA_HEADS = 8
MLA_NOPE_DIM = 64
MLA_ROPE_DIM = 32
MLA_V_DIM = 64
MLA_Q_LORA = 384
MLA_KV_LORA = 256
MLA_QK_DIM = MLA_NOPE_DIM + MLA_ROPE_DIM
MLA_SCALE = MLA_QK_DIM ** -0.5
ROPE_BASE = 10000.0
N_GROUPS = 4
EXPERTS_PER_GROUP = 8
N_EXPERTS = N_GROUPS * EXPERTS_PER_GROUP
EXPERT_FF = 256
LN_EPS = 1e-5
RMS_EPS = 1e-6
NEG_INF = -1e30
DEEPNORM_ALPHA = (2 * DEPTH) ** 0.25

LANES = 128
MLA_HEAD_PAD = LANES
MLA_PAD_WIDTH = MLA_HEADS * MLA_HEAD_PAD
C_Q = 0
C_K = C_Q + NA_WIDTH
C_V = C_K + NA_WIDTH
C_CQ = C_V + NA_WIDTH
C_CKV = C_CQ + MLA_Q_LORA
C_KR = C_CKV + MLA_KV_LORA
C_GATE = C_KR + LANES
C_END = C_GATE + 2 * D_MODEL
R_EXP0 = N_GROUPS

VMEM_LIMIT = 48 * 1024 * 1024


def _rope_partner():
    j = np.arange(MLA_ROPE_DIM)
    n_freq = MLA_ROPE_DIM // 4
    axis, within = j // (2 * n_freq), j % (2 * n_freq)
    half, f = within // n_freq, within % n_freq
    return axis * 2 * n_freq + (1 - half) * n_freq + f, half


def _rope_tables(n_tokens):
    n_freq = MLA_ROPE_DIM // 4
    inv_freq = ROPE_BASE ** (-np.arange(n_freq, dtype=np.float64) / n_freq)
    t = np.arange(n_tokens)
    ang = np.concatenate([(t // GRID_W)[:, None] * inv_freq, (t % GRID_W)[:, None] * inv_freq], axis=-1)
    cos, sin = np.cos(ang), np.sin(ang)
    cos32 = np.concatenate([cos[:, :n_freq], cos[:, :n_freq], cos[:, n_freq:], cos[:, n_freq:]], axis=-1)
    sin32 = np.concatenate([-sin[:, :n_freq], sin[:, :n_freq], -sin[:, n_freq:], sin[:, n_freq:]], axis=-1)
    return cos32, sin32


def _ln_plain(x):
    mu = jnp.mean(x, axis=-1, keepdims=True)
    xc = x - mu
    var = jnp.mean(xc * xc, axis=-1, keepdims=True)
    return xc * lax.rsqrt(var + LN_EPS)


def _rms(x, g):
    return x * lax.rsqrt(jnp.mean(x * x, axis=-1, keepdims=True) + RMS_EPS) * g


def _sigmoid(x):
    return 1.0 / (1.0 + jnp.exp(-x))


def _dot(a, b):
    return jnp.dot(a, b, preferred_element_type=F32)


def _params(sem):
    return pltpu.CompilerParams(dimension_semantics=sem, vmem_limit_bytes=VMEM_LIMIT)


def _layer_spec(arr, layer):
    zeros = (0,) * (arr.ndim - 1)
    return pl.BlockSpec((None,) + arr.shape[1:], lambda *_: (layer,) + zeros)


def _ada_spec(ada, layer, batch_of):
    _, first, step = ada
    return pl.BlockSpec((None, None, 8, D_MODEL), lambda *idx: (layer, first + step * batch_of(*idx), 0, 0))


def _const_spec(arr):
    zeros = (0,) * arr.ndim
    return pl.BlockSpec(arr.shape, lambda *_: zeros)


TOKEN_TILE = (D_MODEL // LANES, LANES)


def _store_token_tiles(ref, val):
    for k in range(TOKEN_TILE[0]):
        ref[:, k, :] = val[:, k * LANES:(k + 1) * LANES]


def _load_token_tiles(ref):
    return jnp.concatenate([ref[:, k, :] for k in range(TOKEN_TILE[0])], axis=-1)


def _ada_body(c_ref, w_ref, b_ref, o_ref):
    c = c_ref[...]
    s = (c * _sigmoid(c)).astype(BF16)
    o_ref[0] = _dot(s, w_ref[0].astype(BF16)) + b_ref[0]


def _ada_call(cond, w_ada, b_ada):
    n_rows = cond.shape[0]
    tn = 1024
    return pl.pallas_call(
        _ada_body,
        grid=(DEPTH, 6 * D_MODEL // tn),
        in_specs=[
            pl.BlockSpec((n_rows, D_MODEL), lambda l, j: (0, 0)),
            pl.BlockSpec((1, D_MODEL, tn), lambda l, j: (l, 0, j)),
            pl.BlockSpec((1, 1, tn), lambda l, j: (l, 0, j)),
        ],
        out_specs=pl.BlockSpec((1, n_rows, tn), lambda l, j: (l, 0, j)),
        out_shape=jax.ShapeDtypeStruct((DEPTH, n_rows, 6 * D_MODEL), F32),
        compiler_params=_params(("arbitrary", "arbitrary")),
        name="ada",
    )(cond, w_ada, b_ada)


def _proj_body(*refs, rope, emit_cache):
    it = iter(refs)
    x_ref, ada_ref, w_ref, qn_ref, kvn_ref, wuq_ref = (next(it) for _ in range(6))
    wuqs_ref = next(it) if rope else None
    wuk_ref, wuv_ref, e_ref = next(it), next(it), next(it)
    if rope:
        cq_ref, sq_ref, tk_ref = next(it), next(it), next(it)
    oq_ref, ok_ref, ov_ref, oqm_ref, okm_ref, ovm_ref, og_ref = (next(it) for _ in range(7))
    if emit_cache:
        ockv_ref, okr_ref = next(it), next(it)

    x = x_ref[...]
    sh = ada_ref[0:1, :]
    sc = ada_ref[1:2, :]
    h = (_ln_plain(x) * (1.0 + sc) + sh).astype(BF16)

    oq_ref[...] = _dot(h, w_ref[:, C_Q:C_K]).astype(oq_ref.dtype)
    ok_ref[...] = _dot(h, w_ref[:, C_K:C_V]).astype(ok_ref.dtype)
    ov_ref[...] = _dot(h, w_ref[:, C_V:C_CQ]).astype(ov_ref.dtype)

    c_q = _rms(_dot(h, w_ref[:, C_CQ:C_CKV]), qn_ref[...]).astype(BF16)
    q = _dot(c_q, wuq_ref[...])
    if rope:
        q = q * cq_ref[...] + _dot(c_q, wuqs_ref[...]) * sq_ref[...]
    else:
        q = q * MLA_SCALE
    oqm_ref[...] = q.astype(BF16)

    c_kv = _rms(_dot(h, w_ref[:, C_CKV:C_KR]), kvn_ref[...])
    slab = _dot(h, w_ref[:, C_KR:C_GATE])
    if emit_cache:
        ockv_ref[...] = c_kv
        okr_ref[...] = slab[:, :MLA_ROPE_DIM]
    if rope:
        slab = slab * tk_ref[...]
    c_kv = c_kv.astype(BF16)
    okm_ref[...] = (_dot(c_kv, wuk_ref[...]) + _dot(slab.astype(BF16), e_ref[...])).astype(BF16)
    ovm_ref[...] = _dot(c_kv, wuv_ref[...]).astype(BF16)

    og_ref[...] = _sigmoid(_dot(h, w_ref[:, C_GATE:C_END])).astype(BF16)


def _proj_call(x, ada, lw, layer, rope_tabs, kv_dtype, emit_cache, tm):
    bsz, seq, _ = x.shape
    nj = seq // tm
    rope = rope_tabs is not None

    def tok(width):
        return pl.BlockSpec((None, tm, width), lambda b, j: (b, j, 0))

    def full(arr):
        return _layer_spec(arr, layer)

    def tab(width):
        return pl.BlockSpec((tm, width), lambda b, j: (j, 0))

    args = [x, ada[0], lw["w_in"], lw["q_norm"], lw["kv_norm"], lw["w_uq"]]
    specs = [tok(D_MODEL), _ada_spec(ada, layer, lambda b, j: b),
             full(lw["w_in"]), full(lw["q_norm"]), full(lw["kv_norm"]), full(lw["w_uq"])]
    if rope:
        args.append(lw["w_uq_sw"])
        specs.append(full(lw["w_uq_sw"]))
    e_mat = lw["e_rope"] if rope else lw["e_plain"]
    args += [lw["w_uk"], lw["w_uv"], e_mat]
    specs += [full(lw["w_uk"]), full(lw["w_uv"]), _const_spec(e_mat)]
    if rope:
        args += list(rope_tabs)
        specs += [tab(MLA_PAD_WIDTH), tab(MLA_PAD_WIDTH), tab(LANES)]

    def shp(width, dt):
        return jax.ShapeDtypeStruct((bsz, seq, width), dt)

    out_shape = [shp(NA_WIDTH, BF16), shp(NA_WIDTH, kv_dtype), shp(NA_WIDTH, kv_dtype),
                 shp(MLA_PAD_WIDTH, BF16), shp(MLA_PAD_WIDTH, BF16), shp(NA_WIDTH, BF16),
                 shp(2 * D_MODEL, BF16)]
    out_specs = [tok(NA_WIDTH), tok(NA_WIDTH), tok(NA_WIDTH), tok(MLA_PAD_WIDTH), tok(MLA_PAD_WIDTH),
                 tok(NA_WIDTH), tok(2 * D_MODEL)]
    if emit_cache:
        out_shape += [shp(MLA_KV_LORA, F32), shp(MLA_ROPE_DIM, F32)]
        out_specs += [tok(MLA_KV_LORA), tok(MLA_ROPE_DIM)]
    return pl.pallas_call(
        functools.partial(_proj_body, rope=rope, emit_cache=emit_cache),
        grid=(bsz, nj),
        in_specs=specs,
        out_specs=out_specs,
        out_shape=out_shape,
        compiler_params=_params(("arbitrary", "arbitrary")),
        name="proj_lat" if rope else "proj_ctx",
    )(*args)


def _ctxkv_body(ckv_ref, kr_ref, wuk_ref, wuv_ref, e_ref, okm_ref, ovm_ref):
    c_kv = ckv_ref[...].astype(BF16)
    okm_ref[...] = (_dot(c_kv, wuk_ref[...]) + _dot(kr_ref[...].astype(BF16), e_ref[...])).astype(BF16)
    ovm_ref[...] = _dot(c_kv, wuv_ref[...]).astype(BF16)


def _ctxkv_call(cache_ckv, cache_kr_pad, layer, lw):
    bsz, _, past, _ = cache_ckv.shape

    def full(arr):
        return _layer_spec(arr, layer)

    return pl.pallas_call(
        _ctxkv_body,
        grid=(bsz,),
        in_specs=[
            pl.BlockSpec((None, None, past, MLA_KV_LORA), lambda b: (b, layer, 0, 0)),
            pl.BlockSpec((None, None, past, LANES), lambda b: (b, layer, 0, 0)),
            full(lw["w_uk"]), full(lw["w_uv"]), _const_spec(lw["e_plain"]),
        ],
        out_specs=[pl.BlockSpec((None, past, MLA_PAD_WIDTH), lambda b: (b, 0, 0)),
                   pl.BlockSpec((None, past, NA_WIDTH), lambda b: (b, 0, 0))],
        out_shape=[jax.ShapeDtypeStruct((bsz, past, MLA_PAD_WIDTH), BF16),
                   jax.ShapeDtypeStruct((bsz, past, NA_WIDTH), BF16)],
        compiler_params=_params(("arbitrary",)),
        name="ctxkv",
    )(cache_ckv, cache_kr_pad, lw["w_uk"], lw["w_uv"], lw["e_plain"])


def _local_bias(pb_ref, rm_ref, hh, q0, k0, q_rows, k_rows, grid_rows):
    rows = []
    for qr in range(q_rows):
        blocks = []
        for p in range(k_rows // 2):
            pair = k0 // 2 + p
            e = 2 * pair - (q0 + qr) + grid_rows
            blocks.append(pb_ref[hh, e] + rm_ref[pl.ds((q0 + qr) * (grid_rows // 2) + pair, 1), :])
        rows.append(jnp.concatenate(blocks, axis=-1))
    return jnp.concatenate(rows, axis=0)


def _attn_body(*refs, n_seg, dq, local, heads):
    q_ref = refs[0]
    seg_refs = [(refs[1 + 2 * i], refs[2 + 2 * i]) for i in range(n_seg)]
    o_ref = refs[-1]
    if local is not None:
        pb_ref, rm_ref = refs[1 + 2 * n_seg], refs[2 + 2 * n_seg]
        q_rows, k_rows, grid_rows = local
        q0 = pl.program_id(1) * q_rows
        k0 = jnp.minimum(jnp.clip(q0 - NA_KR // 2, 0, grid_rows - NA_KR), grid_rows - k_rows)
        keys = pl.ds(pl.multiple_of(k0 * GRID_W, GRID_W), k_rows * GRID_W)
    outs = []
    for hh in range(heads):
        qh = q_ref[:, hh * dq:(hh + 1) * dq]
        scores = []
        values = []
        for i, (k_ref, v_ref) in enumerate(seg_refs):
            windowed = local is not None and i == n_seg - 1
            rows = keys if windowed else slice(None)
            kh = k_ref[rows, hh * dq:(hh + 1) * dq].astype(BF16)
            values.append(v_ref[rows, hh * MLA_V_DIM:(hh + 1) * MLA_V_DIM].astype(BF16))
            s = lax.dot_general(qh, kh, (((1,), (1,)), ((), ())), preferred_element_type=F32)
            if windowed:
                s = s + _local_bias(pb_ref, rm_ref, hh, q0, k0, q_rows, k_rows, grid_rows)
            scores.append(s)
        m = functools.reduce(jnp.maximum, [jnp.max(s, axis=-1, keepdims=True) for s in scores])
        den = None
        acc = None
        for s, vh in zip(scores, values):
            e = jnp.exp(s - m)
            d = jnp.sum(e, axis=-1, keepdims=True)
            a = _dot(e.astype(BF16), vh)
            den = d if den is None else den + d
            acc = a if acc is None else acc + a
        outs.append(acc / den)
    o_ref[...] = jnp.concatenate(outs, axis=-1).astype(o_ref.dtype)


def _attn_call(q, segs, local_bias, dq, tq, heads, name, layer=0):
    bsz, seq, _ = q.shape
    args = [q]
    specs = [pl.BlockSpec((None, tq, heads * dq), lambda hp, qt, b: (b, qt, hp))]
    for k, v, ks, vs in segs:
        args += [k, v]
        specs += [ks, vs]
    local = None
    if local_bias is not None:
        pair_blocks, row_masks, k_rows = local_bias
        args += [pair_blocks, row_masks]
        specs += [pl.BlockSpec((None, heads) + pair_blocks.shape[2:], lambda hp, qt, b: (layer, hp, 0, 0, 0)),
                  _const_spec(row_masks)]
        local = (tq // GRID_W, k_rows, seq // GRID_W)
    return pl.pallas_call(
        functools.partial(_attn_body, n_seg=len(segs), dq=dq, local=local, heads=heads),
        grid=(NA_HEADS // heads, seq // tq, bsz),
        in_specs=specs,
        out_specs=pl.BlockSpec((None, tq, heads * MLA_V_DIM), lambda hp, qt, b: (b, qt, hp)),
        out_shape=jax.ShapeDtypeStruct((bsz, seq, NA_WIDTH), BF16),
        compiler_params=_params(("arbitrary", "arbitrary", "arbitrary")),
        name=name,
    )(*args)


def _seg3(arr, width):
    n = arr.shape[1]
    return pl.BlockSpec((None, n, width), lambda hp, qt, b: (b, 0, hp))


def _seg4(arr, layer, width):
    n = arr.shape[2]
    return pl.BlockSpec((None, None, n, width), lambda hp, qt, b: (b, layer, 0, hp))


def _route(logits):
    lane = lax.broadcasted_iota(jnp.int32, logits.shape, 1)
    lane_f = lane.astype(F32)
    far = float(LANES)
    is_g = lane < N_GROUPS
    gl = jnp.where(is_g, logits, NEG_INF)
    gmax = jnp.max(gl, axis=-1, keepdims=True)
    gsum = jnp.sum(jnp.exp(gl - gmax), axis=-1, keepdims=True)
    g_w = 1.0 / gsum
    g_idx = jnp.min(jnp.where(gl == gmax, lane_f, far), axis=-1, keepdims=True)
    e_group = ((lane - R_EXP0) >> 3).astype(F32)
    is_e = (lane >= R_EXP0) & (lane < R_EXP0 + N_EXPERTS) & (e_group == g_idx)
    el = jnp.where(is_e, logits, NEG_INF)
    m1 = jnp.max(el, axis=-1, keepdims=True)
    i1 = jnp.min(jnp.where(el == m1, lane_f, far), axis=-1, keepdims=True)
    el2 = jnp.where(lane_f == i1, NEG_INF, el)
    m2 = jnp.max(el2, axis=-1, keepdims=True)
    i2 = jnp.min(jnp.where(el2 == m2, lane_f, far), axis=-1, keepdims=True)
    r = jnp.exp(m2 - m1)
    w1 = g_w / (1.0 + r)
    w2 = g_w * r / (1.0 + r)
    return lane_f, i1, i2, w1, w2


def _merge_body(ona_ref, omla_ref, g_ref, x_ref, ada_ref, wna_ref, wmla_ref, wout_ref, lng_ref, lnb_ref,
                wr_ref, wrl_ref, br_ref, ox_ref, oh_ref, or_ref, ocnt_ref, carry_ref):
    first = (pl.program_id(0) == 0) & (pl.program_id(1) == 0)

    @pl.when(first)
    def _():
        carry_ref[...] = jnp.zeros_like(carry_ref)

    y_na = _dot(ona_ref[...], wna_ref[...])
    y_mla = _dot(omla_ref[...], wmla_ref[...])
    g_na = g_ref[:, :D_MODEL].astype(F32)
    g_mla = g_ref[:, D_MODEL:].astype(F32)
    y = _dot((g_na * y_na + g_mla * y_mla).astype(BF16), wout_ref[...])
    g1 = ada_ref[2:3, :]
    x1 = _ln_plain(DEEPNORM_ALPHA * x_ref[...] + g1 * y) * lng_ref[...] + lnb_ref[...]
    ox_ref[...] = x1
    h2 = _ln_plain(x1) * (1.0 + ada_ref[4:5, :]) + ada_ref[3:4, :]
    _store_token_tiles(oh_ref, h2)
    h_hi = h2.astype(BF16)
    h_lo = (h2 - h_hi.astype(F32)).astype(BF16)
    logits = (_dot(h_hi, wr_ref[...]) + (_dot(h_lo, wr_ref[...]) + _dot(h_hi, wrl_ref[...]))) + br_ref[...]
    lane_f, i1, i2, w1, w2 = _route(logits)

    tm = logits.shape[0]
    sel1 = lane_f == i1
    sel2 = lane_f == i2
    used = (jnp.where(sel1, 1.0, 0.0) + jnp.where(sel2, 1.0, 0.0)).astype(BF16)
    row = lax.broadcasted_iota(jnp.int32, (tm, tm), 0)
    col = lax.broadcasted_iota(jnp.int32, (tm, tm), 1)
    earlier = jnp.where(row > col, 1.0, 0.0).astype(BF16)
    before = _dot(earlier, used) + carry_ref[0:1, :]
    rank1 = jnp.sum(jnp.where(sel1, before, 0.0), axis=-1, keepdims=True)
    rank2 = jnp.sum(jnp.where(sel2, before, 0.0), axis=-1, keepdims=True)
    carry_ref[...] += _dot(jnp.ones((8, tm), BF16), used)
    ocnt_ref[...] = carry_ref[...]

    fields = (i1 - R_EXP0, i2 - R_EXP0, w1, w2, rank1, rank2)
    route = jnp.zeros_like(logits)
    for k, val in enumerate(fields):
        route = jnp.where(lane_f == float(k), val, route)
    or_ref[...] = route


def _merge_call(o_na, o_mla, gates, x, ada, lw, layer, tm, name):
    bsz, seq, _ = x.shape

    def tok(width):
        return pl.BlockSpec((None, tm, width), lambda b, j: (b, j, 0))

    def full(arr):
        return _layer_spec(arr, layer)

    ws = [lw["w_o_na"], lw["w_o_mla"], lw["w_out"], lw["ln1_g"], lw["ln1_b"], lw["w_r"], lw["w_r_lo"], lw["b_r"]]
    return pl.pallas_call(
        _merge_body,
        grid=(bsz, seq // tm),
        in_specs=[tok(NA_WIDTH), tok(NA_WIDTH), tok(2 * D_MODEL), tok(D_MODEL),
                  _ada_spec(ada, layer, lambda b, j: b)] + [full(w) for w in ws],
        out_specs=[tok(D_MODEL), pl.BlockSpec((None, tm) + TOKEN_TILE, lambda b, j: (b, j, 0, 0)), tok(LANES),
                   pl.BlockSpec((8, LANES), lambda b, j: (0, 0))],
        out_shape=[jax.ShapeDtypeStruct((bsz, seq, D_MODEL), F32),
                   jax.ShapeDtypeStruct((bsz, seq) + TOKEN_TILE, F32),
                   jax.ShapeDtypeStruct((bsz, seq, LANES), F32),
                   jax.ShapeDtypeStruct((8, LANES), F32)],
        scratch_shapes=[pltpu.VMEM((8, LANES), F32)],
        compiler_params=_params(("arbitrary", "arbitrary")),
        name=name,
    )(o_na, o_mla, gates, x, ada[0], *ws)


POS_BITS = 16
POS_MASK = (1 << POS_BITS) - 1


def _plan(route, counts, n_tok, tr):
    r = route.reshape(n_tok, LANES)
    e1, e2 = r[:, 0].astype(jnp.int32), r[:, 1].astype(jnp.int32)
    rk1, rk2 = r[:, 4].astype(jnp.int32), r[:, 5].astype(jnp.int32)
    cnt = counts[0, R_EXP0:R_EXP0 + N_EXPERTS].astype(jnp.int32)
    tiles_e = (cnt + tr - 1) // tr
    tile_end = jnp.cumsum(tiles_e)
    row_off = (tile_end - tiles_e) * tr
    eid = jnp.arange(N_EXPERTS, dtype=jnp.int32)
    p1 = jnp.sum(jnp.where(e1[:, None] == eid, row_off, 0), axis=-1) + rk1
    p2 = jnp.sum(jnp.where(e2[:, None] == eid, row_off, 0), axis=-1) + rk2
    n_tiles = 2 * n_tok // tr + N_EXPERTS
    n_used = tile_end[-1]
    ti = jnp.arange(n_tiles, dtype=jnp.int32)
    tile_expert = jnp.sum(jnp.minimum(ti, n_used - 1)[:, None] >= tile_end[None, :], axis=-1).astype(jnp.int32)
    own = tile_expert[:, None] == eid
    seg_end = jnp.sum(jnp.where(own, row_off + cnt, 0), axis=-1)
    tile_rows = jnp.clip(seg_end - ti * tr, 1, tr).astype(jnp.int32)
    return p1 | (p2 << POS_BITS), tile_expert, tile_rows, jnp.stack([n_used, n_used]).astype(jnp.int32), n_tiles


ROW_GROUP = 8
ROW_CHUNK = 64


def _expert_body(pos_ref, te_ref, rows_ref, meta_ref, h_hbm, wg_ref, wu_ref, wd_ref, o_ref,
                 src_ref, hbuf, sem, wg_b, wu_b, wd_b, *, n_tok, tr):
    i = pl.program_id(0)
    n_used = meta_ref[0]

    def row_copy(tile, slot, r):
        t = src_ref[tile * tr + jnp.minimum(r, rows_ref[tile] - 1)]
        return pltpu.make_async_copy(h_hbm.at[t], hbuf.at[slot, r], sem.at[slot])

    def for_rows(fn):
        def body(g, carry):
            for k in range(ROW_GROUP):
                fn(g * ROW_GROUP + k)
            return carry
        lax.fori_loop(0, tr // ROW_GROUP, body, 0)

    @pl.when(i == 0)
    def _():
        def place(t, carry):
            packed = pos_ref[t]
            src_ref[packed & POS_MASK] = t
            src_ref[packed >> POS_BITS] = t
            return carry
        lax.fori_loop(0, n_tok, place, 0, unroll=8)
        for_rows(lambda r: row_copy(0, 0, r).start())

    @pl.when(i >= n_used)
    def _():
        o_ref[...] = jnp.zeros_like(o_ref)

    @pl.when(i < n_used)
    def _():
        slot = i & 1
        for_rows(lambda r: row_copy(i, slot, r).wait())

        @pl.when((i == 0) | (te_ref[i] != te_ref[jnp.maximum(i - 1, 0)]))
        def _():
            wg_b[...] = wg_ref[...].astype(BF16)
            wu_b[...] = wu_ref[...].astype(BF16)
            wd_b[...] = wd_ref[...].astype(BF16)

        nxt = jnp.minimum(i + 1, n_used - 1)
        for c in range(tr // ROW_CHUNK):
            rows = pl.ds(c * ROW_CHUNK, ROW_CHUNK)
            h = _load_token_tiles(hbuf.at[slot, rows]).astype(BF16)
            gate = _dot(h, wg_b[...])
            up = _dot(h, wu_b[...])
            hid = (gate * _sigmoid(gate) * up).astype(BF16)
            _store_token_tiles(o_ref.at[rows], _dot(hid, wd_b[...]))
            for r in range(c * ROW_CHUNK, (c + 1) * ROW_CHUNK):
                row_copy(nxt, 1 - slot, r).start()

        @pl.when(i == n_used - 1)
        def _():
            for_rows(lambda r: row_copy(nxt, 1 - slot, r).wait())


def _expert_call(h2, plan, w_gate, w_up, w_down, layer, tr, name):
    packed, tile_expert, tile_rows, meta, n_tiles = plan
    n_tok = h2.shape[0]
    n_rows = n_tiles * tr
    return pl.pallas_call(
        functools.partial(_expert_body, n_tok=n_tok, tr=tr),
        grid_spec=pltpu.PrefetchScalarGridSpec(
            num_scalar_prefetch=4,
            grid=(n_tiles,),
            in_specs=[
                pl.BlockSpec(memory_space=pl.ANY),
                pl.BlockSpec((None, None, D_MODEL, EXPERT_FF), lambda i, pos, te, nr, meta: (layer, te[i], 0, 0)),
                pl.BlockSpec((None, None, D_MODEL, EXPERT_FF), lambda i, pos, te, nr, meta: (layer, te[i], 0, 0)),
                pl.BlockSpec((None, None, EXPERT_FF, D_MODEL), lambda i, pos, te, nr, meta: (layer, te[i], 0, 0)),
            ],
            out_specs=pl.BlockSpec((tr,) + TOKEN_TILE, lambda i, pos, te, nr, meta: (i, 0, 0)),
            scratch_shapes=[
                pltpu.SMEM((n_rows,), jnp.int32),
                pltpu.VMEM((2, tr) + TOKEN_TILE, F32),
                pltpu.SemaphoreType.DMA((2,)),
                pltpu.VMEM((D_MODEL, EXPERT_FF), BF16),
                pltpu.VMEM((D_MODEL, EXPERT_FF), BF16),
                pltpu.VMEM((EXPERT_FF, D_MODEL), BF16),
            ],
        ),
        out_shape=jax.ShapeDtypeStruct((n_rows,) + TOKEN_TILE, F32),
        compiler_params=_params(("arbitrary",)),
        name=name,
    )(packed, tile_expert, tile_rows, meta, h2, w_gate, w_up, w_down)


def _combine_body(pos_ref, y_hbm, r_ref, x_ref, ada_ref, lng_ref, lnb_ref, o_ref, ybuf, sem, *, tm):
    i = pl.program_id(0)
    last = pl.num_programs(0) - 1

    def row_copies(tile, slot, r):
        packed = pos_ref[tile * tm + r]
        return (pltpu.make_async_copy(y_hbm.at[packed & POS_MASK], ybuf.at[slot, 0, r], sem.at[slot]),
                pltpu.make_async_copy(y_hbm.at[packed >> POS_BITS], ybuf.at[slot, 1, r], sem.at[slot]))

    def loop_rows(tile, slot, fn):
        def body(g, carry):
            for k in range(ROW_GROUP):
                for cp in row_copies(tile, slot, g * ROW_GROUP + k):
                    fn(cp)
            return carry
        lax.fori_loop(0, tm // ROW_GROUP, body, 0)

    @pl.when(i == 0)
    def _():
        loop_rows(0, 0, lambda cp: cp.start())

    slot = i & 1
    loop_rows(i, slot, lambda cp: cp.wait())

    nxt = jnp.minimum(i + 1, last)
    g2 = ada_ref[5:6, :]
    for c in range(tm // ROW_CHUNK):
        rows = pl.ds(c * ROW_CHUNK, ROW_CHUNK)
        route = r_ref[rows, :]
        lane = lax.broadcasted_iota(jnp.int32, route.shape, 1)
        w1 = jnp.sum(jnp.where(lane == 2, route, 0.0), axis=-1, keepdims=True)
        w2 = jnp.sum(jnp.where(lane == 3, route, 0.0), axis=-1, keepdims=True)
        y = w1 * _load_token_tiles(ybuf.at[slot, 0, rows]) + w2 * _load_token_tiles(ybuf.at[slot, 1, rows])
        o_ref[rows, :] = _ln_plain(DEEPNORM_ALPHA * x_ref[rows, :] + g2 * y) * lng_ref[...] + lnb_ref[...]
        for r in range(c * ROW_CHUNK, (c + 1) * ROW_CHUNK):
            for cp in row_copies(nxt, 1 - slot, r):
                cp.start()

    @pl.when(i == last)
    def _():
        loop_rows(nxt, 1 - slot, lambda cp: cp.wait())


def _combine_call(ys, packed, route, x1, ada, lw, layer, seq, tm, name):
    n_tok = x1.shape[0]
    per_batch = seq // tm

    def tok(width):
        return pl.BlockSpec((tm, width), lambda i, pos: (i, 0))

    def full(arr):
        return _layer_spec(arr, layer)

    return pl.pallas_call(
        functools.partial(_combine_body, tm=tm),
        grid_spec=pltpu.PrefetchScalarGridSpec(
            num_scalar_prefetch=1,
            grid=(n_tok // tm,),
            in_specs=[pl.BlockSpec(memory_space=pl.ANY), tok(LANES), tok(D_MODEL),
                      _ada_spec(ada, layer, lambda i, pos: i // per_batch),
                      full(lw["ln2_g"]), full(lw["ln2_b"])],
            out_specs=tok(D_MODEL),
            scratch_shapes=[pltpu.VMEM((2, 2, tm) + TOKEN_TILE, F32), pltpu.SemaphoreType.DMA((2,))],
        ),
        out_shape=jax.ShapeDtypeStruct((n_tok, D_MODEL), F32),
        compiler_params=_params(("arbitrary",)),
        name=name,
    )(packed, ys, route, x1, ada[0], lw["ln2_g"], lw["ln2_b"])


def _moe(h2, route, counts, x1, ada, lw, experts, layer, tr, tm, name):
    bsz, seq, _ = x1.shape
    n_tok = bsz * seq
    plan = _plan(route, counts, n_tok, tr)
    ys = _expert_call(h2.reshape((n_tok,) + TOKEN_TILE), plan, *experts, layer, tr, "expert_" + name)
    out = _combine_call(ys, plan[0], route.reshape(n_tok, LANES), x1.reshape(n_tok, D_MODEL), ada, lw, layer,
                        seq, tm, "combine_" + name)
    return out.reshape(bsz, seq, D_MODEL)


def _placement(rows_used):
    e = np.zeros((LANES, MLA_PAD_WIDTH), np.float32)
    for r in range(rows_used):
        for hd in range(MLA_HEADS):
            e[r, hd * MLA_HEAD_PAD + MLA_NOPE_DIM + (r % MLA_ROPE_DIM)] = 1.0
    return jnp.asarray(e, BF16)


def _prepare_weights(p):
    partner, _ = _rope_partner()
    n_layers = p["w_in"].shape[0]
    w = p["w_in"]
    k_r = w[:, :, 2176:2208]
    w_in = jnp.concatenate(
        [w[:, :, 0:512] * NA_SCALE, w[:, :, 512:2176], k_r, k_r[:, :, partner],
         jnp.zeros((n_layers, D_MODEL, LANES - 2 * MLA_ROPE_DIM), F32), w[:, :, 2208:]], axis=2).astype(BF16)
    uq = p["w_uq"].reshape(n_layers, MLA_Q_LORA, MLA_HEADS, MLA_QK_DIM)
    pad = MLA_HEAD_PAD - MLA_QK_DIM
    no_pad = ((0, 0), (0, 0), (0, 0))
    w_uq = jnp.pad(uq, no_pad + ((0, pad),)).reshape(n_layers, MLA_Q_LORA, MLA_PAD_WIDTH).astype(BF16)
    uq_sw = jnp.pad(uq[..., MLA_NOPE_DIM + partner], no_pad + ((MLA_NOPE_DIM, pad),))
    w_uq_sw = uq_sw.reshape(n_layers, MLA_Q_LORA, MLA_PAD_WIDTH).astype(BF16)
    uk = p["w_uk"].reshape(n_layers, MLA_KV_LORA, MLA_HEADS, MLA_NOPE_DIM)
    w_uk = jnp.pad(uk, no_pad + ((0, MLA_HEAD_PAD - MLA_NOPE_DIM),)).reshape(
        n_layers, MLA_KV_LORA, MLA_PAD_WIDTH).astype(BF16)
    n_pad = LANES - N_GROUPS - N_EXPERTS
    w_r = jnp.concatenate([p["router_group_w"], p["router_expert_w"],
                           jnp.zeros((n_layers, D_MODEL, n_pad), F32)], axis=2)
    b_r = jnp.concatenate([p["router_group_b"], p["router_expert_b"], jnp.zeros((n_layers, n_pad), F32)], axis=1)
    w_r_hi = w_r.astype(BF16)

    def row(name):
        return p[name][:, None, :]

    return dict(
        w_in=w_in, w_uq=w_uq, w_uq_sw=w_uq_sw, w_uk=w_uk, w_uv=p["w_uv"].astype(BF16),
        q_norm=row("mla_q_norm"), kv_norm=row("mla_kv_norm"),
        e_plain=_placement(MLA_ROPE_DIM), e_rope=_placement(2 * MLA_ROPE_DIM),
        w_o_na=p["w_o_na"].astype(BF16), w_o_mla=p["w_o_mla"].astype(BF16), w_out=p["w_out"].astype(BF16),
        ln1_g=row("ln1_g"), ln1_b=row("ln1_b"), ln2_g=row("ln2_g"), ln2_b=row("ln2_b"),
        w_r=w_r_hi, w_r_lo=(w_r - w_r_hi.astype(F32)).astype(BF16), b_r=b_r[:, None, :],
    )


def _window_rows(rows, q_rows):
    firsts = np.arange(0, rows, q_rows)
    lo = np.clip(firsts - NA_KR // 2, 0, rows - NA_KR)
    hi = np.clip(firsts + q_rows - 1 - NA_KR // 2, 0, rows - NA_KR) + NA_KR
    k_rows = min(rows, int(np.max(hi - lo)) + int(np.max(hi - lo)) % 2)
    k0 = np.minimum(lo, rows - k_rows)
    assert np.all(k0 % 2 == 0) and np.all(k0 + k_rows >= hi), (k0, k_rows)
    return k_rows, k0


def _na_bias_tables(rpb, rows):
    assert rows >= NA_KR and rows % 2 == 0
    r = np.arange(rows)
    rs = np.clip(r - NA_KR // 2, 0, rows - NA_KR)
    vrow = (r[None, :] >= rs[:, None]) & (r[None, :] < rs[:, None] + NA_KR)
    c = np.arange(GRID_W)
    cs = np.clip(c - NA_KC // 2, 0, GRID_W - NA_KC)
    vcol = (c[None, :] >= cs[:, None]) & (c[None, :] < cs[:, None] + NA_KC)
    coff = c[None, :] - c[:, None] + NA_KC - 1
    onehot = ((coff[None] == np.arange(2 * NA_KC - 1)[:, None, None]) & vcol[None]).astype(np.float32)
    blocks = jnp.einsum("lhdj,jqk->lhdqk", rpb, jnp.asarray(onehot), precision=lax.Precision.HIGHEST)
    blocks = jnp.where(jnp.asarray(vcol), blocks, NEG_INF)
    lead = rpb.shape[:2]
    reach = NA_KR - 1
    by_offset = jnp.concatenate(
        [jnp.full(lead + (rows - reach, GRID_W, GRID_W), NEG_INF, F32), blocks,
         jnp.full(lead + (rows - reach - 1, GRID_W, GRID_W), NEG_INF, F32)], axis=2)
    pair_blocks = jnp.concatenate([by_offset[:, :, :-1], by_offset[:, :, 1:]], axis=-1)
    in_window = np.repeat(vrow.reshape(rows, rows // 2, 2), GRID_W, axis=-1)
    row_masks = np.where(in_window, 0.0, NEG_INF).astype(np.float32).reshape(rows * rows // 2, 2 * GRID_W)
    return pair_blocks, jnp.asarray(row_masks)


def kernel(x_prompt, x_sample, cache_na_k, cache_na_v, cache_mla_ckv, cache_mla_kr, c, c_ctx, w_in, mla_q_norm, mla_kv_norm, w_uq, w_uk, w_uv, na_rpb, w_o_na, w_o_mla, w_out, w_ada, b_ada, ln1_g, ln1_b, ln2_g, ln2_b, router_group_w, router_group_b, router_expert_w, router_expert_b, expert_w_gate, expert_w_up, expert_w_down):
    p = dict(w_in=w_in, mla_q_norm=mla_q_norm, mla_kv_norm=mla_kv_norm, w_uq=w_uq, w_uk=w_uk, w_uv=w_uv,
             w_o_na=w_o_na, w_o_mla=w_o_mla, w_out=w_out, ln1_g=ln1_g, ln1_b=ln1_b, ln2_g=ln2_g, ln2_b=ln2_b,
             router_group_w=router_group_w, router_group_b=router_group_b,
             router_expert_w=router_expert_w, router_expert_b=router_expert_b,
             expert_w_gate=expert_w_gate, expert_w_up=expert_w_up, expert_w_down=expert_w_down)
    pb, ps, _ = x_prompt.shape
    sb, ss, _ = x_sample.shape
    past = cache_na_k.shape[2]

    n_cond = 16
    cond = jnp.zeros((n_cond, D_MODEL), F32).at[0].set(c_ctx).at[1:1 + sb].set(c)
    ada_all = _ada_call(cond, w_ada, b_ada[:, None, :]).reshape(DEPTH, n_cond, 6, D_MODEL)
    ada_all = jnp.pad(ada_all, ((0, 0), (0, 0), (0, 2), (0, 0)))
    ada_p = (ada_all, 0, 0)
    ada_s = (ada_all, 1, 1)

    cos32, sin32 = _rope_tables(ss)
    head_c = np.concatenate([np.ones((ss, MLA_NOPE_DIM)), cos32, np.zeros((ss, MLA_HEAD_PAD - MLA_QK_DIM))], -1)
    head_s = np.concatenate([np.zeros((ss, MLA_NOPE_DIM)), sin32, np.zeros((ss, MLA_HEAD_PAD - MLA_QK_DIM))], -1)
    rope_tabs = tuple(jnp.asarray(t, F32) for t in (
        np.tile(head_c, (1, MLA_HEADS)) * MLA_SCALE, np.tile(head_s, (1, MLA_HEADS)) * MLA_SCALE,
        np.concatenate([cos32, sin32, np.zeros((ss, LANES - 2 * MLA_ROPE_DIM))], -1)))

    cna_k = cache_na_k.reshape(sb, DEPTH, past, NA_WIDTH)
    cna_v = cache_na_v.reshape(sb, DEPTH, past, NA_WIDTH)
    ckr_pad = jnp.pad(cache_mla_kr, ((0, 0), (0, 0), (0, 0), (0, LANES - MLA_ROPE_DIM)))

    experts = (expert_w_gate, expert_w_up, expert_w_down)
    lw = _prepare_weights(p)
    tq = 512
    grid_rows = ss // GRID_W
    local_bias = _na_bias_tables(na_rpb, grid_rows) + (_window_rows(grid_rows, tq // GRID_W)[0],)
    xp, xs = x_prompt, x_sample
    new_k, new_v, new_ckv, new_kr = [], [], [], []
    for l in range(DEPTH):

        qn, kn, vn, qm, km, vm, gates, ckv, k_r = _proj_call(xp, ada_p, lw, l, None, F32, True, ps)
        o_na = _attn_call(qn, [(kn, vn, _seg3(kn, NA_WIDTH), _seg3(vn, NA_WIDTH))], None, NA_HEAD_DIM, ps,
                          NA_HEADS, "attn_ctx_na")
        o_mla = _attn_call(qm, [(km, vm, _seg3(km, MLA_PAD_WIDTH), _seg3(vm, NA_WIDTH))], None, MLA_HEAD_PAD, ps,
                           MLA_HEADS, "attn_ctx_mla")
        x1, h2, route, counts = _merge_call(o_na, o_mla, gates, xp, ada_p, lw, l, ps, "merge_ctx")
        xp = _moe(h2, route, counts, x1, ada_p, lw, experts, l, 128, ps, "ctx")
        new_k.append(kn)
        new_v.append(vn)
        new_ckv.append(ckv)
        new_kr.append(k_r)

        qn, kn, vn, qm, km, vm, gates = _proj_call(xs, ada_s, lw, l, rope_tabs, BF16, False, 256)
        km_ctx, vm_ctx = _ctxkv_call(cache_mla_ckv, ckr_pad, l, lw)
        o_na = _attn_call(
            qn, [(cna_k, cna_v, _seg4(cna_k, l, 128), _seg4(cna_v, l, 128)),
                 (kn, vn, _seg3(kn, 128), _seg3(vn, 128))], local_bias, NA_HEAD_DIM, tq, 2, "attn_lat_na", l)
        o_mla = _attn_call(
            qm, [(km_ctx, vm_ctx, _seg3(km_ctx, 256), _seg3(vm_ctx, 128)),
                 (km, vm, _seg3(km, 256), _seg3(vm, 128))], None, MLA_HEAD_PAD, tq, 2, "attn_lat_mla")
        x1, h2, route, counts = _merge_call(o_na, o_mla, gates, xs, ada_s, lw, l, 256, "merge_lat")
        xs = _moe(h2, route, counts, x1, ada_s, lw, experts, l, 256, 256, "lat")

    def stack(parts, tail):
        return jnp.stack(parts, axis=1).reshape((pb, DEPTH, ps) + tail)

    return (xp, xs, stack(new_k, (NA_HEADS, NA_HEAD_DIM)), stack(new_v, (NA_HEADS, NA_HEAD_DIM)),
            stack(new_ckv, (MLA_KV_LORA,)), stack(new_kr, (MLA_ROPE_DIM,)))
```

```python
import functools

import numpy as np
import jax
import jax.numpy as jnp
from jax import lax
from jax.experimental import pallas as pl
from jax.experimental.pallas import tpu as pltpu

F32 = jnp.float32
BF16 = jnp.bfloat16

D_MODEL = 1024
DEPTH = 2
GRID_W = 64
NA_HEADS = 8
NA_HEAD_DIM = 64
NA_KR = 8
NA_KC = 16
NA_WIDTH = NA_HEADS * NA_HEAD_DIM
NA_SCALE = NA_HEAD_DIM ** -0.5
MLA_HEADS = 8
MLA_NOPE_DIM = 64
MLA_ROPE_DIM = 32
MLA_V_DIM = 64
MLA_Q_LORA = 384
MLA_KV_LORA = 256
MLA_QK_DIM = MLA_NOPE_DIM + MLA_ROPE_DIM
MLA_SCALE = MLA_QK_DIM ** -0.5
ROPE_BASE = 10000.0
N_GROUPS = 4
EXPERTS_PER_GROUP = 8
N_EXPERTS = N_GROUPS * EXPERTS_PER_GROUP
EXPERT_FF = 256
LN_EPS = 1e-5
RMS_EPS = 1e-6
NEG_INF = -1e30
DEEPNORM_ALPHA = (2 * DEPTH) ** 0.25

LANES = 128
MLA_HEAD_PAD = LANES
MLA_PAD_WIDTH = MLA_HEADS * MLA_HEAD_PAD
C_Q = 0
C_K = C_Q + NA_WIDTH
C_V = C_K + NA_WIDTH
C_CQ = C_V + NA_WIDTH
C_CKV = C_CQ + MLA_Q_LORA
C_KR = C_CKV + MLA_KV_LORA
C_GATE = C_KR + LANES
C_END = C_GATE + 2 * D_MODEL
R_EXP0 = N_GROUPS

VMEM_LIMIT = 56 * 1024 * 1024


def _rope_partner():
    j = np.arange(MLA_ROPE_DIM)
    n_freq = MLA_ROPE_DIM // 4
    axis, within = j // (2 * n_freq), j % (2 * n_freq)
    half, f = within // n_freq, within % n_freq
    return axis * 2 * n_freq + (1 - half) * n_freq + f, half


def _rope_tables(n_tokens):
    n_freq = MLA_ROPE_DIM // 4
    inv_freq = ROPE_BASE ** (-np.arange(n_freq, dtype=np.float64) / n_freq)
    t = np.arange(n_tokens)
    ang = np.concatenate([(t // GRID_W)[:, None] * inv_freq, (t % GRID_W)[:, None] * inv_freq], axis=-1)
    cos, sin = np.cos(ang), np.sin(ang)
    cos32 = np.concatenate([cos[:, :n_freq], cos[:, :n_freq], cos[:, n_freq:], cos[:, n_freq:]], axis=-1)
    sin32 = np.concatenate([-sin[:, :n_freq], sin[:, :n_freq], -sin[:, n_freq:], sin[:, n_freq:]], axis=-1)
    return cos32, sin32


def _ln_plain(x):
    mu = jnp.mean(x, axis=-1, keepdims=True)
    xc = x - mu
    var = jnp.mean(xc * xc, axis=-1, keepdims=True)
    return xc * lax.rsqrt(var + LN_EPS)


def _rms(x, g):
    return x * lax.rsqrt(jnp.mean(x * x, axis=-1, keepdims=True) + RMS_EPS) * g


def _sigmoid(x):
    return 1.0 / (1.0 + jnp.exp(-x))


def _dot(a, b):
    return jnp.dot(a, b, preferred_element_type=F32)


def _params(sem):
    return pltpu.CompilerParams(dimension_semantics=sem, vmem_limit_bytes=VMEM_LIMIT)


def _layer_spec(arr, layer):
    zeros = (0,) * (arr.ndim - 1)
    return pl.BlockSpec((None,) + arr.shape[1:], lambda *_: (layer,) + zeros, pipeline_mode=pl.Buffered(1))


def _ada_spec(ada, layer, batch_of):
    _, first, step = ada
    return pl.BlockSpec((None, None, 8, D_MODEL), lambda *idx: (layer, first + step * batch_of(*idx), 0, 0))


def _const_spec(arr):
    zeros = (0,) * arr.ndim
    return pl.BlockSpec(arr.shape, lambda *_: zeros, pipeline_mode=pl.Buffered(1))


TOKEN_TILE = (D_MODEL // LANES, LANES)


def _store_token_tiles(ref, val):
    chunks = jnp.stack([val[:, k * LANES:(k + 1) * LANES] for k in range(TOKEN_TILE[0])], axis=0)
    ref[...] = jnp.swapaxes(chunks, 0, 1)


def _load_token_tiles(ref):
    chunks = jnp.swapaxes(ref[...], 0, 1)
    return jnp.concatenate([chunks[k] for k in range(TOKEN_TILE[0])], axis=-1)


def _ada_body(c_ref, w_ref, b_ref, o_ref):
    c = c_ref[...]
    s = (c * _sigmoid(c)).astype(BF16)
    o_ref[0] = _dot(s, w_ref[0].astype(BF16)) + b_ref[0]


def _ada_call(cond, w_ada, b_ada):
    n_rows = cond.shape[0]
    tn = 1024
    return pl.pallas_call(
        _ada_body,
        grid=(DEPTH, 6 * D_MODEL // tn),
        in_specs=[
            pl.BlockSpec((n_rows, D_MODEL), lambda l, j: (0, 0)),
            pl.BlockSpec((1, D_MODEL, tn), lambda l, j: (l, 0, j)),
            pl.BlockSpec((1, 1, tn), lambda l, j: (l, 0, j)),
        ],
        out_specs=pl.BlockSpec((1, n_rows, tn), lambda l, j: (l, 0, j)),
        out_shape=jax.ShapeDtypeStruct((DEPTH, n_rows, 6 * D_MODEL), F32),
        compiler_params=_params(("arbitrary", "arbitrary")),
        name="ada",
    )(cond, w_ada, b_ada)


def _proj_body(*refs, rope, emit_cache):
    it = iter(refs)
    x_ref, ada_ref, w_ref, qn_ref, kvn_ref, wuq_ref = (next(it) for _ in range(6))
    wuqs_ref = next(it) if rope else None
    wuk_ref, wuv_ref, e_ref = next(it), next(it), next(it)
    if rope:
        cq_ref, sq_ref, tk_ref = next(it), next(it), next(it)
    oq_ref, ok_ref, ov_ref, oqm_ref, okm_ref, ovm_ref, og_ref = (next(it) for _ in range(7))
    if emit_cache:
        ockv_ref, okr_ref = next(it), next(it)

    x = x_ref[...]
    sh = ada_ref[0:1, :]
    sc = ada_ref[1:2, :]
    h = (_ln_plain(x) * (1.0 + sc) + sh).astype(BF16)

    oq_ref[...] = _dot(h, w_ref[:, C_Q:C_K]).astype(oq_ref.dtype)
    ok_ref[...] = _dot(h, w_ref[:, C_K:C_V]).astype(ok_ref.dtype)
    ov_ref[...] = _dot(h, w_ref[:, C_V:C_CQ]).astype(ov_ref.dtype)

    c_q = _rms(_dot(h, w_ref[:, C_CQ:C_CKV]), qn_ref[...]).astype(BF16)
    q = _dot(c_q, wuq_ref[...])
    if rope:
        q = (q * jnp.tile(cq_ref[...], (1, MLA_HEADS))
             + _dot(c_q, wuqs_ref[...]) * jnp.tile(sq_ref[...], (1, MLA_HEADS)))
    else:
        q = q * MLA_SCALE
    oqm_ref[...] = q.astype(BF16)

    c_kv = _rms(_dot(h, w_ref[:, C_CKV:C_KR]), kvn_ref[...])
    slab = _dot(h, w_ref[:, C_KR:C_GATE])
    if emit_cache:
        ockv_ref[...] = c_kv
        okr_ref[...] = slab[:, :MLA_ROPE_DIM]
    if rope:
        slab = slab * tk_ref[...]
    c_kv = c_kv.astype(BF16)
    okm_ref[...] = (_dot(c_kv, wuk_ref[...]) + _dot(slab.astype(BF16), e_ref[...])).astype(BF16)
    ovm_ref[...] = _dot(c_kv, wuv_ref[...]).astype(BF16)

    og_ref[...] = _sigmoid(_dot(h, w_ref[:, C_GATE:C_END])).astype(BF16)


def _proj_call(x, ada, lw, layer, rope_tabs, kv_dtype, emit_cache, tm):
    bsz, seq, _ = x.shape
    nj = seq // tm
    rope = rope_tabs is not None

    def tok(width):
        return pl.BlockSpec((None, tm, width), lambda b, j: (b, j, 0))

    def full(arr):
        return _layer_spec(arr, layer)

    def tab(width):
        return pl.BlockSpec((tm, width), lambda b, j: (j, 0))

    args = [x, ada[0], lw["w_in"], lw["q_norm"], lw["kv_norm"], lw["w_uq"]]
    specs = [tok(D_MODEL), _ada_spec(ada, layer, lambda b, j: b),
             full(lw["w_in"]), full(lw["q_norm"]), full(lw["kv_norm"]), full(lw["w_uq"])]
    if rope:
        args.append(lw["w_uq_sw"])
        specs.append(full(lw["w_uq_sw"]))
    e_mat = lw["e_rope"] if rope else lw["e_plain"]
    args += [lw["w_uk"], lw["w_uv"], e_mat]
    specs += [full(lw["w_uk"]), full(lw["w_uv"]), _const_spec(e_mat)]
    if rope:
        args += list(rope_tabs)
        specs += [tab(MLA_HEAD_PAD), tab(MLA_HEAD_PAD), tab(LANES)]

    def shp(width, dt):
        return jax.ShapeDtypeStruct((bsz, seq, width), dt)

    out_shape = [shp(NA_WIDTH, BF16), shp(NA_WIDTH, kv_dtype), shp(NA_WIDTH, kv_dtype),
                 shp(MLA_PAD_WIDTH, BF16), shp(MLA_PAD_WIDTH, BF16), shp(NA_WIDTH, BF16),
                 shp(2 * D_MODEL, BF16)]
    out_specs = [tok(NA_WIDTH), tok(NA_WIDTH), tok(NA_WIDTH), tok(MLA_PAD_WIDTH), tok(MLA_PAD_WIDTH),
                 tok(NA_WIDTH), tok(2 * D_MODEL)]
    if emit_cache:
        out_shape += [shp(MLA_KV_LORA, F32), shp(MLA_ROPE_DIM, F32)]
        out_specs += [tok(MLA_KV_LORA), tok(MLA_ROPE_DIM)]
    return pl.pallas_call(
        functools.partial(_proj_body, rope=rope, emit_cache=emit_cache),
        grid=(bsz, nj),
        in_specs=specs,
        out_specs=out_specs,
        out_shape=out_shape,
        compiler_params=_params(("arbitrary", "arbitrary")),
        name="proj_lat" if rope else "proj_ctx",
    )(*args)


def _ctxkv_body(ckv_ref, kr_ref, wuk_ref, wuv_ref, e_ref, okm_ref, ovm_ref):
    c_kv = ckv_ref[...].astype(BF16)
    okm_ref[...] = (_dot(c_kv, wuk_ref[...]) + _dot(kr_ref[...].astype(BF16), e_ref[...])).astype(BF16)
    ovm_ref[...] = _dot(c_kv, wuv_ref[...]).astype(BF16)


def _ctxkv_call(cache_ckv, cache_kr_pad, layer, lw):
    bsz, _, past, _ = cache_ckv.shape

    def full(arr):
        return _layer_spec(arr, layer)

    return pl.pallas_call(
        _ctxkv_body,
        grid=(bsz,),
        in_specs=[
            pl.BlockSpec((None, None, past, MLA_KV_LORA), lambda b: (b, layer, 0, 0)),
            pl.BlockSpec((None, None, past, LANES), lambda b: (b, layer, 0, 0)),
            full(lw["w_uk"]), full(lw["w_uv"]), _const_spec(lw["e_plain"]),
        ],
        out_specs=[pl.BlockSpec((None, past, MLA_PAD_WIDTH), lambda b: (b, 0, 0)),
                   pl.BlockSpec((None, past, NA_WIDTH), lambda b: (b, 0, 0))],
        out_shape=[jax.ShapeDtypeStruct((bsz, past, MLA_PAD_WIDTH), BF16),
                   jax.ShapeDtypeStruct((bsz, past, NA_WIDTH), BF16)],
        compiler_params=_params(("arbitrary",)),
        name="ctxkv",
    )(cache_ckv, cache_kr_pad, lw["w_uk"], lw["w_uv"], lw["e_plain"])


def _local_bias(pb_ref, rm_ref, hh, q0, k0, q_rows, k_rows, grid_rows):
    rows = []
    for qr in range(q_rows):
        blocks = []
        for p in range(k_rows // 2):
            pair = k0 // 2 + p
            e = 2 * pair - (q0 + qr) + grid_rows
            blocks.append(pb_ref[hh, e] + rm_ref[pl.ds((q0 + qr) * (grid_rows // 2) + pair, 1), :])
        rows.append(jnp.concatenate(blocks, axis=-1))
    return jnp.concatenate(rows, axis=0)


def _attn_body(*refs, n_seg, dq, local, heads):
    q_ref = refs[0]
    seg_refs = [(refs[1 + 2 * i], refs[2 + 2 * i]) for i in range(n_seg)]
    o_ref = refs[-1]
    if local is not None:
        pb_ref, rm_ref = refs[1 + 2 * n_seg], refs[2 + 2 * n_seg]
        q_rows, k_rows, grid_rows = local
        q0 = pl.program_id(1) * q_rows
        k0 = jnp.minimum(jnp.clip(q0 - NA_KR // 2, 0, grid_rows - NA_KR), grid_rows - k_rows)
        keys = pl.ds(pl.multiple_of(k0 * GRID_W, GRID_W), k_rows * GRID_W)
    outs = []
    for hh in range(heads):
        qh = q_ref[:, hh * dq:(hh + 1) * dq]
        scores = []
        values = []
        for i, (k_ref, v_ref) in enumerate(seg_refs):
            windowed = local is not None and i == n_seg - 1
            rows = keys if windowed else slice(None)
            kh = k_ref[rows, hh * dq:(hh + 1) * dq].astype(BF16)
            values.append(v_ref[rows, hh * MLA_V_DIM:(hh + 1) * MLA_V_DIM].astype(BF16))
            s = lax.dot_general(qh, kh, (((1,), (1,)), ((), ())), preferred_element_type=F32)
            if windowed:
                s = s + _local_bias(pb_ref, rm_ref, hh, q0, k0, q_rows, k_rows, grid_rows)
            scores.append(s)
        m = functools.reduce(jnp.maximum, [jnp.max(s, axis=-1, keepdims=True) for s in scores])
        den = None
        acc = None
        for s, vh in zip(scores, values):
            e = jnp.exp(s - m)
            d = jnp.sum(e, axis=-1, keepdims=True)
            a = _dot(e.astype(BF16), vh)
            den = d if den is None else den + d
            acc = a if acc is None else acc + a
        outs.append(acc / den)
    o_ref[...] = jnp.concatenate(outs, axis=-1).astype(o_ref.dtype)


def _attn_call(q, segs, local_bias, dq, tq, heads, name, layer=0):
    bsz, seq, _ = q.shape
    args = [q]
    specs = [pl.BlockSpec((None, tq, heads * dq), lambda hp, qt, b: (b, qt, hp))]
    for k, v, ks, vs in segs:
        args += [k, v]
        specs += [ks, vs]
    local = None
    if local_bias is not None:
        pair_blocks, row_masks, k_rows = local_bias
        args += [pair_blocks, row_masks]
        specs += [pl.BlockSpec((None, heads) + pair_blocks.shape[2:], lambda hp, qt, b: (layer, hp, 0, 0, 0)),
                  _const_spec(row_masks)]
        local = (tq // GRID_W, k_rows, seq // GRID_W)
    return pl.pallas_call(
        functools.partial(_attn_body, n_seg=len(segs), dq=dq, local=local, heads=heads),
        grid=(NA_HEADS // heads, seq // tq, bsz),
        in_specs=specs,
        out_specs=pl.BlockSpec((None, tq, heads * MLA_V_DIM), lambda hp, qt, b: (b, qt, hp)),
        out_shape=jax.ShapeDtypeStruct((bsz, seq, NA_WIDTH), BF16),
        compiler_params=_params(("arbitrary", "arbitrary", "arbitrary")),
        name=name,
    )(*args)


def _seg3(arr, width):
    n = arr.shape[1]
    return pl.BlockSpec((None, n, width), lambda hp, qt, b: (b, 0, hp))


def _seg4(arr, layer, width):
    n = arr.shape[2]
    return pl.BlockSpec((None, None, n, width), lambda hp, qt, b: (b, layer, 0, hp))


def _route(logits):
    lane = lax.broadcasted_iota(jnp.int32, logits.shape, 1)
    lane_f = lane.astype(F32)
    far = float(LANES)
    is_g = lane < N_GROUPS
    gl = jnp.where(is_g, logits, NEG_INF)
    gmax = jnp.max(gl, axis=-1, keepdims=True)
    gsum = jnp.sum(jnp.exp(gl - gmax), axis=-1, keepdims=True)
    g_w = 1.0 / gsum
    g_idx = jnp.min(jnp.where(gl == gmax, lane_f, far), axis=-1, keepdims=True)
    e_group = ((lane - R_EXP0) >> 3).astype(F32)
    is_e = (lane >= R_EXP0) & (lane < R_EXP0 + N_EXPERTS) & (e_group == g_idx)
    el = jnp.where(is_e, logits, NEG_INF)
    m1 = jnp.max(el, axis=-1, keepdims=True)
    i1 = jnp.min(jnp.where(el == m1, lane_f, far), axis=-1, keepdims=True)
    el2 = jnp.where(lane_f == i1, NEG_INF, el)
    m2 = jnp.max(el2, axis=-1, keepdims=True)
    i2 = jnp.min(jnp.where(el2 == m2, lane_f, far), axis=-1, keepdims=True)
    r = jnp.exp(m2 - m1)
    w1 = g_w / (1.0 + r)
    w2 = g_w * r / (1.0 + r)
    return lane_f, i1, i2, w1, w2


def _merge_body(ona_ref, omla_ref, g_ref, x_ref, ada_ref, wna_ref, wmla_ref, wout_ref, lng_ref, lnb_ref,
                wr_ref, wrl_ref, br_ref, ox_ref, oh_ref, or_ref, ocnt_ref, carry_ref):
    first = (pl.program_id(0) == 0) & (pl.program_id(1) == 0)

    @pl.when(first)
    def _():
        carry_ref[...] = jnp.zeros_like(carry_ref)

    y_na = _dot(ona_ref[...], wna_ref[...])
    y_mla = _dot(omla_ref[...], wmla_ref[...])
    g_na = g_ref[:, :D_MODEL].astype(F32)
    g_mla = g_ref[:, D_MODEL:].astype(F32)
    y = _dot((g_na * y_na + g_mla * y_mla).astype(BF16), wout_ref[...])
    g1 = ada_ref[2:3, :]
    x1 = _ln_plain(DEEPNORM_ALPHA * x_ref[...] + g1 * y) * lng_ref[...] + lnb_ref[...]
    ox_ref[...] = x1
    h2 = _ln_plain(x1) * (1.0 + ada_ref[4:5, :]) + ada_ref[3:4, :]
    _store_token_tiles(oh_ref, h2)
    h_hi = h2.astype(BF16)
    h_lo = (h2 - h_hi.astype(F32)).astype(BF16)
    logits = (_dot(h_hi, wr_ref[...]) + (_dot(h_lo, wr_ref[...]) + _dot(h_hi, wrl_ref[...]))) + br_ref[...]
    lane_f, i1, i2, w1, w2 = _route(logits)

    tm = logits.shape[0]
    sel1 = lane_f == i1
    sel2 = lane_f == i2
    used = (jnp.where(sel1, 1.0, 0.0) + jnp.where(sel2, 1.0, 0.0)).astype(BF16)
    row = lax.broadcasted_iota(jnp.int32, (tm, tm), 0)
    col = lax.broadcasted_iota(jnp.int32, (tm, tm), 1)
    earlier = jnp.where(row > col, 1.0, 0.0).astype(BF16)
    before = _dot(earlier, used) + carry_ref[0:1, :]
    rank1 = jnp.sum(jnp.where(sel1, before, 0.0), axis=-1, keepdims=True)
    rank2 = jnp.sum(jnp.where(sel2, before, 0.0), axis=-1, keepdims=True)
    carry_ref[...] += _dot(jnp.ones((8, tm), BF16), used)
    ocnt_ref[...] = carry_ref[...]

    fields = (i1 - R_EXP0, i2 - R_EXP0, w1, w2, rank1, rank2)
    route = jnp.zeros_like(logits)
    for k, val in enumerate(fields):
        route = jnp.where(lane_f == float(k), val, route)
    or_ref[...] = route


def _merge_call(o_na, o_mla, gates, x, ada, lw, layer, tm, name):
    bsz, seq, _ = x.shape

    def tok(width):
        return pl.BlockSpec((None, tm, width), lambda b, j: (b, j, 0))

    def full(arr):
        return _layer_spec(arr, layer)

    ws = [lw["w_o_na"], lw["w_o_mla"], lw["w_out"], lw["ln1_g"], lw["ln1_b"], lw["w_r"], lw["w_r_lo"], lw["b_r"]]
    return pl.pallas_call(
        _merge_body,
        grid=(bsz, seq // tm),
        in_specs=[tok(NA_WIDTH), tok(NA_WIDTH), tok(2 * D_MODEL), tok(D_MODEL),
                  _ada_spec(ada, layer, lambda b, j: b)] + [full(w) for w in ws],
        out_specs=[tok(D_MODEL), pl.BlockSpec((None, tm) + TOKEN_TILE, lambda b, j: (b, j, 0, 0)), tok(LANES),
                   pl.BlockSpec((8, LANES), lambda b, j: (0, 0))],
        out_shape=[jax.ShapeDtypeStruct((bsz, seq, D_MODEL), F32),
                   jax.ShapeDtypeStruct((bsz, seq) + TOKEN_TILE, F32),
                   jax.ShapeDtypeStruct((bsz, seq, LANES), F32),
                   jax.ShapeDtypeStruct((8, LANES), F32)],
        scratch_shapes=[pltpu.VMEM((8, LANES), F32)],
        compiler_params=_params(("arbitrary", "arbitrary")),
        name=name,
    )(o_na, o_mla, gates, x, ada[0], *ws)


POS_BITS = 16
POS_MASK = (1 << POS_BITS) - 1


def _plan(route, counts, n_tok, tr):
    r = route.reshape(n_tok, LANES)
    e1, e2 = r[:, 0].astype(jnp.int32), r[:, 1].astype(jnp.int32)
    rk1, rk2 = r[:, 4].astype(jnp.int32), r[:, 5].astype(jnp.int32)
    cnt = counts[0, R_EXP0:R_EXP0 + N_EXPERTS].astype(jnp.int32)
    tiles_e = (cnt + tr - 1) // tr
    tile_end = jnp.cumsum(tiles_e)
    row_off = (tile_end - tiles_e) * tr
    eid = jnp.arange(N_EXPERTS, dtype=jnp.int32)
    p1 = jnp.sum(jnp.where(e1[:, None] == eid, row_off, 0), axis=-1) + rk1
    p2 = jnp.sum(jnp.where(e2[:, None] == eid, row_off, 0), axis=-1) + rk2
    n_tiles = 2 * n_tok // tr + N_EXPERTS
    n_used = tile_end[-1]
    ti = jnp.arange(n_tiles, dtype=jnp.int32)
    tile_expert = jnp.sum(jnp.minimum(ti, n_used - 1)[:, None] >= tile_end[None, :], axis=-1).astype(jnp.int32)
    own = tile_expert[:, None] == eid
    seg_end = jnp.sum(jnp.where(own, row_off + cnt, 0), axis=-1)
    tile_rows = jnp.clip(seg_end - ti * tr, 1, tr).astype(jnp.int32)
    return p1 | (p2 << POS_BITS), tile_expert, tile_rows, jnp.stack([n_used, n_used]).astype(jnp.int32), n_tiles


ROW_GROUP = 8


def _expert_body(pos_ref, te_ref, rows_ref, meta_ref, h_hbm, wg_ref, wu_ref, wd_ref, o_ref,
                 src_ref, hbuf, sem, wg_b, wu_b, wd_b, *, n_tok, tr):
    i = pl.program_id(0)
    n_used = meta_ref[0]

    def for_tile_rows(tile, slot, fn):
        base = tile * tr
        last = rows_ref[tile] - 1
        dst = hbuf.at[slot]
        dsem = sem.at[slot]

        def body(g, carry):
            for k in range(ROW_GROUP):
                r = g * ROW_GROUP + k
                t = src_ref[base + jnp.minimum(r, last)]
                fn(pltpu.make_async_copy(h_hbm.at[t], dst.at[r], dsem))
            return carry
        lax.fori_loop(0, (last + ROW_GROUP) // ROW_GROUP, body, 0)

    @pl.when(i == 0)
    def _():
        hbuf[...] = jnp.zeros_like(hbuf)

        def place(t, carry):
            packed = pos_ref[t]
            src_ref[packed & POS_MASK] = t
            src_ref[packed >> POS_BITS] = t
            return carry
        lax.fori_loop(0, n_tok, place, 0, unroll=8)
        for_tile_rows(0, 0, lambda cp: cp.start())

    @pl.when(i + 1 < n_used)
    def _():
        for_tile_rows(i + 1, (i + 1) & 1, lambda cp: cp.start())

    @pl.when(i >= n_used)
    def _():
        o_ref[...] = jnp.zeros_like(o_ref)

    @pl.when(i < n_used)
    def _():
        slot = i & 1
        for_tile_rows(i, slot, lambda cp: cp.wait())

        @pl.when((i == 0) | (te_ref[i] != te_ref[jnp.maximum(i - 1, 0)]))
        def _():
            wg_b[...] = wg_ref[...].astype(BF16)
            wu_b[...] = wu_ref[...].astype(BF16)
            wd_b[...] = wd_ref[...].astype(BF16)

        h = _load_token_tiles(hbuf.at[slot]).astype(BF16)
        gate = _dot(h, wg_b[...])
        up = _dot(h, wu_b[...])
        hid = (gate * _sigmoid(gate) * up).astype(BF16)
        _store_token_tiles(o_ref, _dot(hid, wd_b[...]))


def _expert_call(h2, plan, w_gate, w_up, w_down, layer, tr, name):
    packed, tile_expert, tile_rows, meta, n_tiles = plan
    n_tok = h2.shape[0]
    n_rows = n_tiles * tr
    return pl.pallas_call(
        functools.partial(_expert_body, n_tok=n_tok, tr=tr),
        grid_spec=pltpu.PrefetchScalarGridSpec(
            num_scalar_prefetch=4,
            grid=(n_tiles,),
            in_specs=[
                pl.BlockSpec(memory_space=pl.ANY),
                pl.BlockSpec((None, None, D_MODEL, EXPERT_FF), lambda i, pos, te, nr, meta: (layer, te[i], 0, 0)),
                pl.BlockSpec((None, None, D_MODEL, EXPERT_FF), lambda i, pos, te, nr, meta: (layer, te[i], 0, 0)),
                pl.BlockSpec((None, None, EXPERT_FF, D_MODEL), lambda i, pos, te, nr, meta: (layer, te[i], 0, 0)),
            ],
            out_specs=pl.BlockSpec((tr,) + TOKEN_TILE, lambda i, pos, te, nr, meta: (i, 0, 0)),
            scratch_shapes=[
                pltpu.SMEM((n_rows,), jnp.int32),
                pltpu.VMEM((2, tr) + TOKEN_TILE, F32),
                pltpu.SemaphoreType.DMA((2,)),
                pltpu.VMEM((D_MODEL, EXPERT_FF), BF16),
                pltpu.VMEM((D_MODEL, EXPERT_FF), BF16),
                pltpu.VMEM((EXPERT_FF, D_MODEL), BF16),
            ],
        ),
        out_shape=jax.ShapeDtypeStruct((n_rows,) + TOKEN_TILE, F32),
        compiler_params=_params(("arbitrary",)),
        name=name,
    )(packed, tile_expert, tile_rows, meta, h2, w_gate, w_up, w_down)


def _combine_body(pos_ref, y_hbm, r_ref, x_ref, ada_ref, lng_ref, lnb_ref, o_ref, ybuf, sem, *, tm):
    i = pl.program_id(0)
    n = pl.num_programs(0)

    def for_tile_rows(tile, slot, fn):
        base = tile * tm
        dst1 = ybuf.at[slot, 0]
        dst2 = ybuf.at[slot, 1]
        dsem = sem.at[slot]

        def body(g, carry):
            for k in range(ROW_GROUP):
                r = g * ROW_GROUP + k
                packed = pos_ref[base + r]
                fn(pltpu.make_async_copy(y_hbm.at[packed & POS_MASK], dst1.at[r], dsem))
                fn(pltpu.make_async_copy(y_hbm.at[packed >> POS_BITS], dst2.at[r], dsem))
            return carry
        lax.fori_loop(0, tm // ROW_GROUP, body, 0)

    @pl.when(i == 0)
    def _():
        for_tile_rows(0, 0, lambda cp: cp.start())

    @pl.when(i + 1 < n)
    def _():
        for_tile_rows(i + 1, (i + 1) & 1, lambda cp: cp.start())

    slot = i & 1
    for_tile_rows(i, slot, lambda cp: cp.wait())

    route = r_ref[...]
    lane = lax.broadcasted_iota(jnp.int32, route.shape, 1)
    w1 = jnp.sum(jnp.where(lane == 2, route, 0.0), axis=-1, keepdims=True)
    w2 = jnp.sum(jnp.where(lane == 3, route, 0.0), axis=-1, keepdims=True)
    y = w1 * _load_token_tiles(ybuf.at[slot, 0]) + w2 * _load_token_tiles(ybuf.at[slot, 1])
    g2 = ada_ref[5:6, :]
    o_ref[...] = _ln_plain(DEEPNORM_ALPHA * x_ref[...] + g2 * y) * lng_ref[...] + lnb_ref[...]


def _combine_call(ys, packed, route, x1, ada, lw, layer, seq, tm, name):
    n_tok = x1.shape[0]
    per_batch = seq // tm

    def tok(width):
        return pl.BlockSpec((tm, width), lambda i, pos: (i, 0))

    def full(arr):
        return _layer_spec(arr, layer)

    return pl.pallas_call(
        functools.partial(_combine_body, tm=tm),
        grid_spec=pltpu.PrefetchScalarGridSpec(
            num_scalar_prefetch=1,
            grid=(n_tok // tm,),
            in_specs=[pl.BlockSpec(memory_space=pl.ANY), tok(LANES), tok(D_MODEL),
                      _ada_spec(ada, layer, lambda i, pos: i // per_batch),
                      full(lw["ln2_g"]), full(lw["ln2_b"])],
            out_specs=tok(D_MODEL),
            scratch_shapes=[pltpu.VMEM((2, 2, tm) + TOKEN_TILE, F32), pltpu.SemaphoreType.DMA((2,))],
        ),
        out_shape=jax.ShapeDtypeStruct((n_tok, D_MODEL), F32),
        compiler_params=_params(("arbitrary",)),
        name=name,
    )(packed, ys, route, x1, ada[0], lw["ln2_g"], lw["ln2_b"])


def _moe(h2, route, counts, x1, ada, lw, experts, layer, tr, tm, name):
    bsz, seq, _ = x1.shape
    n_tok = bsz * seq
    plan = _plan(route, counts, n_tok, tr)
    ys = _expert_call(h2.reshape((n_tok,) + TOKEN_TILE), plan, *experts, layer, tr, "expert_" + name)
    out = _combine_call(ys, plan[0], route.reshape(n_tok, LANES), x1.reshape(n_tok, D_MODEL), ada, lw, layer,
                        seq, tm, "combine_" + name)
    return out.reshape(bsz, seq, D_MODEL)


def _placement(rows_used):
    e = np.zeros((LANES, MLA_PAD_WIDTH), np.float32)
    for r in range(rows_used):
        for hd in range(MLA_HEADS):
            e[r, hd * MLA_HEAD_PAD + MLA_NOPE_DIM + (r % MLA_ROPE_DIM)] = 1.0
    return jnp.asarray(e, BF16)


def _prepare_weights(p):
    partner, _ = _rope_partner()
    n_layers = p["w_in"].shape[0]
    w = p["w_in"]
    k_r = w[:, :, 2176:2208]
    w_in = jnp.concatenate(
        [w[:, :, 0:512] * NA_SCALE, w[:, :, 512:2176], k_r, k_r[:, :, partner],
         jnp.zeros((n_layers, D_MODEL, LANES - 2 * MLA_ROPE_DIM), F32), w[:, :, 2208:]], axis=2).astype(BF16)
    uq = p["w_uq"].reshape(n_layers, MLA_Q_LORA, MLA_HEADS, MLA_QK_DIM)
    pad = MLA_HEAD_PAD - MLA_QK_DIM
    no_pad = ((0, 0), (0, 0), (0, 0))
    w_uq = jnp.pad(uq, no_pad + ((0, pad),)).reshape(n_layers, MLA_Q_LORA, MLA_PAD_WIDTH).astype(BF16)
    uq_sw = jnp.pad(uq[..., MLA_NOPE_DIM + partner], no_pad + ((MLA_NOPE_DIM, pad),))
    w_uq_sw = uq_sw.reshape(n_layers, MLA_Q_LORA, MLA_PAD_WIDTH).astype(BF16)
    uk = p["w_uk"].reshape(n_layers, MLA_KV_LORA, MLA_HEADS, MLA_NOPE_DIM)
    w_uk = jnp.pad(uk, no_pad + ((0, MLA_HEAD_PAD - MLA_NOPE_DIM),)).reshape(
        n_layers, MLA_KV_LORA, MLA_PAD_WIDTH).astype(BF16)
    n_pad = LANES - N_GROUPS - N_EXPERTS
    w_r = jnp.concatenate([p["router_group_w"], p["router_expert_w"],
                           jnp.zeros((n_layers, D_MODEL, n_pad), F32)], axis=2)
    b_r = jnp.concatenate([p["router_group_b"], p["router_expert_b"], jnp.zeros((n_layers, n_pad), F32)], axis=1)
    w_r_hi = w_r.astype(BF16)

    def row(name):
        return p[name][:, None, :]

    return dict(
        w_in=w_in, w_uq=w_uq, w_uq_sw=w_uq_sw, w_uk=w_uk, w_uv=p["w_uv"].astype(BF16),
        q_norm=row("mla_q_norm"), kv_norm=row("mla_kv_norm"),
        e_plain=_placement(MLA_ROPE_DIM), e_rope=_placement(2 * MLA_ROPE_DIM),
        w_o_na=p["w_o_na"].astype(BF16), w_o_mla=p["w_o_mla"].astype(BF16), w_out=p["w_out"].astype(BF16),
        ln1_g=row("ln1_g"), ln1_b=row("ln1_b"), ln2_g=row("ln2_g"), ln2_b=row("ln2_b"),
        w_r=w_r_hi, w_r_lo=(w_r - w_r_hi.astype(F32)).astype(BF16), b_r=b_r[:, None, :],
    )


def _window_rows(rows, q_rows):
    firsts = np.arange(0, rows, q_rows)
    lo = np.clip(firsts - NA_KR // 2, 0, rows - NA_KR)
    hi = np.clip(firsts + q_rows - 1 - NA_KR // 2, 0, rows - NA_KR) + NA_KR
    k_rows = min(rows, int(np.max(hi - lo)) + int(np.max(hi - lo)) % 2)
    k0 = np.minimum(lo, rows - k_rows)
    assert np.all(k0 % 2 == 0) and np.all(k0 + k_rows >= hi), (k0, k_rows)
    return k_rows, k0


def _na_bias_tables(rpb, rows):
    assert rows >= NA_KR and rows % 2 == 0
    r = np.arange(rows)
    rs = np.clip(r - NA_KR // 2, 0, rows - NA_KR)
    vrow = (r[None, :] >= rs[:, None]) & (r[None, :] < rs[:, None] + NA_KR)
    c = np.arange(GRID_W)
    cs = np.clip(c - NA_KC // 2, 0, GRID_W - NA_KC)
    vcol = (c[None, :] >= cs[:, None]) & (c[None, :] < cs[:, None] + NA_KC)
    coff = c[None, :] - c[:, None] + NA_KC - 1
    onehot = ((coff[None] == np.arange(2 * NA_KC - 1)[:, None, None]) & vcol[None]).astype(np.float32)
    blocks = jnp.einsum("lhdj,jqk->lhdqk", rpb, jnp.asarray(onehot), precision=lax.Precision.HIGHEST)
    blocks = jnp.where(jnp.asarray(vcol), blocks, NEG_INF)
    lead = rpb.shape[:2]
    reach = NA_KR - 1
    by_offset = jnp.concatenate(
        [jnp.full(lead + (rows - reach, GRID_W, GRID_W), NEG_INF, F32), blocks,
         jnp.full(lead + (rows - reach - 1, GRID_W, GRID_W), NEG_INF, F32)], axis=2)
    pair_blocks = jnp.concatenate([by_offset[:, :, :-1], by_offset[:, :, 1:]], axis=-1)
    in_window = np.repeat(vrow.reshape(rows, rows // 2, 2), GRID_W, axis=-1)
    row_masks = np.where(in_window, 0.0, NEG_INF).astype(np.float32).reshape(rows * rows // 2, 2 * GRID_W)
    return pair_blocks, jnp.asarray(row_masks)


def kernel(x_prompt, x_sample, cache_na_k, cache_na_v, cache_mla_ckv, cache_mla_kr, c, c_ctx, w_in, mla_q_norm, mla_kv_norm, w_uq, w_uk, w_uv, na_rpb, w_o_na, w_o_mla, w_out, w_ada, b_ada, ln1_g, ln1_b, ln2_g, ln2_b, router_group_w, router_group_b, router_expert_w, router_expert_b, expert_w_gate, expert_w_up, expert_w_down):
    p = dict(w_in=w_in, mla_q_norm=mla_q_norm, mla_kv_norm=mla_kv_norm, w_uq=w_uq, w_uk=w_uk, w_uv=w_uv,
             w_o_na=w_o_na, w_o_mla=w_o_mla, w_out=w_out, ln1_g=ln1_g, ln1_b=ln1_b, ln2_g=ln2_g, ln2_b=ln2_b,
             router_group_w=router_group_w, router_group_b=router_group_b,
             router_expert_w=router_expert_w, router_expert_b=router_expert_b,
             expert_w_gate=expert_w_gate, expert_w_up=expert_w_up, expert_w_down=expert_w_down)
    pb, ps, _ = x_prompt.shape
    sb, ss, _ = x_sample.shape
    past = cache_na_k.shape[2]

    n_cond = 16
    cond = jnp.zeros((n_cond, D_MODEL), F32).at[0].set(c_ctx).at[1:1 + sb].set(c)
    ada_all = _ada_call(cond, w_ada, b_ada[:, None, :]).reshape(DEPTH, n_cond, 6, D_MODEL)
    ada_all = jnp.pad(ada_all, ((0, 0), (0, 0), (0, 2), (0, 0)))
    ada_p = (ada_all, 0, 0)
    ada_s = (ada_all, 1, 1)

    cos32, sin32 = _rope_tables(ss)
    head_c = np.concatenate([np.ones((ss, MLA_NOPE_DIM)), cos32, np.zeros((ss, MLA_HEAD_PAD - MLA_QK_DIM))], -1)
    head_s = np.concatenate([np.zeros((ss, MLA_NOPE_DIM)), sin32, np.zeros((ss, MLA_HEAD_PAD - MLA_QK_DIM))], -1)
    rope_tabs = tuple(jnp.asarray(t, F32) for t in (
        head_c * MLA_SCALE, head_s * MLA_SCALE,
        np.concatenate([cos32, sin32, np.zeros((ss, LANES - 2 * MLA_ROPE_DIM))], -1)))

    cna_k = cache_na_k.reshape(sb, DEPTH, past, NA_WIDTH)
    cna_v = cache_na_v.reshape(sb, DEPTH, past, NA_WIDTH)
    ckr_pad = jnp.pad(cache_mla_kr, ((0, 0), (0, 0), (0, 0), (0, LANES - MLA_ROPE_DIM)))

    experts = (expert_w_gate, expert_w_up, expert_w_down)
    lw = _prepare_weights(p)
    tq = 512
    grid_rows = ss // GRID_W
    local_bias = _na_bias_tables(na_rpb, grid_rows) + (_window_rows(grid_rows, tq // GRID_W)[0],)
    xp, xs = x_prompt, x_sample
    new_k, new_v, new_ckv, new_kr = [], [], [], []
    for l in range(DEPTH):

        qn, kn, vn, qm, km, vm, gates, ckv, k_r = _proj_call(xp, ada_p, lw, l, None, F32, True, ps)
        o_na = _attn_call(qn, [(kn, vn, _seg3(kn, NA_WIDTH), _seg3(vn, NA_WIDTH))], None, NA_HEAD_DIM, ps,
                          NA_HEADS, "attn_ctx_na")
        o_mla = _attn_call(qm, [(km, vm, _seg3(km, MLA_PAD_WIDTH), _seg3(vm, NA_WIDTH))], None, MLA_HEAD_PAD, ps,
                           MLA_HEADS, "attn_ctx_mla")
        x1, h2, route, counts = _merge_call(o_na, o_mla, gates, xp, ada_p, lw, l, ps, "merge_ctx")
        xp = _moe(h2, route, counts, x1, ada_p, lw, experts, l, 128, ps, "ctx")
        new_k.append(kn)
        new_v.append(vn)
        new_ckv.append(ckv)
        new_kr.append(k_r)

        qn, kn, vn, qm, km, vm, gates = _proj_call(xs, ada_s, lw, l, rope_tabs, BF16, False, 512)
        km_ctx, vm_ctx = _ctxkv_call(cache_mla_ckv, ckr_pad, l, lw)
        o_na = _attn_call(
            qn, [(cna_k, cna_v, _seg4(cna_k, l, 128), _seg4(cna_v, l, 128)),
                 (kn, vn, _seg3(kn, 128), _seg3(vn, 128))], local_bias, NA_HEAD_DIM, tq, 2, "attn_lat_na", l)
        o_mla = _attn_call(
            qm, [(km_ctx, vm_ctx, _seg3(km_ctx, 256), _seg3(vm_ctx, 128)),
                 (km, vm, _seg3(km, 256), _seg3(vm, 128))], None, MLA_HEAD_PAD, ss, 2, "attn_lat_mla")
        x1, h2, route, counts = _merge_call(o_na, o_mla, gates, xs, ada_s, lw, l, 512, "merge_lat")
        xs = _moe(h2, route, counts, x1, ada_s, lw, experts, l, 256, 256, "lat")

    def stack(parts, tail):
        return jnp.stack(parts, axis=1).reshape((pb, DEPTH, ps) + tail)

    return (xp, xs, stack(new_k, (NA_HEADS, NA_HEAD_DIM)), stack(new_v, (NA_HEADS, NA_HEAD_DIM)),
            stack(new_ckv, (MLA_KV_LORA,)), stack(new_kr, (MLA_ROPE_DIM,)))
```

```python
import functools

import numpy as np
import jax
import jax.numpy as jnp
from jax import lax
from jax.experimental import pallas as pl
from jax.experimental.pallas import tpu as pltpu

F32 = jnp.float32
BF16 = jnp.bfloat16

D_MODEL = 1024
DEPTH = 2
GRID_W = 64
NA_HEADS = 8
NA_HEAD_DIM = 64
NA_KR = 8
NA_KC = 16
NA_WIDTH = NA_HEADS * NA_HEAD_DIM
NA_SCALE = NA_HEAD_DIM ** -0.5
MLA_HEADS = 8
MLA_NOPE_DIM = 64
MLA_ROPE_DIM = 32
MLA_V_DIM = 64
MLA_Q_LORA = 384
MLA_KV_LORA = 256
MLA_QK_DIM = MLA_NOPE_DIM + MLA_ROPE_DIM
MLA_SCALE = MLA_QK_DIM ** -0.5
ROPE_BASE = 10000.0
N_GROUPS = 4
EXPERTS_PER_GROUP = 8
N_EXPERTS = N_GROUPS * EXPERTS_PER_GROUP
EXPERT_FF = 256
LN_EPS = 1e-5
RMS_EPS = 1e-6
NEG_INF = -1e30
LOG2E = 1.4426950408889634
DEEPNORM_ALPHA = (2 * DEPTH) ** 0.25

LANES = 128
MLA_HEAD_PAD = LANES
MLA_PAD_WIDTH = MLA_HEADS * MLA_HEAD_PAD
C_Q = 0
C_K = C_Q + NA_WIDTH
C_V = C_K + NA_WIDTH
C_CQ = C_V + NA_WIDTH
C_CKV = C_CQ + MLA_Q_LORA
C_KR = C_CKV + MLA_KV_LORA
C_GATE = C_KR + LANES
C_END = C_GATE + 2 * D_MODEL
R_EXP0 = N_GROUPS

VMEM_LIMIT = 56 * 1024 * 1024


def _rope_partner():
    j = np.arange(MLA_ROPE_DIM)
    n_freq = MLA_ROPE_DIM // 4
    axis, within = j // (2 * n_freq), j % (2 * n_freq)
    half, f = within // n_freq, within % n_freq
    return axis * 2 * n_freq + (1 - half) * n_freq + f, half


def _rope_tables(n_tokens):
    n_freq = MLA_ROPE_DIM // 4
    inv_freq = ROPE_BASE ** (-np.arange(n_freq, dtype=np.float64) / n_freq)
    t = np.arange(n_tokens)
    ang = np.concatenate([(t // GRID_W)[:, None] * inv_freq, (t % GRID_W)[:, None] * inv_freq], axis=-1)
    cos, sin = np.cos(ang), np.sin(ang)
    cos32 = np.concatenate([cos[:, :n_freq], cos[:, :n_freq], cos[:, n_freq:], cos[:, n_freq:]], axis=-1)
    sin32 = np.concatenate([-sin[:, :n_freq], sin[:, :n_freq], -sin[:, n_freq:], sin[:, n_freq:]], axis=-1)
    return cos32, sin32


def _ln_plain(x):
    mu = jnp.mean(x, axis=-1, keepdims=True)
    xc = x - mu
    var = jnp.mean(xc * xc, axis=-1, keepdims=True)
    return xc * lax.rsqrt(var + LN_EPS)


def _rms(x, g):
    return x * lax.rsqrt(jnp.mean(x * x, axis=-1, keepdims=True) + RMS_EPS) * g


def _sigmoid(x):
    return 1.0 / (1.0 + jnp.exp(-x))


def _dot(a, b):
    return jnp.dot(a, b, preferred_element_type=F32)


def _params(sem):
    return pltpu.CompilerParams(dimension_semantics=sem, vmem_limit_bytes=VMEM_LIMIT)


def _layer_spec(arr, layer):
    zeros = (0,) * (arr.ndim - 1)
    return pl.BlockSpec((None,) + arr.shape[1:], lambda *_: (layer,) + zeros, pipeline_mode=pl.Buffered(1))


def _ada_spec(ada, layer, batch_of):
    _, first, step = ada
    return pl.BlockSpec((None, None, 8, D_MODEL), lambda *idx: (layer, first + step * batch_of(*idx), 0, 0))


def _const_spec(arr):
    zeros = (0,) * arr.ndim
    return pl.BlockSpec(arr.shape, lambda *_: zeros, pipeline_mode=pl.Buffered(1))


TOKEN_TILE = (D_MODEL // LANES, LANES)


def _store_token_tiles(ref, val):
    chunks = jnp.stack([val[:, k * LANES:(k + 1) * LANES] for k in range(TOKEN_TILE[0])], axis=0)
    ref[...] = jnp.swapaxes(chunks, 0, 1)


def _load_token_tiles(ref):
    chunks = jnp.swapaxes(ref[...], 0, 1)
    return jnp.concatenate([chunks[k] for k in range(TOKEN_TILE[0])], axis=-1)


def _ada_body(c_ref, w_ref, b_ref, o_ref):
    c = c_ref[...]
    s = (c * _sigmoid(c)).astype(BF16)
    o_ref[0] = _dot(s, w_ref[0].astype(BF16)) + b_ref[0]


def _ada_call(cond, w_ada, b_ada):
    n_rows = cond.shape[0]
    tn = 1024
    return pl.pallas_call(
        _ada_body,
        grid=(DEPTH, 6 * D_MODEL // tn),
        in_specs=[
            pl.BlockSpec((n_rows, D_MODEL), lambda l, j: (0, 0)),
            pl.BlockSpec((1, D_MODEL, tn), lambda l, j: (l, 0, j)),
            pl.BlockSpec((1, 1, tn), lambda l, j: (l, 0, j)),
        ],
        out_specs=pl.BlockSpec((1, n_rows, tn), lambda l, j: (l, 0, j)),
        out_shape=jax.ShapeDtypeStruct((DEPTH, n_rows, 6 * D_MODEL), F32),
        compiler_params=_params(("arbitrary", "arbitrary")),
        name="ada",
    )(cond, w_ada, b_ada)


def _proj_body(*refs, rope, emit_cache):
    it = iter(refs)
    x_ref, ada_ref, w_ref, qn_ref, kvn_ref, wuq_ref = (next(it) for _ in range(6))
    wuqs_ref = next(it) if rope else None
    wuk_ref, wuv_ref, e_ref = next(it), next(it), next(it)
    if rope:
        cq_ref, sq_ref, tk_ref = next(it), next(it), next(it)
    oq_ref, ok_ref, ov_ref, oqm_ref, okm_ref, ovm_ref, og_ref = (next(it) for _ in range(7))
    if emit_cache:
        ockv_ref, okr_ref = next(it), next(it)

    x = x_ref[...]
    sh = ada_ref[0:1, :]
    sc = ada_ref[1:2, :]
    h = (_ln_plain(x) * (1.0 + sc) + sh).astype(BF16)

    oq_ref[...] = (_dot(h, w_ref[:, C_Q:C_K]) * LOG2E).astype(oq_ref.dtype)
    ok_ref[...] = _dot(h, w_ref[:, C_K:C_V]).astype(ok_ref.dtype)
    ov_ref[...] = _dot(h, w_ref[:, C_V:C_CQ]).astype(ov_ref.dtype)

    c_q = _rms(_dot(h, w_ref[:, C_CQ:C_CKV]), qn_ref[...]).astype(BF16)
    q = _dot(c_q, wuq_ref[...])
    if rope:
        q = (q * jnp.tile(cq_ref[...], (1, MLA_HEADS))
             + _dot(c_q, wuqs_ref[...]) * jnp.tile(sq_ref[...], (1, MLA_HEADS)))
    else:
        q = q * (MLA_SCALE * LOG2E)
    oqm_ref[...] = q.astype(BF16)

    c_kv = _rms(_dot(h, w_ref[:, C_CKV:C_KR]), kvn_ref[...])
    slab = _dot(h, w_ref[:, C_KR:C_GATE])
    if emit_cache:
        ockv_ref[...] = c_kv
        okr_ref[...] = slab[:, :MLA_ROPE_DIM]
    if rope:
        slab = slab * tk_ref[...]
    c_kv = c_kv.astype(BF16)
    okm_ref[...] = (_dot(c_kv, wuk_ref[...]) + _dot(slab.astype(BF16), e_ref[...])).astype(BF16)
    ovm_ref[...] = _dot(c_kv, wuv_ref[...]).astype(BF16)

    og_ref[...] = _sigmoid(_dot(h, w_ref[:, C_GATE:C_END])).astype(BF16)


def _proj_call(x, ada, lw, layer, rope_tabs, kv_dtype, emit_cache, tm):
    bsz, seq, _ = x.shape
    nj = seq // tm
    rope = rope_tabs is not None

    def tok(width):
        return pl.BlockSpec((None, tm, width), lambda b, j: (b, j, 0))

    def full(arr):
        return _layer_spec(arr, layer)

    def tab(width):
        return pl.BlockSpec((tm, width), lambda b, j: (j, 0))

    args = [x, ada[0], lw["w_in"], lw["q_norm"], lw["kv_norm"], lw["w_uq"]]
    specs = [tok(D_MODEL), _ada_spec(ada, layer, lambda b, j: b),
             full(lw["w_in"]), full(lw["q_norm"]), full(lw["kv_norm"]), full(lw["w_uq"])]
    if rope:
        args.append(lw["w_uq_sw"])
        specs.append(full(lw["w_uq_sw"]))
    e_mat = lw["e_rope"] if rope else lw["e_plain"]
    args += [lw["w_uk"], lw["w_uv"], e_mat]
    specs += [full(lw["w_uk"]), full(lw["w_uv"]), _const_spec(e_mat)]
    if rope:
        args += list(rope_tabs)
        specs += [tab(MLA_HEAD_PAD), tab(MLA_HEAD_PAD), tab(LANES)]

    def shp(width, dt):
        return jax.ShapeDtypeStruct((bsz, seq, width), dt)

    out_shape = [shp(NA_WIDTH, BF16), shp(NA_WIDTH, kv_dtype), shp(NA_WIDTH, kv_dtype),
                 shp(MLA_PAD_WIDTH, BF16), shp(MLA_PAD_WIDTH, BF16), shp(NA_WIDTH, BF16),
                 shp(2 * D_MODEL, BF16)]
    out_specs = [tok(NA_WIDTH), tok(NA_WIDTH), tok(NA_WIDTH), tok(MLA_PAD_WIDTH), tok(MLA_PAD_WIDTH),
                 tok(NA_WIDTH), tok(2 * D_MODEL)]
    if emit_cache:
        out_shape += [shp(MLA_KV_LORA, F32), shp(MLA_ROPE_DIM, F32)]
        out_specs += [tok(MLA_KV_LORA), tok(MLA_ROPE_DIM)]
    return pl.pallas_call(
        functools.partial(_proj_body, rope=rope, emit_cache=emit_cache),
        grid=(bsz, nj),
        in_specs=specs,
        out_specs=out_specs,
        out_shape=out_shape,
        compiler_params=_params(("arbitrary", "arbitrary")),
        name="proj_lat" if rope else "proj_ctx",
    )(*args)


def _ctxkv_body(ckv_ref, kr_ref, wuk_ref, wuv_ref, e_ref, okm_ref, ovm_ref):
    c_kv = ckv_ref[...].astype(BF16)
    okm_ref[...] = (_dot(c_kv, wuk_ref[...]) + _dot(kr_ref[...].astype(BF16), e_ref[...])).astype(BF16)
    ovm_ref[...] = _dot(c_kv, wuv_ref[...]).astype(BF16)


def _ctxkv_call(cache_ckv, cache_kr_pad, layer, lw):
    bsz, _, past, _ = cache_ckv.shape

    def full(arr):
        return _layer_spec(arr, layer)

    return pl.pallas_call(
        _ctxkv_body,
        grid=(bsz,),
        in_specs=[
            pl.BlockSpec((None, None, past, MLA_KV_LORA), lambda b: (b, layer, 0, 0)),
            pl.BlockSpec((None, None, past, LANES), lambda b: (b, layer, 0, 0)),
            full(lw["w_uk"]), full(lw["w_uv"]), _const_spec(lw["e_plain"]),
        ],
        out_specs=[pl.BlockSpec((None, past, MLA_PAD_WIDTH), lambda b: (b, 0, 0)),
                   pl.BlockSpec((None, past, NA_WIDTH), lambda b: (b, 0, 0))],
        out_shape=[jax.ShapeDtypeStruct((bsz, past, MLA_PAD_WIDTH), BF16),
                   jax.ShapeDtypeStruct((bsz, past, NA_WIDTH), BF16)],
        compiler_params=_params(("arbitrary",)),
        name="ctxkv",
    )(cache_ckv, cache_kr_pad, lw["w_uk"], lw["w_uv"], lw["e_plain"])


def _local_bias(pb_ref, rm_ref, hh, q0, k0, q_rows, k_rows, grid_rows):
    rows = []
    for qr in range(q_rows):
        blocks = []
        for p in range(k_rows // 2):
            pair = k0 // 2 + p
            e = 2 * pair - (q0 + qr) + grid_rows
            blocks.append(pb_ref[hh, e] + rm_ref[pl.ds((q0 + qr) * (grid_rows // 2) + pair, 1), :])
        rows.append(jnp.concatenate(blocks, axis=-1))
    return jnp.concatenate(rows, axis=0)


def _attn_body(*refs, n_seg, dq, local, heads):
    q_ref = refs[0]
    seg_refs = [(refs[1 + 2 * i], refs[2 + 2 * i]) for i in range(n_seg)]
    o_ref = refs[-1]
    if local is not None:
        pb_ref, rm_ref = refs[1 + 2 * n_seg], refs[2 + 2 * n_seg]
        q_rows, k_rows, grid_rows = local
        q0 = pl.program_id(1) * q_rows
        k0 = jnp.minimum(jnp.clip(q0 - NA_KR // 2, 0, grid_rows - NA_KR), grid_rows - k_rows)
        keys = pl.ds(pl.multiple_of(k0 * GRID_W, GRID_W), k_rows * GRID_W)
    outs = []
    for hh in range(heads):
        qh = q_ref[:, hh * dq:(hh + 1) * dq]
        scores = []
        values = []
        for i, (k_ref, v_ref) in enumerate(seg_refs):
            windowed = local is not None and i == n_seg - 1
            rows = keys if windowed else slice(None)
            kh = k_ref[rows, hh * dq:(hh + 1) * dq].astype(BF16)
            values.append(v_ref[rows, hh * MLA_V_DIM:(hh + 1) * MLA_V_DIM].astype(BF16))
            s = lax.dot_general(qh, kh, (((1,), (1,)), ((), ())), preferred_element_type=F32)
            if windowed:
                s = s + _local_bias(pb_ref, rm_ref, hh, q0, k0, q_rows, k_rows, grid_rows)
            scores.append(s)
        m = functools.reduce(jnp.maximum, [jnp.max(s, axis=-1, keepdims=True) for s in scores])
        den = None
        acc = None
        for s, vh in zip(scores, values):
            e = jnp.exp2(s - m)
            d = jnp.sum(e, axis=-1, keepdims=True)
            a = _dot(e.astype(BF16), vh)
            den = d if den is None else den + d
            acc = a if acc is None else acc + a
        outs.append(acc / den)
    o_ref[...] = jnp.concatenate(outs, axis=-1).astype(o_ref.dtype)


def _attn_call(q, segs, local_bias, dq, tq, heads, name, layer=0):
    bsz, seq, _ = q.shape
    args = [q]
    specs = [pl.BlockSpec((None, tq, heads * dq), lambda hp, qt, b: (b, qt, hp))]
    for k, v, ks, vs in segs:
        args += [k, v]
        specs += [ks, vs]
    local = None
    if local_bias is not None:
        pair_blocks, row_masks, k_rows = local_bias
        args += [pair_blocks, row_masks]
        specs += [pl.BlockSpec((None, heads) + pair_blocks.shape[2:], lambda hp, qt, b: (layer, hp, 0, 0, 0),
                               pipeline_mode=pl.Buffered(1)),
                  _const_spec(row_masks)]
        local = (tq // GRID_W, k_rows, seq // GRID_W)
    return pl.pallas_call(
        functools.partial(_attn_body, n_seg=len(segs), dq=dq, local=local, heads=heads),
        grid=(NA_HEADS // heads, seq // tq, bsz),
        in_specs=specs,
        out_specs=pl.BlockSpec((None, tq, heads * MLA_V_DIM), lambda hp, qt, b: (b, qt, hp)),
        out_shape=jax.ShapeDtypeStruct((bsz, seq, NA_WIDTH), BF16),
        compiler_params=_params(("arbitrary", "arbitrary", "arbitrary")),
        name=name,
    )(*args)


def _seg3(arr, width):
    n = arr.shape[1]
    return pl.BlockSpec((None, n, width), lambda hp, qt, b: (b, 0, hp))


def _seg4(arr, layer, width):
    n = arr.shape[2]
    return pl.BlockSpec((None, None, n, width), lambda hp, qt, b: (b, layer, 0, hp))


def _route(logits):
    lane = lax.broadcasted_iota(jnp.int32, logits.shape, 1)
    lane_f = lane.astype(F32)
    far = float(LANES)
    is_g = lane < N_GROUPS
    gl = jnp.where(is_g, logits, NEG_INF)
    gmax = jnp.max(gl, axis=-1, keepdims=True)
    gsum = jnp.sum(jnp.exp(gl - gmax), axis=-1, keepdims=True)
    g_w = 1.0 / gsum
    g_idx = jnp.min(jnp.where(gl == gmax, lane_f, far), axis=-1, keepdims=True)
    e_group = ((lane - R_EXP0) >> 3).astype(F32)
    is_e = (lane >= R_EXP0) & (lane < R_EXP0 + N_EXPERTS) & (e_group == g_idx)
    el = jnp.where(is_e, logits, NEG_INF)
    m1 = jnp.max(el, axis=-1, keepdims=True)
    i1 = jnp.min(jnp.where(el == m1, lane_f, far), axis=-1, keepdims=True)
    el2 = jnp.where(lane_f == i1, NEG_INF, el)
    m2 = jnp.max(el2, axis=-1, keepdims=True)
    i2 = jnp.min(jnp.where(el2 == m2, lane_f, far), axis=-1, keepdims=True)
    r = jnp.exp(m2 - m1)
    w1 = g_w / (1.0 + r)
    w2 = g_w * r / (1.0 + r)
    return lane_f, i1, i2, w1, w2


def _merge_body(ona_ref, omla_ref, g_ref, x_ref, ada_ref, wna_ref, wmla_ref, wout_ref, lng_ref, lnb_ref,
                wr_ref, wrl_ref, br_ref, ox_ref, oh_ref, or_ref, ocnt_ref, carry_ref):
    first = (pl.program_id(0) == 0) & (pl.program_id(1) == 0)

    @pl.when(first)
    def _():
        carry_ref[...] = jnp.zeros_like(carry_ref)

    y_na = _dot(ona_ref[...], wna_ref[...])
    y_mla = _dot(omla_ref[...], wmla_ref[...])
    g_na = g_ref[:, :D_MODEL].astype(F32)
    g_mla = g_ref[:, D_MODEL:].astype(F32)
    y = _dot((g_na * y_na + g_mla * y_mla).astype(BF16), wout_ref[...])
    g1 = ada_ref[2:3, :]
    x1 = _ln_plain(DEEPNORM_ALPHA * x_ref[...] + g1 * y) * lng_ref[...] + lnb_ref[...]
    ox_ref[...] = x1
    h2 = _ln_plain(x1) * (1.0 + ada_ref[4:5, :]) + ada_ref[3:4, :]
    _store_token_tiles(oh_ref, h2)
    h_hi = h2.astype(BF16)
    h_lo = (h2 - h_hi.astype(F32)).astype(BF16)
    logits = (_dot(h_hi, wr_ref[...]) + (_dot(h_lo, wr_ref[...]) + _dot(h_hi, wrl_ref[...]))) + br_ref[...]
    lane_f, i1, i2, w1, w2 = _route(logits)

    tm = logits.shape[0]
    sel1 = lane_f == i1
    sel2 = lane_f == i2
    used = (jnp.where(sel1, 1.0, 0.0) + jnp.where(sel2, 1.0, 0.0)).astype(BF16)
    row = lax.broadcasted_iota(jnp.int32, (tm, tm), 0)
    col = lax.broadcasted_iota(jnp.int32, (tm, tm), 1)
    earlier = jnp.where(row > col, 1.0, 0.0).astype(BF16)
    before = _dot(earlier, used) + carry_ref[0:1, :]
    rank1 = jnp.sum(jnp.where(sel1, before, 0.0), axis=-1, keepdims=True)
    rank2 = jnp.sum(jnp.where(sel2, before, 0.0), axis=-1, keepdims=True)
    carry_ref[...] += _dot(jnp.ones((8, tm), BF16), used)
    ocnt_ref[...] = carry_ref[...]

    fields = (i1 - R_EXP0, i2 - R_EXP0, w1, w2, rank1, rank2)
    route = jnp.zeros_like(logits)
    for k, val in enumerate(fields):
        route = jnp.where(lane_f == float(k), val, route)
    or_ref[...] = route


def _merge_call(o_na, o_mla, gates, x, ada, lw, layer, tm, name):
    bsz, seq, _ = x.shape

    def tok(width):
        return pl.BlockSpec((None, tm, width), lambda b, j: (b, j, 0))

    def full(arr):
        return _layer_spec(arr, layer)

    ws = [lw["w_o_na"], lw["w_o_mla"], lw["w_out"], lw["ln1_g"], lw["ln1_b"], lw["w_r"], lw["w_r_lo"], lw["b_r"]]
    return pl.pallas_call(
        _merge_body,
        grid=(bsz, seq // tm),
        in_specs=[tok(NA_WIDTH), tok(NA_WIDTH), tok(2 * D_MODEL), tok(D_MODEL),
                  _ada_spec(ada, layer, lambda b, j: b)] + [full(w) for w in ws],
        out_specs=[tok(D_MODEL), pl.BlockSpec((None, tm) + TOKEN_TILE, lambda b, j: (b, j, 0, 0)), tok(LANES),
                   pl.BlockSpec((8, LANES), lambda b, j: (0, 0))],
        out_shape=[jax.ShapeDtypeStruct((bsz, seq, D_MODEL), F32),
                   jax.ShapeDtypeStruct((bsz, seq) + TOKEN_TILE, F32),
                   jax.ShapeDtypeStruct((bsz, seq, LANES), F32),
                   jax.ShapeDtypeStruct((8, LANES), F32)],
        scratch_shapes=[pltpu.VMEM((8, LANES), F32)],
        compiler_params=_params(("arbitrary", "arbitrary")),
        name=name,
    )(o_na, o_mla, gates, x, ada[0], *ws)


POS_BITS = 16
POS_MASK = (1 << POS_BITS) - 1


def _plan(route, counts, n_tok, tr):
    r = route.reshape(n_tok, LANES)
    e1, e2 = r[:, 0].astype(jnp.int32), r[:, 1].astype(jnp.int32)
    rk1, rk2 = r[:, 4].astype(jnp.int32), r[:, 5].astype(jnp.int32)
    cnt = counts[0, R_EXP0:R_EXP0 + N_EXPERTS].astype(jnp.int32)
    tiles_e = (cnt + tr - 1) // tr
    tile_end = jnp.cumsum(tiles_e)
    row_off = (tile_end - tiles_e) * tr
    eid = jnp.arange(N_EXPERTS, dtype=jnp.int32)
    p1 = jnp.sum(jnp.where(e1[:, None] == eid, row_off, 0), axis=-1) + rk1
    p2 = jnp.sum(jnp.where(e2[:, None] == eid, row_off, 0), axis=-1) + rk2
    n_tiles = 2 * n_tok // tr + N_EXPERTS
    n_used = tile_end[-1]
    ti = jnp.arange(n_tiles, dtype=jnp.int32)
    tile_expert = jnp.sum(jnp.minimum(ti, n_used - 1)[:, None] >= tile_end[None, :], axis=-1).astype(jnp.int32)
    own = tile_expert[:, None] == eid
    seg_end = jnp.sum(jnp.where(own, row_off + cnt, 0), axis=-1)
    tile_rows = jnp.clip(seg_end - ti * tr, 1, tr).astype(jnp.int32)
    return p1 | (p2 << POS_BITS), tile_expert, tile_rows, jnp.stack([n_used, n_used]).astype(jnp.int32), n_tiles


ROW_GROUP = 8


def _expert_body(pos_ref, te_ref, rows_ref, meta_ref, h_hbm, wg_ref, wu_ref, wd_ref, o_ref,
                 src_ref, hbuf, sem, wg_b, wu_b, wd_b, *, n_tok, tr):
    i = pl.program_id(0)
    n_used = meta_ref[0]

    def for_tile_rows(tile, slot, fn):
        base = tile * tr
        last = rows_ref[tile] - 1
        dst = hbuf.at[slot]
        dsem = sem.at[slot]

        def body(g, carry):
            for k in range(ROW_GROUP):
                r = g * ROW_GROUP + k
                t = src_ref[base + jnp.minimum(r, last)]
                fn(pltpu.make_async_copy(h_hbm.at[t], dst.at[r], dsem))
            return carry
        lax.fori_loop(0, (last + ROW_GROUP) // ROW_GROUP, body, 0)

    @pl.when(i == 0)
    def _():
        hbuf[...] = jnp.zeros_like(hbuf)

        def place(t, carry):
            packed = pos_ref[t]
            src_ref[packed & POS_MASK] = t
            src_ref[packed >> POS_BITS] = t
            return carry
        lax.fori_loop(0, n_tok, place, 0, unroll=8)
        for_tile_rows(0, 0, lambda cp: cp.start())

    @pl.when(i + 1 < n_used)
    def _():
        for_tile_rows(i + 1, (i + 1) & 1, lambda cp: cp.start())

    @pl.when(i >= n_used)
    def _():
        o_ref[...] = jnp.zeros_like(o_ref)

    @pl.when(i < n_used)
    def _():
        slot = i & 1
        for_tile_rows(i, slot, lambda cp: cp.wait())

        @pl.when((i == 0) | (te_ref[i] != te_ref[jnp.maximum(i - 1, 0)]))
        def _():
            wg_b[...] = wg_ref[...].astype(BF16)
            wu_b[...] = wu_ref[...].astype(BF16)
            wd_b[...] = wd_ref[...].astype(BF16)

        h = _load_token_tiles(hbuf.at[slot]).astype(BF16)
        gate = _dot(h, wg_b[...])
        up = _dot(h, wu_b[...])
        hid = (gate * _sigmoid(gate) * up).astype(BF16)
        _store_token_tiles(o_ref, _dot(hid, wd_b[...]))


def _expert_call(h2, plan, w_gate, w_up, w_down, layer, tr, name):
    packed, tile_expert, tile_rows, meta, n_tiles = plan
    n_tok = h2.shape[0]
    n_rows = n_tiles * tr
    return pl.pallas_call(
        functools.partial(_expert_body, n_tok=n_tok, tr=tr),
        grid_spec=pltpu.PrefetchScalarGridSpec(
            num_scalar_prefetch=4,
            grid=(n_tiles,),
            in_specs=[
                pl.BlockSpec(memory_space=pl.ANY),
                pl.BlockSpec((None, None, D_MODEL, EXPERT_FF), lambda i, pos, te, nr, meta: (layer, te[i], 0, 0)),
                pl.BlockSpec((None, None, D_MODEL, EXPERT_FF), lambda i, pos, te, nr, meta: (layer, te[i], 0, 0)),
                pl.BlockSpec((None, None, EXPERT_FF, D_MODEL), lambda i, pos, te, nr, meta: (layer, te[i], 0, 0)),
            ],
            out_specs=pl.BlockSpec((tr,) + TOKEN_TILE, lambda i, pos, te, nr, meta: (i, 0, 0)),
            scratch_shapes=[
                pltpu.SMEM((n_rows,), jnp.int32),
                pltpu.VMEM((2, tr) + TOKEN_TILE, F32),
                pltpu.SemaphoreType.DMA((2,)),
                pltpu.VMEM((D_MODEL, EXPERT_FF), BF16),
                pltpu.VMEM((D_MODEL, EXPERT_FF), BF16),
                pltpu.VMEM((EXPERT_FF, D_MODEL), BF16),
            ],
        ),
        out_shape=jax.ShapeDtypeStruct((n_rows,) + TOKEN_TILE, F32),
        compiler_params=_params(("arbitrary",)),
        name=name,
    )(packed, tile_expert, tile_rows, meta, h2, w_gate, w_up, w_down)


def _combine_body(pos_ref, y_hbm, r_ref, x_ref, ada_ref, lng_ref, lnb_ref, o_ref, ybuf, sem, *, tm):
    i = pl.program_id(0)
    n = pl.num_programs(0)

    def for_tile_rows(tile, slot, fn):
        base = tile * tm
        dst1 = ybuf.at[slot, 0]
        dst2 = ybuf.at[slot, 1]
        dsem = sem.at[slot]

        def body(g, carry):
            for k in range(ROW_GROUP):
                r = g * ROW_GROUP + k
                packed = pos_ref[base + r]
                fn(pltpu.make_async_copy(y_hbm.at[packed & POS_MASK], dst1.at[r], dsem))
                fn(pltpu.make_async_copy(y_hbm.at[packed >> POS_BITS], dst2.at[r], dsem))
            return carry
        lax.fori_loop(0, tm // ROW_GROUP, body, 0)

    @pl.when(i == 0)
    def _():
        for_tile_rows(0, 0, lambda cp: cp.start())

    @pl.when(i + 1 < n)
    def _():
        for_tile_rows(i + 1, (i + 1) & 1, lambda cp: cp.start())

    slot = i & 1
    for_tile_rows(i, slot, lambda cp: cp.wait())

    route = r_ref[...]
    lane = lax.broadcasted_iota(jnp.int32, route.shape, 1)
    w1 = jnp.sum(jnp.where(lane == 2, route, 0.0), axis=-1, keepdims=True)
    w2 = jnp.sum(jnp.where(lane == 3, route, 0.0), axis=-1, keepdims=True)
    y = w1 * _load_token_tiles(ybuf.at[slot, 0]) + w2 * _load_token_tiles(ybuf.at[slot, 1])
    g2 = ada_ref[5:6, :]
    o_ref[...] = _ln_plain(DEEPNORM_ALPHA * x_ref[...] + g2 * y) * lng_ref[...] + lnb_ref[...]


def _combine_call(ys, packed, route, x1, ada, lw, layer, seq, tm, name):
    n_tok = x1.shape[0]
    per_batch = seq // tm

    def tok(width):
        return pl.BlockSpec((tm, width), lambda i, pos: (i, 0))

    def full(arr):
        return _layer_spec(arr, layer)

    return pl.pallas_call(
        functools.partial(_combine_body, tm=tm),
        grid_spec=pltpu.PrefetchScalarGridSpec(
            num_scalar_prefetch=1,
            grid=(n_tok // tm,),
            in_specs=[pl.BlockSpec(memory_space=pl.ANY), tok(LANES), tok(D_MODEL),
                      _ada_spec(ada, layer, lambda i, pos: i // per_batch),
                      full(lw["ln2_g"]), full(lw["ln2_b"])],
            out_specs=tok(D_MODEL),
            scratch_shapes=[pltpu.VMEM((2, 2, tm) + TOKEN_TILE, F32), pltpu.SemaphoreType.DMA((2,))],
        ),
        out_shape=jax.ShapeDtypeStruct((n_tok, D_MODEL), F32),
        compiler_params=_params(("arbitrary",)),
        name=name,
    )(packed, ys, route, x1, ada[0], lw["ln2_g"], lw["ln2_b"])


def _moe(h2, route, counts, x1, ada, lw, experts, layer, tr, tm, name):
    bsz, seq, _ = x1.shape
    n_tok = bsz * seq
    plan = _plan(route, counts, n_tok, tr)
    ys = _expert_call(h2.reshape((n_tok,) + TOKEN_TILE), plan, *experts, layer, tr, "expert_" + name)
    out = _combine_call(ys, plan[0], route.reshape(n_tok, LANES), x1.reshape(n_tok, D_MODEL), ada, lw, layer,
                        seq, tm, "combine_" + name)
    return out.reshape(bsz, seq, D_MODEL)


def _placement(rows_used):
    e = np.zeros((LANES, MLA_PAD_WIDTH), np.float32)
    for r in range(rows_used):
        for hd in range(MLA_HEADS):
            e[r, hd * MLA_HEAD_PAD + MLA_NOPE_DIM + (r % MLA_ROPE_DIM)] = 1.0
    return jnp.asarray(e, BF16)


def _prepare_weights(p):
    partner, _ = _rope_partner()
    n_layers = p["w_in"].shape[0]
    w = p["w_in"]
    k_r = w[:, :, 2176:2208]
    w_in = jnp.concatenate(
        [w[:, :, 0:512] * NA_SCALE, w[:, :, 512:2176], k_r, k_r[:, :, partner],
         jnp.zeros((n_layers, D_MODEL, LANES - 2 * MLA_ROPE_DIM), F32), w[:, :, 2208:]], axis=2).astype(BF16)
    uq = p["w_uq"].reshape(n_layers, MLA_Q_LORA, MLA_HEADS, MLA_QK_DIM)
    pad = MLA_HEAD_PAD - MLA_QK_DIM
    no_pad = ((0, 0), (0, 0), (0, 0))
    w_uq = jnp.pad(uq, no_pad + ((0, pad),)).reshape(n_layers, MLA_Q_LORA, MLA_PAD_WIDTH).astype(BF16)
    uq_sw = jnp.pad(uq[..., MLA_NOPE_DIM + partner], no_pad + ((MLA_NOPE_DIM, pad),))
    w_uq_sw = uq_sw.reshape(n_layers, MLA_Q_LORA, MLA_PAD_WIDTH).astype(BF16)
    uk = p["w_uk"].reshape(n_layers, MLA_KV_LORA, MLA_HEADS, MLA_NOPE_DIM)
    w_uk = jnp.pad(uk, no_pad + ((0, MLA_HEAD_PAD - MLA_NOPE_DIM),)).reshape(
        n_layers, MLA_KV_LORA, MLA_PAD_WIDTH).astype(BF16)
    n_pad = LANES - N_GROUPS - N_EXPERTS
    w_r = jnp.concatenate([p["router_group_w"], p["router_expert_w"],
                           jnp.zeros((n_layers, D_MODEL, n_pad), F32)], axis=2)
    b_r = jnp.concatenate([p["router_group_b"], p["router_expert_b"], jnp.zeros((n_layers, n_pad), F32)], axis=1)
    w_r_hi = w_r.astype(BF16)

    def row(name):
        return p[name][:, None, :]

    return dict(
        w_in=w_in, w_uq=w_uq, w_uq_sw=w_uq_sw, w_uk=w_uk, w_uv=p["w_uv"].astype(BF16),
        q_norm=row("mla_q_norm"), kv_norm=row("mla_kv_norm"),
        e_plain=_placement(MLA_ROPE_DIM), e_rope=_placement(2 * MLA_ROPE_DIM),
        w_o_na=p["w_o_na"].astype(BF16), w_o_mla=p["w_o_mla"].astype(BF16), w_out=p["w_out"].astype(BF16),
        ln1_g=row("ln1_g"), ln1_b=row("ln1_b"), ln2_g=row("ln2_g"), ln2_b=row("ln2_b"),
        w_r=w_r_hi, w_r_lo=(w_r - w_r_hi.astype(F32)).astype(BF16), b_r=b_r[:, None, :],
    )


def _window_rows(rows, q_rows):
    firsts = np.arange(0, rows, q_rows)
    lo = np.clip(firsts - NA_KR // 2, 0, rows - NA_KR)
    hi = np.clip(firsts + q_rows - 1 - NA_KR // 2, 0, rows - NA_KR) + NA_KR
    k_rows = min(rows, int(np.max(hi - lo)) + int(np.max(hi - lo)) % 2)
    k0 = np.minimum(lo, rows - k_rows)
    assert np.all(k0 % 2 == 0) and np.all(k0 + k_rows >= hi), (k0, k_rows)
    return k_rows, k0


def _na_bias_tables(rpb, rows):
    assert rows >= NA_KR and rows % 2 == 0
    r = np.arange(rows)
    rs = np.clip(r - NA_KR // 2, 0, rows - NA_KR)
    vrow = (r[None, :] >= rs[:, None]) & (r[None, :] < rs[:, None] + NA_KR)
    c = np.arange(GRID_W)
    cs = np.clip(c - NA_KC // 2, 0, GRID_W - NA_KC)
    vcol = (c[None, :] >= cs[:, None]) & (c[None, :] < cs[:, None] + NA_KC)
    coff = c[None, :] - c[:, None] + NA_KC - 1
    onehot = ((coff[None] == np.arange(2 * NA_KC - 1)[:, None, None]) & vcol[None]).astype(np.float32)
    blocks = jnp.einsum("lhdj,jqk->lhdqk", rpb * LOG2E, jnp.asarray(onehot), precision=lax.Precision.HIGHEST)
    blocks = jnp.where(jnp.asarray(vcol), blocks, NEG_INF)
    lead = rpb.shape[:2]
    reach = NA_KR - 1
    by_offset = jnp.concatenate(
        [jnp.full(lead + (rows - reach, GRID_W, GRID_W), NEG_INF, F32), blocks,
         jnp.full(lead + (rows - reach - 1, GRID_W, GRID_W), NEG_INF, F32)], axis=2)
    pair_blocks = jnp.concatenate([by_offset[:, :, :-1], by_offset[:, :, 1:]], axis=-1)
    in_window = np.repeat(vrow.reshape(rows, rows // 2, 2), GRID_W, axis=-1)
    row_masks = np.where(in_window, 0.0, NEG_INF).astype(np.float32).reshape(rows * rows // 2, 2 * GRID_W)
    return pair_blocks, jnp.asarray(row_masks)


def kernel(x_prompt, x_sample, cache_na_k, cache_na_v, cache_mla_ckv, cache_mla_kr, c, c_ctx, w_in, mla_q_norm, mla_kv_norm, w_uq, w_uk, w_uv, na_rpb, w_o_na, w_o_mla, w_out, w_ada, b_ada, ln1_g, ln1_b, ln2_g, ln2_b, router_group_w, router_group_b, router_expert_w, router_expert_b, expert_w_gate, expert_w_up, expert_w_down):
    p = dict(w_in=w_in, mla_q_norm=mla_q_norm, mla_kv_norm=mla_kv_norm, w_uq=w_uq, w_uk=w_uk, w_uv=w_uv,
             w_o_na=w_o_na, w_o_mla=w_o_mla, w_out=w_out, ln1_g=ln1_g, ln1_b=ln1_b, ln2_g=ln2_g, ln2_b=ln2_b,
             router_group_w=router_group_w, router_group_b=router_group_b,
             router_expert_w=router_expert_w, router_expert_b=router_expert_b,
             expert_w_gate=expert_w_gate, expert_w_up=expert_w_up, expert_w_down=expert_w_down)
    pb, ps, _ = x_prompt.shape
    sb, ss, _ = x_sample.shape
    past = cache_na_k.shape[2]

    n_cond = 16
    cond = jnp.zeros((n_cond, D_MODEL), F32).at[0].set(c_ctx).at[1:1 + sb].set(c)
    ada_all = _ada_call(cond, w_ada, b_ada[:, None, :]).reshape(DEPTH, n_cond, 6, D_MODEL)
    ada_all = jnp.pad(ada_all, ((0, 0), (0, 0), (0, 2), (0, 0)))
    ada_p = (ada_all, 0, 0)
    ada_s = (ada_all, 1, 1)

    cos32, sin32 = _rope_tables(ss)
    head_c = np.concatenate([np.ones((ss, MLA_NOPE_DIM)), cos32, np.zeros((ss, MLA_HEAD_PAD - MLA_QK_DIM))], -1)
    head_s = np.concatenate([np.zeros((ss, MLA_NOPE_DIM)), sin32, np.zeros((ss, MLA_HEAD_PAD - MLA_QK_DIM))], -1)
    rope_tabs = tuple(jnp.asarray(t, F32) for t in (
        head_c * (MLA_SCALE * LOG2E), head_s * (MLA_SCALE * LOG2E),
        np.concatenate([cos32, sin32, np.zeros((ss, LANES - 2 * MLA_ROPE_DIM))], -1)))

    cna_k = cache_na_k.reshape(sb, DEPTH, past, NA_WIDTH)
    cna_v = cache_na_v.reshape(sb, DEPTH, past, NA_WIDTH)
    ckr_pad = jnp.pad(cache_mla_kr, ((0, 0), (0, 0), (0, 0), (0, LANES - MLA_ROPE_DIM)))

    experts = (expert_w_gate, expert_w_up, expert_w_down)
    lw = _prepare_weights(p)
    tq = 512
    grid_rows = ss // GRID_W
    local_bias = _na_bias_tables(na_rpb, grid_rows) + (_window_rows(grid_rows, tq // GRID_W)[0],)
    xp, xs = x_prompt, x_sample
    new_k, new_v, new_ckv, new_kr = [], [], [], []
    for l in range(DEPTH):

        qn, kn, vn, qm, km, vm, gates, ckv, k_r = _proj_call(xp, ada_p, lw, l, None, F32, True, ps)
        o_na = _attn_call(qn, [(kn, vn, _seg3(kn, NA_WIDTH), _seg3(vn, NA_WIDTH))], None, NA_HEAD_DIM, ps,
                          NA_HEADS, "attn_ctx_na")
        o_mla = _attn_call(qm, [(km, vm, _seg3(km, MLA_PAD_WIDTH), _seg3(vm, NA_WIDTH))], None, MLA_HEAD_PAD, ps,
                           MLA_HEADS, "attn_ctx_mla")
        x1, h2, route, counts = _merge_call(o_na, o_mla, gates, xp, ada_p, lw, l, ps, "merge_ctx")
        xp = _moe(h2, route, counts, x1, ada_p, lw, experts, l, 128, ps, "ctx")
        new_k.append(kn)
        new_v.append(vn)
        new_ckv.append(ckv)
        new_kr.append(k_r)

        qn, kn, vn, qm, km, vm, gates = _proj_call(xs, ada_s, lw, l, rope_tabs, BF16, False, 512)
        km_ctx, vm_ctx = _ctxkv_call(cache_mla_ckv, ckr_pad, l, lw)
        o_na = _attn_call(
            qn, [(cna_k, cna_v, _seg4(cna_k, l, NA_WIDTH), _seg4(cna_v, l, NA_WIDTH)),
                 (kn, vn, _seg3(kn, NA_WIDTH), _seg3(vn, NA_WIDTH))], local_bias, NA_HEAD_DIM, tq, NA_HEADS,
            "attn_lat_na", l)
        o_mla = _attn_call(
            qm, [(km_ctx, vm_ctx, _seg3(km_ctx, MLA_PAD_WIDTH), _seg3(vm_ctx, NA_WIDTH)),
                 (km, vm, _seg3(km, MLA_PAD_WIDTH), _seg3(vm, NA_WIDTH))], None, MLA_HEAD_PAD, 512, MLA_HEADS,
            "attn_lat_mla")
        x1, h2, route, counts = _merge_call(o_na, o_mla, gates, xs, ada_s, lw, l, 512, "merge_lat")
        xs = _moe(h2, route, counts, x1, ada_s, lw, experts, l, 256, 256, "lat")

    def stack(parts, tail):
        return jnp.stack(parts, axis=1).reshape((pb, DEPTH, ps) + tail)

    return (xp, xs, stack(new_k, (NA_HEADS, NA_HEAD_DIM)), stack(new_v, (NA_HEADS, NA_HEAD_DIM)),
            stack(new_ckv, (MLA_KV_LORA,)), stack(new_kr, (MLA_ROPE_DIM,)))
```

```python
import functools

import numpy as np
import jax
import jax.numpy as jnp
from jax import lax
from jax.experimental import pallas as pl
from jax.experimental.pallas import tpu as pltpu

F32 = jnp.float32
BF16 = jnp.bfloat16

D_MODEL = 1024
DEPTH = 2
GRID_W = 64
NA_HEADS = 8
NA_HEAD_DIM = 64
NA_KR = 8
NA_KC = 16
NA_WIDTH = NA_HEADS * NA_HEAD_DIM
NA_SCALE = NA_HEAD_DIM ** -0.5
MLA_HEADS = 8
MLA_NOPE_DIM = 64
MLA_ROPE_DIM = 32
MLA_V_DIM = 64
MLA_Q_LORA = 384
MLA_KV_LORA = 256
MLA_QK_DIM = MLA_NOPE_DIM + MLA_ROPE_DIM
MLA_SCALE = MLA_QK_DIM ** -0.5
ROPE_BASE = 10000.0
N_GROUPS = 4
EXPERTS_PER_GROUP = 8
N_EXPERTS = N_GROUPS * EXPERTS_PER_GROUP
EXPERT_FF = 256
LN_EPS = 1e-5
RMS_EPS = 1e-6
NEG_INF = -1e30
LOG2E = 1.4426950408889634
DEEPNORM_ALPHA = (2 * DEPTH) ** 0.25

LANES = 128
MLA_HEAD_PAD = LANES
MLA_PAD_WIDTH = MLA_HEADS * MLA_HEAD_PAD
C_Q = 0
C_K = C_Q + NA_WIDTH
C_V = C_K + NA_WIDTH
C_CQ = C_V + NA_WIDTH
C_CKV = C_CQ + MLA_Q_LORA
C_KR = C_CKV + MLA_KV_LORA
C_GATE = C_KR + LANES
C_END = C_GATE + 2 * D_MODEL
R_EXP0 = N_GROUPS

VMEM_LIMIT = 56 * 1024 * 1024


def _rope_partner():
    j = np.arange(MLA_ROPE_DIM)
    n_freq = MLA_ROPE_DIM // 4
    axis, within = j // (2 * n_freq), j % (2 * n_freq)
    half, f = within // n_freq, within % n_freq
    return axis * 2 * n_freq + (1 - half) * n_freq + f, half


def _rope_tables(n_tokens):
    n_freq = MLA_ROPE_DIM // 4
    inv_freq = ROPE_BASE ** (-np.arange(n_freq, dtype=np.float64) / n_freq)
    t = np.arange(n_tokens)
    ang = np.concatenate([(t // GRID_W)[:, None] * inv_freq, (t % GRID_W)[:, None] * inv_freq], axis=-1)
    cos, sin = np.cos(ang), np.sin(ang)
    cos32 = np.concatenate([cos[:, :n_freq], cos[:, :n_freq], cos[:, n_freq:], cos[:, n_freq:]], axis=-1)
    sin32 = np.concatenate([-sin[:, :n_freq], sin[:, :n_freq], -sin[:, n_freq:], sin[:, n_freq:]], axis=-1)
    return cos32, sin32


def _ln_plain(x):
    mu = jnp.mean(x, axis=-1, keepdims=True)
    xc = x - mu
    var = jnp.mean(xc * xc, axis=-1, keepdims=True)
    return xc * lax.rsqrt(var + LN_EPS)


def _rms(x, g):
    return x * lax.rsqrt(jnp.mean(x * x, axis=-1, keepdims=True) + RMS_EPS) * g


def _sigmoid(x):
    return 1.0 / (1.0 + jnp.exp(-x))


def _dot(a, b):
    return jnp.dot(a, b, preferred_element_type=F32)


def _params(sem):
    return pltpu.CompilerParams(dimension_semantics=sem, vmem_limit_bytes=VMEM_LIMIT)


def _layer_spec(arr, layer):
    zeros = (0,) * (arr.ndim - 1)
    return pl.BlockSpec((None,) + arr.shape[1:], lambda *_: (layer,) + zeros, pipeline_mode=pl.Buffered(1))


def _ada_spec(ada, layer, batch_of):
    _, first, step = ada
    return pl.BlockSpec((None, None, 8, D_MODEL), lambda *idx: (layer, first + step * batch_of(*idx), 0, 0))


def _const_spec(arr):
    zeros = (0,) * arr.ndim
    return pl.BlockSpec(arr.shape, lambda *_: zeros, pipeline_mode=pl.Buffered(1))


TOKEN_TILE = (D_MODEL // LANES, LANES)


def _store_token_tiles(ref, val):
    chunks = jnp.stack([val[:, k * LANES:(k + 1) * LANES] for k in range(TOKEN_TILE[0])], axis=0)
    ref[...] = jnp.swapaxes(chunks, 0, 1)


def _load_token_tiles(ref):
    chunks = jnp.swapaxes(ref[...], 0, 1)
    return jnp.concatenate([chunks[k] for k in range(TOKEN_TILE[0])], axis=-1)


def _ada_body(c_ref, w_ref, b_ref, o_ref):
    c = c_ref[...]
    s = (c * _sigmoid(c)).astype(BF16)
    o_ref[0] = _dot(s, w_ref[0].astype(BF16)) + b_ref[0]


def _ada_call(cond, w_ada, b_ada):
    n_rows = cond.shape[0]
    tn = 1024
    return pl.pallas_call(
        _ada_body,
        grid=(DEPTH, 6 * D_MODEL // tn),
        in_specs=[
            pl.BlockSpec((n_rows, D_MODEL), lambda l, j: (0, 0)),
            pl.BlockSpec((1, D_MODEL, tn), lambda l, j: (l, 0, j)),
            pl.BlockSpec((1, 1, tn), lambda l, j: (l, 0, j)),
        ],
        out_specs=pl.BlockSpec((1, n_rows, tn), lambda l, j: (l, 0, j)),
        out_shape=jax.ShapeDtypeStruct((DEPTH, n_rows, 6 * D_MODEL), F32),
        compiler_params=_params(("arbitrary", "arbitrary")),
        name="ada",
    )(cond, w_ada, b_ada)


def _proj_body(*refs, rope, emit_cache):
    it = iter(refs)
    x_ref, ada_ref, w_ref, qn_ref, kvn_ref, wuq_ref = (next(it) for _ in range(6))
    wuqs_ref = next(it) if rope else None
    wuk_ref, wuv_ref, e_ref = next(it), next(it), next(it)
    if rope:
        cq_ref, sq_ref, tk_ref = next(it), next(it), next(it)
    oq_ref, ok_ref, ov_ref, oqm_ref, okm_ref, ovm_ref, og_ref = (next(it) for _ in range(7))
    if emit_cache:
        ockv_ref, okr_ref = next(it), next(it)

    x = x_ref[...]
    sh = ada_ref[0:1, :]
    sc = ada_ref[1:2, :]
    h = (_ln_plain(x) * (1.0 + sc) + sh).astype(BF16)

    oq_ref[...] = (_dot(h, w_ref[:, C_Q:C_K]) * LOG2E).astype(oq_ref.dtype)
    ok_ref[...] = _dot(h, w_ref[:, C_K:C_V]).astype(ok_ref.dtype)
    ov_ref[...] = _dot(h, w_ref[:, C_V:C_CQ]).astype(ov_ref.dtype)

    c_q = _rms(_dot(h, w_ref[:, C_CQ:C_CKV]), qn_ref[...]).astype(BF16)
    q = _dot(c_q, wuq_ref[...])
    if rope:
        q = (q * jnp.tile(cq_ref[...], (1, MLA_HEADS))
             + _dot(c_q, wuqs_ref[...]) * jnp.tile(sq_ref[...], (1, MLA_HEADS)))
    else:
        q = q * (MLA_SCALE * LOG2E)
    oqm_ref[...] = q.astype(BF16)

    c_kv = _rms(_dot(h, w_ref[:, C_CKV:C_KR]), kvn_ref[...])
    slab = _dot(h, w_ref[:, C_KR:C_GATE])
    if emit_cache:
        ockv_ref[...] = c_kv
        okr_ref[...] = slab[:, :MLA_ROPE_DIM]
    if rope:
        slab = slab * tk_ref[...]
    c_kv = c_kv.astype(BF16)
    okm_ref[...] = (_dot(c_kv, wuk_ref[...]) + _dot(slab.astype(BF16), e_ref[...])).astype(BF16)
    ovm_ref[...] = _dot(c_kv, wuv_ref[...]).astype(BF16)

    og_ref[...] = _sigmoid(_dot(h, w_ref[:, C_GATE:C_END])).astype(BF16)


def _proj_call(x, ada, lw, layer, rope_tabs, kv_dtype, emit_cache, tm):
    bsz, seq, _ = x.shape
    nj = seq // tm
    rope = rope_tabs is not None

    def tok(width):
        return pl.BlockSpec((None, tm, width), lambda b, j: (b, j, 0))

    def full(arr):
        return _layer_spec(arr, layer)

    def tab(width):
        return pl.BlockSpec((tm, width), lambda b, j: (j, 0))

    args = [x, ada[0], lw["w_in"], lw["q_norm"], lw["kv_norm"], lw["w_uq"]]
    specs = [tok(D_MODEL), _ada_spec(ada, layer, lambda b, j: b),
             full(lw["w_in"]), full(lw["q_norm"]), full(lw["kv_norm"]), full(lw["w_uq"])]
    if rope:
        args.append(lw["w_uq_sw"])
        specs.append(full(lw["w_uq_sw"]))
    e_mat = lw["e_rope"] if rope else lw["e_plain"]
    args += [lw["w_uk"], lw["w_uv"], e_mat]
    specs += [full(lw["w_uk"]), full(lw["w_uv"]), _const_spec(e_mat)]
    if rope:
        args += list(rope_tabs)
        specs += [tab(MLA_HEAD_PAD), tab(MLA_HEAD_PAD), tab(LANES)]

    def shp(width, dt):
        return jax.ShapeDtypeStruct((bsz, seq, width), dt)

    out_shape = [shp(NA_WIDTH, BF16), shp(NA_WIDTH, kv_dtype), shp(NA_WIDTH, kv_dtype),
                 shp(MLA_PAD_WIDTH, BF16), shp(MLA_PAD_WIDTH, BF16), shp(NA_WIDTH, BF16),
                 shp(2 * D_MODEL, BF16)]
    out_specs = [tok(NA_WIDTH), tok(NA_WIDTH), tok(NA_WIDTH), tok(MLA_PAD_WIDTH), tok(MLA_PAD_WIDTH),
                 tok(NA_WIDTH), tok(2 * D_MODEL)]
    if emit_cache:
        out_shape += [shp(MLA_KV_LORA, F32), shp(MLA_ROPE_DIM, F32)]
        out_specs += [tok(MLA_KV_LORA), tok(MLA_ROPE_DIM)]
    return pl.pallas_call(
        functools.partial(_proj_body, rope=rope, emit_cache=emit_cache),
        grid=(bsz, nj),
        in_specs=specs,
        out_specs=out_specs,
        out_shape=out_shape,
        compiler_params=_params(("arbitrary", "arbitrary")),
        name="proj_lat" if rope else "proj_ctx",
    )(*args)


def _ctxkv_body(ckv_ref, kr_ref, wuk_ref, wuv_ref, e_ref, okm_ref, ovm_ref):
    c_kv = ckv_ref[...].astype(BF16)
    okm_ref[...] = (_dot(c_kv, wuk_ref[...]) + _dot(kr_ref[...].astype(BF16), e_ref[...])).astype(BF16)
    ovm_ref[...] = _dot(c_kv, wuv_ref[...]).astype(BF16)


def _ctxkv_call(cache_ckv, cache_kr_pad, layer, lw):
    bsz, _, past, _ = cache_ckv.shape

    def full(arr):
        return _layer_spec(arr, layer)

    return pl.pallas_call(
        _ctxkv_body,
        grid=(bsz,),
        in_specs=[
            pl.BlockSpec((None, None, past, MLA_KV_LORA), lambda b: (b, layer, 0, 0)),
            pl.BlockSpec((None, None, past, LANES), lambda b: (b, layer, 0, 0)),
            full(lw["w_uk"]), full(lw["w_uv"]), _const_spec(lw["e_plain"]),
        ],
        out_specs=[pl.BlockSpec((None, past, MLA_PAD_WIDTH), lambda b: (b, 0, 0)),
                   pl.BlockSpec((None, past, NA_WIDTH), lambda b: (b, 0, 0))],
        out_shape=[jax.ShapeDtypeStruct((bsz, past, MLA_PAD_WIDTH), BF16),
                   jax.ShapeDtypeStruct((bsz, past, NA_WIDTH), BF16)],
        compiler_params=_params(("arbitrary",)),
        name="ctxkv",
    )(cache_ckv, cache_kr_pad, lw["w_uk"], lw["w_uv"], lw["e_plain"])


def _local_bias(pb_ref, rm_ref, hh, q0, k0, q_rows, k_rows, grid_rows):
    rows = []
    for qr in range(q_rows):
        blocks = []
        for p in range(k_rows // 2):
            pair = k0 // 2 + p
            e = 2 * pair - (q0 + qr) + grid_rows
            blocks.append(pb_ref[hh, e] + rm_ref[pl.ds((q0 + qr) * (grid_rows // 2) + pair, 1), :])
        rows.append(jnp.concatenate(blocks, axis=-1))
    return jnp.concatenate(rows, axis=0)


def _attn_body(*refs, n_seg, dq, local, heads):
    q_ref = refs[0]
    seg_refs = [(refs[1 + 2 * i], refs[2 + 2 * i]) for i in range(n_seg)]
    o_ref = refs[-1]
    if local is not None:
        pb_ref, rm_ref = refs[1 + 2 * n_seg], refs[2 + 2 * n_seg]
        q_rows, k_rows, grid_rows = local
        q0 = pl.program_id(1) * q_rows
        k0 = jnp.minimum(jnp.clip(q0 - NA_KR // 2, 0, grid_rows - NA_KR), grid_rows - k_rows)
        keys = pl.ds(pl.multiple_of(k0 * GRID_W, GRID_W), k_rows * GRID_W)
    head_major = [tuple(jnp.swapaxes(r[...], 0, 1).astype(BF16) for r in seg) if len(seg[0].shape) == 3 else None
                  for seg in seg_refs]
    outs = []
    for hh in range(heads):
        qh = q_ref[:, hh * dq:(hh + 1) * dq]
        scores = []
        values = []
        for i, (k_ref, v_ref) in enumerate(seg_refs):
            windowed = local is not None and i == n_seg - 1
            rows = keys if windowed else slice(None)
            if head_major[i] is not None:
                kh, vh = head_major[i][0][hh], head_major[i][1][hh]
            else:
                kh = k_ref[rows, hh * dq:(hh + 1) * dq].astype(BF16)
                vh = v_ref[rows, hh * MLA_V_DIM:(hh + 1) * MLA_V_DIM].astype(BF16)
            values.append(vh)
            s = lax.dot_general(qh, kh, (((1,), (1,)), ((), ())), preferred_element_type=F32)
            if windowed:
                s = s + _local_bias(pb_ref, rm_ref, hh, q0, k0, q_rows, k_rows, grid_rows)
            scores.append(s)
        m = functools.reduce(jnp.maximum, [jnp.max(s, axis=-1, keepdims=True) for s in scores])
        den = None
        acc = None
        for s, vh in zip(scores, values):
            e = jnp.exp2(s - m)
            d = jnp.sum(e, axis=-1, keepdims=True)
            a = _dot(e.astype(BF16), vh)
            den = d if den is None else den + d
            acc = a if acc is None else acc + a
        outs.append(acc / den)
    o_ref[...] = jnp.concatenate(outs, axis=-1).astype(o_ref.dtype)


def _attn_call(q, segs, local_bias, dq, tq, heads, name, layer=0):
    bsz, seq, _ = q.shape
    args = [q]
    specs = [pl.BlockSpec((None, tq, heads * dq), lambda hp, qt, b: (b, qt, hp))]
    for k, v, ks, vs in segs:
        args += [k, v]
        specs += [ks, vs]
    local = None
    if local_bias is not None:
        pair_blocks, row_masks, k_rows = local_bias
        args += [pair_blocks, row_masks]
        specs += [pl.BlockSpec((None, heads) + pair_blocks.shape[2:], lambda hp, qt, b: (layer, hp, 0, 0, 0),
                               pipeline_mode=pl.Buffered(1)),
                  _const_spec(row_masks)]
        local = (tq // GRID_W, k_rows, seq // GRID_W)
    return pl.pallas_call(
        functools.partial(_attn_body, n_seg=len(segs), dq=dq, local=local, heads=heads),
        grid=(NA_HEADS // heads, seq // tq, bsz),
        in_specs=specs,
        out_specs=pl.BlockSpec((None, tq, heads * MLA_V_DIM), lambda hp, qt, b: (b, qt, hp)),
        out_shape=jax.ShapeDtypeStruct((bsz, seq, NA_WIDTH), BF16),
        compiler_params=_params(("arbitrary", "arbitrary", "arbitrary")),
        name=name,
    )(*args)


def _seg3(arr, width):
    n = arr.shape[1]
    return pl.BlockSpec((None, n, width), lambda hp, qt, b: (b, 0, hp))


def _seg_cache(arr, layer):
    return pl.BlockSpec((None, None) + arr.shape[2:], lambda hp, qt, b: (b, layer, 0, 0, 0))


def _route(logits):
    lane = lax.broadcasted_iota(jnp.int32, logits.shape, 1)
    lane_f = lane.astype(F32)
    far = float(LANES)
    is_g = lane < N_GROUPS
    gl = jnp.where(is_g, logits, NEG_INF)
    gmax = jnp.max(gl, axis=-1, keepdims=True)
    gsum = jnp.sum(jnp.exp(gl - gmax), axis=-1, keepdims=True)
    g_w = 1.0 / gsum
    g_idx = jnp.min(jnp.where(gl == gmax, lane_f, far), axis=-1, keepdims=True)
    e_group = ((lane - R_EXP0) >> 3).astype(F32)
    is_e = (lane >= R_EXP0) & (lane < R_EXP0 + N_EXPERTS) & (e_group == g_idx)
    el = jnp.where(is_e, logits, NEG_INF)
    m1 = jnp.max(el, axis=-1, keepdims=True)
    i1 = jnp.min(jnp.where(el == m1, lane_f, far), axis=-1, keepdims=True)
    el2 = jnp.where(lane_f == i1, NEG_INF, el)
    m2 = jnp.max(el2, axis=-1, keepdims=True)
    i2 = jnp.min(jnp.where(el2 == m2, lane_f, far), axis=-1, keepdims=True)
    r = jnp.exp(m2 - m1)
    w1 = g_w / (1.0 + r)
    w2 = g_w * r / (1.0 + r)
    return lane_f, i1, i2, w1, w2


def _merge_body(ona_ref, omla_ref, g_ref, x_ref, ada_ref, wna_ref, wmla_ref, wout_ref, lng_ref, lnb_ref,
                wr_ref, wrl_ref, br_ref, ox_ref, oh_ref, or_ref, ocnt_ref, carry_ref):
    first = (pl.program_id(0) == 0) & (pl.program_id(1) == 0)

    @pl.when(first)
    def _():
        carry_ref[...] = jnp.zeros_like(carry_ref)

    y_na = _dot(ona_ref[...], wna_ref[...])
    y_mla = _dot(omla_ref[...], wmla_ref[...])
    g_na = g_ref[:, :D_MODEL].astype(F32)
    g_mla = g_ref[:, D_MODEL:].astype(F32)
    y = _dot((g_na * y_na + g_mla * y_mla).astype(BF16), wout_ref[...])
    g1 = ada_ref[2:3, :]
    x1 = _ln_plain(DEEPNORM_ALPHA * x_ref[...] + g1 * y) * lng_ref[...] + lnb_ref[...]
    ox_ref[...] = x1
    h2 = _ln_plain(x1) * (1.0 + ada_ref[4:5, :]) + ada_ref[3:4, :]
    _store_token_tiles(oh_ref, h2)
    h_hi = h2.astype(BF16)
    h_lo = (h2 - h_hi.astype(F32)).astype(BF16)
    logits = (_dot(h_hi, wr_ref[...]) + (_dot(h_lo, wr_ref[...]) + _dot(h_hi, wrl_ref[...]))) + br_ref[...]
    lane_f, i1, i2, w1, w2 = _route(logits)

    tm = logits.shape[0]
    sel1 = lane_f == i1
    sel2 = lane_f == i2
    used = (jnp.where(sel1, 1.0, 0.0) + jnp.where(sel2, 1.0, 0.0)).astype(BF16)
    row = lax.broadcasted_iota(jnp.int32, (tm, tm), 0)
    col = lax.broadcasted_iota(jnp.int32, (tm, tm), 1)
    earlier = jnp.where(row > col, 1.0, 0.0).astype(BF16)
    before = _dot(earlier, used) + carry_ref[0:1, :]
    rank1 = jnp.sum(jnp.where(sel1, before, 0.0), axis=-1, keepdims=True)
    rank2 = jnp.sum(jnp.where(sel2, before, 0.0), axis=-1, keepdims=True)
    carry_ref[...] += _dot(jnp.ones((8, tm), BF16), used)
    ocnt_ref[...] = carry_ref[...]

    fields = (i1 - R_EXP0, i2 - R_EXP0, w1, w2, rank1, rank2)
    route = jnp.zeros_like(logits)
    for k, val in enumerate(fields):
        route = jnp.where(lane_f == float(k), val, route)
    or_ref[...] = route


def _merge_call(o_na, o_mla, gates, x, ada, lw, layer, tm, name):
    bsz, seq, _ = x.shape

    def tok(width):
        return pl.BlockSpec((None, tm, width), lambda b, j: (b, j, 0))

    def full(arr):
        return _layer_spec(arr, layer)

    ws = [lw["w_o_na"], lw["w_o_mla"], lw["w_out"], lw["ln1_g"], lw["ln1_b"], lw["w_r"], lw["w_r_lo"], lw["b_r"]]
    return pl.pallas_call(
        _merge_body,
        grid=(bsz, seq // tm),
        in_specs=[tok(NA_WIDTH), tok(NA_WIDTH), tok(2 * D_MODEL), tok(D_MODEL),
                  _ada_spec(ada, layer, lambda b, j: b)] + [full(w) for w in ws],
        out_specs=[tok(D_MODEL), pl.BlockSpec((None, tm) + TOKEN_TILE, lambda b, j: (b, j, 0, 0)), tok(LANES),
                   pl.BlockSpec((8, LANES), lambda b, j: (0, 0))],
        out_shape=[jax.ShapeDtypeStruct((bsz, seq, D_MODEL), F32),
                   jax.ShapeDtypeStruct((bsz, seq) + TOKEN_TILE, F32),
                   jax.ShapeDtypeStruct((bsz, seq, LANES), F32),
                   jax.ShapeDtypeStruct((8, LANES), F32)],
        scratch_shapes=[pltpu.VMEM((8, LANES), F32)],
        compiler_params=_params(("arbitrary", "arbitrary")),
        name=name,
    )(o_na, o_mla, gates, x, ada[0], *ws)


POS_BITS = 16
POS_MASK = (1 << POS_BITS) - 1


def _plan(route, counts, n_tok, tr):
    r = route.reshape(n_tok, LANES)
    e1, e2 = r[:, 0].astype(jnp.int32), r[:, 1].astype(jnp.int32)
    rk1, rk2 = r[:, 4].astype(jnp.int32), r[:, 5].astype(jnp.int32)
    cnt = counts[0, R_EXP0:R_EXP0 + N_EXPERTS].astype(jnp.int32)
    tiles_e = (cnt + tr - 1) // tr
    tile_end = jnp.cumsum(tiles_e)
    row_off = (tile_end - tiles_e) * tr
    eid = jnp.arange(N_EXPERTS, dtype=jnp.int32)
    p1 = jnp.sum(jnp.where(e1[:, None] == eid, row_off, 0), axis=-1) + rk1
    p2 = jnp.sum(jnp.where(e2[:, None] == eid, row_off, 0), axis=-1) + rk2
    n_tiles = 2 * n_tok // tr + N_EXPERTS
    n_used = tile_end[-1]
    ti = jnp.arange(n_tiles, dtype=jnp.int32)
    tile_expert = jnp.sum(jnp.minimum(ti, n_used - 1)[:, None] >= tile_end[None, :], axis=-1).astype(jnp.int32)
    own = tile_expert[:, None] == eid
    seg_end = jnp.sum(jnp.where(own, row_off + cnt, 0), axis=-1)
    tile_rows = jnp.clip(seg_end - ti * tr, 1, tr).astype(jnp.int32)
    return p1 | (p2 << POS_BITS), tile_expert, tile_rows, jnp.stack([n_used, n_used]).astype(jnp.int32), n_tiles


ROW_GROUP = 8


def _expert_body(pos_ref, te_ref, rows_ref, meta_ref, h_hbm, wg_ref, wu_ref, wd_ref, o_ref,
                 src_ref, hbuf, sem, wg_b, wu_b, wd_b, *, n_tok, tr):
    i = pl.program_id(0)
    n_used = meta_ref[0]

    def for_tile_rows(tile, slot, fn):
        base = tile * tr
        last = rows_ref[tile] - 1
        dst = hbuf.at[slot]
        dsem = sem.at[slot]

        def body(g, carry):
            for k in range(ROW_GROUP):
                r = g * ROW_GROUP + k
                t = src_ref[base + jnp.minimum(r, last)]
                fn(pltpu.make_async_copy(h_hbm.at[t], dst.at[r], dsem))
            return carry
        lax.fori_loop(0, (last + ROW_GROUP) // ROW_GROUP, body, 0)

    @pl.when(i == 0)
    def _():
        hbuf[...] = jnp.zeros_like(hbuf)

        def place(t, carry):
            packed = pos_ref[t]
            src_ref[packed & POS_MASK] = t
            src_ref[packed >> POS_BITS] = t
            return carry
        lax.fori_loop(0, n_tok, place, 0, unroll=8)
        for_tile_rows(0, 0, lambda cp: cp.start())

    @pl.when(i + 1 < n_used)
    def _():
        for_tile_rows(i + 1, (i + 1) & 1, lambda cp: cp.start())

    @pl.when(i >= n_used)
    def _():
        o_ref[...] = jnp.zeros_like(o_ref)

    @pl.when(i < n_used)
    def _():
        slot = i & 1
        for_tile_rows(i, slot, lambda cp: cp.wait())

        @pl.when((i == 0) | (te_ref[i] != te_ref[jnp.maximum(i - 1, 0)]))
        def _():
            wg_b[...] = wg_ref[...].astype(BF16)
            wu_b[...] = wu_ref[...].astype(BF16)
            wd_b[...] = wd_ref[...].astype(BF16)

        h = _load_token_tiles(hbuf.at[slot]).astype(BF16)
        gate = _dot(h, wg_b[...])
        up = _dot(h, wu_b[...])
        hid = (gate * _sigmoid(gate) * up).astype(BF16)
        _store_token_tiles(o_ref, _dot(hid, wd_b[...]))


def _expert_call(h2, plan, w_gate, w_up, w_down, layer, tr, name):
    packed, tile_expert, tile_rows, meta, n_tiles = plan
    n_tok = h2.shape[0]
    n_rows = n_tiles * tr
    return pl.pallas_call(
        functools.partial(_expert_body, n_tok=n_tok, tr=tr),
        grid_spec=pltpu.PrefetchScalarGridSpec(
            num_scalar_prefetch=4,
            grid=(n_tiles,),
            in_specs=[
                pl.BlockSpec(memory_space=pl.ANY),
                pl.BlockSpec((None, None, D_MODEL, EXPERT_FF), lambda i, pos, te, nr, meta: (layer, te[i], 0, 0)),
                pl.BlockSpec((None, None, D_MODEL, EXPERT_FF), lambda i, pos, te, nr, meta: (layer, te[i], 0, 0)),
                pl.BlockSpec((None, None, EXPERT_FF, D_MODEL), lambda i, pos, te, nr, meta: (layer, te[i], 0, 0)),
            ],
            out_specs=pl.BlockSpec((tr,) + TOKEN_TILE, lambda i, pos, te, nr, meta: (i, 0, 0)),
            scratch_shapes=[
                pltpu.SMEM((n_rows,), jnp.int32),
                pltpu.VMEM((2, tr) + TOKEN_TILE, F32),
                pltpu.SemaphoreType.DMA((2,)),
                pltpu.VMEM((D_MODEL, EXPERT_FF), BF16),
                pltpu.VMEM((D_MODEL, EXPERT_FF), BF16),
                pltpu.VMEM((EXPERT_FF, D_MODEL), BF16),
            ],
        ),
        out_shape=jax.ShapeDtypeStruct((n_rows,) + TOKEN_TILE, F32),
        compiler_params=_params(("arbitrary",)),
        name=name,
    )(packed, tile_expert, tile_rows, meta, h2, w_gate, w_up, w_down)


def _combine_body(pos_ref, y_hbm, r_ref, x_ref, ada_ref, lng_ref, lnb_ref, o_ref, ybuf, sem, *, tm):
    i = pl.program_id(0)
    n = pl.num_programs(0)

    def for_tile_rows(tile, slot, fn):
        base = tile * tm
        dst1 = ybuf.at[slot, 0]
        dst2 = ybuf.at[slot, 1]
        dsem = sem.at[slot]

        def body(g, carry):
            for k in range(ROW_GROUP):
                r = g * ROW_GROUP + k
                packed = pos_ref[base + r]
                fn(pltpu.make_async_copy(y_hbm.at[packed & POS_MASK], dst1.at[r], dsem))
                fn(pltpu.make_async_copy(y_hbm.at[packed >> POS_BITS], dst2.at[r], dsem))
            return carry
        lax.fori_loop(0, tm // ROW_GROUP, body, 0)

    @pl.when(i == 0)
    def _():
        for_tile_rows(0, 0, lambda cp: cp.start())

    @pl.when(i + 1 < n)
    def _():
        for_tile_rows(i + 1, (i + 1) & 1, lambda cp: cp.start())

    slot = i & 1
    for_tile_rows(i, slot, lambda cp: cp.wait())

    route = r_ref[...]
    lane = lax.broadcasted_iota(jnp.int32, route.shape, 1)
    w1 = jnp.sum(jnp.where(lane == 2, route, 0.0), axis=-1, keepdims=True)
    w2 = jnp.sum(jnp.where(lane == 3, route, 0.0), axis=-1, keepdims=True)
    y = w1 * _load_token_tiles(ybuf.at[slot, 0]) + w2 * _load_token_tiles(ybuf.at[slot, 1])
    g2 = ada_ref[5:6, :]
    o_ref[...] = _ln_plain(DEEPNORM_ALPHA * x_ref[...] + g2 * y) * lng_ref[...] + lnb_ref[...]


def _combine_call(ys, packed, route, x1, ada, lw, layer, seq, tm, name):
    n_tok = x1.shape[0]
    per_batch = seq // tm

    def tok(width):
        return pl.BlockSpec((tm, width), lambda i, pos: (i, 0))

    def full(arr):
        return _layer_spec(arr, layer)

    return pl.pallas_call(
        functools.partial(_combine_body, tm=tm),
        grid_spec=pltpu.PrefetchScalarGridSpec(
            num_scalar_prefetch=1,
            grid=(n_tok // tm,),
            in_specs=[pl.BlockSpec(memory_space=pl.ANY), tok(LANES), tok(D_MODEL),
                      _ada_spec(ada, layer, lambda i, pos: i // per_batch),
                      full(lw["ln2_g"]), full(lw["ln2_b"])],
            out_specs=tok(D_MODEL),
            scratch_shapes=[pltpu.VMEM((2, 2, tm) + TOKEN_TILE, F32), pltpu.SemaphoreType.DMA((2,))],
        ),
        out_shape=jax.ShapeDtypeStruct((n_tok, D_MODEL), F32),
        compiler_params=_params(("arbitrary",)),
        name=name,
    )(packed, ys, route, x1, ada[0], lw["ln2_g"], lw["ln2_b"])


def _moe(h2, route, counts, x1, ada, lw, experts, layer, tr, tm, name):
    bsz, seq, _ = x1.shape
    n_tok = bsz * seq
    plan = _plan(route, counts, n_tok, tr)
    ys = _expert_call(h2.reshape((n_tok,) + TOKEN_TILE), plan, *experts, layer, tr, "expert_" + name)
    out = _combine_call(ys, plan[0], route.reshape(n_tok, LANES), x1.reshape(n_tok, D_MODEL), ada, lw, layer,
                        seq, tm, "combine_" + name)
    return out.reshape(bsz, seq, D_MODEL)


def _placement(rows_used):
    e = np.zeros((LANES, MLA_PAD_WIDTH), np.float32)
    for r in range(rows_used):
        for hd in range(MLA_HEADS):
            e[r, hd * MLA_HEAD_PAD + MLA_NOPE_DIM + (r % MLA_ROPE_DIM)] = 1.0
    return jnp.asarray(e, BF16)


def _prepare_weights(p):
    partner, _ = _rope_partner()
    n_layers = p["w_in"].shape[0]
    w = p["w_in"]
    k_r = w[:, :, 2176:2208]
    pieces = [w[:, :, 0:512] * NA_SCALE, w[:, :, 512:2176], k_r, k_r[:, :, partner],
              jnp.zeros((n_layers, D_MODEL, LANES - 2 * MLA_ROPE_DIM), F32), w[:, :, 2208:]]
    w_in = jnp.concatenate([piece.astype(BF16) for piece in pieces], axis=2)
    uq = p["w_uq"].reshape(n_layers, MLA_Q_LORA, MLA_HEADS, MLA_QK_DIM)
    pad = MLA_HEAD_PAD - MLA_QK_DIM
    no_pad = ((0, 0), (0, 0), (0, 0))
    w_uq = jnp.pad(uq, no_pad + ((0, pad),)).reshape(n_layers, MLA_Q_LORA, MLA_PAD_WIDTH).astype(BF16)
    uq_sw = jnp.pad(uq[..., MLA_NOPE_DIM + partner], no_pad + ((MLA_NOPE_DIM, pad),))
    w_uq_sw = uq_sw.reshape(n_layers, MLA_Q_LORA, MLA_PAD_WIDTH).astype(BF16)
    uk = p["w_uk"].reshape(n_layers, MLA_KV_LORA, MLA_HEADS, MLA_NOPE_DIM)
    w_uk = jnp.pad(uk, no_pad + ((0, MLA_HEAD_PAD - MLA_NOPE_DIM),)).reshape(
        n_layers, MLA_KV_LORA, MLA_PAD_WIDTH).astype(BF16)
    n_pad = LANES - N_GROUPS - N_EXPERTS
    w_r = jnp.concatenate([p["router_group_w"], p["router_expert_w"],
                           jnp.zeros((n_layers, D_MODEL, n_pad), F32)], axis=2)
    b_r = jnp.concatenate([p["router_group_b"], p["router_expert_b"], jnp.zeros((n_layers, n_pad), F32)], axis=1)
    w_r_hi = w_r.astype(BF16)

    def row(name):
        return p[name][:, None, :]

    return dict(
        w_in=w_in, w_uq=w_uq, w_uq_sw=w_uq_sw, w_uk=w_uk, w_uv=p["w_uv"].astype(BF16),
        q_norm=row("mla_q_norm"), kv_norm=row("mla_kv_norm"),
        e_plain=_placement(MLA_ROPE_DIM), e_rope=_placement(2 * MLA_ROPE_DIM),
        w_o_na=p["w_o_na"].astype(BF16), w_o_mla=p["w_o_mla"].astype(BF16), w_out=p["w_out"].astype(BF16),
        ln1_g=row("ln1_g"), ln1_b=row("ln1_b"), ln2_g=row("ln2_g"), ln2_b=row("ln2_b"),
        w_r=w_r_hi, w_r_lo=(w_r - w_r_hi.astype(F32)).astype(BF16), b_r=b_r[:, None, :],
    )


def _window_rows(rows, q_rows):
    firsts = np.arange(0, rows, q_rows)
    lo = np.clip(firsts - NA_KR // 2, 0, rows - NA_KR)
    hi = np.clip(firsts + q_rows - 1 - NA_KR // 2, 0, rows - NA_KR) + NA_KR
    k_rows = min(rows, int(np.max(hi - lo)) + int(np.max(hi - lo)) % 2)
    k0 = np.minimum(lo, rows - k_rows)
    assert np.all(k0 % 2 == 0) and np.all(k0 + k_rows >= hi), (k0, k_rows)
    return k_rows, k0


def _na_bias_tables(rpb, rows):
    assert rows >= NA_KR and rows % 2 == 0
    r = np.arange(rows)
    rs = np.clip(r - NA_KR // 2, 0, rows - NA_KR)
    vrow = (r[None, :] >= rs[:, None]) & (r[None, :] < rs[:, None] + NA_KR)
    c = np.arange(GRID_W)
    cs = np.clip(c - NA_KC // 2, 0, GRID_W - NA_KC)
    vcol = (c[None, :] >= cs[:, None]) & (c[None, :] < cs[:, None] + NA_KC)
    coff = c[None, :] - c[:, None] + NA_KC - 1
    onehot = ((coff[None] == np.arange(2 * NA_KC - 1)[:, None, None]) & vcol[None]).astype(np.float32)
    zero = np.zeros_like(onehot)
    reach = NA_KR - 1
    halves = []
    for placed in (np.concatenate([onehot, zero], axis=-1), np.concatenate([zero, onehot], axis=-1)):
        vals = jnp.einsum("lhdj,jqk->lhdqk", rpb * LOG2E, jnp.asarray(placed), precision=lax.Precision.HIGHEST)
        halves.append(jnp.pad(vals, ((0, 0), (0, 0), (rows - reach, rows - reach - 1), (0, 0), (0, 0))))
    valid_d = np.abs(np.arange(-rows, rows)) <= reach
    valid = valid_d[:, None, None] & vcol[None]
    valid_pair = np.concatenate([valid[:-1], valid[1:]], axis=-1)
    neg_mask = jnp.asarray(np.where(valid_pair, 0.0, NEG_INF).astype(np.float32))
    pair_blocks = halves[0][:, :, :-1] + halves[1][:, :, 1:] + neg_mask
    in_window = np.repeat(vrow.reshape(rows, rows // 2, 2), GRID_W, axis=-1)
    row_masks = np.where(in_window, 0.0, NEG_INF).astype(np.float32).reshape(rows * rows // 2, 2 * GRID_W)
    return pair_blocks, jnp.asarray(row_masks)


def kernel(x_prompt, x_sample, cache_na_k, cache_na_v, cache_mla_ckv, cache_mla_kr, c, c_ctx, w_in, mla_q_norm, mla_kv_norm, w_uq, w_uk, w_uv, na_rpb, w_o_na, w_o_mla, w_out, w_ada, b_ada, ln1_g, ln1_b, ln2_g, ln2_b, router_group_w, router_group_b, router_expert_w, router_expert_b, expert_w_gate, expert_w_up, expert_w_down):
    p = dict(w_in=w_in, mla_q_norm=mla_q_norm, mla_kv_norm=mla_kv_norm, w_uq=w_uq, w_uk=w_uk, w_uv=w_uv,
             w_o_na=w_o_na, w_o_mla=w_o_mla, w_out=w_out, ln1_g=ln1_g, ln1_b=ln1_b, ln2_g=ln2_g, ln2_b=ln2_b,
             router_group_w=router_group_w, router_group_b=router_group_b,
             router_expert_w=router_expert_w, router_expert_b=router_expert_b,
             expert_w_gate=expert_w_gate, expert_w_up=expert_w_up, expert_w_down=expert_w_down)
    pb, ps, _ = x_prompt.shape
    sb, ss, _ = x_sample.shape

    n_cond = 16
    cond = jnp.zeros((n_cond, D_MODEL), F32).at[0].set(c_ctx).at[1:1 + sb].set(c)
    ada_all = _ada_call(cond, w_ada, b_ada[:, None, :]).reshape(DEPTH, n_cond, 6, D_MODEL)
    ada_all = jnp.pad(ada_all, ((0, 0), (0, 0), (0, 2), (0, 0)))
    ada_p = (ada_all, 0, 0)
    ada_s = (ada_all, 1, 1)

    cos32, sin32 = _rope_tables(ss)
    head_c = np.concatenate([np.ones((ss, MLA_NOPE_DIM)), cos32, np.zeros((ss, MLA_HEAD_PAD - MLA_QK_DIM))], -1)
    head_s = np.concatenate([np.zeros((ss, MLA_NOPE_DIM)), sin32, np.zeros((ss, MLA_HEAD_PAD - MLA_QK_DIM))], -1)
    rope_tabs = tuple(jnp.asarray(t, F32) for t in (
        head_c * (MLA_SCALE * LOG2E), head_s * (MLA_SCALE * LOG2E),
        np.concatenate([cos32, sin32, np.zeros((ss, LANES - 2 * MLA_ROPE_DIM))], -1)))

    ckr_pad = jnp.pad(cache_mla_kr, ((0, 0), (0, 0), (0, 0), (0, LANES - MLA_ROPE_DIM)))

    experts = (expert_w_gate, expert_w_up, expert_w_down)
    lw = _prepare_weights(p)
    tq = 512
    grid_rows = ss // GRID_W
    local_bias = _na_bias_tables(na_rpb, grid_rows) + (_window_rows(grid_rows, tq // GRID_W)[0],)
    xp, xs = x_prompt, x_sample
    new_k, new_v, new_ckv, new_kr = [], [], [], []
    for l in range(DEPTH):

        qn, kn, vn, qm, km, vm, gates, ckv, k_r = _proj_call(xp, ada_p, lw, l, None, F32, True, ps)
        o_na = _attn_call(qn, [(kn, vn, _seg3(kn, NA_WIDTH), _seg3(vn, NA_WIDTH))], None, NA_HEAD_DIM, ps,
                          NA_HEADS, "attn_ctx_na")
        o_mla = _attn_call(qm, [(km, vm, _seg3(km, MLA_PAD_WIDTH), _seg3(vm, NA_WIDTH))], None, MLA_HEAD_PAD, ps,
                           MLA_HEADS, "attn_ctx_mla")
        pair = lambda a: a.reshape((pb // 2, 2 * ps) + a.shape[2:])
        unpair = lambda a: a.reshape((pb, ps) + a.shape[2:])
        x1, h2, route, counts = _merge_call(pair(o_na), pair(o_mla), pair(gates), pair(xp), ada_p, lw, l, 2 * ps,
                                            "merge_ctx")
        xp = _moe(unpair(h2), unpair(route), counts, unpair(x1), ada_p, lw, experts, l, 128, ps, "ctx")
        new_k.append(kn)
        new_v.append(vn)
        new_ckv.append(ckv)
        new_kr.append(k_r)

        qn, kn, vn, qm, km, vm, gates = _proj_call(xs, ada_s, lw, l, rope_tabs, BF16, False, 512)
        km_ctx, vm_ctx = _ctxkv_call(cache_mla_ckv, ckr_pad, l, lw)
        o_na = _attn_call(
            qn, [(cache_na_k, cache_na_v, _seg_cache(cache_na_k, l), _seg_cache(cache_na_v, l)),
                 (kn, vn, _seg3(kn, NA_WIDTH), _seg3(vn, NA_WIDTH))], local_bias, NA_HEAD_DIM, tq, NA_HEADS,
            "attn_lat_na", l)
        o_mla = _attn_call(
            qm, [(km_ctx, vm_ctx, _seg3(km_ctx, MLA_PAD_WIDTH), _seg3(vm_ctx, NA_WIDTH)),
                 (km, vm, _seg3(km, MLA_PAD_WIDTH), _seg3(vm, NA_WIDTH))], None, MLA_HEAD_PAD, 512, MLA_HEADS,
            "attn_lat_mla")
        x1, h2, route, counts = _merge_call(o_na, o_mla, gates, xs, ada_s, lw, l, 512, "merge_lat")
        xs = _moe(h2, route, counts, x1, ada_s, lw, experts, l, 256, 256, "lat")

    def stack(parts, tail):
        return jnp.stack(parts, axis=1).reshape((pb, DEPTH, ps) + tail)

    return (xp, xs, stack(new_k, (NA_HEADS, NA_HEAD_DIM)), stack(new_v, (NA_HEADS, NA_HEAD_DIM)),
            stack(new_ckv, (MLA_KV_LORA,)), stack(new_kr, (MLA_ROPE_DIM,)))
```

```python
import functools

import numpy as np
import jax
import jax.numpy as jnp
from jax import lax
from jax.experimental import pallas as pl
from jax.experimental.pallas import tpu as pltpu

F32 = jnp.float32
BF16 = jnp.bfloat16

D_MODEL = 1024
DEPTH = 2
GRID_W = 64
NA_HEADS = 8
NA_HEAD_DIM = 64
NA_KR = 8
NA_KC = 16
NA_WIDTH = NA_HEADS * NA_HEAD_DIM
NA_SCALE = NA_HEAD_DIM ** -0.5
MLA_HEADS = 8
MLA_NOPE_DIM = 64
MLA_ROPE_DIM = 32
MLA_V_DIM = 64
MLA_Q_LORA = 384
MLA_KV_LORA = 256
MLA_QK_DIM = MLA_NOPE_DIM + MLA_ROPE_DIM
MLA_SCALE = MLA_QK_DIM ** -0.5
ROPE_BASE = 10000.0
N_GROUPS = 4
EXPERTS_PER_GROUP = 8
N_EXPERTS = N_GROUPS * EXPERTS_PER_GROUP
EXPERT_FF = 256
LN_EPS = 1e-5
RMS_EPS = 1e-6
NEG_INF = -1e30
LOG2E = 1.4426950408889634
DEEPNORM_ALPHA = (2 * DEPTH) ** 0.25

LANES = 128
MLA_HEAD_PAD = LANES
MLA_PAD_WIDTH = MLA_HEADS * MLA_HEAD_PAD
C_Q = 0
C_K = C_Q + NA_WIDTH
C_V = C_K + NA_WIDTH
C_CQ = C_V + NA_WIDTH
C_CKV = C_CQ + MLA_Q_LORA
C_KR = C_CKV + MLA_KV_LORA
C_GATE = C_KR + LANES
C_END = C_GATE + 2 * D_MODEL
R_EXP0 = N_GROUPS

VMEM_LIMIT = 56 * 1024 * 1024


def _rope_partner():
    j = np.arange(MLA_ROPE_DIM)
    n_freq = MLA_ROPE_DIM // 4
    axis, within = j // (2 * n_freq), j % (2 * n_freq)
    half, f = within // n_freq, within % n_freq
    return axis * 2 * n_freq + (1 - half) * n_freq + f, half


def _rope_tables(n_tokens):
    n_freq = MLA_ROPE_DIM // 4
    inv_freq = ROPE_BASE ** (-np.arange(n_freq, dtype=np.float64) / n_freq)
    t = np.arange(n_tokens)
    ang = np.concatenate([(t // GRID_W)[:, None] * inv_freq, (t % GRID_W)[:, None] * inv_freq], axis=-1)
    cos, sin = np.cos(ang), np.sin(ang)
    cos32 = np.concatenate([cos[:, :n_freq], cos[:, :n_freq], cos[:, n_freq:], cos[:, n_freq:]], axis=-1)
    sin32 = np.concatenate([-sin[:, :n_freq], sin[:, :n_freq], -sin[:, n_freq:], sin[:, n_freq:]], axis=-1)
    return cos32, sin32


def _ln_plain(x):
    mu = jnp.mean(x, axis=-1, keepdims=True)
    xc = x - mu
    var = jnp.mean(xc * xc, axis=-1, keepdims=True)
    return xc * lax.rsqrt(var + LN_EPS)


def _rms(x, g):
    return x * lax.rsqrt(jnp.mean(x * x, axis=-1, keepdims=True) + RMS_EPS) * g


def _sigmoid(x):
    return 1.0 / (1.0 + jnp.exp(-x))


def _dot(a, b):
    return jnp.dot(a, b, preferred_element_type=F32)


def _params(sem):
    return pltpu.CompilerParams(dimension_semantics=sem, vmem_limit_bytes=VMEM_LIMIT)


def _layer_spec(arr, layer):
    zeros = (0,) * (arr.ndim - 1)
    return pl.BlockSpec((None,) + arr.shape[1:], lambda *_: (layer,) + zeros, pipeline_mode=pl.Buffered(1))


def _ada_spec(ada, layer, batch_of):
    _, first, step = ada
    return pl.BlockSpec((None, None, 8, D_MODEL), lambda *idx: (layer, first + step * batch_of(*idx), 0, 0))


def _const_spec(arr):
    zeros = (0,) * arr.ndim
    return pl.BlockSpec(arr.shape, lambda *_: zeros, pipeline_mode=pl.Buffered(1))


TOKEN_TILE = (D_MODEL // LANES, LANES)


def _store_token_tiles(ref, val):
    chunks = jnp.stack([val[:, k * LANES:(k + 1) * LANES] for k in range(TOKEN_TILE[0])], axis=0)
    ref[...] = jnp.swapaxes(chunks, 0, 1)


def _load_token_tiles(ref):
    chunks = jnp.swapaxes(ref[...], 0, 1)
    return jnp.concatenate([chunks[k] for k in range(TOKEN_TILE[0])], axis=-1)


def _ada_body(c_ref, w_ref, b_ref, o_ref):
    c = c_ref[...]
    s = (c * _sigmoid(c)).astype(BF16)
    o_ref[0] = _dot(s, w_ref[0].astype(BF16)) + b_ref[0]


def _ada_call(cond, w_ada, b_ada):
    n_rows = cond.shape[0]
    tn = 1024
    return pl.pallas_call(
        _ada_body,
        grid=(DEPTH, 6 * D_MODEL // tn),
        in_specs=[
            pl.BlockSpec((n_rows, D_MODEL), lambda l, j: (0, 0)),
            pl.BlockSpec((1, D_MODEL, tn), lambda l, j: (l, 0, j)),
            pl.BlockSpec((1, 1, tn), lambda l, j: (l, 0, j)),
        ],
        out_specs=pl.BlockSpec((1, n_rows, tn), lambda l, j: (l, 0, j)),
        out_shape=jax.ShapeDtypeStruct((DEPTH, n_rows, 6 * D_MODEL), F32),
        compiler_params=_params(("arbitrary", "arbitrary")),
        name="ada",
    )(cond, w_ada, b_ada)


def _proj_body(*refs, rope, emit_cache):
    it = iter(refs)
    x_ref, ada_ref, w_ref, qn_ref, kvn_ref, wuq_ref = (next(it) for _ in range(6))
    wuqs_ref = next(it) if rope else None
    wuk_ref, wuv_ref, e_ref = next(it), next(it), next(it)
    if rope:
        cq_ref, sq_ref, tk_ref = next(it), next(it), next(it)
    oq_ref, ok_ref, ov_ref, oqm_ref, okm_ref, ovm_ref, og_ref = (next(it) for _ in range(7))
    if emit_cache:
        ockv_ref, okr_ref = next(it), next(it)

    x = x_ref[...]
    sh = ada_ref[0:1, :]
    sc = ada_ref[1:2, :]
    h = (_ln_plain(x) * (1.0 + sc) + sh).astype(BF16)

    oq_ref[...] = (_dot(h, w_ref[:, C_Q:C_K]) * LOG2E).astype(oq_ref.dtype)
    ok_ref[...] = _dot(h, w_ref[:, C_K:C_V]).astype(ok_ref.dtype)
    ov_ref[...] = _dot(h, w_ref[:, C_V:C_CQ]).astype(ov_ref.dtype)

    c_q = _rms(_dot(h, w_ref[:, C_CQ:C_CKV]), qn_ref[...]).astype(BF16)
    q = _dot(c_q, wuq_ref[...])
    if rope:
        q = (q * jnp.tile(cq_ref[...], (1, MLA_HEADS))
             + _dot(c_q, wuqs_ref[...]) * jnp.tile(sq_ref[...], (1, MLA_HEADS)))
    else:
        q = q * (MLA_SCALE * LOG2E)
    oqm_ref[...] = q.astype(BF16)

    c_kv = _rms(_dot(h, w_ref[:, C_CKV:C_KR]), kvn_ref[...])
    slab = _dot(h, w_ref[:, C_KR:C_GATE])
    if emit_cache:
        ockv_ref[...] = c_kv
        okr_ref[...] = slab[:, :MLA_ROPE_DIM]
    if rope:
        slab = slab * tk_ref[...]
    c_kv = c_kv.astype(BF16)
    okm_ref[...] = (_dot(c_kv, wuk_ref[...]) + _dot(slab.astype(BF16), e_ref[...])).astype(BF16)
    ovm_ref[...] = _dot(c_kv, wuv_ref[...]).astype(BF16)

    og_ref[...] = _sigmoid(_dot(h, w_ref[:, C_GATE:C_END])).astype(BF16)


def _proj_call(x, ada, lw, layer, rope_tabs, kv_dtype, emit_cache, tm):
    bsz, seq, _ = x.shape
    nj = seq // tm
    rope = rope_tabs is not None

    def tok(width):
        return pl.BlockSpec((None, tm, width), lambda b, j: (b, j, 0))

    def full(arr):
        return _layer_spec(arr, layer)

    def tab(width):
        return pl.BlockSpec((tm, width), lambda b, j: (j, 0))

    args = [x, ada[0], lw["w_in"], lw["q_norm"], lw["kv_norm"], lw["w_uq"]]
    specs = [tok(D_MODEL), _ada_spec(ada, layer, lambda b, j: b),
             full(lw["w_in"]), full(lw["q_norm"]), full(lw["kv_norm"]), full(lw["w_uq"])]
    if rope:
        args.append(lw["w_uq_sw"])
        specs.append(full(lw["w_uq_sw"]))
    e_mat = lw["e_rope"] if rope else lw["e_plain"]
    args += [lw["w_uk"], lw["w_uv"], e_mat]
    specs += [full(lw["w_uk"]), full(lw["w_uv"]), _const_spec(e_mat)]
    if rope:
        args += list(rope_tabs)
        specs += [tab(MLA_HEAD_PAD), tab(MLA_HEAD_PAD), tab(LANES)]

    def shp(width, dt):
        return jax.ShapeDtypeStruct((bsz, seq, width), dt)

    out_shape = [shp(NA_WIDTH, BF16), shp(NA_WIDTH, kv_dtype), shp(NA_WIDTH, kv_dtype),
                 shp(MLA_PAD_WIDTH, BF16), shp(MLA_PAD_WIDTH, BF16), shp(NA_WIDTH, BF16),
                 shp(2 * D_MODEL, BF16)]
    out_specs = [tok(NA_WIDTH), tok(NA_WIDTH), tok(NA_WIDTH), tok(MLA_PAD_WIDTH), tok(MLA_PAD_WIDTH),
                 tok(NA_WIDTH), tok(2 * D_MODEL)]
    if emit_cache:
        out_shape += [shp(MLA_KV_LORA, F32), shp(MLA_ROPE_DIM, F32)]
        out_specs += [tok(MLA_KV_LORA), tok(MLA_ROPE_DIM)]
    return pl.pallas_call(
        functools.partial(_proj_body, rope=rope, emit_cache=emit_cache),
        grid=(bsz, nj),
        in_specs=specs,
        out_specs=out_specs,
        out_shape=out_shape,
        compiler_params=_params(("arbitrary", "arbitrary")),
        name="proj_lat" if rope else "proj_ctx",
    )(*args)


def _ctxkv_body(ckv_ref, kr_ref, wuk_ref, wuv_ref, e_ref, okm_ref, ovm_ref):
    c_kv = ckv_ref[...].astype(BF16)
    okm_ref[...] = (_dot(c_kv, wuk_ref[...]) + _dot(kr_ref[...].astype(BF16), e_ref[...])).astype(BF16)
    ovm_ref[...] = _dot(c_kv, wuv_ref[...]).astype(BF16)


def _ctxkv_call(cache_ckv, cache_kr_pad, layer, lw):
    bsz, _, past, _ = cache_ckv.shape

    def full(arr):
        return _layer_spec(arr, layer)

    return pl.pallas_call(
        _ctxkv_body,
        grid=(bsz,),
        in_specs=[
            pl.BlockSpec((None, None, past, MLA_KV_LORA), lambda b: (b, layer, 0, 0)),
            pl.BlockSpec((None, None, past, LANES), lambda b: (b, layer, 0, 0)),
            full(lw["w_uk"]), full(lw["w_uv"]), _const_spec(lw["e_plain"]),
        ],
        out_specs=[pl.BlockSpec((None, past, MLA_PAD_WIDTH), lambda b: (b, 0, 0)),
                   pl.BlockSpec((None, past, NA_WIDTH), lambda b: (b, 0, 0))],
        out_shape=[jax.ShapeDtypeStruct((bsz, past, MLA_PAD_WIDTH), BF16),
                   jax.ShapeDtypeStruct((bsz, past, NA_WIDTH), BF16)],
        compiler_params=_params(("arbitrary",)),
        name="ctxkv",
    )(cache_ckv, cache_kr_pad, lw["w_uk"], lw["w_uv"], lw["e_plain"])


def _local_bias(pb_ref, rm_ref, hh, q0, k0, q_rows, k_rows, grid_rows):
    rows = []
    for qr in range(q_rows):
        blocks = []
        for p in range(k_rows // 2):
            pair = k0 // 2 + p
            e = 2 * pair - (q0 + qr) + grid_rows
            blocks.append(pb_ref[hh, e] + rm_ref[pl.ds((q0 + qr) * (grid_rows // 2) + pair, 1), :])
        rows.append(jnp.concatenate(blocks, axis=-1))
    return jnp.concatenate(rows, axis=0)


def _attn_body(*refs, n_seg, dq, local, heads):
    q_ref = refs[0]
    seg_refs = [(refs[1 + 2 * i], refs[2 + 2 * i]) for i in range(n_seg)]
    o_ref = refs[-1]
    if local is not None:
        pb_ref, rm_ref = refs[1 + 2 * n_seg], refs[2 + 2 * n_seg]
        q_rows, k_rows, grid_rows = local
        q0 = pl.program_id(1) * q_rows
        k0 = jnp.minimum(jnp.clip(q0 - NA_KR // 2, 0, grid_rows - NA_KR), grid_rows - k_rows)
        keys = pl.ds(pl.multiple_of(k0 * GRID_W, GRID_W), k_rows * GRID_W)
    head_major = [tuple(jnp.swapaxes(r[...], 0, 1).astype(BF16) for r in seg) if len(seg[0].shape) == 3 else None
                  for seg in seg_refs]
    outs = []
    for hh in range(heads):
        qh = q_ref[:, hh * dq:(hh + 1) * dq]
        scores = []
        values = []
        for i, (k_ref, v_ref) in enumerate(seg_refs):
            windowed = local is not None and i == n_seg - 1
            rows = keys if windowed else slice(None)
            if head_major[i] is not None:
                kh, vh = head_major[i][0][hh], head_major[i][1][hh]
            else:
                kh = k_ref[rows, hh * dq:(hh + 1) * dq].astype(BF16)
                vh = v_ref[rows, hh * MLA_V_DIM:(hh + 1) * MLA_V_DIM].astype(BF16)
            values.append(vh)
            s = lax.dot_general(qh, kh, (((1,), (1,)), ((), ())), preferred_element_type=F32)
            if windowed:
                s = s + _local_bias(pb_ref, rm_ref, hh, q0, k0, q_rows, k_rows, grid_rows)
            scores.append(s)
        m = functools.reduce(jnp.maximum, [jnp.max(s, axis=-1, keepdims=True) for s in scores])
        den = None
        acc = None
        for s, vh in zip(scores, values):
            e = jnp.exp2(s - m)
            d = jnp.sum(e, axis=-1, keepdims=True)
            a = _dot(e.astype(BF16), vh)
            den = d if den is None else den + d
            acc = a if acc is None else acc + a
        outs.append(acc / den)
    o_ref[...] = jnp.concatenate(outs, axis=-1).astype(o_ref.dtype)


def _attn_call(q, segs, local_bias, dq, tq, heads, name, layer=0):
    bsz, seq, _ = q.shape
    args = [q]
    specs = [pl.BlockSpec((None, tq, heads * dq), lambda hp, qt, b: (b, qt, hp))]
    for k, v, ks, vs in segs:
        args += [k, v]
        specs += [ks, vs]
    local = None
    if local_bias is not None:
        pair_blocks, row_masks, k_rows = local_bias
        args += [pair_blocks, row_masks]
        specs += [pl.BlockSpec((None, heads) + pair_blocks.shape[2:], lambda hp, qt, b: (layer, hp, 0, 0, 0),
                               pipeline_mode=pl.Buffered(1)),
                  _const_spec(row_masks)]
        local = (tq // GRID_W, k_rows, seq // GRID_W)
    return pl.pallas_call(
        functools.partial(_attn_body, n_seg=len(segs), dq=dq, local=local, heads=heads),
        grid=(NA_HEADS // heads, seq // tq, bsz),
        in_specs=specs,
        out_specs=pl.BlockSpec((None, tq, heads * MLA_V_DIM), lambda hp, qt, b: (b, qt, hp)),
        out_shape=jax.ShapeDtypeStruct((bsz, seq, NA_WIDTH), BF16),
        compiler_params=_params(("arbitrary", "arbitrary", "arbitrary")),
        name=name,
    )(*args)


def _seg3(arr, width):
    n = arr.shape[1]
    return pl.BlockSpec((None, n, width), lambda hp, qt, b: (b, 0, hp))


def _seg_cache(arr, layer):
    return pl.BlockSpec((None, None) + arr.shape[2:], lambda hp, qt, b: (b, layer, 0, 0, 0))


def _route(logits):
    lane = lax.broadcasted_iota(jnp.int32, logits.shape, 1)
    lane_f = lane.astype(F32)
    far = float(LANES)
    is_g = lane < N_GROUPS
    gl = jnp.where(is_g, logits, NEG_INF)
    gmax = jnp.max(gl, axis=-1, keepdims=True)
    gsum = jnp.sum(jnp.exp(gl - gmax), axis=-1, keepdims=True)
    g_w = 1.0 / gsum
    g_idx = jnp.min(jnp.where(gl == gmax, lane_f, far), axis=-1, keepdims=True)
    e_group = ((lane - R_EXP0) >> 3).astype(F32)
    is_e = (lane >= R_EXP0) & (lane < R_EXP0 + N_EXPERTS) & (e_group == g_idx)
    el = jnp.where(is_e, logits, NEG_INF)
    m1 = jnp.max(el, axis=-1, keepdims=True)
    i1 = jnp.min(jnp.where(el == m1, lane_f, far), axis=-1, keepdims=True)
    el2 = jnp.where(lane_f == i1, NEG_INF, el)
    m2 = jnp.max(el2, axis=-1, keepdims=True)
    i2 = jnp.min(jnp.where(el2 == m2, lane_f, far), axis=-1, keepdims=True)
    r = jnp.exp(m2 - m1)
    w1 = g_w / (1.0 + r)
    w2 = g_w * r / (1.0 + r)
    return lane_f, i1, i2, w1, w2


def _merge_body(ona_ref, omla_ref, g_ref, x_ref, ada_ref, wna_ref, wmla_ref, wout_ref, lng_ref, lnb_ref,
                wr_ref, wrl_ref, br_ref, ox_ref, oh_ref, or_ref, ocnt_ref, carry_ref):
    first = (pl.program_id(0) == 0) & (pl.program_id(1) == 0)

    @pl.when(first)
    def _():
        carry_ref[...] = jnp.zeros_like(carry_ref)

    y_na = _dot(ona_ref[...], wna_ref[...])
    y_mla = _dot(omla_ref[...], wmla_ref[...])
    g_na = g_ref[:, :D_MODEL].astype(F32)
    g_mla = g_ref[:, D_MODEL:].astype(F32)
    y = _dot((g_na * y_na + g_mla * y_mla).astype(BF16), wout_ref[...])
    g1 = ada_ref[2:3, :]
    x1 = _ln_plain(DEEPNORM_ALPHA * x_ref[...] + g1 * y) * lng_ref[...] + lnb_ref[...]
    ox_ref[...] = x1
    h2 = _ln_plain(x1) * (1.0 + ada_ref[4:5, :]) + ada_ref[3:4, :]
    _store_token_tiles(oh_ref, h2)
    h_hi = h2.astype(BF16)
    h_lo = (h2 - h_hi.astype(F32)).astype(BF16)
    logits = (_dot(h_hi, wr_ref[...]) + (_dot(h_lo, wr_ref[...]) + _dot(h_hi, wrl_ref[...]))) + br_ref[...]
    lane_f, i1, i2, w1, w2 = _route(logits)

    tm = logits.shape[0]
    sel1 = lane_f == i1
    sel2 = lane_f == i2
    used = (jnp.where(sel1, 1.0, 0.0) + jnp.where(sel2, 1.0, 0.0)).astype(BF16)
    row = lax.broadcasted_iota(jnp.int32, (tm, tm), 0)
    col = lax.broadcasted_iota(jnp.int32, (tm, tm), 1)
    earlier = jnp.where(row > col, 1.0, 0.0).astype(BF16)
    before = _dot(earlier, used) + carry_ref[0:1, :]
    rank1 = jnp.sum(jnp.where(sel1, before, 0.0), axis=-1, keepdims=True)
    rank2 = jnp.sum(jnp.where(sel2, before, 0.0), axis=-1, keepdims=True)
    carry_ref[...] += _dot(jnp.ones((8, tm), BF16), used)
    ocnt_ref[...] = carry_ref[...]

    fields = (i1 - R_EXP0, i2 - R_EXP0, w1, w2, rank1, rank2)
    route = jnp.zeros_like(logits)
    for k, val in enumerate(fields):
        route = jnp.where(lane_f == float(k), val, route)
    or_ref[...] = route


def _merge_call(o_na, o_mla, gates, x, ada, lw, layer, tm, name):
    bsz, seq, _ = x.shape

    def tok(width):
        return pl.BlockSpec((None, tm, width), lambda b, j: (b, j, 0))

    def full(arr):
        return _layer_spec(arr, layer)

    ws = [lw["w_o_na"], lw["w_o_mla"], lw["w_out"], lw["ln1_g"], lw["ln1_b"], lw["w_r"], lw["w_r_lo"], lw["b_r"]]
    return pl.pallas_call(
        _merge_body,
        grid=(bsz, seq // tm),
        in_specs=[tok(NA_WIDTH), tok(NA_WIDTH), tok(2 * D_MODEL), tok(D_MODEL),
                  _ada_spec(ada, layer, lambda b, j: b)] + [full(w) for w in ws],
        out_specs=[tok(D_MODEL), pl.BlockSpec((None, tm) + TOKEN_TILE, lambda b, j: (b, j, 0, 0)), tok(LANES),
                   pl.BlockSpec((8, LANES), lambda b, j: (0, 0))],
        out_shape=[jax.ShapeDtypeStruct((bsz, seq, D_MODEL), F32),
                   jax.ShapeDtypeStruct((bsz, seq) + TOKEN_TILE, F32),
                   jax.ShapeDtypeStruct((bsz, seq, LANES), F32),
                   jax.ShapeDtypeStruct((8, LANES), F32)],
        scratch_shapes=[pltpu.VMEM((8, LANES), F32)],
        compiler_params=_params(("arbitrary", "arbitrary")),
        name=name,
    )(o_na, o_mla, gates, x, ada[0], *ws)


POS_BITS = 16
POS_MASK = (1 << POS_BITS) - 1


def _plan(route, counts, n_tok, tr):
    r = route.reshape(n_tok, LANES)
    e1, e2 = r[:, 0].astype(jnp.int32), r[:, 1].astype(jnp.int32)
    rk1, rk2 = r[:, 4].astype(jnp.int32), r[:, 5].astype(jnp.int32)
    cnt = counts[0, R_EXP0:R_EXP0 + N_EXPERTS].astype(jnp.int32)
    tiles_e = (cnt + tr - 1) // tr
    tile_end = jnp.cumsum(tiles_e)
    row_off = (tile_end - tiles_e) * tr
    eid = jnp.arange(N_EXPERTS, dtype=jnp.int32)
    p1 = jnp.sum(jnp.where(e1[:, None] == eid, row_off, 0), axis=-1) + rk1
    p2 = jnp.sum(jnp.where(e2[:, None] == eid, row_off, 0), axis=-1) + rk2
    n_tiles = 2 * n_tok // tr + N_EXPERTS
    n_used = tile_end[-1]
    ti = jnp.arange(n_tiles, dtype=jnp.int32)
    tile_expert = jnp.sum(jnp.minimum(ti, n_used - 1)[:, None] >= tile_end[None, :], axis=-1).astype(jnp.int32)
    own = tile_expert[:, None] == eid
    seg_end = jnp.sum(jnp.where(own, row_off + cnt, 0), axis=-1)
    tile_rows = jnp.clip(seg_end - ti * tr, 1, tr).astype(jnp.int32)
    return p1 | (p2 << POS_BITS), tile_expert, tile_rows, jnp.stack([n_used, n_used]).astype(jnp.int32), n_tiles


ROW_GROUP = 8


def _expert_body(pos_ref, te_ref, rows_ref, meta_ref, h_ref, wg_ref, wu_ref, wd_ref, o_ref,
                 src_ref, hbuf, wg_b, wu_b, wd_b, *, n_tok, tr):
    i = pl.program_id(0)
    n_used = meta_ref[0]

    @pl.when(i == 0)
    def _():
        hbuf[...] = jnp.zeros_like(hbuf)

        def place(t, carry):
            packed = pos_ref[t]
            src_ref[packed & POS_MASK] = t
            src_ref[packed >> POS_BITS] = t
            return carry
        lax.fori_loop(0, n_tok, place, 0, unroll=8)

    @pl.when(i >= n_used)
    def _():
        o_ref[...] = jnp.zeros_like(o_ref)

    @pl.when(i < n_used)
    def _():
        base = i * tr
        last = rows_ref[i] - 1

        def gather(g, carry):
            for k in range(ROW_GROUP):
                r = g * ROW_GROUP + k
                hbuf[r] = h_ref[src_ref[base + jnp.minimum(r, last)]]
            return carry
        lax.fori_loop(0, (last + ROW_GROUP) // ROW_GROUP, gather, 0)

        @pl.when((i == 0) | (te_ref[i] != te_ref[jnp.maximum(i - 1, 0)]))
        def _():
            wg_b[...] = wg_ref[...].astype(BF16)
            wu_b[...] = wu_ref[...].astype(BF16)
            wd_b[...] = wd_ref[...].astype(BF16)

        h = _load_token_tiles(hbuf).astype(BF16)
        gate = _dot(h, wg_b[...])
        up = _dot(h, wu_b[...])
        hid = (gate * _sigmoid(gate) * up).astype(BF16)
        _store_token_tiles(o_ref, _dot(hid, wd_b[...]))


def _expert_call(h2, plan, w_gate, w_up, w_down, layer, tr, name):
    packed, tile_expert, tile_rows, meta, n_tiles = plan
    n_tok = h2.shape[0]
    n_rows = n_tiles * tr
    return pl.pallas_call(
        functools.partial(_expert_body, n_tok=n_tok, tr=tr),
        grid_spec=pltpu.PrefetchScalarGridSpec(
            num_scalar_prefetch=4,
            grid=(n_tiles,),
            in_specs=[
                _const_spec(h2),
                pl.BlockSpec((None, None, D_MODEL, EXPERT_FF), lambda i, pos, te, nr, meta: (layer, te[i], 0, 0)),
                pl.BlockSpec((None, None, D_MODEL, EXPERT_FF), lambda i, pos, te, nr, meta: (layer, te[i], 0, 0)),
                pl.BlockSpec((None, None, EXPERT_FF, D_MODEL), lambda i, pos, te, nr, meta: (layer, te[i], 0, 0)),
            ],
            out_specs=pl.BlockSpec((tr,) + TOKEN_TILE, lambda i, pos, te, nr, meta: (i, 0, 0)),
            scratch_shapes=[
                pltpu.SMEM((n_rows,), jnp.int32),
                pltpu.VMEM((tr,) + TOKEN_TILE, F32),
                pltpu.VMEM((D_MODEL, EXPERT_FF), BF16),
                pltpu.VMEM((D_MODEL, EXPERT_FF), BF16),
                pltpu.VMEM((EXPERT_FF, D_MODEL), BF16),
            ],
        ),
        out_shape=jax.ShapeDtypeStruct((n_rows,) + TOKEN_TILE, F32),
        compiler_params=_params(("arbitrary",)),
        name=name,
    )(packed, tile_expert, tile_rows, meta, h2, w_gate, w_up, w_down)


def _combine_body(pos_ref, y_hbm, r_ref, x_ref, ada_ref, lng_ref, lnb_ref, o_ref, ybuf, sem, *, tm):
    i = pl.program_id(0)
    n = pl.num_programs(0)

    def for_tile_rows(tile, slot, fn):
        base = tile * tm
        dst1 = ybuf.at[slot, 0]
        dst2 = ybuf.at[slot, 1]
        dsem = sem.at[slot]

        def body(g, carry):
            for k in range(ROW_GROUP):
                r = g * ROW_GROUP + k
                packed = pos_ref[base + r]
                fn(pltpu.make_async_copy(y_hbm.at[packed & POS_MASK], dst1.at[r], dsem))
                fn(pltpu.make_async_copy(y_hbm.at[packed >> POS_BITS], dst2.at[r], dsem))
            return carry
        lax.fori_loop(0, tm // ROW_GROUP, body, 0)

    @pl.when(i == 0)
    def _():
        for_tile_rows(0, 0, lambda cp: cp.start())

    @pl.when(i + 1 < n)
    def _():
        for_tile_rows(i + 1, (i + 1) & 1, lambda cp: cp.start())

    slot = i & 1
    for_tile_rows(i, slot, lambda cp: cp.wait())

    route = r_ref[...]
    lane = lax.broadcasted_iota(jnp.int32, route.shape, 1)
    w1 = jnp.sum(jnp.where(lane == 2, route, 0.0), axis=-1, keepdims=True)
    w2 = jnp.sum(jnp.where(lane == 3, route, 0.0), axis=-1, keepdims=True)
    y = w1 * _load_token_tiles(ybuf.at[slot, 0]) + w2 * _load_token_tiles(ybuf.at[slot, 1])
    g2 = ada_ref[5:6, :]
    o_ref[...] = _ln_plain(DEEPNORM_ALPHA * x_ref[...] + g2 * y) * lng_ref[...] + lnb_ref[...]


def _combine_call(ys, packed, route, x1, ada, lw, layer, seq, tm, name):
    n_tok = x1.shape[0]
    per_batch = seq // tm

    def tok(width):
        return pl.BlockSpec((tm, width), lambda i, pos: (i, 0))

    def full(arr):
        return _layer_spec(arr, layer)

    return pl.pallas_call(
        functools.partial(_combine_body, tm=tm),
        grid_spec=pltpu.PrefetchScalarGridSpec(
            num_scalar_prefetch=1,
            grid=(n_tok // tm,),
            in_specs=[pl.BlockSpec(memory_space=pl.ANY), tok(LANES), tok(D_MODEL),
                      _ada_spec(ada, layer, lambda i, pos: i // per_batch),
                      full(lw["ln2_g"]), full(lw["ln2_b"])],
            out_specs=tok(D_MODEL),
            scratch_shapes=[pltpu.VMEM((2, 2, tm) + TOKEN_TILE, F32), pltpu.SemaphoreType.DMA((2,))],
        ),
        out_shape=jax.ShapeDtypeStruct((n_tok, D_MODEL), F32),
        compiler_params=_params(("arbitrary",)),
        name=name,
    )(packed, ys, route, x1, ada[0], lw["ln2_g"], lw["ln2_b"])


def _moe(h2, route, counts, x1, ada, lw, experts, layer, tr, tm, name):
    bsz, seq, _ = x1.shape
    n_tok = bsz * seq
    plan = _plan(route, counts, n_tok, tr)
    ys = _expert_call(h2.reshape((n_tok,) + TOKEN_TILE), plan, *experts, layer, tr, "expert_" + name)
    out = _combine_call(ys, plan[0], route.reshape(n_tok, LANES), x1.reshape(n_tok, D_MODEL), ada, lw, layer,
                        seq, tm, "combine_" + name)
    return out.reshape(bsz, seq, D_MODEL)


def _placement(rows_used):
    e = np.zeros((LANES, MLA_PAD_WIDTH), np.float32)
    for r in range(rows_used):
        for hd in range(MLA_HEADS):
            e[r, hd * MLA_HEAD_PAD + MLA_NOPE_DIM + (r % MLA_ROPE_DIM)] = 1.0
    return jnp.asarray(e, BF16)


def _prepare_weights(p):
    partner, _ = _rope_partner()
    n_layers = p["w_in"].shape[0]
    w = p["w_in"]
    k_r = w[:, :, 2176:2208]
    pieces = [w[:, :, 0:512] * NA_SCALE, w[:, :, 512:2176], k_r, k_r[:, :, partner],
              jnp.zeros((n_layers, D_MODEL, LANES - 2 * MLA_ROPE_DIM), F32), w[:, :, 2208:]]
    w_in = jnp.concatenate([piece.astype(BF16) for piece in pieces], axis=2)
    uq = p["w_uq"].reshape(n_layers, MLA_Q_LORA, MLA_HEADS, MLA_QK_DIM)
    pad = MLA_HEAD_PAD - MLA_QK_DIM
    no_pad = ((0, 0), (0, 0), (0, 0))
    w_uq = jnp.pad(uq, no_pad + ((0, pad),)).reshape(n_layers, MLA_Q_LORA, MLA_PAD_WIDTH).astype(BF16)
    uq_sw = jnp.pad(uq[..., MLA_NOPE_DIM + partner], no_pad + ((MLA_NOPE_DIM, pad),))
    w_uq_sw = uq_sw.reshape(n_layers, MLA_Q_LORA, MLA_PAD_WIDTH).astype(BF16)
    uk = p["w_uk"].reshape(n_layers, MLA_KV_LORA, MLA_HEADS, MLA_NOPE_DIM)
    w_uk = jnp.pad(uk, no_pad + ((0, MLA_HEAD_PAD - MLA_NOPE_DIM),)).reshape(
        n_layers, MLA_KV_LORA, MLA_PAD_WIDTH).astype(BF16)
    n_pad = LANES - N_GROUPS - N_EXPERTS
    w_r = jnp.concatenate([p["router_group_w"], p["router_expert_w"],
                           jnp.zeros((n_layers, D_MODEL, n_pad), F32)], axis=2)
    b_r = jnp.concatenate([p["router_group_b"], p["router_expert_b"], jnp.zeros((n_layers, n_pad), F32)], axis=1)
    w_r_hi = w_r.astype(BF16)

    def row(name):
        return p[name][:, None, :]

    return dict(
        w_in=w_in, w_uq=w_uq, w_uq_sw=w_uq_sw, w_uk=w_uk, w_uv=p["w_uv"].astype(BF16),
        q_norm=row("mla_q_norm"), kv_norm=row("mla_kv_norm"),
        e_plain=_placement(MLA_ROPE_DIM), e_rope=_placement(2 * MLA_ROPE_DIM),
        w_o_na=p["w_o_na"].astype(BF16), w_o_mla=p["w_o_mla"].astype(BF16), w_out=p["w_out"].astype(BF16),
        ln1_g=row("ln1_g"), ln1_b=row("ln1_b"), ln2_g=row("ln2_g"), ln2_b=row("ln2_b"),
        w_r=w_r_hi, w_r_lo=(w_r - w_r_hi.astype(F32)).astype(BF16), b_r=b_r[:, None, :],
    )


def _window_rows(rows, q_rows):
    firsts = np.arange(0, rows, q_rows)
    lo = np.clip(firsts - NA_KR // 2, 0, rows - NA_KR)
    hi = np.clip(firsts + q_rows - 1 - NA_KR // 2, 0, rows - NA_KR) + NA_KR
    k_rows = min(rows, int(np.max(hi - lo)) + int(np.max(hi - lo)) % 2)
    k0 = np.minimum(lo, rows - k_rows)
    assert np.all(k0 % 2 == 0) and np.all(k0 + k_rows >= hi), (k0, k_rows)
    return k_rows, k0


def _na_bias_tables(rpb, rows):
    assert rows >= NA_KR and rows % 2 == 0
    r = np.arange(rows)
    rs = np.clip(r - NA_KR // 2, 0, rows - NA_KR)
    vrow = (r[None, :] >= rs[:, None]) & (r[None, :] < rs[:, None] + NA_KR)
    c = np.arange(GRID_W)
    cs = np.clip(c - NA_KC // 2, 0, GRID_W - NA_KC)
    vcol = (c[None, :] >= cs[:, None]) & (c[None, :] < cs[:, None] + NA_KC)
    coff = c[None, :] - c[:, None] + NA_KC - 1
    onehot = ((coff[None] == np.arange(2 * NA_KC - 1)[:, None, None]) & vcol[None]).astype(np.float32)
    zero = np.zeros_like(onehot)
    reach = NA_KR - 1
    halves = []
    for placed in (np.concatenate([onehot, zero], axis=-1), np.concatenate([zero, onehot], axis=-1)):
        vals = jnp.einsum("lhdj,jqk->lhdqk", rpb * LOG2E, jnp.asarray(placed), precision=lax.Precision.HIGHEST)
        halves.append(jnp.pad(vals, ((0, 0), (0, 0), (rows - reach, rows - reach - 1), (0, 0), (0, 0))))
    valid_d = np.abs(np.arange(-rows, rows)) <= reach
    valid = valid_d[:, None, None] & vcol[None]
    valid_pair = np.concatenate([valid[:-1], valid[1:]], axis=-1)
    neg_mask = jnp.asarray(np.where(valid_pair, 0.0, NEG_INF).astype(np.float32))
    pair_blocks = halves[0][:, :, :-1] + halves[1][:, :, 1:] + neg_mask
    in_window = np.repeat(vrow.reshape(rows, rows // 2, 2), GRID_W, axis=-1)
    row_masks = np.where(in_window, 0.0, NEG_INF).astype(np.float32).reshape(rows * rows // 2, 2 * GRID_W)
    return pair_blocks, jnp.asarray(row_masks)


def kernel(x_prompt, x_sample, cache_na_k, cache_na_v, cache_mla_ckv, cache_mla_kr, c, c_ctx, w_in, mla_q_norm, mla_kv_norm, w_uq, w_uk, w_uv, na_rpb, w_o_na, w_o_mla, w_out, w_ada, b_ada, ln1_g, ln1_b, ln2_g, ln2_b, router_group_w, router_group_b, router_expert_w, router_expert_b, expert_w_gate, expert_w_up, expert_w_down):
    p = dict(w_in=w_in, mla_q_norm=mla_q_norm, mla_kv_norm=mla_kv_norm, w_uq=w_uq, w_uk=w_uk, w_uv=w_uv,
             w_o_na=w_o_na, w_o_mla=w_o_mla, w_out=w_out, ln1_g=ln1_g, ln1_b=ln1_b, ln2_g=ln2_g, ln2_b=ln2_b,
             router_group_w=router_group_w, router_group_b=router_group_b,
             router_expert_w=router_expert_w, router_expert_b=router_expert_b,
             expert_w_gate=expert_w_gate, expert_w_up=expert_w_up, expert_w_down=expert_w_down)
    pb, ps, _ = x_prompt.shape
    sb, ss, _ = x_sample.shape

    n_cond = 16
    cond = jnp.zeros((n_cond, D_MODEL), F32).at[0].set(c_ctx).at[1:1 + sb].set(c)
    ada_all = _ada_call(cond, w_ada, b_ada[:, None, :]).reshape(DEPTH, n_cond, 6, D_MODEL)
    ada_all = jnp.pad(ada_all, ((0, 0), (0, 0), (0, 2), (0, 0)))
    ada_p = (ada_all, 0, 0)
    ada_s = (ada_all, 1, 1)

    cos32, sin32 = _rope_tables(ss)
    head_c = np.concatenate([np.ones((ss, MLA_NOPE_DIM)), cos32, np.zeros((ss, MLA_HEAD_PAD - MLA_QK_DIM))], -1)
    head_s = np.concatenate([np.zeros((ss, MLA_NOPE_DIM)), sin32, np.zeros((ss, MLA_HEAD_PAD - MLA_QK_DIM))], -1)
    rope_tabs = tuple(jnp.asarray(t, F32) for t in (
        head_c * (MLA_SCALE * LOG2E), head_s * (MLA_SCALE * LOG2E),
        np.concatenate([cos32, sin32, np.zeros((ss, LANES - 2 * MLA_ROPE_DIM))], -1)))

    ckr_pad = jnp.pad(cache_mla_kr, ((0, 0), (0, 0), (0, 0), (0, LANES - MLA_ROPE_DIM)))

    experts = (expert_w_gate, expert_w_up, expert_w_down)
    lw = _prepare_weights(p)
    tq = 512
    grid_rows = ss // GRID_W
    local_bias = _na_bias_tables(na_rpb, grid_rows) + (_window_rows(grid_rows, tq // GRID_W)[0],)
    xp, xs = x_prompt, x_sample
    new_k, new_v, new_ckv, new_kr = [], [], [], []
    for l in range(DEPTH):

        qn, kn, vn, qm, km, vm, gates, ckv, k_r = _proj_call(xp, ada_p, lw, l, None, F32, True, ps)
        o_na = _attn_call(qn, [(kn, vn, _seg3(kn, NA_WIDTH), _seg3(vn, NA_WIDTH))], None, NA_HEAD_DIM, ps,
                          NA_HEADS, "attn_ctx_na")
        o_mla = _attn_call(qm, [(km, vm, _seg3(km, MLA_PAD_WIDTH), _seg3(vm, NA_WIDTH))], None, MLA_HEAD_PAD, ps,
                           MLA_HEADS, "attn_ctx_mla")
        pair = lambda a: a.reshape((pb // 2, 2 * ps) + a.shape[2:])
        unpair = lambda a: a.reshape((pb, ps) + a.shape[2:])
        x1, h2, route, counts = _merge_call(pair(o_na), pair(o_mla), pair(gates), pair(xp), ada_p, lw, l, 2 * ps,
                                            "merge_ctx")
        xp = _moe(unpair(h2), unpair(route), counts, unpair(x1), ada_p, lw, experts, l, 128, ps, "ctx")
        new_k.append(kn)
        new_v.append(vn)
        new_ckv.append(ckv)
        new_kr.append(k_r)

        qn, kn, vn, qm, km, vm, gates = _proj_call(xs, ada_s, lw, l, rope_tabs, BF16, False, 512)
        km_ctx, vm_ctx = _ctxkv_call(cache_mla_ckv, ckr_pad, l, lw)
        o_na = _attn_call(
            qn, [(cache_na_k, cache_na_v, _seg_cache(cache_na_k, l), _seg_cache(cache_na_v, l)),
                 (kn, vn, _seg3(kn, NA_WIDTH), _seg3(vn, NA_WIDTH))], local_bias, NA_HEAD_DIM, tq, NA_HEADS,
            "attn_lat_na", l)
        o_mla = _attn_call(
            qm, [(km_ctx, vm_ctx, _seg3(km_ctx, MLA_PAD_WIDTH), _seg3(vm_ctx, NA_WIDTH)),
                 (km, vm, _seg3(km, MLA_PAD_WIDTH), _seg3(vm, NA_WIDTH))], None, MLA_HEAD_PAD, 512, MLA_HEADS,
            "attn_lat_mla")
        x1, h2, route, counts = _merge_call(o_na, o_mla, gates, xs, ada_s, lw, l, 512, "merge_lat")
        xs = _moe(h2, route, counts, x1, ada_s, lw, experts, l, 256, 256, "lat")

    def stack(parts, tail):
        return jnp.stack(parts, axis=1).reshape((pb, DEPTH, ps) + tail)

    return (xp, xs, stack(new_k, (NA_HEADS, NA_HEAD_DIM)), stack(new_v, (NA_HEADS, NA_HEAD_DIM)),
            stack(new_ckv, (MLA_KV_LORA,)), stack(new_kr, (MLA_ROPE_DIM,)))
```

```python
import functools

import numpy as np
import jax
import jax.numpy as jnp
from jax import lax
from jax.experimental import pallas as pl
from jax.experimental.pallas import tpu as pltpu

F32 = jnp.float32
BF16 = jnp.bfloat16

D_MODEL = 1024
DEPTH = 2
GRID_W = 64
NA_HEADS = 8
NA_HEAD_DIM = 64
NA_KR = 8
NA_KC = 16
NA_WIDTH = NA_HEADS * NA_HEAD_DIM
NA_SCALE = NA_HEAD_DIM ** -0.5
MLA_HEADS = 8
MLA_NOPE_DIM = 64
MLA_ROPE_DIM = 32
MLA_V_DIM = 64
MLA_Q_LORA = 384
MLA_KV_LORA = 256
MLA_QK_DIM = MLA_NOPE_DIM + MLA_ROPE_DIM
MLA_SCALE = MLA_QK_DIM ** -0.5
ROPE_BASE = 10000.0
N_GROUPS = 4
EXPERTS_PER_GROUP = 8
N_EXPERTS = N_GROUPS * EXPERTS_PER_GROUP
EXPERT_FF = 256
LN_EPS = 1e-5
RMS_EPS = 1e-6
NEG_INF = -1e30
LOG2E = 1.4426950408889634
DEEPNORM_ALPHA = (2 * DEPTH) ** 0.25

LANES = 128
MLA_HEAD_PAD = LANES
MLA_PAD_WIDTH = MLA_HEADS * MLA_HEAD_PAD
C_Q = 0
C_K = C_Q + NA_WIDTH
C_V = C_K + NA_WIDTH
C_CQ = C_V + NA_WIDTH
C_CKV = C_CQ + MLA_Q_LORA
C_KR = C_CKV + MLA_KV_LORA
C_GATE = C_KR + LANES
C_END = C_GATE + 2 * D_MODEL
R_EXP0 = N_GROUPS

VMEM_LIMIT = 56 * 1024 * 1024


def _rope_partner():
    j = np.arange(MLA_ROPE_DIM)
    n_freq = MLA_ROPE_DIM // 4
    axis, within = j // (2 * n_freq), j % (2 * n_freq)
    half, f = within // n_freq, within % n_freq
    return axis * 2 * n_freq + (1 - half) * n_freq + f, half


def _rope_tables(n_tokens):
    n_freq = MLA_ROPE_DIM // 4
    inv_freq = ROPE_BASE ** (-np.arange(n_freq, dtype=np.float64) / n_freq)
    t = np.arange(n_tokens)
    ang = np.concatenate([(t // GRID_W)[:, None] * inv_freq, (t % GRID_W)[:, None] * inv_freq], axis=-1)
    cos, sin = np.cos(ang), np.sin(ang)
    cos32 = np.concatenate([cos[:, :n_freq], cos[:, :n_freq], cos[:, n_freq:], cos[:, n_freq:]], axis=-1)
    sin32 = np.concatenate([-sin[:, :n_freq], sin[:, :n_freq], -sin[:, n_freq:], sin[:, n_freq:]], axis=-1)
    return cos32, sin32


def _ln_plain(x):
    mu = jnp.mean(x, axis=-1, keepdims=True)
    xc = x - mu
    var = jnp.mean(xc * xc, axis=-1, keepdims=True)
    return xc * lax.rsqrt(var + LN_EPS)


def _rms(x, g):
    return x * lax.rsqrt(jnp.mean(x * x, axis=-1, keepdims=True) + RMS_EPS) * g


def _sigmoid(x):
    return 1.0 / (1.0 + jnp.exp(-x))


def _dot(a, b):
    return jnp.dot(a, b, preferred_element_type=F32)


def _params(sem):
    return pltpu.CompilerParams(dimension_semantics=sem, vmem_limit_bytes=VMEM_LIMIT)


def _layer_spec(arr, layer):
    zeros = (0,) * (arr.ndim - 1)
    return pl.BlockSpec((None,) + arr.shape[1:], lambda *_: (layer,) + zeros, pipeline_mode=pl.Buffered(1))


def _ada_spec(ada, layer, batch_of):
    _, first, step = ada
    return pl.BlockSpec((None, None, 8, D_MODEL), lambda *idx: (layer, first + step * batch_of(*idx), 0, 0))


def _const_spec(arr):
    zeros = (0,) * arr.ndim
    return pl.BlockSpec(arr.shape, lambda *_: zeros, pipeline_mode=pl.Buffered(1))


TOKEN_TILE = (D_MODEL // LANES, LANES)


def _store_token_tiles(ref, val):
    chunks = jnp.stack([val[:, k * LANES:(k + 1) * LANES] for k in range(TOKEN_TILE[0])], axis=0)
    ref[...] = jnp.swapaxes(chunks, 0, 1)


def _load_token_tiles(ref):
    chunks = jnp.swapaxes(ref[...], 0, 1)
    return jnp.concatenate([chunks[k] for k in range(TOKEN_TILE[0])], axis=-1)


def _ada_body(c_ref, w_ref, b_ref, o_ref):
    c = c_ref[...]
    s = (c * _sigmoid(c)).astype(BF16)
    o_ref[0] = _dot(s, w_ref[0].astype(BF16)) + b_ref[0]


def _ada_call(cond, w_ada, b_ada):
    n_rows = cond.shape[0]
    tn = 1024
    return pl.pallas_call(
        _ada_body,
        grid=(DEPTH, 6 * D_MODEL // tn),
        in_specs=[
            pl.BlockSpec((n_rows, D_MODEL), lambda l, j: (0, 0)),
            pl.BlockSpec((1, D_MODEL, tn), lambda l, j: (l, 0, j)),
            pl.BlockSpec((1, 1, tn), lambda l, j: (l, 0, j)),
        ],
        out_specs=pl.BlockSpec((1, n_rows, tn), lambda l, j: (l, 0, j)),
        out_shape=jax.ShapeDtypeStruct((DEPTH, n_rows, 6 * D_MODEL), F32),
        compiler_params=_params(("arbitrary", "arbitrary")),
        name="ada",
    )(cond, w_ada, b_ada)


def _proj_body(*refs, rope, emit_cache):
    it = iter(refs)
    x_ref, ada_ref, w_ref, qn_ref, kvn_ref, wuq_ref = (next(it) for _ in range(6))
    wuqs_ref = next(it) if rope else None
    wuk_ref, wuv_ref, e_ref = next(it), next(it), next(it)
    if rope:
        cq_ref, sq_ref, tk_ref = next(it), next(it), next(it)
    oq_ref, ok_ref, ov_ref, oqm_ref, okm_ref, ovm_ref, og_ref = (next(it) for _ in range(7))
    if emit_cache:
        ockv_ref, okr_ref = next(it), next(it)

    x = x_ref[...]
    sh = ada_ref[0:1, :]
    sc = ada_ref[1:2, :]
    h = (_ln_plain(x) * (1.0 + sc) + sh).astype(BF16)

    oq_ref[...] = (_dot(h, w_ref[:, C_Q:C_K]) * LOG2E).astype(oq_ref.dtype)
    ok_ref[...] = _dot(h, w_ref[:, C_K:C_V]).astype(ok_ref.dtype)
    ov_ref[...] = _dot(h, w_ref[:, C_V:C_CQ]).astype(ov_ref.dtype)

    c_q = _rms(_dot(h, w_ref[:, C_CQ:C_CKV]), qn_ref[...]).astype(BF16)
    q = _dot(c_q, wuq_ref[...])
    if rope:
        q = (q * jnp.tile(cq_ref[...], (1, MLA_HEADS))
             + _dot(c_q, wuqs_ref[...]) * jnp.tile(sq_ref[...], (1, MLA_HEADS)))
    else:
        q = q * (MLA_SCALE * LOG2E)
    oqm_ref[...] = q.astype(BF16)

    c_kv = _rms(_dot(h, w_ref[:, C_CKV:C_KR]), kvn_ref[...])
    slab = _dot(h, w_ref[:, C_KR:C_GATE])
    if emit_cache:
        ockv_ref[...] = c_kv
        okr_ref[...] = slab[:, :MLA_ROPE_DIM]
    if rope:
        slab = slab * tk_ref[...]
    c_kv = c_kv.astype(BF16)
    okm_ref[...] = (_dot(c_kv, wuk_ref[...]) + _dot(slab.astype(BF16), e_ref[...])).astype(BF16)
    ovm_ref[...] = _dot(c_kv, wuv_ref[...]).astype(BF16)

    og_ref[...] = _sigmoid(_dot(h, w_ref[:, C_GATE:C_END])).astype(BF16)


def _proj_call(x, ada, lw, layer, rope_tabs, kv_dtype, emit_cache, tm):
    bsz, seq, _ = x.shape
    nj = seq // tm
    rope = rope_tabs is not None

    def tok(width):
        return pl.BlockSpec((None, tm, width), lambda b, j: (b, j, 0))

    def full(arr):
        return _layer_spec(arr, layer)

    def tab(width):
        return pl.BlockSpec((tm, width), lambda b, j: (j, 0))

    args = [x, ada[0], lw["w_in"], lw["q_norm"], lw["kv_norm"], lw["w_uq"]]
    specs = [tok(D_MODEL), _ada_spec(ada, layer, lambda b, j: b),
             full(lw["w_in"]), full(lw["q_norm"]), full(lw["kv_norm"]), full(lw["w_uq"])]
    if rope:
        args.append(lw["w_uq_sw"])
        specs.append(full(lw["w_uq_sw"]))
    e_mat = lw["e_rope"] if rope else lw["e_plain"]
    args += [lw["w_uk"], lw["w_uv"], e_mat]
    specs += [full(lw["w_uk"]), full(lw["w_uv"]), _const_spec(e_mat)]
    if rope:
        args += list(rope_tabs)
        specs += [tab(MLA_HEAD_PAD), tab(MLA_HEAD_PAD), tab(LANES)]

    def shp(width, dt):
        return jax.ShapeDtypeStruct((bsz, seq, width), dt)

    out_shape = [shp(NA_WIDTH, BF16), shp(NA_WIDTH, kv_dtype), shp(NA_WIDTH, kv_dtype),
                 shp(MLA_PAD_WIDTH, BF16), shp(MLA_PAD_WIDTH, BF16), shp(NA_WIDTH, BF16),
                 shp(2 * D_MODEL, BF16)]
    out_specs = [tok(NA_WIDTH), tok(NA_WIDTH), tok(NA_WIDTH), tok(MLA_PAD_WIDTH), tok(MLA_PAD_WIDTH),
                 tok(NA_WIDTH), tok(2 * D_MODEL)]
    if emit_cache:
        out_shape += [shp(MLA_KV_LORA, F32), shp(MLA_ROPE_DIM, F32)]
        out_specs += [tok(MLA_KV_LORA), tok(MLA_ROPE_DIM)]
    return pl.pallas_call(
        functools.partial(_proj_body, rope=rope, emit_cache=emit_cache),
        grid=(bsz, nj),
        in_specs=specs,
        out_specs=out_specs,
        out_shape=out_shape,
        compiler_params=_params(("arbitrary", "arbitrary")),
        name="proj_lat" if rope else "proj_ctx",
    )(*args)


def _ctxkv_body(ckv_ref, kr_ref, wuk_ref, wuv_ref, e_ref, okm_ref, ovm_ref):
    c_kv = ckv_ref[...].astype(BF16)
    okm_ref[...] = (_dot(c_kv, wuk_ref[...]) + _dot(kr_ref[...].astype(BF16), e_ref[...])).astype(BF16)
    ovm_ref[...] = _dot(c_kv, wuv_ref[...]).astype(BF16)


def _ctxkv_call(cache_ckv, cache_kr_pad, layer, lw):
    bsz, _, past, _ = cache_ckv.shape

    def full(arr):
        return _layer_spec(arr, layer)

    return pl.pallas_call(
        _ctxkv_body,
        grid=(bsz,),
        in_specs=[
            pl.BlockSpec((None, None, past, MLA_KV_LORA), lambda b: (b, layer, 0, 0)),
            pl.BlockSpec((None, None, past, LANES), lambda b: (b, layer, 0, 0)),
            full(lw["w_uk"]), full(lw["w_uv"]), _const_spec(lw["e_plain"]),
        ],
        out_specs=[pl.BlockSpec((None, past, MLA_PAD_WIDTH), lambda b: (b, 0, 0)),
                   pl.BlockSpec((None, past, NA_WIDTH), lambda b: (b, 0, 0))],
        out_shape=[jax.ShapeDtypeStruct((bsz, past, MLA_PAD_WIDTH), BF16),
                   jax.ShapeDtypeStruct((bsz, past, NA_WIDTH), BF16)],
        compiler_params=_params(("arbitrary",)),
        name="ctxkv",
    )(cache_ckv, cache_kr_pad, lw["w_uk"], lw["w_uv"], lw["e_plain"])


def _local_bias(pb_ref, rm_ref, hh, q0, k0, q_rows, k_rows, grid_rows):
    rows = []
    for qr in range(q_rows):
        blocks = []
        for p in range(k_rows // 2):
            pair = k0 // 2 + p
            e = 2 * pair - (q0 + qr) + grid_rows
            blocks.append(pb_ref[hh, e] + rm_ref[pl.ds((q0 + qr) * (grid_rows // 2) + pair, 1), :])
        rows.append(jnp.concatenate(blocks, axis=-1))
    return jnp.concatenate(rows, axis=0)


def _attn_body(*refs, n_seg, dq, local, heads):
    q_ref = refs[0]
    seg_refs = [(refs[1 + 2 * i], refs[2 + 2 * i]) for i in range(n_seg)]
    o_ref = refs[-1]
    if local is not None:
        pb_ref, rm_ref = refs[1 + 2 * n_seg], refs[2 + 2 * n_seg]
        q_rows, k_rows, grid_rows = local
        q0 = pl.program_id(1) * q_rows
        k0 = jnp.minimum(jnp.clip(q0 - NA_KR // 2, 0, grid_rows - NA_KR), grid_rows - k_rows)
        keys = pl.ds(pl.multiple_of(k0 * GRID_W, GRID_W), k_rows * GRID_W)
    head_major = [tuple(jnp.swapaxes(r[...], 0, 1).astype(BF16) for r in seg) if len(seg[0].shape) == 3 else None
                  for seg in seg_refs]
    outs = []
    for hh in range(heads):
        qh = q_ref[:, hh * dq:(hh + 1) * dq]
        scores = []
        values = []
        for i, (k_ref, v_ref) in enumerate(seg_refs):
            windowed = local is not None and i == n_seg - 1
            rows = keys if windowed else slice(None)
            if head_major[i] is not None:
                kh, vh = head_major[i][0][hh], head_major[i][1][hh]
            else:
                kh = k_ref[rows, hh * dq:(hh + 1) * dq].astype(BF16)
                vh = v_ref[rows, hh * MLA_V_DIM:(hh + 1) * MLA_V_DIM].astype(BF16)
            values.append(vh)
            s = lax.dot_general(qh, kh, (((1,), (1,)), ((), ())), preferred_element_type=F32)
            if windowed:
                s = s + _local_bias(pb_ref, rm_ref, hh, q0, k0, q_rows, k_rows, grid_rows)
            scores.append(s)
        m = functools.reduce(jnp.maximum, [jnp.max(s, axis=-1, keepdims=True) for s in scores])
        den = None
        acc = None
        for s, vh in zip(scores, values):
            e = jnp.exp2(s - m)
            d = jnp.sum(e, axis=-1, keepdims=True)
            a = _dot(e.astype(BF16), vh)
            den = d if den is None else den + d
            acc = a if acc is None else acc + a
        outs.append(acc / den)
    o_ref[...] = jnp.concatenate(outs, axis=-1).astype(o_ref.dtype)


def _attn_call(q, segs, local_bias, dq, tq, heads, name, layer=0):
    bsz, seq, _ = q.shape
    args = [q]
    specs = [pl.BlockSpec((None, tq, heads * dq), lambda hp, qt, b: (b, qt, hp))]
    for k, v, ks, vs in segs:
        args += [k, v]
        specs += [ks, vs]
    local = None
    if local_bias is not None:
        pair_blocks, row_masks, k_rows = local_bias
        args += [pair_blocks, row_masks]
        specs += [pl.BlockSpec((None, heads) + pair_blocks.shape[2:], lambda hp, qt, b: (layer, hp, 0, 0, 0),
                               pipeline_mode=pl.Buffered(1)),
                  _const_spec(row_masks)]
        local = (tq // GRID_W, k_rows, seq // GRID_W)
    return pl.pallas_call(
        functools.partial(_attn_body, n_seg=len(segs), dq=dq, local=local, heads=heads),
        grid=(NA_HEADS // heads, seq // tq, bsz),
        in_specs=specs,
        out_specs=pl.BlockSpec((None, tq, heads * MLA_V_DIM), lambda hp, qt, b: (b, qt, hp)),
        out_shape=jax.ShapeDtypeStruct((bsz, seq, NA_WIDTH), BF16),
        compiler_params=_params(("arbitrary", "arbitrary", "arbitrary")),
        name=name,
    )(*args)


def _seg3(arr, width):
    n = arr.shape[1]
    return pl.BlockSpec((None, n, width), lambda hp, qt, b: (b, 0, hp))


def _seg_cache(arr, layer):
    return pl.BlockSpec((None, None) + arr.shape[2:], lambda hp, qt, b: (b, layer, 0, 0, 0))


def _route(logits):
    lane = lax.broadcasted_iota(jnp.int32, logits.shape, 1)
    lane_f = lane.astype(F32)
    far = float(LANES)
    is_g = lane < N_GROUPS
    gl = jnp.where(is_g, logits, NEG_INF)
    gmax = jnp.max(gl, axis=-1, keepdims=True)
    gsum = jnp.sum(jnp.exp(gl - gmax), axis=-1, keepdims=True)
    g_w = 1.0 / gsum
    g_idx = jnp.min(jnp.where(gl == gmax, lane_f, far), axis=-1, keepdims=True)
    e_group = ((lane - R_EXP0) >> 3).astype(F32)
    is_e = (lane >= R_EXP0) & (lane < R_EXP0 + N_EXPERTS) & (e_group == g_idx)
    el = jnp.where(is_e, logits, NEG_INF)
    m1 = jnp.max(el, axis=-1, keepdims=True)
    i1 = jnp.min(jnp.where(el == m1, lane_f, far), axis=-1, keepdims=True)
    el2 = jnp.where(lane_f == i1, NEG_INF, el)
    m2 = jnp.max(el2, axis=-1, keepdims=True)
    i2 = jnp.min(jnp.where(el2 == m2, lane_f, far), axis=-1, keepdims=True)
    r = jnp.exp(m2 - m1)
    w1 = g_w / (1.0 + r)
    w2 = g_w * r / (1.0 + r)
    return lane_f, i1, i2, w1, w2


def _merge_body(ona_ref, omla_ref, g_ref, x_ref, ada_ref, wna_ref, wmla_ref, wout_ref, lng_ref, lnb_ref,
                wr_ref, wrl_ref, br_ref, ox_ref, oh_ref, or_ref, ocnt_ref, carry_ref):
    first = (pl.program_id(0) == 0) & (pl.program_id(1) == 0)

    @pl.when(first)
    def _():
        carry_ref[...] = jnp.zeros_like(carry_ref)

    y_na = _dot(ona_ref[...], wna_ref[...])
    y_mla = _dot(omla_ref[...], wmla_ref[...])
    g_na = g_ref[:, :D_MODEL].astype(F32)
    g_mla = g_ref[:, D_MODEL:].astype(F32)
    y = _dot((g_na * y_na + g_mla * y_mla).astype(BF16), wout_ref[...])
    g1 = ada_ref[2:3, :]
    x1 = _ln_plain(DEEPNORM_ALPHA * x_ref[...] + g1 * y) * lng_ref[...] + lnb_ref[...]
    ox_ref[...] = x1
    h2 = _ln_plain(x1) * (1.0 + ada_ref[4:5, :]) + ada_ref[3:4, :]
    _store_token_tiles(oh_ref, h2)
    h_hi = h2.astype(BF16)
    h_lo = (h2 - h_hi.astype(F32)).astype(BF16)
    logits = (_dot(h_hi, wr_ref[...]) + (_dot(h_lo, wr_ref[...]) + _dot(h_hi, wrl_ref[...]))) + br_ref[...]
    lane_f, i1, i2, w1, w2 = _route(logits)

    tm = logits.shape[0]
    sel1 = lane_f == i1
    sel2 = lane_f == i2
    used = (jnp.where(sel1, 1.0, 0.0) + jnp.where(sel2, 1.0, 0.0)).astype(BF16)
    row = lax.broadcasted_iota(jnp.int32, (tm, tm), 0)
    col = lax.broadcasted_iota(jnp.int32, (tm, tm), 1)
    earlier = jnp.where(row > col, 1.0, 0.0).astype(BF16)
    before = _dot(earlier, used) + carry_ref[0:1, :]
    rank1 = jnp.sum(jnp.where(sel1, before, 0.0), axis=-1, keepdims=True)
    rank2 = jnp.sum(jnp.where(sel2, before, 0.0), axis=-1, keepdims=True)
    carry_ref[...] += _dot(jnp.ones((8, tm), BF16), used)
    ocnt_ref[...] = carry_ref[...]

    fields = (i1 - R_EXP0, i2 - R_EXP0, w1, w2, rank1, rank2)
    route = jnp.zeros_like(logits)
    for k, val in enumerate(fields):
        route = jnp.where(lane_f == float(k), val, route)
    or_ref[...] = route


def _merge_call(o_na, o_mla, gates, x, ada, lw, layer, tm, name):
    bsz, seq, _ = x.shape

    def tok(width):
        return pl.BlockSpec((None, tm, width), lambda b, j: (b, j, 0))

    def full(arr):
        return _layer_spec(arr, layer)

    ws = [lw["w_o_na"], lw["w_o_mla"], lw["w_out"], lw["ln1_g"], lw["ln1_b"], lw["w_r"], lw["w_r_lo"], lw["b_r"]]
    return pl.pallas_call(
        _merge_body,
        grid=(bsz, seq // tm),
        in_specs=[tok(NA_WIDTH), tok(NA_WIDTH), tok(2 * D_MODEL), tok(D_MODEL),
                  _ada_spec(ada, layer, lambda b, j: b)] + [full(w) for w in ws],
        out_specs=[tok(D_MODEL), pl.BlockSpec((None, tm) + TOKEN_TILE, lambda b, j: (b, j, 0, 0)), tok(LANES),
                   pl.BlockSpec((8, LANES), lambda b, j: (0, 0))],
        out_shape=[jax.ShapeDtypeStruct((bsz, seq, D_MODEL), F32),
                   jax.ShapeDtypeStruct((bsz, seq) + TOKEN_TILE, F32),
                   jax.ShapeDtypeStruct((bsz, seq, LANES), F32),
                   jax.ShapeDtypeStruct((8, LANES), F32)],
        scratch_shapes=[pltpu.VMEM((8, LANES), F32)],
        compiler_params=_params(("arbitrary", "arbitrary")),
        name=name,
    )(o_na, o_mla, gates, x, ada[0], *ws)


POS_BITS = 16
POS_MASK = (1 << POS_BITS) - 1


def _plan(route, counts, n_tok, tr):
    r = route.reshape(n_tok, LANES)
    e1, e2 = r[:, 0].astype(jnp.int32), r[:, 1].astype(jnp.int32)
    rk1, rk2 = r[:, 4].astype(jnp.int32), r[:, 5].astype(jnp.int32)
    cnt = counts[0, R_EXP0:R_EXP0 + N_EXPERTS].astype(jnp.int32)
    tiles_e = (cnt + tr - 1) // tr
    tile_end = jnp.cumsum(tiles_e)
    row_off = (tile_end - tiles_e) * tr
    eid = jnp.arange(N_EXPERTS, dtype=jnp.int32)
    p1 = jnp.sum(jnp.where(e1[:, None] == eid, row_off, 0), axis=-1) + rk1
    p2 = jnp.sum(jnp.where(e2[:, None] == eid, row_off, 0), axis=-1) + rk2
    n_tiles = 2 * n_tok // tr + N_EXPERTS
    tile_off = jnp.concatenate([jnp.zeros((1,), jnp.int32), tile_end.astype(jnp.int32)])
    return p1 | (p2 << POS_BITS), tile_off, cnt, n_tiles


ROW_GROUP = 8


def _expert_body(pos_ref, toff_ref, cnt_ref, h_ref, wg_ref, wu_ref, wd_ref, y_hbm,
                 src_ref, hbuf, obuf, sem, wg_b, wu_b, wd_b, *, n_tok, tr, n_tiles):
    e = pl.program_id(0)
    t0 = toff_ref[e]
    n_mine = toff_ref[e + 1] - t0
    n_used = toff_ref[N_EXPERTS]

    def out_copy(slot, tile):
        return pltpu.make_async_copy(obuf.at[slot], y_hbm.at[pl.ds(tile * tr, tr)], sem.at[slot])

    @pl.when(e == 0)
    def _():
        hbuf[...] = jnp.zeros_like(hbuf)

        def place(t, carry):
            packed = pos_ref[t]
            src_ref[packed & POS_MASK] = t
            src_ref[packed >> POS_BITS] = t
            return carry
        lax.fori_loop(0, n_tok, place, 0, unroll=8)

    @pl.when(n_mine > 0)
    def _():
        wg_b[...] = wg_ref[...].astype(BF16)
        wu_b[...] = wu_ref[...].astype(BF16)
        wd_b[...] = wd_ref[...].astype(BF16)

    def tile_body(j, carry):
        tile = t0 + j
        slot = tile & 1
        base = tile * tr
        last = jnp.minimum(cnt_ref[e] - j * tr, tr) - 1

        def gather(g, inner):
            for k in range(ROW_GROUP):
                r = g * ROW_GROUP + k
                hbuf[r] = h_ref[src_ref[base + jnp.minimum(r, last)]]
            return inner
        lax.fori_loop(0, (last + ROW_GROUP) // ROW_GROUP, gather, 0)

        @pl.when(tile >= 2)
        def _():
            out_copy(slot, tile - 2).wait()

        h = _load_token_tiles(hbuf).astype(BF16)
        gate = _dot(h, wg_b[...])
        up = _dot(h, wu_b[...])
        hid = (gate * _sigmoid(gate) * up).astype(BF16)
        _store_token_tiles(obuf.at[slot], _dot(hid, wd_b[...]))
        out_copy(slot, tile).start()
        return carry
    lax.fori_loop(0, n_mine, tile_body, 0)

    @pl.when(e == pl.num_programs(0) - 1)
    def _():
        @pl.when(n_used >= 2)
        def _():
            out_copy(n_used & 1, n_used - 2).wait()
        out_copy((n_used - 1) & 1, n_used - 1).wait()
        obuf[0] = jnp.zeros(obuf.shape[1:], obuf.dtype)

        def zero_start(tile, carry):
            out_copy(0, tile).start()
            return carry

        def zero_wait(tile, carry):
            out_copy(0, tile).wait()
            return carry
        lax.fori_loop(n_used, n_tiles, zero_start, 0)
        lax.fori_loop(n_used, n_tiles, zero_wait, 0)


def _expert_call(h2, plan, w_gate, w_up, w_down, layer, tr, name):
    packed, tile_off, cnt, n_tiles = plan
    n_tok = h2.shape[0]
    n_rows = n_tiles * tr

    def weight_spec(rows, cols):
        return pl.BlockSpec((None, None, rows, cols), lambda e, pos, toff, cnt: (layer, e, 0, 0))

    return pl.pallas_call(
        functools.partial(_expert_body, n_tok=n_tok, tr=tr, n_tiles=n_tiles),
        grid_spec=pltpu.PrefetchScalarGridSpec(
            num_scalar_prefetch=3,
            grid=(N_EXPERTS,),
            in_specs=[_const_spec(h2), weight_spec(D_MODEL, EXPERT_FF), weight_spec(D_MODEL, EXPERT_FF),
                      weight_spec(EXPERT_FF, D_MODEL)],
            out_specs=pl.BlockSpec(memory_space=pl.ANY),
            scratch_shapes=[
                pltpu.SMEM((n_rows,), jnp.int32),
                pltpu.VMEM((tr,) + TOKEN_TILE, F32),
                pltpu.VMEM((2, tr) + TOKEN_TILE, F32),
                pltpu.SemaphoreType.DMA((2,)),
                pltpu.VMEM((D_MODEL, EXPERT_FF), BF16),
                pltpu.VMEM((D_MODEL, EXPERT_FF), BF16),
                pltpu.VMEM((EXPERT_FF, D_MODEL), BF16),
            ],
        ),
        out_shape=jax.ShapeDtypeStruct((n_rows,) + TOKEN_TILE, F32),
        compiler_params=_params(("arbitrary",)),
        name=name,
    )(packed, tile_off, cnt, h2, w_gate, w_up, w_down)


def _combine_body(pos_ref, y_hbm, r_ref, x_ref, ada_ref, lng_ref, lnb_ref, o_ref, ybuf, sem, *, tm):
    i = pl.program_id(0)
    n = pl.num_programs(0)

    def for_tile_rows(tile, slot, fn):
        base = tile * tm
        dst1 = ybuf.at[slot, 0]
        dst2 = ybuf.at[slot, 1]
        dsem = sem.at[slot]

        def body(g, carry):
            for k in range(ROW_GROUP):
                r = g * ROW_GROUP + k
                packed = pos_ref[base + r]
                fn(pltpu.make_async_copy(y_hbm.at[packed & POS_MASK], dst1.at[r], dsem))
                fn(pltpu.make_async_copy(y_hbm.at[packed >> POS_BITS], dst2.at[r], dsem))
            return carry
        lax.fori_loop(0, tm // ROW_GROUP, body, 0)

    @pl.when(i == 0)
    def _():
        for_tile_rows(0, 0, lambda cp: cp.start())

    @pl.when(i + 1 < n)
    def _():
        for_tile_rows(i + 1, (i + 1) & 1, lambda cp: cp.start())

    slot = i & 1
    for_tile_rows(i, slot, lambda cp: cp.wait())

    route = r_ref[...]
    lane = lax.broadcasted_iota(jnp.int32, route.shape, 1)
    w1 = jnp.sum(jnp.where(lane == 2, route, 0.0), axis=-1, keepdims=True)
    w2 = jnp.sum(jnp.where(lane == 3, route, 0.0), axis=-1, keepdims=True)
    y = w1 * _load_token_tiles(ybuf.at[slot, 0]) + w2 * _load_token_tiles(ybuf.at[slot, 1])
    g2 = ada_ref[5:6, :]
    o_ref[...] = _ln_plain(DEEPNORM_ALPHA * x_ref[...] + g2 * y) * lng_ref[...] + lnb_ref[...]


def _combine_call(ys, packed, route, x1, ada, lw, layer, seq, tm, name):
    n_tok = x1.shape[0]
    per_batch = seq // tm

    def tok(width):
        return pl.BlockSpec((tm, width), lambda i, pos: (i, 0))

    def full(arr):
        return _layer_spec(arr, layer)

    return pl.pallas_call(
        functools.partial(_combine_body, tm=tm),
        grid_spec=pltpu.PrefetchScalarGridSpec(
            num_scalar_prefetch=1,
            grid=(n_tok // tm,),
            in_specs=[pl.BlockSpec(memory_space=pl.ANY), tok(LANES), tok(D_MODEL),
                      _ada_spec(ada, layer, lambda i, pos: i // per_batch),
                      full(lw["ln2_g"]), full(lw["ln2_b"])],
            out_specs=tok(D_MODEL),
            scratch_shapes=[pltpu.VMEM((2, 2, tm) + TOKEN_TILE, F32), pltpu.SemaphoreType.DMA((2,))],
        ),
        out_shape=jax.ShapeDtypeStruct((n_tok, D_MODEL), F32),
        compiler_params=_params(("arbitrary",)),
        name=name,
    )(packed, ys, route, x1, ada[0], lw["ln2_g"], lw["ln2_b"])


def _moe(h2, route, counts, x1, ada, lw, experts, layer, tr, tm, name):
    bsz, seq, _ = x1.shape
    n_tok = bsz * seq
    plan = _plan(route, counts, n_tok, tr)
    ys = _expert_call(h2.reshape((n_tok,) + TOKEN_TILE), plan, *experts, layer, tr, "expert_" + name)
    out = _combine_call(ys, plan[0], route.reshape(n_tok, LANES), x1.reshape(n_tok, D_MODEL), ada, lw, layer,
                        seq, tm, "combine_" + name)
    return out.reshape(bsz, seq, D_MODEL)


def _placement(rows_used):
    e = np.zeros((LANES, MLA_PAD_WIDTH), np.float32)
    for r in range(rows_used):
        for hd in range(MLA_HEADS):
            e[r, hd * MLA_HEAD_PAD + MLA_NOPE_DIM + (r % MLA_ROPE_DIM)] = 1.0
    return jnp.asarray(e, BF16)


def _prepare_weights(p):
    partner, _ = _rope_partner()
    n_layers = p["w_in"].shape[0]
    w = p["w_in"]
    k_r = w[:, :, 2176:2208]
    pieces = [w[:, :, 0:512] * NA_SCALE, w[:, :, 512:2176], k_r, k_r[:, :, partner],
              jnp.zeros((n_layers, D_MODEL, LANES - 2 * MLA_ROPE_DIM), F32), w[:, :, 2208:]]
    w_in = jnp.concatenate([piece.astype(BF16) for piece in pieces], axis=2)
    uq = p["w_uq"].reshape(n_layers, MLA_Q_LORA, MLA_HEADS, MLA_QK_DIM)
    pad = MLA_HEAD_PAD - MLA_QK_DIM
    no_pad = ((0, 0), (0, 0), (0, 0))
    w_uq = jnp.pad(uq, no_pad + ((0, pad),)).reshape(n_layers, MLA_Q_LORA, MLA_PAD_WIDTH).astype(BF16)
    uq_sw = jnp.pad(uq[..., MLA_NOPE_DIM + partner], no_pad + ((MLA_NOPE_DIM, pad),))
    w_uq_sw = uq_sw.reshape(n_layers, MLA_Q_LORA, MLA_PAD_WIDTH).astype(BF16)
    uk = p["w_uk"].reshape(n_layers, MLA_KV_LORA, MLA_HEADS, MLA_NOPE_DIM)
    w_uk = jnp.pad(uk, no_pad + ((0, MLA_HEAD_PAD - MLA_NOPE_DIM),)).reshape(
        n_layers, MLA_KV_LORA, MLA_PAD_WIDTH).astype(BF16)
    n_pad = LANES - N_GROUPS - N_EXPERTS
    w_r = jnp.concatenate([p["router_group_w"], p["router_expert_w"],
                           jnp.zeros((n_layers, D_MODEL, n_pad), F32)], axis=2)
    b_r = jnp.concatenate([p["router_group_b"], p["router_expert_b"], jnp.zeros((n_layers, n_pad), F32)], axis=1)
    w_r_hi = w_r.astype(BF16)

    def row(name):
        return p[name][:, None, :]

    return dict(
        w_in=w_in, w_uq=w_uq, w_uq_sw=w_uq_sw, w_uk=w_uk, w_uv=p["w_uv"].astype(BF16),
        q_norm=row("mla_q_norm"), kv_norm=row("mla_kv_norm"),
        e_plain=_placement(MLA_ROPE_DIM), e_rope=_placement(2 * MLA_ROPE_DIM),
        w_o_na=p["w_o_na"].astype(BF16), w_o_mla=p["w_o_mla"].astype(BF16), w_out=p["w_out"].astype(BF16),
        ln1_g=row("ln1_g"), ln1_b=row("ln1_b"), ln2_g=row("ln2_g"), ln2_b=row("ln2_b"),
        w_r=w_r_hi, w_r_lo=(w_r - w_r_hi.astype(F32)).astype(BF16), b_r=b_r[:, None, :],
    )


def _window_rows(rows, q_rows):
    firsts = np.arange(0, rows, q_rows)
    lo = np.clip(firsts - NA_KR // 2, 0, rows - NA_KR)
    hi = np.clip(firsts + q_rows - 1 - NA_KR // 2, 0, rows - NA_KR) + NA_KR
    k_rows = min(rows, int(np.max(hi - lo)) + int(np.max(hi - lo)) % 2)
    k0 = np.minimum(lo, rows - k_rows)
    assert np.all(k0 % 2 == 0) and np.all(k0 + k_rows >= hi), (k0, k_rows)
    return k_rows, k0


def _na_bias_tables(rpb, rows):
    assert rows >= NA_KR and rows % 2 == 0
    r = np.arange(rows)
    rs = np.clip(r - NA_KR // 2, 0, rows - NA_KR)
    vrow = (r[None, :] >= rs[:, None]) & (r[None, :] < rs[:, None] + NA_KR)
    c = np.arange(GRID_W)
    cs = np.clip(c - NA_KC // 2, 0, GRID_W - NA_KC)
    vcol = (c[None, :] >= cs[:, None]) & (c[None, :] < cs[:, None] + NA_KC)
    coff = c[None, :] - c[:, None] + NA_KC - 1
    onehot = ((coff[None] == np.arange(2 * NA_KC - 1)[:, None, None]) & vcol[None]).astype(np.float32)
    zero = np.zeros_like(onehot)
    reach = NA_KR - 1
    halves = []
    for placed in (np.concatenate([onehot, zero], axis=-1), np.concatenate([zero, onehot], axis=-1)):
        vals = jnp.einsum("lhdj,jqk->lhdqk", rpb * LOG2E, jnp.asarray(placed), precision=lax.Precision.HIGHEST)
        halves.append(jnp.pad(vals, ((0, 0), (0, 0), (rows - reach, rows - reach - 1), (0, 0), (0, 0))))
    valid_d = np.abs(np.arange(-rows, rows)) <= reach
    valid = valid_d[:, None, None] & vcol[None]
    valid_pair = np.concatenate([valid[:-1], valid[1:]], axis=-1)
    neg_mask = jnp.asarray(np.where(valid_pair, 0.0, NEG_INF).astype(np.float32))
    pair_blocks = halves[0][:, :, :-1] + halves[1][:, :, 1:] + neg_mask
    in_window = np.repeat(vrow.reshape(rows, rows // 2, 2), GRID_W, axis=-1)
    row_masks = np.where(in_window, 0.0, NEG_INF).astype(np.float32).reshape(rows * rows // 2, 2 * GRID_W)
    return pair_blocks, jnp.asarray(row_masks)


def kernel(x_prompt, x_sample, cache_na_k, cache_na_v, cache_mla_ckv, cache_mla_kr, c, c_ctx, w_in, mla_q_norm, mla_kv_norm, w_uq, w_uk, w_uv, na_rpb, w_o_na, w_o_mla, w_out, w_ada, b_ada, ln1_g, ln1_b, ln2_g, ln2_b, router_group_w, router_group_b, router_expert_w, router_expert_b, expert_w_gate, expert_w_up, expert_w_down):
    p = dict(w_in=w_in, mla_q_norm=mla_q_norm, mla_kv_norm=mla_kv_norm, w_uq=w_uq, w_uk=w_uk, w_uv=w_uv,
             w_o_na=w_o_na, w_o_mla=w_o_mla, w_out=w_out, ln1_g=ln1_g, ln1_b=ln1_b, ln2_g=ln2_g, ln2_b=ln2_b,
             router_group_w=router_group_w, router_group_b=router_group_b,
             router_expert_w=router_expert_w, router_expert_b=router_expert_b,
             expert_w_gate=expert_w_gate, expert_w_up=expert_w_up, expert_w_down=expert_w_down)
    pb, ps, _ = x_prompt.shape
    sb, ss, _ = x_sample.shape

    n_cond = 16
    cond = jnp.zeros((n_cond, D_MODEL), F32).at[0].set(c_ctx).at[1:1 + sb].set(c)
    ada_all = _ada_call(cond, w_ada, b_ada[:, None, :]).reshape(DEPTH, n_cond, 6, D_MODEL)
    ada_all = jnp.pad(ada_all, ((0, 0), (0, 0), (0, 2), (0, 0)))
    ada_p = (ada_all, 0, 0)
    ada_s = (ada_all, 1, 1)

    cos32, sin32 = _rope_tables(ss)
    head_c = np.concatenate([np.ones((ss, MLA_NOPE_DIM)), cos32, np.zeros((ss, MLA_HEAD_PAD - MLA_QK_DIM))], -1)
    head_s = np.concatenate([np.zeros((ss, MLA_NOPE_DIM)), sin32, np.zeros((ss, MLA_HEAD_PAD - MLA_QK_DIM))], -1)
    rope_tabs = tuple(jnp.asarray(t, F32) for t in (
        head_c * (MLA_SCALE * LOG2E), head_s * (MLA_SCALE * LOG2E),
        np.concatenate([cos32, sin32, np.zeros((ss, LANES - 2 * MLA_ROPE_DIM))], -1)))

    ckr_pad = jnp.pad(cache_mla_kr, ((0, 0), (0, 0), (0, 0), (0, LANES - MLA_ROPE_DIM)))

    experts = (expert_w_gate, expert_w_up, expert_w_down)
    lw = _prepare_weights(p)
    tq = 512
    grid_rows = ss // GRID_W
    local_bias = _na_bias_tables(na_rpb, grid_rows) + (_window_rows(grid_rows, tq // GRID_W)[0],)
    xp, xs = x_prompt, x_sample
    new_k, new_v, new_ckv, new_kr = [], [], [], []
    for l in range(DEPTH):

        qn, kn, vn, qm, km, vm, gates, ckv, k_r = _proj_call(xp, ada_p, lw, l, None, F32, True, ps)
        o_na = _attn_call(qn, [(kn, vn, _seg3(kn, NA_WIDTH), _seg3(vn, NA_WIDTH))], None, NA_HEAD_DIM, ps,
                          NA_HEADS, "attn_ctx_na")
        o_mla = _attn_call(qm, [(km, vm, _seg3(km, MLA_PAD_WIDTH), _seg3(vm, NA_WIDTH))], None, MLA_HEAD_PAD, ps,
                           MLA_HEADS, "attn_ctx_mla")
        pair = lambda a: a.reshape((pb // 2, 2 * ps) + a.shape[2:])
        unpair = lambda a: a.reshape((pb, ps) + a.shape[2:])
        x1, h2, route, counts = _merge_call(pair(o_na), pair(o_mla), pair(gates), pair(xp), ada_p, lw, l, 2 * ps,
                                            "merge_ctx")
        xp = _moe(unpair(h2), unpair(route), counts, unpair(x1), ada_p, lw, experts, l, 128, ps, "ctx")
        new_k.append(kn)
        new_v.append(vn)
        new_ckv.append(ckv)
        new_kr.append(k_r)

        qn, kn, vn, qm, km, vm, gates = _proj_call(xs, ada_s, lw, l, rope_tabs, BF16, False, 512)
        km_ctx, vm_ctx = _ctxkv_call(cache_mla_ckv, ckr_pad, l, lw)
        o_na = _attn_call(
            qn, [(cache_na_k, cache_na_v, _seg_cache(cache_na_k, l), _seg_cache(cache_na_v, l)),
                 (kn, vn, _seg3(kn, NA_WIDTH), _seg3(vn, NA_WIDTH))], local_bias, NA_HEAD_DIM, tq, NA_HEADS,
            "attn_lat_na", l)
        o_mla = _attn_call(
            qm, [(km_ctx, vm_ctx, _seg3(km_ctx, MLA_PAD_WIDTH), _seg3(vm_ctx, NA_WIDTH)),
                 (km, vm, _seg3(km, MLA_PAD_WIDTH), _seg3(vm, NA_WIDTH))], None, MLA_HEAD_PAD, 512, MLA_HEADS,
            "attn_lat_mla")
        x1, h2, route, counts = _merge_call(o_na, o_mla, gates, xs, ada_s, lw, l, 512, "merge_lat")
        xs = _moe(h2, route, counts, x1, ada_s, lw, experts, l, 256, 256, "lat")

    def stack(parts, tail):
        return jnp.stack(parts, axis=1).reshape((pb, DEPTH, ps) + tail)

    return (xp, xs, stack(new_k, (NA_HEADS, NA_HEAD_DIM)), stack(new_v, (NA_HEADS, NA_HEAD_DIM)),
            stack(new_ckv, (MLA_KV_LORA,)), stack(new_kr, (MLA_ROPE_DIM,)))
```

```python
import functools

import numpy as np
import jax
import jax.numpy as jnp
from jax import lax
from jax.experimental import pallas as pl
from jax.experimental.pallas import tpu as pltpu

F32 = jnp.float32
BF16 = jnp.bfloat16

D_MODEL = 1024
DEPTH = 2
GRID_W = 64
NA_HEADS = 8
NA_HEAD_DIM = 64
NA_KR = 8
NA_KC = 16
NA_WIDTH = NA_HEADS * NA_HEAD_DIM
NA_SCALE = NA_HEAD_DIM ** -0.5
MLA_HEADS = 8
MLA_NOPE_DIM = 64
MLA_ROPE_DIM = 32
MLA_V_DIM = 64
MLA_Q_LORA = 384
MLA_KV_LORA = 256
MLA_QK_DIM = MLA_NOPE_DIM + MLA_ROPE_DIM
MLA_SCALE = MLA_QK_DIM ** -0.5
ROPE_BASE = 10000.0
N_GROUPS = 4
EXPERTS_PER_GROUP = 8
N_EXPERTS = N_GROUPS * EXPERTS_PER_GROUP
EXPERT_FF = 256
LN_EPS = 1e-5
RMS_EPS = 1e-6
NEG_INF = -1e30
LOG2E = 1.4426950408889634
DEEPNORM_ALPHA = (2 * DEPTH) ** 0.25

LANES = 128
MLA_HEAD_PAD = LANES
MLA_PAD_WIDTH = MLA_HEADS * MLA_HEAD_PAD
C_Q = 0
C_K = C_Q + NA_WIDTH
C_V = C_K + NA_WIDTH
C_CQ = C_V + NA_WIDTH
C_CKV = C_CQ + MLA_Q_LORA
C_KR = C_CKV + MLA_KV_LORA
C_GATE = C_KR + LANES
C_END = C_GATE + 2 * D_MODEL
R_EXP0 = N_GROUPS

VMEM_LIMIT = 56 * 1024 * 1024


def _rope_partner():
    j = np.arange(MLA_ROPE_DIM)
    n_freq = MLA_ROPE_DIM // 4
    axis, within = j // (2 * n_freq), j % (2 * n_freq)
    half, f = within // n_freq, within % n_freq
    return axis * 2 * n_freq + (1 - half) * n_freq + f, half


def _rope_tables(n_tokens):
    n_freq = MLA_ROPE_DIM // 4
    inv_freq = ROPE_BASE ** (-np.arange(n_freq, dtype=np.float64) / n_freq)
    t = np.arange(n_tokens)
    ang = np.concatenate([(t // GRID_W)[:, None] * inv_freq, (t % GRID_W)[:, None] * inv_freq], axis=-1)
    cos, sin = np.cos(ang), np.sin(ang)
    cos32 = np.concatenate([cos[:, :n_freq], cos[:, :n_freq], cos[:, n_freq:], cos[:, n_freq:]], axis=-1)
    sin32 = np.concatenate([-sin[:, :n_freq], sin[:, :n_freq], -sin[:, n_freq:], sin[:, n_freq:]], axis=-1)
    return cos32, sin32


def _ln_plain(x):
    mu = jnp.mean(x, axis=-1, keepdims=True)
    xc = x - mu
    var = jnp.mean(xc * xc, axis=-1, keepdims=True)
    return xc * lax.rsqrt(var + LN_EPS)


def _rms(x, g):
    return x * lax.rsqrt(jnp.mean(x * x, axis=-1, keepdims=True) + RMS_EPS) * g


def _sigmoid(x):
    return 1.0 / (1.0 + jnp.exp(-x))


def _dot(a, b):
    return jnp.dot(a, b, preferred_element_type=F32)


def _params(sem):
    return pltpu.CompilerParams(dimension_semantics=sem, vmem_limit_bytes=VMEM_LIMIT)


def _layer_spec(arr, layer):
    zeros = (0,) * (arr.ndim - 1)
    return pl.BlockSpec((None,) + arr.shape[1:], lambda *_: (layer,) + zeros, pipeline_mode=pl.Buffered(1))


def _ada_spec(ada, layer, batch_of):
    _, first, step = ada
    return pl.BlockSpec((None, None, 8, D_MODEL), lambda *idx: (layer, first + step * batch_of(*idx), 0, 0))


def _const_spec(arr):
    zeros = (0,) * arr.ndim
    return pl.BlockSpec(arr.shape, lambda *_: zeros, pipeline_mode=pl.Buffered(1))


TOKEN_TILE = (D_MODEL // LANES, LANES)


def _store_token_tiles(ref, val):
    chunks = jnp.stack([val[:, k * LANES:(k + 1) * LANES] for k in range(TOKEN_TILE[0])], axis=0)
    ref[...] = jnp.swapaxes(chunks, 0, 1)


def _load_token_tiles(ref):
    chunks = jnp.swapaxes(ref[...], 0, 1)
    return jnp.concatenate([chunks[k] for k in range(TOKEN_TILE[0])], axis=-1)


def _ada_body(c_ref, w_ref, b_ref, o_ref):
    c = c_ref[...]
    s = (c * _sigmoid(c)).astype(BF16)
    o_ref[0] = _dot(s, w_ref[0].astype(BF16)) + b_ref[0]


def _ada_call(cond, w_ada, b_ada):
    n_rows = cond.shape[0]
    tn = 1024
    return pl.pallas_call(
        _ada_body,
        grid=(DEPTH, 6 * D_MODEL // tn),
        in_specs=[
            pl.BlockSpec((n_rows, D_MODEL), lambda l, j: (0, 0)),
            pl.BlockSpec((1, D_MODEL, tn), lambda l, j: (l, 0, j)),
            pl.BlockSpec((1, 1, tn), lambda l, j: (l, 0, j)),
        ],
        out_specs=pl.BlockSpec((1, n_rows, tn), lambda l, j: (l, 0, j)),
        out_shape=jax.ShapeDtypeStruct((DEPTH, n_rows, 6 * D_MODEL), F32),
        compiler_params=_params(("arbitrary", "arbitrary")),
        name="ada",
    )(cond, w_ada, b_ada)


def _proj_body(*refs, rope, emit_cache):
    it = iter(refs)
    x_ref, ada_ref, w_ref, qn_ref, kvn_ref, wuq_ref = (next(it) for _ in range(6))
    wuqs_ref = next(it) if rope else None
    wuk_ref, wuv_ref, e_ref = next(it), next(it), next(it)
    if rope:
        cq_ref, sq_ref, tk_ref = next(it), next(it), next(it)
    oq_ref, ok_ref, ov_ref, oqm_ref, okm_ref, ovm_ref, og_ref = (next(it) for _ in range(7))
    if emit_cache:
        ockv_ref, okr_ref = next(it), next(it)

    x = x_ref[...]
    sh = ada_ref[0:1, :]
    sc = ada_ref[1:2, :]
    h = (_ln_plain(x) * (1.0 + sc) + sh).astype(BF16)

    oq_ref[...] = (_dot(h, w_ref[:, C_Q:C_K]) * LOG2E).astype(oq_ref.dtype)
    ok_ref[...] = _dot(h, w_ref[:, C_K:C_V]).astype(ok_ref.dtype)
    ov_ref[...] = _dot(h, w_ref[:, C_V:C_CQ]).astype(ov_ref.dtype)

    c_q = _rms(_dot(h, w_ref[:, C_CQ:C_CKV]), qn_ref[...]).astype(BF16)
    q = _dot(c_q, wuq_ref[...])
    if rope:
        q = (q * jnp.tile(cq_ref[...], (1, MLA_HEADS))
             + _dot(c_q, wuqs_ref[...]) * jnp.tile(sq_ref[...], (1, MLA_HEADS)))
    else:
        q = q * (MLA_SCALE * LOG2E)
    oqm_ref[...] = q.astype(BF16)

    c_kv = _rms(_dot(h, w_ref[:, C_CKV:C_KR]), kvn_ref[...])
    slab = _dot(h, w_ref[:, C_KR:C_GATE])
    if emit_cache:
        ockv_ref[...] = c_kv
        okr_ref[...] = slab[:, :MLA_ROPE_DIM]
    if rope:
        slab = slab * tk_ref[...]
    c_kv = c_kv.astype(BF16)
    okm_ref[...] = (_dot(c_kv, wuk_ref[...]) + _dot(slab.astype(BF16), e_ref[...])).astype(BF16)
    ovm_ref[...] = _dot(c_kv, wuv_ref[...]).astype(BF16)

    og_ref[...] = _sigmoid(_dot(h, w_ref[:, C_GATE:C_END])).astype(BF16)


def _proj_call(x, ada, lw, layer, rope_tabs, kv_dtype, emit_cache, tm):
    bsz, seq, _ = x.shape
    nj = seq // tm
    rope = rope_tabs is not None

    def tok(width):
        return pl.BlockSpec((None, tm, width), lambda b, j: (b, j, 0))

    def full(arr):
        return _layer_spec(arr, layer)

    def tab(width):
        return pl.BlockSpec((tm, width), lambda b, j: (j, 0))

    args = [x, ada[0], lw["w_in"], lw["q_norm"], lw["kv_norm"], lw["w_uq"]]
    specs = [tok(D_MODEL), _ada_spec(ada, layer, lambda b, j: b),
             full(lw["w_in"]), full(lw["q_norm"]), full(lw["kv_norm"]), full(lw["w_uq"])]
    if rope:
        args.append(lw["w_uq_sw"])
        specs.append(full(lw["w_uq_sw"]))
    e_mat = lw["e_rope"] if rope else lw["e_plain"]
    args += [lw["w_uk"], lw["w_uv"], e_mat]
    specs += [full(lw["w_uk"]), full(lw["w_uv"]), _const_spec(e_mat)]
    if rope:
        args += list(rope_tabs)
        specs += [tab(MLA_HEAD_PAD), tab(MLA_HEAD_PAD), tab(LANES)]

    def shp(width, dt):
        return jax.ShapeDtypeStruct((bsz, seq, width), dt)

    out_shape = [shp(NA_WIDTH, BF16), shp(NA_WIDTH, kv_dtype), shp(NA_WIDTH, kv_dtype),
                 shp(MLA_PAD_WIDTH, BF16), shp(MLA_PAD_WIDTH, BF16), shp(NA_WIDTH, BF16),
                 shp(2 * D_MODEL, BF16)]
    out_specs = [tok(NA_WIDTH), tok(NA_WIDTH), tok(NA_WIDTH), tok(MLA_PAD_WIDTH), tok(MLA_PAD_WIDTH),
                 tok(NA_WIDTH), tok(2 * D_MODEL)]
    if emit_cache:
        out_shape += [shp(MLA_KV_LORA, F32), shp(MLA_ROPE_DIM, F32)]
        out_specs += [tok(MLA_KV_LORA), tok(MLA_ROPE_DIM)]
    return pl.pallas_call(
        functools.partial(_proj_body, rope=rope, emit_cache=emit_cache),
        grid=(bsz, nj),
        in_specs=specs,
        out_specs=out_specs,
        out_shape=out_shape,
        compiler_params=_params(("arbitrary", "arbitrary")),
        name="proj_lat" if rope else "proj_ctx",
    )(*args)


def _ctxkv_body(ckv_ref, kr_ref, wuk_ref, wuv_ref, e_ref, okm_ref, ovm_ref):
    c_kv = ckv_ref[...].astype(BF16)
    okm_ref[...] = (_dot(c_kv, wuk_ref[...]) + _dot(kr_ref[...].astype(BF16), e_ref[...])).astype(BF16)
    ovm_ref[...] = _dot(c_kv, wuv_ref[...]).astype(BF16)


def _ctxkv_call(cache_ckv, cache_kr_pad, layer, lw):
    bsz, _, past, _ = cache_ckv.shape

    def full(arr):
        return _layer_spec(arr, layer)

    return pl.pallas_call(
        _ctxkv_body,
        grid=(bsz,),
        in_specs=[
            pl.BlockSpec((None, None, past, MLA_KV_LORA), lambda b: (b, layer, 0, 0)),
            pl.BlockSpec((None, None, past, LANES), lambda b: (b, layer, 0, 0)),
            full(lw["w_uk"]), full(lw["w_uv"]), _const_spec(lw["e_plain"]),
        ],
        out_specs=[pl.BlockSpec((None, past, MLA_PAD_WIDTH), lambda b: (b, 0, 0)),
                   pl.BlockSpec((None, past, NA_WIDTH), lambda b: (b, 0, 0))],
        out_shape=[jax.ShapeDtypeStruct((bsz, past, MLA_PAD_WIDTH), BF16),
                   jax.ShapeDtypeStruct((bsz, past, NA_WIDTH), BF16)],
        compiler_params=_params(("arbitrary",)),
        name="ctxkv",
    )(cache_ckv, cache_kr_pad, lw["w_uk"], lw["w_uv"], lw["e_plain"])


def _local_bias(pb_ref, rm_ref, hh, q0, k0, q_rows, k_rows, grid_rows):
    rows = []
    for qr in range(q_rows):
        blocks = []
        for p in range(k_rows // 2):
            pair = k0 // 2 + p
            e = 2 * pair - (q0 + qr) + grid_rows
            blocks.append(pb_ref[hh, e] + rm_ref[pl.ds((q0 + qr) * (grid_rows // 2) + pair, 1), :])
        rows.append(jnp.concatenate(blocks, axis=-1))
    return jnp.concatenate(rows, axis=0)


def _attn_body(*refs, n_seg, dq, local, heads):
    q_ref = refs[0]
    seg_refs = [(refs[1 + 2 * i], refs[2 + 2 * i]) for i in range(n_seg)]
    o_ref = refs[-1]
    if local is not None:
        pb_ref, rm_ref = refs[1 + 2 * n_seg], refs[2 + 2 * n_seg]
        q_rows, k_rows, grid_rows = local
        q0 = pl.program_id(1) * q_rows
        k0 = jnp.minimum(jnp.clip(q0 - NA_KR // 2, 0, grid_rows - NA_KR), grid_rows - k_rows)
        keys = pl.ds(pl.multiple_of(k0 * GRID_W, GRID_W), k_rows * GRID_W)
    head_major = [tuple(jnp.swapaxes(r[...], 0, 1).astype(BF16) for r in seg) if len(seg[0].shape) == 3 else None
                  for seg in seg_refs]
    outs = []
    for hh in range(heads):
        qh = q_ref[:, hh * dq:(hh + 1) * dq]
        scores = []
        values = []
        for i, (k_ref, v_ref) in enumerate(seg_refs):
            windowed = local is not None and i == n_seg - 1
            rows = keys if windowed else slice(None)
            if head_major[i] is not None:
                kh, vh = head_major[i][0][hh], head_major[i][1][hh]
            else:
                kh = k_ref[rows, hh * dq:(hh + 1) * dq].astype(BF16)
                vh = v_ref[rows, hh * MLA_V_DIM:(hh + 1) * MLA_V_DIM].astype(BF16)
            values.append(vh)
            s = lax.dot_general(qh, kh, (((1,), (1,)), ((), ())), preferred_element_type=F32)
            if windowed:
                s = s + _local_bias(pb_ref, rm_ref, hh, q0, k0, q_rows, k_rows, grid_rows)
            scores.append(s)
        m = functools.reduce(jnp.maximum, [jnp.max(s, axis=-1, keepdims=True) for s in scores])
        den = None
        acc = None
        for s, vh in zip(scores, values):
            e = jnp.exp2(s - m)
            d = jnp.sum(e, axis=-1, keepdims=True)
            a = _dot(e.astype(BF16), vh)
            den = d if den is None else den + d
            acc = a if acc is None else acc + a
        outs.append(acc / den)
    o_ref[...] = jnp.concatenate(outs, axis=-1).astype(o_ref.dtype)


def _attn_call(q, segs, local_bias, dq, tq, heads, name, layer=0):
    bsz, seq, _ = q.shape
    args = [q]
    specs = [pl.BlockSpec((None, tq, heads * dq), lambda hp, qt, b: (b, qt, hp))]
    for k, v, ks, vs in segs:
        args += [k, v]
        specs += [ks, vs]
    local = None
    if local_bias is not None:
        pair_blocks, row_masks, k_rows = local_bias
        args += [pair_blocks, row_masks]
        specs += [pl.BlockSpec((None, heads) + pair_blocks.shape[2:], lambda hp, qt, b: (layer, hp, 0, 0, 0),
                               pipeline_mode=pl.Buffered(1)),
                  _const_spec(row_masks)]
        local = (tq // GRID_W, k_rows, seq // GRID_W)
    return pl.pallas_call(
        functools.partial(_attn_body, n_seg=len(segs), dq=dq, local=local, heads=heads),
        grid=(NA_HEADS // heads, seq // tq, bsz),
        in_specs=specs,
        out_specs=pl.BlockSpec((None, tq, heads * MLA_V_DIM), lambda hp, qt, b: (b, qt, hp)),
        out_shape=jax.ShapeDtypeStruct((bsz, seq, NA_WIDTH), BF16),
        compiler_params=_params(("arbitrary", "arbitrary", "arbitrary")),
        name=name,
    )(*args)


def _seg3(arr, width):
    n = arr.shape[1]
    return pl.BlockSpec((None, n, width), lambda hp, qt, b: (b, 0, hp))


def _seg_cache(arr, layer):
    return pl.BlockSpec((None, None) + arr.shape[2:], lambda hp, qt, b: (b, layer, 0, 0, 0))


def _route(logits):
    lane = lax.broadcasted_iota(jnp.int32, logits.shape, 1)
    lane_f = lane.astype(F32)
    far = float(LANES)
    is_g = lane < N_GROUPS
    gl = jnp.where(is_g, logits, NEG_INF)
    gmax = jnp.max(gl, axis=-1, keepdims=True)
    gsum = jnp.sum(jnp.exp(gl - gmax), axis=-1, keepdims=True)
    g_w = 1.0 / gsum
    g_idx = jnp.min(jnp.where(gl == gmax, lane_f, far), axis=-1, keepdims=True)
    e_group = ((lane - R_EXP0) >> 3).astype(F32)
    is_e = (lane >= R_EXP0) & (lane < R_EXP0 + N_EXPERTS) & (e_group == g_idx)
    el = jnp.where(is_e, logits, NEG_INF)
    m1 = jnp.max(el, axis=-1, keepdims=True)
    i1 = jnp.min(jnp.where(el == m1, lane_f, far), axis=-1, keepdims=True)
    el2 = jnp.where(lane_f == i1, NEG_INF, el)
    m2 = jnp.max(el2, axis=-1, keepdims=True)
    i2 = jnp.min(jnp.where(el2 == m2, lane_f, far), axis=-1, keepdims=True)
    r = jnp.exp(m2 - m1)
    w1 = g_w / (1.0 + r)
    w2 = g_w * r / (1.0 + r)
    return lane_f, i1, i2, w1, w2


def _merge_body(ona_ref, omla_ref, g_ref, x_ref, ada_ref, wna_ref, wmla_ref, wout_ref, lng_ref, lnb_ref,
                wr_ref, wrl_ref, br_ref, ox_ref, oh_ref, or_ref, ocnt_ref, carry_ref):
    first = (pl.program_id(0) == 0) & (pl.program_id(1) == 0)

    @pl.when(first)
    def _():
        carry_ref[...] = jnp.zeros_like(carry_ref)

    y_na = _dot(ona_ref[...], wna_ref[...])
    y_mla = _dot(omla_ref[...], wmla_ref[...])
    g_na = g_ref[:, :D_MODEL].astype(F32)
    g_mla = g_ref[:, D_MODEL:].astype(F32)
    y = _dot((g_na * y_na + g_mla * y_mla).astype(BF16), wout_ref[...])
    g1 = ada_ref[2:3, :]
    x1 = _ln_plain(DEEPNORM_ALPHA * x_ref[...] + g1 * y) * lng_ref[...] + lnb_ref[...]
    ox_ref[...] = x1
    h2 = _ln_plain(x1) * (1.0 + ada_ref[4:5, :]) + ada_ref[3:4, :]
    _store_token_tiles(oh_ref, h2)
    h_hi = h2.astype(BF16)
    h_lo = (h2 - h_hi.astype(F32)).astype(BF16)
    logits = (_dot(h_hi, wr_ref[...]) + (_dot(h_lo, wr_ref[...]) + _dot(h_hi, wrl_ref[...]))) + br_ref[...]
    lane_f, i1, i2, w1, w2 = _route(logits)

    tm = logits.shape[0]
    sel1 = lane_f == i1
    sel2 = lane_f == i2
    used = (jnp.where(sel1, 1.0, 0.0) + jnp.where(sel2, 1.0, 0.0)).astype(BF16)
    row = lax.broadcasted_iota(jnp.int32, (tm, tm), 0)
    col = lax.broadcasted_iota(jnp.int32, (tm, tm), 1)
    earlier = jnp.where(row > col, 1.0, 0.0).astype(BF16)
    before = _dot(earlier, used) + carry_ref[0:1, :]
    rank1 = jnp.sum(jnp.where(sel1, before, 0.0), axis=-1, keepdims=True)
    rank2 = jnp.sum(jnp.where(sel2, before, 0.0), axis=-1, keepdims=True)
    carry_ref[...] += _dot(jnp.ones((8, tm), BF16), used)
    ocnt_ref[...] = carry_ref[...]

    fields = (i1 - R_EXP0, i2 - R_EXP0, w1, w2, rank1, rank2)
    route = jnp.zeros_like(logits)
    for k, val in enumerate(fields):
        route = jnp.where(lane_f == float(k), val, route)
    or_ref[...] = route


def _merge_call(o_na, o_mla, gates, x, ada, lw, layer, tm, name):
    bsz, seq, _ = x.shape

    def tok(width):
        return pl.BlockSpec((None, tm, width), lambda b, j: (b, j, 0))

    def full(arr):
        return _layer_spec(arr, layer)

    ws = [lw["w_o_na"], lw["w_o_mla"], lw["w_out"], lw["ln1_g"], lw["ln1_b"], lw["w_r"], lw["w_r_lo"], lw["b_r"]]
    return pl.pallas_call(
        _merge_body,
        grid=(bsz, seq // tm),
        in_specs=[tok(NA_WIDTH), tok(NA_WIDTH), tok(2 * D_MODEL), tok(D_MODEL),
                  _ada_spec(ada, layer, lambda b, j: b)] + [full(w) for w in ws],
        out_specs=[tok(D_MODEL), pl.BlockSpec((None, tm) + TOKEN_TILE, lambda b, j: (b, j, 0, 0)), tok(LANES),
                   pl.BlockSpec((8, LANES), lambda b, j: (0, 0))],
        out_shape=[jax.ShapeDtypeStruct((bsz, seq, D_MODEL), F32),
                   jax.ShapeDtypeStruct((bsz, seq) + TOKEN_TILE, F32),
                   jax.ShapeDtypeStruct((bsz, seq, LANES), F32),
                   jax.ShapeDtypeStruct((8, LANES), F32)],
        scratch_shapes=[pltpu.VMEM((8, LANES), F32)],
        compiler_params=_params(("arbitrary", "arbitrary")),
        name=name,
    )(o_na, o_mla, gates, x, ada[0], *ws)


POS_BITS = 16
POS_MASK = (1 << POS_BITS) - 1


def _plan(route, counts, n_tok, tr):
    r = route.reshape(n_tok, LANES)
    e1, e2 = r[:, 0].astype(jnp.int32), r[:, 1].astype(jnp.int32)
    rk1, rk2 = r[:, 4].astype(jnp.int32), r[:, 5].astype(jnp.int32)
    cnt = counts[0, R_EXP0:R_EXP0 + N_EXPERTS].astype(jnp.int32)
    tiles_e = (cnt + tr - 1) // tr
    tile_end = jnp.cumsum(tiles_e)
    row_off = (tile_end - tiles_e) * tr
    eid = jnp.arange(N_EXPERTS, dtype=jnp.int32)
    p1 = jnp.sum(jnp.where(e1[:, None] == eid, row_off, 0), axis=-1) + rk1
    p2 = jnp.sum(jnp.where(e2[:, None] == eid, row_off, 0), axis=-1) + rk2
    n_tiles = 2 * n_tok // tr + N_EXPERTS
    tile_off = jnp.concatenate([jnp.zeros((1,), jnp.int32), tile_end.astype(jnp.int32)])
    return p1 | (p2 << POS_BITS), tile_off, cnt, n_tiles


ROW_GROUP = 8


def _expert_body(pos_ref, toff_ref, cnt_ref, h_hbm, wg_ref, wu_ref, wd_ref, y_hbm,
                 src_ref, h_ref, hsem, hbuf, obuf, sem, wg_b, wu_b, wd_b, *, n_tok, tr, n_tiles):
    e = pl.program_id(0)
    t0 = toff_ref[e]
    n_mine = toff_ref[e + 1] - t0
    n_used = toff_ref[N_EXPERTS]

    def out_copy(slot, tile):
        return pltpu.make_async_copy(obuf.at[slot], y_hbm.at[pl.ds(tile * tr, tr)], sem.at[slot])

    @pl.when(e == 0)
    def _():
        hbuf[...] = jnp.zeros_like(hbuf)
        fetch = pltpu.make_async_copy(h_hbm, h_ref, hsem.at[0])
        fetch.start()

        def place(t, carry):
            packed = pos_ref[t]
            src_ref[packed & POS_MASK] = t
            src_ref[packed >> POS_BITS] = t
            return carry
        lax.fori_loop(0, n_tok, place, 0, unroll=8)
        fetch.wait()

    @pl.when(n_mine > 0)
    def _():
        wg_b[...] = wg_ref[...].astype(BF16)
        wu_b[...] = wu_ref[...].astype(BF16)
        wd_b[...] = wd_ref[...].astype(BF16)

    def tile_body(j, carry):
        tile = t0 + j
        slot = tile & 1
        base = tile * tr
        last = jnp.minimum(cnt_ref[e] - j * tr, tr) - 1

        def gather(g, inner):
            for k in range(ROW_GROUP):
                r = g * ROW_GROUP + k
                hbuf[r] = h_ref[src_ref[base + jnp.minimum(r, last)]]
            return inner
        lax.fori_loop(0, (last + ROW_GROUP) // ROW_GROUP, gather, 0)

        @pl.when(tile >= 2)
        def _():
            out_copy(slot, tile - 2).wait()

        h = _load_token_tiles(hbuf).astype(BF16)
        gate = _dot(h, wg_b[...])
        up = _dot(h, wu_b[...])
        hid = (gate * _sigmoid(gate) * up).astype(BF16)
        _store_token_tiles(obuf.at[slot], _dot(hid, wd_b[...]))
        out_copy(slot, tile).start()
        return carry
    lax.fori_loop(0, n_mine, tile_body, 0)

    @pl.when(e == pl.num_programs(0) - 1)
    def _():
        @pl.when(n_used >= 2)
        def _():
            out_copy(n_used & 1, n_used - 2).wait()
        out_copy((n_used - 1) & 1, n_used - 1).wait()
        obuf[0] = jnp.zeros(obuf.shape[1:], obuf.dtype)

        def zero_start(tile, carry):
            out_copy(0, tile).start()
            return carry

        def zero_wait(tile, carry):
            out_copy(0, tile).wait()
            return carry
        lax.fori_loop(n_used, n_tiles, zero_start, 0)
        lax.fori_loop(n_used, n_tiles, zero_wait, 0)


def _expert_call(h2, plan, w_gate, w_up, w_down, layer, tr, name):
    packed, tile_off, cnt, n_tiles = plan
    n_tok = h2.shape[0]
    n_rows = n_tiles * tr

    def weight_spec(rows, cols):
        return pl.BlockSpec((None, None, rows, cols), lambda e, pos, toff, cnt: (layer, e, 0, 0))

    return pl.pallas_call(
        functools.partial(_expert_body, n_tok=n_tok, tr=tr, n_tiles=n_tiles),
        grid_spec=pltpu.PrefetchScalarGridSpec(
            num_scalar_prefetch=3,
            grid=(N_EXPERTS,),
            in_specs=[pl.BlockSpec(memory_space=pl.ANY), weight_spec(D_MODEL, EXPERT_FF),
                      weight_spec(D_MODEL, EXPERT_FF),
                      weight_spec(EXPERT_FF, D_MODEL)],
            out_specs=pl.BlockSpec(memory_space=pl.ANY),
            scratch_shapes=[
                pltpu.SMEM((n_rows,), jnp.int32),
                pltpu.VMEM(h2.shape, F32),
                pltpu.SemaphoreType.DMA((1,)),
                pltpu.VMEM((tr,) + TOKEN_TILE, F32),
                pltpu.VMEM((2, tr) + TOKEN_TILE, F32),
                pltpu.SemaphoreType.DMA((2,)),
                pltpu.VMEM((D_MODEL, EXPERT_FF), BF16),
                pltpu.VMEM((D_MODEL, EXPERT_FF), BF16),
                pltpu.VMEM((EXPERT_FF, D_MODEL), BF16),
            ],
        ),
        out_shape=jax.ShapeDtypeStruct((n_rows,) + TOKEN_TILE, F32),
        compiler_params=_params(("arbitrary",)),
        name=name,
    )(packed, tile_off, cnt, h2, w_gate, w_up, w_down)


def _combine_body(pos_ref, y_hbm, r_ref, x_ref, ada_ref, lng_ref, lnb_ref, o_ref, ybuf, sem, *, tm):
    i = pl.program_id(0)
    n = pl.num_programs(0)

    def for_tile_rows(tile, slot, fn):
        base = tile * tm
        dst1 = ybuf.at[slot, 0]
        dst2 = ybuf.at[slot, 1]
        dsem = sem.at[slot]

        def body(g, carry):
            for k in range(ROW_GROUP):
                r = g * ROW_GROUP + k
                packed = pos_ref[base + r]
                fn(pltpu.make_async_copy(y_hbm.at[packed & POS_MASK], dst1.at[r], dsem))
                fn(pltpu.make_async_copy(y_hbm.at[packed >> POS_BITS], dst2.at[r], dsem))
            return carry
        lax.fori_loop(0, tm // ROW_GROUP, body, 0)

    @pl.when(i == 0)
    def _():
        for_tile_rows(0, 0, lambda cp: cp.start())

    @pl.when(i + 1 < n)
    def _():
        for_tile_rows(i + 1, (i + 1) & 1, lambda cp: cp.start())

    slot = i & 1
    for_tile_rows(i, slot, lambda cp: cp.wait())

    route = r_ref[...]
    lane = lax.broadcasted_iota(jnp.int32, route.shape, 1)
    w1 = jnp.sum(jnp.where(lane == 2, route, 0.0), axis=-1, keepdims=True)
    w2 = jnp.sum(jnp.where(lane == 3, route, 0.0), axis=-1, keepdims=True)
    y = w1 * _load_token_tiles(ybuf.at[slot, 0]) + w2 * _load_token_tiles(ybuf.at[slot, 1])
    g2 = ada_ref[5:6, :]
    o_ref[...] = _ln_plain(DEEPNORM_ALPHA * x_ref[...] + g2 * y) * lng_ref[...] + lnb_ref[...]


def _combine_call(ys, packed, route, x1, ada, lw, layer, seq, tm, name):
    n_tok = x1.shape[0]
    per_batch = seq // tm

    def tok(width):
        return pl.BlockSpec((tm, width), lambda i, pos: (i, 0))

    def full(arr):
        return _layer_spec(arr, layer)

    return pl.pallas_call(
        functools.partial(_combine_body, tm=tm),
        grid_spec=pltpu.PrefetchScalarGridSpec(
            num_scalar_prefetch=1,
            grid=(n_tok // tm,),
            in_specs=[pl.BlockSpec(memory_space=pl.ANY), tok(LANES), tok(D_MODEL),
                      _ada_spec(ada, layer, lambda i, pos: i // per_batch),
                      full(lw["ln2_g"]), full(lw["ln2_b"])],
            out_specs=tok(D_MODEL),
            scratch_shapes=[pltpu.VMEM((2, 2, tm) + TOKEN_TILE, F32), pltpu.SemaphoreType.DMA((2,))],
        ),
        out_shape=jax.ShapeDtypeStruct((n_tok, D_MODEL), F32),
        compiler_params=_params(("arbitrary",)),
        name=name,
    )(packed, ys, route, x1, ada[0], lw["ln2_g"], lw["ln2_b"])


def _moe(h2, route, counts, x1, ada, lw, experts, layer, tr, tm, name):
    bsz, seq, _ = x1.shape
    n_tok = bsz * seq
    plan = _plan(route, counts, n_tok, tr)
    ys = _expert_call(h2.reshape((n_tok,) + TOKEN_TILE), plan, *experts, layer, tr, "expert_" + name)
    out = _combine_call(ys, plan[0], route.reshape(n_tok, LANES), x1.reshape(n_tok, D_MODEL), ada, lw, layer,
                        seq, tm, "combine_" + name)
    return out.reshape(bsz, seq, D_MODEL)


def _placement(rows_used):
    e = np.zeros((LANES, MLA_PAD_WIDTH), np.float32)
    for r in range(rows_used):
        for hd in range(MLA_HEADS):
            e[r, hd * MLA_HEAD_PAD + MLA_NOPE_DIM + (r % MLA_ROPE_DIM)] = 1.0
    return jnp.asarray(e, BF16)


def _prepare_weights(p):
    partner, _ = _rope_partner()
    n_layers = p["w_in"].shape[0]
    w = p["w_in"]
    k_r = w[:, :, 2176:2208]
    pieces = [w[:, :, 0:512] * NA_SCALE, w[:, :, 512:2176], k_r, k_r[:, :, partner],
              jnp.zeros((n_layers, D_MODEL, LANES - 2 * MLA_ROPE_DIM), F32), w[:, :, 2208:]]
    w_in = jnp.concatenate([piece.astype(BF16) for piece in pieces], axis=2)
    uq = p["w_uq"].reshape(n_layers, MLA_Q_LORA, MLA_HEADS, MLA_QK_DIM)
    pad = MLA_HEAD_PAD - MLA_QK_DIM
    no_pad = ((0, 0), (0, 0), (0, 0))
    w_uq = jnp.pad(uq, no_pad + ((0, pad),)).reshape(n_layers, MLA_Q_LORA, MLA_PAD_WIDTH).astype(BF16)
    uq_sw = jnp.pad(uq[..., MLA_NOPE_DIM + partner], no_pad + ((MLA_NOPE_DIM, pad),))
    w_uq_sw = uq_sw.reshape(n_layers, MLA_Q_LORA, MLA_PAD_WIDTH).astype(BF16)
    uk = p["w_uk"].reshape(n_layers, MLA_KV_LORA, MLA_HEADS, MLA_NOPE_DIM)
    w_uk = jnp.pad(uk, no_pad + ((0, MLA_HEAD_PAD - MLA_NOPE_DIM),)).reshape(
        n_layers, MLA_KV_LORA, MLA_PAD_WIDTH).astype(BF16)
    n_pad = LANES - N_GROUPS - N_EXPERTS
    w_r = jnp.concatenate([p["router_group_w"], p["router_expert_w"],
                           jnp.zeros((n_layers, D_MODEL, n_pad), F32)], axis=2)
    b_r = jnp.concatenate([p["router_group_b"], p["router_expert_b"], jnp.zeros((n_layers, n_pad), F32)], axis=1)
    w_r_hi = w_r.astype(BF16)

    def row(name):
        return p[name][:, None, :]

    return dict(
        w_in=w_in, w_uq=w_uq, w_uq_sw=w_uq_sw, w_uk=w_uk, w_uv=p["w_uv"].astype(BF16),
        q_norm=row("mla_q_norm"), kv_norm=row("mla_kv_norm"),
        e_plain=_placement(MLA_ROPE_DIM), e_rope=_placement(2 * MLA_ROPE_DIM),
        w_o_na=p["w_o_na"].astype(BF16), w_o_mla=p["w_o_mla"].astype(BF16), w_out=p["w_out"].astype(BF16),
        ln1_g=row("ln1_g"), ln1_b=row("ln1_b"), ln2_g=row("ln2_g"), ln2_b=row("ln2_b"),
        w_r=w_r_hi, w_r_lo=(w_r - w_r_hi.astype(F32)).astype(BF16), b_r=b_r[:, None, :],
    )


def _window_rows(rows, q_rows):
    firsts = np.arange(0, rows, q_rows)
    lo = np.clip(firsts - NA_KR // 2, 0, rows - NA_KR)
    hi = np.clip(firsts + q_rows - 1 - NA_KR // 2, 0, rows - NA_KR) + NA_KR
    k_rows = min(rows, int(np.max(hi - lo)) + int(np.max(hi - lo)) % 2)
    k0 = np.minimum(lo, rows - k_rows)
    assert np.all(k0 % 2 == 0) and np.all(k0 + k_rows >= hi), (k0, k_rows)
    return k_rows, k0


def _na_bias_tables(rpb, rows):
    assert rows >= NA_KR and rows % 2 == 0
    r = np.arange(rows)
    rs = np.clip(r - NA_KR // 2, 0, rows - NA_KR)
    vrow = (r[None, :] >= rs[:, None]) & (r[None, :] < rs[:, None] + NA_KR)
    c = np.arange(GRID_W)
    cs = np.clip(c - NA_KC // 2, 0, GRID_W - NA_KC)
    vcol = (c[None, :] >= cs[:, None]) & (c[None, :] < cs[:, None] + NA_KC)
    coff = c[None, :] - c[:, None] + NA_KC - 1
    onehot = ((coff[None] == np.arange(2 * NA_KC - 1)[:, None, None]) & vcol[None]).astype(np.float32)
    zero = np.zeros_like(onehot)
    reach = NA_KR - 1
    halves = []
    for placed in (np.concatenate([onehot, zero], axis=-1), np.concatenate([zero, onehot], axis=-1)):
        vals = jnp.einsum("lhdj,jqk->lhdqk", rpb * LOG2E, jnp.asarray(placed), precision=lax.Precision.HIGHEST)
        halves.append(jnp.pad(vals, ((0, 0), (0, 0), (rows - reach, rows - reach - 1), (0, 0), (0, 0))))
    valid_d = np.abs(np.arange(-rows, rows)) <= reach
    valid = valid_d[:, None, None] & vcol[None]
    valid_pair = np.concatenate([valid[:-1], valid[1:]], axis=-1)
    neg_mask = jnp.asarray(np.where(valid_pair, 0.0, NEG_INF).astype(np.float32))
    pair_blocks = halves[0][:, :, :-1] + halves[1][:, :, 1:] + neg_mask
    in_window = np.repeat(vrow.reshape(rows, rows // 2, 2), GRID_W, axis=-1)
    row_masks = np.where(in_window, 0.0, NEG_INF).astype(np.float32).reshape(rows * rows // 2, 2 * GRID_W)
    return pair_blocks, jnp.asarray(row_masks)


def kernel(x_prompt, x_sample, cache_na_k, cache_na_v, cache_mla_ckv, cache_mla_kr, c, c_ctx, w_in, mla_q_norm, mla_kv_norm, w_uq, w_uk, w_uv, na_rpb, w_o_na, w_o_mla, w_out, w_ada, b_ada, ln1_g, ln1_b, ln2_g, ln2_b, router_group_w, router_group_b, router_expert_w, router_expert_b, expert_w_gate, expert_w_up, expert_w_down):
    p = dict(w_in=w_in, mla_q_norm=mla_q_norm, mla_kv_norm=mla_kv_norm, w_uq=w_uq, w_uk=w_uk, w_uv=w_uv,
             w_o_na=w_o_na, w_o_mla=w_o_mla, w_out=w_out, ln1_g=ln1_g, ln1_b=ln1_b, ln2_g=ln2_g, ln2_b=ln2_b,
             router_group_w=router_group_w, router_group_b=router_group_b,
             router_expert_w=router_expert_w, router_expert_b=router_expert_b,
             expert_w_gate=expert_w_gate, expert_w_up=expert_w_up, expert_w_down=expert_w_down)
    pb, ps, _ = x_prompt.shape
    sb, ss, _ = x_sample.shape

    n_cond = 16
    cond = jnp.zeros((n_cond, D_MODEL), F32).at[0].set(c_ctx).at[1:1 + sb].set(c)
    ada_all = _ada_call(cond, w_ada, b_ada[:, None, :]).reshape(DEPTH, n_cond, 6, D_MODEL)
    ada_all = jnp.pad(ada_all, ((0, 0), (0, 0), (0, 2), (0, 0)))
    ada_p = (ada_all, 0, 0)
    ada_s = (ada_all, 1, 1)

    cos32, sin32 = _rope_tables(ss)
    head_c = np.concatenate([np.ones((ss, MLA_NOPE_DIM)), cos32, np.zeros((ss, MLA_HEAD_PAD - MLA_QK_DIM))], -1)
    head_s = np.concatenate([np.zeros((ss, MLA_NOPE_DIM)), sin32, np.zeros((ss, MLA_HEAD_PAD - MLA_QK_DIM))], -1)
    rope_tabs = tuple(jnp.asarray(t, F32) for t in (
        head_c * (MLA_SCALE * LOG2E), head_s * (MLA_SCALE * LOG2E),
        np.concatenate([cos32, sin32, np.zeros((ss, LANES - 2 * MLA_ROPE_DIM))], -1)))

    ckr_pad = jnp.pad(cache_mla_kr, ((0, 0), (0, 0), (0, 0), (0, LANES - MLA_ROPE_DIM)))

    experts = (expert_w_gate, expert_w_up, expert_w_down)
    lw = _prepare_weights(p)
    tq = 512
    grid_rows = ss // GRID_W
    local_bias = _na_bias_tables(na_rpb, grid_rows) + (_window_rows(grid_rows, tq // GRID_W)[0],)
    xp, xs = x_prompt, x_sample
    new_k, new_v, new_ckv, new_kr = [], [], [], []
    for l in range(DEPTH):

        qn, kn, vn, qm, km, vm, gates, ckv, k_r = _proj_call(xp, ada_p, lw, l, None, F32, True, ps)
        o_na = _attn_call(qn, [(kn, vn, _seg3(kn, NA_WIDTH), _seg3(vn, NA_WIDTH))], None, NA_HEAD_DIM, ps,
                          NA_HEADS, "attn_ctx_na")
        o_mla = _attn_call(qm, [(km, vm, _seg3(km, MLA_PAD_WIDTH), _seg3(vm, NA_WIDTH))], None, MLA_HEAD_PAD, ps,
                           MLA_HEADS, "attn_ctx_mla")
        pair = lambda a: a.reshape((pb // 2, 2 * ps) + a.shape[2:])
        unpair = lambda a: a.reshape((pb, ps) + a.shape[2:])
        x1, h2, route, counts = _merge_call(pair(o_na), pair(o_mla), pair(gates), pair(xp), ada_p, lw, l, 2 * ps,
                                            "merge_ctx")
        xp = _moe(unpair(h2), unpair(route), counts, unpair(x1), ada_p, lw, experts, l, 128, ps, "ctx")
        new_k.append(kn)
        new_v.append(vn)
        new_ckv.append(ckv)
        new_kr.append(k_r)

        qn, kn, vn, qm, km, vm, gates = _proj_call(xs, ada_s, lw, l, rope_tabs, BF16, False, 512)
        km_ctx, vm_ctx = _ctxkv_call(cache_mla_ckv, ckr_pad, l, lw)
        o_na = _attn_call(
            qn, [(cache_na_k, cache_na_v, _seg_cache(cache_na_k, l), _seg_cache(cache_na_v, l)),
                 (kn, vn, _seg3(kn, NA_WIDTH), _seg3(vn, NA_WIDTH))], local_bias, NA_HEAD_DIM, tq, NA_HEADS,
            "attn_lat_na", l)
        o_mla = _attn_call(
            qm, [(km_ctx, vm_ctx, _seg3(km_ctx, MLA_PAD_WIDTH), _seg3(vm_ctx, NA_WIDTH)),
                 (km, vm, _seg3(km, MLA_PAD_WIDTH), _seg3(vm, NA_WIDTH))], None, MLA_HEAD_PAD, 512, MLA_HEADS,
            "attn_lat_mla")
        x1, h2, route, counts = _merge_call(o_na, o_mla, gates, xs, ada_s, lw, l, 512, "merge_lat")
        xs = _moe(h2, route, counts, x1, ada_s, lw, experts, l, 256, 256, "lat")

    def stack(parts, tail):
        return jnp.stack(parts, axis=1).reshape((pb, DEPTH, ps) + tail)

    return (xp, xs, stack(new_k, (NA_HEADS, NA_HEAD_DIM)), stack(new_v, (NA_HEADS, NA_HEAD_DIM)),
            stack(new_ckv, (MLA_KV_LORA,)), stack(new_kr, (MLA_ROPE_DIM,)))
```

```python
import functools

import numpy as np
import jax
import jax.numpy as jnp
from jax import lax
from jax.experimental import pallas as pl
from jax.experimental.pallas import tpu as pltpu

F32 = jnp.float32
BF16 = jnp.bfloat16

D_MODEL = 1024
DEPTH = 2
GRID_W = 64
NA_HEADS = 8
NA_HEAD_DIM = 64
NA_KR = 8
NA_KC = 16
NA_WIDTH = NA_HEADS * NA_HEAD_DIM
NA_SCALE = NA_HEAD_DIM ** -0.5
MLA_HEADS = 8
MLA_NOPE_DIM = 64
MLA_ROPE_DIM = 32
MLA_V_DIM = 64
MLA_Q_LORA = 384
MLA_KV_LORA = 256
MLA_QK_DIM = MLA_NOPE_DIM + MLA_ROPE_DIM
MLA_SCALE = MLA_QK_DIM ** -0.5
ROPE_BASE = 10000.0
N_GROUPS = 4
EXPERTS_PER_GROUP = 8
N_EXPERTS = N_GROUPS * EXPERTS_PER_GROUP
EXPERT_FF = 256
LN_EPS = 1e-5
RMS_EPS = 1e-6
NEG_INF = -1e30
LOG2E = 1.4426950408889634
DEEPNORM_ALPHA = (2 * DEPTH) ** 0.25

LANES = 128
MLA_HEAD_PAD = LANES
MLA_PAD_WIDTH = MLA_HEADS * MLA_HEAD_PAD
C_Q = 0
C_K = C_Q + NA_WIDTH
C_V = C_K + NA_WIDTH
C_CQ = C_V + NA_WIDTH
C_CKV = C_CQ + MLA_Q_LORA
C_KR = C_CKV + MLA_KV_LORA
C_GATE = C_KR + LANES
C_END = C_GATE + 2 * D_MODEL
R_EXP0 = N_GROUPS

VMEM_LIMIT = 56 * 1024 * 1024


def _rope_partner():
    j = np.arange(MLA_ROPE_DIM)
    n_freq = MLA_ROPE_DIM // 4
    axis, within = j // (2 * n_freq), j % (2 * n_freq)
    half, f = within // n_freq, within % n_freq
    return axis * 2 * n_freq + (1 - half) * n_freq + f, half


def _rope_tables(n_tokens):
    n_freq = MLA_ROPE_DIM // 4
    inv_freq = ROPE_BASE ** (-np.arange(n_freq, dtype=np.float64) / n_freq)
    t = np.arange(n_tokens)
    ang = np.concatenate([(t // GRID_W)[:, None] * inv_freq, (t % GRID_W)[:, None] * inv_freq], axis=-1)
    cos, sin = np.cos(ang), np.sin(ang)
    cos32 = np.concatenate([cos[:, :n_freq], cos[:, :n_freq], cos[:, n_freq:], cos[:, n_freq:]], axis=-1)
    sin32 = np.concatenate([-sin[:, :n_freq], sin[:, :n_freq], -sin[:, n_freq:], sin[:, n_freq:]], axis=-1)
    return cos32, sin32


def _ln_plain(x):
    mu = jnp.mean(x, axis=-1, keepdims=True)
    xc = x - mu
    var = jnp.mean(xc * xc, axis=-1, keepdims=True)
    return xc * lax.rsqrt(var + LN_EPS)


def _rms(x, g):
    return x * lax.rsqrt(jnp.mean(x * x, axis=-1, keepdims=True) + RMS_EPS) * g


def _sigmoid(x):
    return 1.0 / (1.0 + jnp.exp(-x))


def _dot(a, b):
    return jnp.dot(a, b, preferred_element_type=F32)


def _params(sem):
    return pltpu.CompilerParams(dimension_semantics=sem, vmem_limit_bytes=VMEM_LIMIT)


def _layer_spec(arr, layer):
    zeros = (0,) * (arr.ndim - 1)
    return pl.BlockSpec((None,) + arr.shape[1:], lambda *_: (layer,) + zeros, pipeline_mode=pl.Buffered(1))


def _ada_spec(ada, layer, batch_of):
    _, first, step = ada
    return pl.BlockSpec((None, None, 8, D_MODEL), lambda *idx: (layer, first + step * batch_of(*idx), 0, 0))


def _const_spec(arr):
    zeros = (0,) * arr.ndim
    return pl.BlockSpec(arr.shape, lambda *_: zeros, pipeline_mode=pl.Buffered(1))


TOKEN_TILE = (D_MODEL // LANES, LANES)


def _store_token_tiles(ref, val):
    chunks = jnp.stack([val[:, k * LANES:(k + 1) * LANES] for k in range(TOKEN_TILE[0])], axis=0)
    ref[...] = jnp.swapaxes(chunks, 0, 1)


def _load_token_tiles(ref):
    chunks = jnp.swapaxes(ref[...], 0, 1)
    return jnp.concatenate([chunks[k] for k in range(TOKEN_TILE[0])], axis=-1)


def _ada_body(c_ref, w_ref, b_ref, o_ref):
    c = c_ref[...]
    s = (c * _sigmoid(c)).astype(BF16)
    o_ref[0] = _dot(s, w_ref[0].astype(BF16)) + b_ref[0]


def _ada_call(cond, w_ada, b_ada):
    n_rows = cond.shape[0]
    tn = 1024
    return pl.pallas_call(
        _ada_body,
        grid=(DEPTH, 6 * D_MODEL // tn),
        in_specs=[
            pl.BlockSpec((n_rows, D_MODEL), lambda l, j: (0, 0)),
            pl.BlockSpec((1, D_MODEL, tn), lambda l, j: (l, 0, j)),
            pl.BlockSpec((1, 1, tn), lambda l, j: (l, 0, j)),
        ],
        out_specs=pl.BlockSpec((1, n_rows, tn), lambda l, j: (l, 0, j)),
        out_shape=jax.ShapeDtypeStruct((DEPTH, n_rows, 6 * D_MODEL), F32),
        compiler_params=_params(("arbitrary", "arbitrary")),
        name="ada",
    )(cond, w_ada, b_ada)


def _proj_body(*refs, rope, emit_cache):
    it = iter(refs)
    x_ref, ada_ref, w_ref, qn_ref, kvn_ref, wuq_ref = (next(it) for _ in range(6))
    wuqs_ref = next(it) if rope else None
    wuk_ref, wuv_ref, e_ref = next(it), next(it), next(it)
    if rope:
        cq_ref, sq_ref, tk_ref = next(it), next(it), next(it)
    oq_ref, ok_ref, ov_ref, oqm_ref, okm_ref, ovm_ref, og_ref = (next(it) for _ in range(7))
    if emit_cache:
        ockv_ref, okr_ref = next(it), next(it)

    x = x_ref[...]
    sh = ada_ref[0:1, :]
    sc = ada_ref[1:2, :]
    h = (_ln_plain(x) * (1.0 + sc) + sh).astype(BF16)

    oq_ref[...] = (_dot(h, w_ref[:, C_Q:C_K]) * LOG2E).astype(oq_ref.dtype)
    ok_ref[...] = _dot(h, w_ref[:, C_K:C_V]).astype(ok_ref.dtype)
    ov_ref[...] = _dot(h, w_ref[:, C_V:C_CQ]).astype(ov_ref.dtype)

    c_q = _rms(_dot(h, w_ref[:, C_CQ:C_CKV]), qn_ref[...]).astype(BF16)
    q = _dot(c_q, wuq_ref[...])
    if rope:
        q = (q * jnp.tile(cq_ref[...], (1, MLA_HEADS))
             + _dot(c_q, wuqs_ref[...]) * jnp.tile(sq_ref[...], (1, MLA_HEADS)))
    else:
        q = q * (MLA_SCALE * LOG2E)
    oqm_ref[...] = q.astype(BF16)

    c_kv = _rms(_dot(h, w_ref[:, C_CKV:C_KR]), kvn_ref[...])
    slab = _dot(h, w_ref[:, C_KR:C_GATE])
    if emit_cache:
        ockv_ref[...] = c_kv
        okr_ref[...] = slab[:, :MLA_ROPE_DIM]
    if rope:
        slab = slab * tk_ref[...]
    c_kv = c_kv.astype(BF16)
    okm_ref[...] = (_dot(c_kv, wuk_ref[...]) + _dot(slab.astype(BF16), e_ref[...])).astype(BF16)
    ovm_ref[...] = _dot(c_kv, wuv_ref[...]).astype(BF16)

    og_ref[...] = _sigmoid(_dot(h, w_ref[:, C_GATE:C_END])).astype(BF16)


def _proj_call(x, ada, lw, layer, rope_tabs, kv_dtype, emit_cache, tm):
    bsz, seq, _ = x.shape
    nj = seq // tm
    rope = rope_tabs is not None

    def tok(width):
        return pl.BlockSpec((None, tm, width), lambda b, j: (b, j, 0))

    def full(arr):
        return _layer_spec(arr, layer)

    def tab(width):
        return pl.BlockSpec((tm, width), lambda b, j: (j, 0))

    args = [x, ada[0], lw["w_in"], lw["q_norm"], lw["kv_norm"], lw["w_uq"]]
    specs = [tok(D_MODEL), _ada_spec(ada, layer, lambda b, j: b),
             full(lw["w_in"]), full(lw["q_norm"]), full(lw["kv_norm"]), full(lw["w_uq"])]
    if rope:
        args.append(lw["w_uq_sw"])
        specs.append(full(lw["w_uq_sw"]))
    e_mat = lw["e_rope"] if rope else lw["e_plain"]
    args += [lw["w_uk"], lw["w_uv"], e_mat]
    specs += [full(lw["w_uk"]), full(lw["w_uv"]), _const_spec(e_mat)]
    if rope:
        args += list(rope_tabs)
        specs += [tab(MLA_HEAD_PAD), tab(MLA_HEAD_PAD), tab(LANES)]

    def shp(width, dt):
        return jax.ShapeDtypeStruct((bsz, seq, width), dt)

    out_shape = [shp(NA_WIDTH, BF16), shp(NA_WIDTH, kv_dtype), shp(NA_WIDTH, kv_dtype),
                 shp(MLA_PAD_WIDTH, BF16), shp(MLA_PAD_WIDTH, BF16), shp(NA_WIDTH, BF16),
                 shp(2 * D_MODEL, BF16)]
    out_specs = [tok(NA_WIDTH), tok(NA_WIDTH), tok(NA_WIDTH), tok(MLA_PAD_WIDTH), tok(MLA_PAD_WIDTH),
                 tok(NA_WIDTH), tok(2 * D_MODEL)]
    if emit_cache:
        out_shape += [shp(MLA_KV_LORA, F32), shp(MLA_ROPE_DIM, F32)]
        out_specs += [tok(MLA_KV_LORA), tok(MLA_ROPE_DIM)]
    return pl.pallas_call(
        functools.partial(_proj_body, rope=rope, emit_cache=emit_cache),
        grid=(bsz, nj),
        in_specs=specs,
        out_specs=out_specs,
        out_shape=out_shape,
        compiler_params=_params(("arbitrary", "arbitrary")),
        name="proj_lat" if rope else "proj_ctx",
    )(*args)


def _ctxkv_body(ckv_ref, kr_ref, wuk_ref, wuv_ref, e_ref, okm_ref, ovm_ref):
    c_kv = ckv_ref[...].astype(BF16)
    okm_ref[...] = (_dot(c_kv, wuk_ref[...]) + _dot(kr_ref[...].astype(BF16), e_ref[...])).astype(BF16)
    ovm_ref[...] = _dot(c_kv, wuv_ref[...]).astype(BF16)


def _ctxkv_call(cache_ckv, cache_kr_pad, layer, lw):
    bsz, _, past, _ = cache_ckv.shape

    def full(arr):
        return _layer_spec(arr, layer)

    return pl.pallas_call(
        _ctxkv_body,
        grid=(bsz,),
        in_specs=[
            pl.BlockSpec((None, None, past, MLA_KV_LORA), lambda b: (b, layer, 0, 0)),
            pl.BlockSpec((None, None, past, LANES), lambda b: (b, layer, 0, 0)),
            full(lw["w_uk"]), full(lw["w_uv"]), _const_spec(lw["e_plain"]),
        ],
        out_specs=[pl.BlockSpec((None, past, MLA_PAD_WIDTH), lambda b: (b, 0, 0)),
                   pl.BlockSpec((None, past, NA_WIDTH), lambda b: (b, 0, 0))],
        out_shape=[jax.ShapeDtypeStruct((bsz, past, MLA_PAD_WIDTH), BF16),
                   jax.ShapeDtypeStruct((bsz, past, NA_WIDTH), BF16)],
        compiler_params=_params(("arbitrary",)),
        name="ctxkv",
    )(cache_ckv, cache_kr_pad, lw["w_uk"], lw["w_uv"], lw["e_plain"])


def _local_bias(pb_ref, rm_ref, hh, q0, k0, q_rows, k_rows, grid_rows):
    rows = []
    for qr in range(q_rows):
        blocks = []
        for p in range(k_rows // 2):
            pair = k0 // 2 + p
            e = 2 * pair - (q0 + qr) + grid_rows
            blocks.append(pb_ref[hh, e] + rm_ref[pl.ds((q0 + qr) * (grid_rows // 2) + pair, 1), :])
        rows.append(jnp.concatenate(blocks, axis=-1))
    return jnp.concatenate(rows, axis=0)


def _attn_body(*refs, n_seg, dq, local, heads):
    q_ref = refs[0]
    seg_refs = [(refs[1 + 2 * i], refs[2 + 2 * i]) for i in range(n_seg)]
    o_ref = refs[-1]
    if local is not None:
        pb_ref, rm_ref = refs[1 + 2 * n_seg], refs[2 + 2 * n_seg]
        q_rows, k_rows, grid_rows = local
        q0 = pl.program_id(1) * q_rows
        k0 = jnp.minimum(jnp.clip(q0 - NA_KR // 2, 0, grid_rows - NA_KR), grid_rows - k_rows)
        keys = pl.ds(pl.multiple_of(k0 * GRID_W, GRID_W), k_rows * GRID_W)
    head_major = [tuple(jnp.swapaxes(r[...], 0, 1).astype(BF16) for r in seg) if len(seg[0].shape) == 3 else None
                  for seg in seg_refs]
    outs = []
    for hh in range(heads):
        qh = q_ref[:, hh * dq:(hh + 1) * dq]
        scores = []
        values = []
        for i, (k_ref, v_ref) in enumerate(seg_refs):
            windowed = local is not None and i == n_seg - 1
            rows = keys if windowed else slice(None)
            if head_major[i] is not None:
                kh, vh = head_major[i][0][hh], head_major[i][1][hh]
            else:
                kh = k_ref[rows, hh * dq:(hh + 1) * dq].astype(BF16)
                vh = v_ref[rows, hh * MLA_V_DIM:(hh + 1) * MLA_V_DIM].astype(BF16)
            values.append(vh)
            s = lax.dot_general(qh, kh, (((1,), (1,)), ((), ())), preferred_element_type=F32)
            if windowed:
                s = s + _local_bias(pb_ref, rm_ref, hh, q0, k0, q_rows, k_rows, grid_rows)
            scores.append(s)
        m = functools.reduce(jnp.maximum, [jnp.max(s, axis=-1, keepdims=True) for s in scores])
        den = None
        acc = None
        for s, vh in zip(scores, values):
            e = jnp.exp2(s - m)
            d = jnp.sum(e, axis=-1, keepdims=True)
            a = _dot(e.astype(BF16), vh)
            den = d if den is None else den + d
            acc = a if acc is None else acc + a
        outs.append(acc / den)
    o_ref[...] = jnp.concatenate(outs, axis=-1).astype(o_ref.dtype)


def _attn_call(q, segs, local_bias, dq, tq, heads, name, layer=0):
    bsz, seq, _ = q.shape
    args = [q]
    specs = [pl.BlockSpec((None, tq, heads * dq), lambda hp, qt, b: (b, qt, hp))]
    for k, v, ks, vs in segs:
        args += [k, v]
        specs += [ks, vs]
    local = None
    if local_bias is not None:
        pair_blocks, row_masks, k_rows = local_bias
        args += [pair_blocks, row_masks]
        specs += [pl.BlockSpec((None, heads) + pair_blocks.shape[2:], lambda hp, qt, b: (layer, hp, 0, 0, 0),
                               pipeline_mode=pl.Buffered(1)),
                  _const_spec(row_masks)]
        local = (tq // GRID_W, k_rows, seq // GRID_W)
    return pl.pallas_call(
        functools.partial(_attn_body, n_seg=len(segs), dq=dq, local=local, heads=heads),
        grid=(NA_HEADS // heads, seq // tq, bsz),
        in_specs=specs,
        out_specs=pl.BlockSpec((None, tq, heads * MLA_V_DIM), lambda hp, qt, b: (b, qt, hp)),
        out_shape=jax.ShapeDtypeStruct((bsz, seq, NA_WIDTH), BF16),
        compiler_params=_params(("arbitrary", "arbitrary", "arbitrary")),
        name=name,
    )(*args)


def _seg3(arr, width):
    n = arr.shape[1]
    return pl.BlockSpec((None, n, width), lambda hp, qt, b: (b, 0, hp))


def _seg_cache(arr, layer):
    return pl.BlockSpec((None, None) + arr.shape[2:], lambda hp, qt, b: (b, layer, 0, 0, 0))


def _route(logits):
    lane = lax.broadcasted_iota(jnp.int32, logits.shape, 1)
    lane_f = lane.astype(F32)
    far = float(LANES)
    is_g = lane < N_GROUPS
    gl = jnp.where(is_g, logits, NEG_INF)
    gmax = jnp.max(gl, axis=-1, keepdims=True)
    gsum = jnp.sum(jnp.exp(gl - gmax), axis=-1, keepdims=True)
    g_w = 1.0 / gsum
    g_idx = jnp.min(jnp.where(gl == gmax, lane_f, far), axis=-1, keepdims=True)
    e_group = ((lane - R_EXP0) >> 3).astype(F32)
    is_e = (lane >= R_EXP0) & (lane < R_EXP0 + N_EXPERTS) & (e_group == g_idx)
    el = jnp.where(is_e, logits, NEG_INF)
    m1 = jnp.max(el, axis=-1, keepdims=True)
    i1 = jnp.min(jnp.where(el == m1, lane_f, far), axis=-1, keepdims=True)
    el2 = jnp.where(lane_f == i1, NEG_INF, el)
    m2 = jnp.max(el2, axis=-1, keepdims=True)
    i2 = jnp.min(jnp.where(el2 == m2, lane_f, far), axis=-1, keepdims=True)
    r = jnp.exp(m2 - m1)
    w1 = g_w / (1.0 + r)
    w2 = g_w * r / (1.0 + r)
    return lane_f, i1, i2, w1, w2


def _merge_body(ona_ref, omla_ref, g_ref, x_ref, ada_ref, wna_ref, wmla_ref, wout_ref, lng_ref, lnb_ref,
                wr_ref, wrl_ref, br_ref, ox_ref, oh_ref, or_ref, ocnt_ref, carry_ref):
    first = (pl.program_id(0) == 0) & (pl.program_id(1) == 0)

    @pl.when(first)
    def _():
        carry_ref[...] = jnp.zeros_like(carry_ref)

    y_na = _dot(ona_ref[...], wna_ref[...])
    y_mla = _dot(omla_ref[...], wmla_ref[...])
    g_na = g_ref[:, :D_MODEL].astype(F32)
    g_mla = g_ref[:, D_MODEL:].astype(F32)
    y = _dot((g_na * y_na + g_mla * y_mla).astype(BF16), wout_ref[...])
    g1 = ada_ref[2:3, :]
    x1 = _ln_plain(DEEPNORM_ALPHA * x_ref[...] + g1 * y) * lng_ref[...] + lnb_ref[...]
    ox_ref[...] = x1
    h2 = _ln_plain(x1) * (1.0 + ada_ref[4:5, :]) + ada_ref[3:4, :]
    _store_token_tiles(oh_ref, h2)
    h_hi = h2.astype(BF16)
    h_lo = (h2 - h_hi.astype(F32)).astype(BF16)
    logits = (_dot(h_hi, wr_ref[...]) + (_dot(h_lo, wr_ref[...]) + _dot(h_hi, wrl_ref[...]))) + br_ref[...]
    lane_f, i1, i2, w1, w2 = _route(logits)

    tm = logits.shape[0]
    sel1 = lane_f == i1
    sel2 = lane_f == i2
    used = (jnp.where(sel1, 1.0, 0.0) + jnp.where(sel2, 1.0, 0.0)).astype(BF16)
    row = lax.broadcasted_iota(jnp.int32, (tm, tm), 0)
    col = lax.broadcasted_iota(jnp.int32, (tm, tm), 1)
    earlier = jnp.where(row > col, 1.0, 0.0).astype(BF16)
    before = _dot(earlier, used) + carry_ref[0:1, :]
    rank1 = jnp.sum(jnp.where(sel1, before, 0.0), axis=-1, keepdims=True)
    rank2 = jnp.sum(jnp.where(sel2, before, 0.0), axis=-1, keepdims=True)
    carry_ref[...] += _dot(jnp.ones((8, tm), BF16), used)
    ocnt_ref[...] = carry_ref[...]

    fields = (i1 - R_EXP0, i2 - R_EXP0, w1, w2, rank1, rank2)
    route = jnp.zeros_like(logits)
    for k, val in enumerate(fields):
        route = jnp.where(lane_f == float(k), val, route)
    or_ref[...] = route


def _merge_call(o_na, o_mla, gates, x, ada, lw, layer, tm, name):
    bsz, seq, _ = x.shape

    def tok(width):
        return pl.BlockSpec((None, tm, width), lambda b, j: (b, j, 0))

    def full(arr):
        return _layer_spec(arr, layer)

    ws = [lw["w_o_na"], lw["w_o_mla"], lw["w_out"], lw["ln1_g"], lw["ln1_b"], lw["w_r"], lw["w_r_lo"], lw["b_r"]]
    return pl.pallas_call(
        _merge_body,
        grid=(bsz, seq // tm),
        in_specs=[tok(NA_WIDTH), tok(NA_WIDTH), tok(2 * D_MODEL), tok(D_MODEL),
                  _ada_spec(ada, layer, lambda b, j: b)] + [full(w) for w in ws],
        out_specs=[tok(D_MODEL), pl.BlockSpec((None, tm) + TOKEN_TILE, lambda b, j: (b, j, 0, 0)), tok(LANES),
                   pl.BlockSpec((8, LANES), lambda b, j: (0, 0))],
        out_shape=[jax.ShapeDtypeStruct((bsz, seq, D_MODEL), F32),
                   jax.ShapeDtypeStruct((bsz, seq) + TOKEN_TILE, F32),
                   jax.ShapeDtypeStruct((bsz, seq, LANES), F32),
                   jax.ShapeDtypeStruct((8, LANES), F32)],
        scratch_shapes=[pltpu.VMEM((8, LANES), F32)],
        compiler_params=_params(("arbitrary", "arbitrary")),
        name=name,
    )(o_na, o_mla, gates, x, ada[0], *ws)


POS_BITS = 16
POS_MASK = (1 << POS_BITS) - 1


def _plan(route, counts, n_tok, tr):
    r = route.reshape(n_tok, LANES)
    e1, e2 = r[:, 0].astype(jnp.int32), r[:, 1].astype(jnp.int32)
    rk1, rk2 = r[:, 4].astype(jnp.int32), r[:, 5].astype(jnp.int32)
    cnt = counts[0, R_EXP0:R_EXP0 + N_EXPERTS].astype(jnp.int32)
    tiles_e = (cnt + tr - 1) // tr
    tile_end = jnp.cumsum(tiles_e)
    row_off = (tile_end - tiles_e) * tr
    eid = jnp.arange(N_EXPERTS, dtype=jnp.int32)
    p1 = jnp.sum(jnp.where(e1[:, None] == eid, row_off, 0), axis=-1) + rk1
    p2 = jnp.sum(jnp.where(e2[:, None] == eid, row_off, 0), axis=-1) + rk2
    n_tiles = 2 * n_tok // tr + N_EXPERTS
    tile_off = jnp.concatenate([jnp.zeros((1,), jnp.int32), tile_end.astype(jnp.int32)])
    return p1 | (p2 << POS_BITS), tile_off, cnt, n_tiles


ROW_GROUP = 8


def _expert_body(pos_ref, toff_ref, cnt_ref, h_hbm, wg_ref, wu_ref, wd_ref, y_hbm,
                 src_ref, h_ref, hsem, hbuf, obuf, sem, wg_b, wu_b, wd_b, *, n_tok, tr, n_tiles):
    e = pl.program_id(0)
    t0 = toff_ref[e]
    n_mine = toff_ref[e + 1] - t0
    n_used = toff_ref[N_EXPERTS]

    def out_copy(slot, tile):
        return pltpu.make_async_copy(obuf.at[slot], y_hbm.at[pl.ds(tile * tr, tr)], sem.at[slot])

    @pl.when(e == 0)
    def _():
        hbuf[...] = jnp.zeros_like(hbuf)
        fetch = pltpu.make_async_copy(h_hbm, h_ref, hsem.at[0])
        fetch.start()

        def place(t, carry):
            packed = pos_ref[t]
            src_ref[packed & POS_MASK] = t
            src_ref[packed >> POS_BITS] = t
            return carry
        lax.fori_loop(0, n_tok, place, 0, unroll=8)
        fetch.wait()

    @pl.when(n_mine > 0)
    def _():
        wg_b[...] = wg_ref[...].astype(BF16)
        wu_b[...] = wu_ref[...].astype(BF16)
        wd_b[...] = wd_ref[...].astype(BF16)

    def tile_body(j, carry):
        tile = t0 + j
        slot = tile & 1
        base = tile * tr
        last = jnp.minimum(cnt_ref[e] - j * tr, tr) - 1

        def gather(g, inner):
            for k in range(ROW_GROUP):
                r = g * ROW_GROUP + k
                hbuf[r] = h_ref[src_ref[base + jnp.minimum(r, last)]]
            return inner
        lax.fori_loop(0, (last + ROW_GROUP) // ROW_GROUP, gather, 0)

        @pl.when(tile >= 2)
        def _():
            out_copy(slot, tile - 2).wait()

        h = _load_token_tiles(hbuf).astype(BF16)
        gate = _dot(h, wg_b[...])
        up = _dot(h, wu_b[...])
        hid = (gate * _sigmoid(gate) * up).astype(BF16)
        _store_token_tiles(obuf.at[slot], _dot(hid, wd_b[...]))
        out_copy(slot, tile).start()
        return carry
    lax.fori_loop(0, n_mine, tile_body, 0)

    @pl.when(e == pl.num_programs(0) - 1)
    def _():
        @pl.when(n_used >= 2)
        def _():
            out_copy(n_used & 1, n_used - 2).wait()
        out_copy((n_used - 1) & 1, n_used - 1).wait()
        obuf[0] = jnp.zeros(obuf.shape[1:], obuf.dtype)

        def zero_start(tile, carry):
            out_copy(0, tile).start()
            return carry

        def zero_wait(tile, carry):
            out_copy(0, tile).wait()
            return carry
        lax.fori_loop(n_used, n_tiles, zero_start, 0)
        lax.fori_loop(n_used, n_tiles, zero_wait, 0)


def _expert_call(h2, plan, w_gate, w_up, w_down, layer, tr, name):
    packed, tile_off, cnt, n_tiles = plan
    n_tok = h2.shape[0]
    n_rows = n_tiles * tr

    def weight_spec(rows, cols):
        return pl.BlockSpec((None, None, rows, cols), lambda e, pos, toff, cnt: (layer, e, 0, 0))

    return pl.pallas_call(
        functools.partial(_expert_body, n_tok=n_tok, tr=tr, n_tiles=n_tiles),
        grid_spec=pltpu.PrefetchScalarGridSpec(
            num_scalar_prefetch=3,
            grid=(N_EXPERTS,),
            in_specs=[pl.BlockSpec(memory_space=pl.ANY), weight_spec(D_MODEL, EXPERT_FF),
                      weight_spec(D_MODEL, EXPERT_FF),
                      weight_spec(EXPERT_FF, D_MODEL)],
            out_specs=pl.BlockSpec(memory_space=pl.ANY),
            scratch_shapes=[
                pltpu.SMEM((n_rows,), jnp.int32),
                pltpu.VMEM(h2.shape, F32),
                pltpu.SemaphoreType.DMA((1,)),
                pltpu.VMEM((tr,) + TOKEN_TILE, F32),
                pltpu.VMEM((2, tr) + TOKEN_TILE, F32),
                pltpu.SemaphoreType.DMA((2,)),
                pltpu.VMEM((D_MODEL, EXPERT_FF), BF16),
                pltpu.VMEM((D_MODEL, EXPERT_FF), BF16),
                pltpu.VMEM((EXPERT_FF, D_MODEL), BF16),
            ],
        ),
        out_shape=jax.ShapeDtypeStruct((n_rows,) + TOKEN_TILE, F32),
        compiler_params=_params(("arbitrary",)),
        name=name,
    )(packed, tile_off, cnt, h2, w_gate, w_up, w_down)


def _combine_body(pos_ref, y_hbm, r_ref, x_ref, ada_ref, lng_ref, lnb_ref, o_ref, ybuf, sem, *, tm):
    i = pl.program_id(0)
    n = pl.num_programs(0)

    def for_tile_rows(tile, slot, fn):
        base = tile * tm
        dst1 = ybuf.at[slot, 0]
        dst2 = ybuf.at[slot, 1]
        dsem = sem.at[slot]

        def body(g, carry):
            for k in range(ROW_GROUP):
                r = g * ROW_GROUP + k
                packed = pos_ref[base + r]
                fn(pltpu.make_async_copy(y_hbm.at[packed & POS_MASK], dst1.at[r], dsem))
                fn(pltpu.make_async_copy(y_hbm.at[packed >> POS_BITS], dst2.at[r], dsem))
            return carry
        lax.fori_loop(0, tm // ROW_GROUP, body, 0)

    @pl.when(i == 0)
    def _():
        for_tile_rows(0, 0, lambda cp: cp.start())

    @pl.when(i + 1 < n)
    def _():
        for_tile_rows(i + 1, (i + 1) & 1, lambda cp: cp.start())

    slot = i & 1
    for_tile_rows(i, slot, lambda cp: cp.wait())

    route = r_ref[...]
    lane = lax.broadcasted_iota(jnp.int32, route.shape, 1)
    w1 = jnp.sum(jnp.where(lane == 2, route, 0.0), axis=-1, keepdims=True)
    w2 = jnp.sum(jnp.where(lane == 3, route, 0.0), axis=-1, keepdims=True)
    y = w1 * _load_token_tiles(ybuf.at[slot, 0]) + w2 * _load_token_tiles(ybuf.at[slot, 1])
    g2 = ada_ref[5:6, :]
    o_ref[...] = _ln_plain(DEEPNORM_ALPHA * x_ref[...] + g2 * y) * lng_ref[...] + lnb_ref[...]


def _combine_call(ys, packed, route, x1, ada, lw, layer, seq, tm, name):
    n_tok = x1.shape[0]
    per_batch = seq // tm

    def tok(width):
        return pl.BlockSpec((tm, width), lambda i, pos: (i, 0))

    def full(arr):
        return _layer_spec(arr, layer)

    return pl.pallas_call(
        functools.partial(_combine_body, tm=tm),
        grid_spec=pltpu.PrefetchScalarGridSpec(
            num_scalar_prefetch=1,
            grid=(n_tok // tm,),
            in_specs=[pl.BlockSpec(memory_space=pl.ANY), tok(LANES), tok(D_MODEL),
                      _ada_spec(ada, layer, lambda i, pos: i // per_batch),
                      full(lw["ln2_g"]), full(lw["ln2_b"])],
            out_specs=tok(D_MODEL),
            scratch_shapes=[pltpu.VMEM((2, 2, tm) + TOKEN_TILE, F32), pltpu.SemaphoreType.DMA((2,))],
        ),
        out_shape=jax.ShapeDtypeStruct((n_tok, D_MODEL), F32),
        compiler_params=_params(("arbitrary",)),
        name=name,
    )(packed, ys, route, x1, ada[0], lw["ln2_g"], lw["ln2_b"])


def _moe(h2, route, counts, x1, ada, lw, experts, layer, tr, tm, name):
    bsz, seq, _ = x1.shape
    n_tok = bsz * seq
    plan = _plan(route, counts, n_tok, tr)
    ys = _expert_call(h2.reshape((n_tok,) + TOKEN_TILE), plan, *experts, layer, tr, "expert_" + name)
    out = _combine_call(ys, plan[0], route.reshape(n_tok, LANES), x1.reshape(n_tok, D_MODEL), ada, lw, layer,
                        seq, tm, "combine_" + name)
    return out.reshape(bsz, seq, D_MODEL)


def _placement(rows_used):
    e = np.zeros((LANES, MLA_PAD_WIDTH), np.float32)
    for r in range(rows_used):
        for hd in range(MLA_HEADS):
            e[r, hd * MLA_HEAD_PAD + MLA_NOPE_DIM + (r % MLA_ROPE_DIM)] = 1.0
    return jnp.asarray(e, BF16)


def _prepare_weights(p):
    partner, _ = _rope_partner()
    n_layers = p["w_in"].shape[0]
    w = p["w_in"]
    k_r = w[:, :, 2176:2208]
    pieces = [w[:, :, 0:512] * NA_SCALE, w[:, :, 512:2176], k_r, k_r[:, :, partner],
              jnp.zeros((n_layers, D_MODEL, LANES - 2 * MLA_ROPE_DIM), F32), w[:, :, 2208:]]
    w_in = jnp.concatenate([piece.astype(BF16) for piece in pieces], axis=2)
    uq = p["w_uq"].reshape(n_layers, MLA_Q_LORA, MLA_HEADS, MLA_QK_DIM)
    pad = MLA_HEAD_PAD - MLA_QK_DIM
    no_pad = ((0, 0), (0, 0), (0, 0))
    w_uq = jnp.pad(uq, no_pad + ((0, pad),)).reshape(n_layers, MLA_Q_LORA, MLA_PAD_WIDTH).astype(BF16)
    uq_sw = jnp.pad(uq[..., MLA_NOPE_DIM + partner], no_pad + ((MLA_NOPE_DIM, pad),))
    w_uq_sw = uq_sw.reshape(n_layers, MLA_Q_LORA, MLA_PAD_WIDTH).astype(BF16)
    uk = p["w_uk"].reshape(n_layers, MLA_KV_LORA, MLA_HEADS, MLA_NOPE_DIM)
    w_uk = jnp.pad(uk, no_pad + ((0, MLA_HEAD_PAD - MLA_NOPE_DIM),)).reshape(
        n_layers, MLA_KV_LORA, MLA_PAD_WIDTH).astype(BF16)
    n_pad = LANES - N_GROUPS - N_EXPERTS
    w_r = jnp.concatenate([p["router_group_w"], p["router_expert_w"],
                           jnp.zeros((n_layers, D_MODEL, n_pad), F32)], axis=2)
    b_r = jnp.concatenate([p["router_group_b"], p["router_expert_b"], jnp.zeros((n_layers, n_pad), F32)], axis=1)
    w_r_hi = w_r.astype(BF16)

    def row(name):
        return p[name][:, None, :]

    return dict(
        w_in=w_in, w_uq=w_uq, w_uq_sw=w_uq_sw, w_uk=w_uk, w_uv=p["w_uv"].astype(BF16),
        q_norm=row("mla_q_norm"), kv_norm=row("mla_kv_norm"),
        e_plain=_placement(MLA_ROPE_DIM), e_rope=_placement(2 * MLA_ROPE_DIM),
        w_o_na=p["w_o_na"].astype(BF16), w_o_mla=p["w_o_mla"].astype(BF16), w_out=p["w_out"].astype(BF16),
        ln1_g=row("ln1_g"), ln1_b=row("ln1_b"), ln2_g=row("ln2_g"), ln2_b=row("ln2_b"),
        w_r=w_r_hi, w_r_lo=(w_r - w_r_hi.astype(F32)).astype(BF16), b_r=b_r[:, None, :],
    )


def _window_rows(rows, q_rows):
    firsts = np.arange(0, rows, q_rows)
    lo = np.clip(firsts - NA_KR // 2, 0, rows - NA_KR)
    hi = np.clip(firsts + q_rows - 1 - NA_KR // 2, 0, rows - NA_KR) + NA_KR
    k_rows = min(rows, int(np.max(hi - lo)) + int(np.max(hi - lo)) % 2)
    k0 = np.minimum(lo, rows - k_rows)
    assert np.all(k0 % 2 == 0) and np.all(k0 + k_rows >= hi), (k0, k_rows)
    return k_rows, k0


def _na_bias_tables(rpb, rows):
    assert rows >= NA_KR and rows % 2 == 0
    r = np.arange(rows)
    rs = np.clip(r - NA_KR // 2, 0, rows - NA_KR)
    vrow = (r[None, :] >= rs[:, None]) & (r[None, :] < rs[:, None] + NA_KR)
    c = np.arange(GRID_W)
    cs = np.clip(c - NA_KC // 2, 0, GRID_W - NA_KC)
    vcol = (c[None, :] >= cs[:, None]) & (c[None, :] < cs[:, None] + NA_KC)
    coff = c[None, :] - c[:, None] + NA_KC - 1
    onehot = ((coff[None] == np.arange(2 * NA_KC - 1)[:, None, None]) & vcol[None]).astype(np.float32)
    zero = np.zeros_like(onehot)
    reach = NA_KR - 1
    halves = []
    for placed in (np.concatenate([onehot, zero], axis=-1), np.concatenate([zero, onehot], axis=-1)):
        vals = jnp.einsum("lhdj,jqk->lhdqk", rpb * LOG2E, jnp.asarray(placed), precision=lax.Precision.HIGHEST)
        halves.append(jnp.pad(vals, ((0, 0), (0, 0), (rows - reach, rows - reach - 1), (0, 0), (0, 0))))
    valid_d = np.abs(np.arange(-rows, rows)) <= reach
    valid = valid_d[:, None, None] & vcol[None]
    valid_pair = np.concatenate([valid[:-1], valid[1:]], axis=-1)
    neg_mask = jnp.asarray(np.where(valid_pair, 0.0, NEG_INF).astype(np.float32))
    pair_blocks = halves[0][:, :, :-1] + halves[1][:, :, 1:] + neg_mask
    in_window = np.repeat(vrow.reshape(rows, rows // 2, 2), GRID_W, axis=-1)
    row_masks = np.where(in_window, 0.0, NEG_INF).astype(np.float32).reshape(rows * rows // 2, 2 * GRID_W)
    return pair_blocks, jnp.asarray(row_masks)


def kernel(x_prompt, x_sample, cache_na_k, cache_na_v, cache_mla_ckv, cache_mla_kr, c, c_ctx, w_in, mla_q_norm, mla_kv_norm, w_uq, w_uk, w_uv, na_rpb, w_o_na, w_o_mla, w_out, w_ada, b_ada, ln1_g, ln1_b, ln2_g, ln2_b, router_group_w, router_group_b, router_expert_w, router_expert_b, expert_w_gate, expert_w_up, expert_w_down):
    p = dict(w_in=w_in, mla_q_norm=mla_q_norm, mla_kv_norm=mla_kv_norm, w_uq=w_uq, w_uk=w_uk, w_uv=w_uv,
             w_o_na=w_o_na, w_o_mla=w_o_mla, w_out=w_out, ln1_g=ln1_g, ln1_b=ln1_b, ln2_g=ln2_g, ln2_b=ln2_b,
             router_group_w=router_group_w, router_group_b=router_group_b,
             router_expert_w=router_expert_w, router_expert_b=router_expert_b,
             expert_w_gate=expert_w_gate, expert_w_up=expert_w_up, expert_w_down=expert_w_down)
    pb, ps, _ = x_prompt.shape
    sb, ss, _ = x_sample.shape

    n_cond = 16
    cond = jnp.zeros((n_cond, D_MODEL), F32).at[0].set(c_ctx).at[1:1 + sb].set(c)
    ada_all = _ada_call(cond, w_ada, b_ada[:, None, :]).reshape(DEPTH, n_cond, 6, D_MODEL)
    ada_all = jnp.pad(ada_all, ((0, 0), (0, 0), (0, 2), (0, 0)))
    ada_p = (ada_all, 0, 0)
    ada_s = (ada_all, 1, 1)

    cos32, sin32 = _rope_tables(ss)
    head_c = np.concatenate([np.ones((ss, MLA_NOPE_DIM)), cos32, np.zeros((ss, MLA_HEAD_PAD - MLA_QK_DIM))], -1)
    head_s = np.concatenate([np.zeros((ss, MLA_NOPE_DIM)), sin32, np.zeros((ss, MLA_HEAD_PAD - MLA_QK_DIM))], -1)
    rope_tabs = tuple(jnp.asarray(t, F32) for t in (
        head_c * (MLA_SCALE * LOG2E), head_s * (MLA_SCALE * LOG2E),
        np.concatenate([cos32, sin32, np.zeros((ss, LANES - 2 * MLA_ROPE_DIM))], -1)))

    ckr_pad = jnp.pad(cache_mla_kr, ((0, 0), (0, 0), (0, 0), (0, LANES - MLA_ROPE_DIM)))

    experts = (expert_w_gate, expert_w_up, expert_w_down)
    lw = _prepare_weights(p)
    tq = 512
    grid_rows = ss // GRID_W
    local_bias = _na_bias_tables(na_rpb, grid_rows) + (_window_rows(grid_rows, tq // GRID_W)[0],)
    xp, xs = x_prompt, x_sample
    new_k, new_v, new_ckv, new_kr = [], [], [], []
    for l in range(DEPTH):

        pair = lambda a: a.reshape((pb // 2, 2 * ps) + a.shape[2:])
        unpair = lambda a: a.reshape((pb, ps) + a.shape[2:])
        qn, kn, vn, qm, km, vm, gates, ckv, k_r = map(
            unpair, _proj_call(pair(xp), ada_p, lw, l, None, F32, True, 2 * ps))
        o_na = _attn_call(qn, [(kn, vn, _seg3(kn, NA_WIDTH), _seg3(vn, NA_WIDTH))], None, NA_HEAD_DIM, ps,
                          NA_HEADS, "attn_ctx_na")
        o_mla = _attn_call(qm, [(km, vm, _seg3(km, MLA_PAD_WIDTH), _seg3(vm, NA_WIDTH))], None, MLA_HEAD_PAD, ps,
                           MLA_HEADS, "attn_ctx_mla")
        x1, h2, route, counts = _merge_call(pair(o_na), pair(o_mla), pair(gates), pair(xp), ada_p, lw, l, 2 * ps,
                                            "merge_ctx")
        xp = _moe(unpair(h2), unpair(route), counts, unpair(x1), ada_p, lw, experts, l, 128, ps, "ctx")
        new_k.append(kn)
        new_v.append(vn)
        new_ckv.append(ckv)
        new_kr.append(k_r)

        qn, kn, vn, qm, km, vm, gates = _proj_call(xs, ada_s, lw, l, rope_tabs, BF16, False, 512)
        km_ctx, vm_ctx = _ctxkv_call(cache_mla_ckv, ckr_pad, l, lw)
        o_na = _attn_call(
            qn, [(cache_na_k, cache_na_v, _seg_cache(cache_na_k, l), _seg_cache(cache_na_v, l)),
                 (kn, vn, _seg3(kn, NA_WIDTH), _seg3(vn, NA_WIDTH))], local_bias, NA_HEAD_DIM, tq, NA_HEADS,
            "attn_lat_na", l)
        o_mla = _attn_call(
            qm, [(km_ctx, vm_ctx, _seg3(km_ctx, MLA_PAD_WIDTH), _seg3(vm_ctx, NA_WIDTH)),
                 (km, vm, _seg3(km, MLA_PAD_WIDTH), _seg3(vm, NA_WIDTH))], None, MLA_HEAD_PAD, 512, MLA_HEADS,
            "attn_lat_mla")
        x1, h2, route, counts = _merge_call(o_na, o_mla, gates, xs, ada_s, lw, l, 512, "merge_lat")
        xs = _moe(h2, route, counts, x1, ada_s, lw, experts, l, 256, 256, "lat")

    def stack(parts, tail):
        return jnp.stack(parts, axis=1).reshape((pb, DEPTH, ps) + tail)

    return (xp, xs, stack(new_k, (NA_HEADS, NA_HEAD_DIM)), stack(new_v, (NA_HEADS, NA_HEAD_DIM)),
            stack(new_ckv, (MLA_KV_LORA,)), stack(new_kr, (MLA_ROPE_DIM,)))
```

```python
import functools

import numpy as np
import jax
import jax.numpy as jnp
from jax import lax
from jax.experimental import pallas as pl
from jax.experimental.pallas import tpu as pltpu

F32 = jnp.float32
BF16 = jnp.bfloat16

D_MODEL = 1024
DEPTH = 2
GRID_W = 64
NA_HEADS = 8
NA_HEAD_DIM = 64
NA_KR = 8
NA_KC = 16
NA_WIDTH = NA_HEADS * NA_HEAD_DIM
NA_SCALE = NA_HEAD_DIM ** -0.5
MLA_HEADS = 8
MLA_NOPE_DIM = 64
MLA_ROPE_DIM = 32
MLA_V_DIM = 64
MLA_Q_LORA = 384
MLA_KV_LORA = 256
MLA_QK_DIM = MLA_NOPE_DIM + MLA_ROPE_DIM
MLA_SCALE = MLA_QK_DIM ** -0.5
ROPE_BASE = 10000.0
N_GROUPS = 4
EXPERTS_PER_GROUP = 8
N_EXPERTS = N_GROUPS * EXPERTS_PER_GROUP
EXPERT_FF = 256
LN_EPS = 1e-5
RMS_EPS = 1e-6
NEG_INF = -1e30
LOG2E = 1.4426950408889634
DEEPNORM_ALPHA = (2 * DEPTH) ** 0.25

LANES = 128
MLA_HEAD_PAD = LANES
MLA_PAD_WIDTH = MLA_HEADS * MLA_HEAD_PAD
C_Q = 0
C_K = C_Q + NA_WIDTH
C_V = C_K + NA_WIDTH
C_CQ = C_V + NA_WIDTH
C_CKV = C_CQ + MLA_Q_LORA
C_KR = C_CKV + MLA_KV_LORA
C_GATE = C_KR + LANES
C_END = C_GATE + 2 * D_MODEL
R_EXP0 = N_GROUPS

VMEM_LIMIT = 56 * 1024 * 1024


def _rope_partner():
    j = np.arange(MLA_ROPE_DIM)
    n_freq = MLA_ROPE_DIM // 4
    axis, within = j // (2 * n_freq), j % (2 * n_freq)
    half, f = within // n_freq, within % n_freq
    return axis * 2 * n_freq + (1 - half) * n_freq + f, half


def _rope_tables(n_tokens):
    n_freq = MLA_ROPE_DIM // 4
    inv_freq = ROPE_BASE ** (-np.arange(n_freq, dtype=np.float64) / n_freq)
    t = np.arange(n_tokens)
    ang = np.concatenate([(t // GRID_W)[:, None] * inv_freq, (t % GRID_W)[:, None] * inv_freq], axis=-1)
    cos, sin = np.cos(ang), np.sin(ang)
    cos32 = np.concatenate([cos[:, :n_freq], cos[:, :n_freq], cos[:, n_freq:], cos[:, n_freq:]], axis=-1)
    sin32 = np.concatenate([-sin[:, :n_freq], sin[:, :n_freq], -sin[:, n_freq:], sin[:, n_freq:]], axis=-1)
    return cos32, sin32


def _ln_plain(x):
    mu = jnp.mean(x, axis=-1, keepdims=True)
    xc = x - mu
    var = jnp.mean(xc * xc, axis=-1, keepdims=True)
    return xc * lax.rsqrt(var + LN_EPS)


def _rms(x, g):
    return x * lax.rsqrt(jnp.mean(x * x, axis=-1, keepdims=True) + RMS_EPS) * g


def _sigmoid(x):
    return 1.0 / (1.0 + jnp.exp(-x))


def _dot(a, b):
    return jnp.dot(a, b, preferred_element_type=F32)


def _params(sem):
    return pltpu.CompilerParams(dimension_semantics=sem, vmem_limit_bytes=VMEM_LIMIT)


def _layer_spec(arr, layer):
    zeros = (0,) * (arr.ndim - 1)
    return pl.BlockSpec((None,) + arr.shape[1:], lambda *_: (layer,) + zeros, pipeline_mode=pl.Buffered(1))


def _ada_spec(ada, layer, batch_of):
    _, first, step = ada
    return pl.BlockSpec((None, None, 8, D_MODEL), lambda *idx: (layer, first + step * batch_of(*idx), 0, 0))


def _const_spec(arr):
    zeros = (0,) * arr.ndim
    return pl.BlockSpec(arr.shape, lambda *_: zeros, pipeline_mode=pl.Buffered(1))


TOKEN_TILE = (D_MODEL // LANES, LANES)


def _store_token_tiles(ref, val):
    chunks = jnp.stack([val[:, k * LANES:(k + 1) * LANES] for k in range(TOKEN_TILE[0])], axis=0)
    ref[...] = jnp.swapaxes(chunks, 0, 1)


def _load_token_tiles(ref):
    chunks = jnp.swapaxes(ref[...], 0, 1)
    return jnp.concatenate([chunks[k] for k in range(TOKEN_TILE[0])], axis=-1)


def _ada_body(c_ref, w_ref, b_ref, o_ref):
    c = c_ref[...]
    s = (c * _sigmoid(c)).astype(BF16)
    o_ref[0] = _dot(s, w_ref[0].astype(BF16)) + b_ref[0]


def _ada_call(cond, w_ada, b_ada):
    n_rows = cond.shape[0]
    tn = 1024
    return pl.pallas_call(
        _ada_body,
        grid=(DEPTH, 6 * D_MODEL // tn),
        in_specs=[
            pl.BlockSpec((n_rows, D_MODEL), lambda l, j: (0, 0)),
            pl.BlockSpec((1, D_MODEL, tn), lambda l, j: (l, 0, j)),
            pl.BlockSpec((1, 1, tn), lambda l, j: (l, 0, j)),
        ],
        out_specs=pl.BlockSpec((1, n_rows, tn), lambda l, j: (l, 0, j)),
        out_shape=jax.ShapeDtypeStruct((DEPTH, n_rows, 6 * D_MODEL), F32),
        compiler_params=_params(("arbitrary", "arbitrary")),
        name="ada",
    )(cond, w_ada, b_ada)


def _proj_body(*refs, rope, emit_cache):
    it = iter(refs)
    x_ref, ada_ref, w_ref, qn_ref, kvn_ref, wuq_ref = (next(it) for _ in range(6))
    wuqs_ref = next(it) if rope else None
    wuk_ref, wuv_ref, e_ref = next(it), next(it), next(it)
    if rope:
        cq_ref, sq_ref, tk_ref = next(it), next(it), next(it)
    oq_ref, ok_ref, ov_ref, oqm_ref, okm_ref, ovm_ref, og_ref = (next(it) for _ in range(7))
    if emit_cache:
        ockv_ref, okr_ref = next(it), next(it)

    x = x_ref[...]
    sh = ada_ref[0:1, :]
    sc = ada_ref[1:2, :]
    h = (_ln_plain(x) * (1.0 + sc) + sh).astype(BF16)

    oq_ref[...] = (_dot(h, w_ref[:, C_Q:C_K]) * LOG2E).astype(oq_ref.dtype)
    ok_ref[...] = _dot(h, w_ref[:, C_K:C_V]).astype(ok_ref.dtype)
    ov_ref[...] = _dot(h, w_ref[:, C_V:C_CQ]).astype(ov_ref.dtype)

    c_q = _rms(_dot(h, w_ref[:, C_CQ:C_CKV]), qn_ref[...]).astype(BF16)
    q = _dot(c_q, wuq_ref[...])
    if rope:
        q = (q * jnp.tile(cq_ref[...], (1, MLA_HEADS))
             + _dot(c_q, wuqs_ref[...]) * jnp.tile(sq_ref[...], (1, MLA_HEADS)))
    else:
        q = q * (MLA_SCALE * LOG2E)
    oqm_ref[...] = q.astype(BF16)

    c_kv = _rms(_dot(h, w_ref[:, C_CKV:C_KR]), kvn_ref[...])
    slab = _dot(h, w_ref[:, C_KR:C_GATE])
    if emit_cache:
        ockv_ref[...] = c_kv
        okr_ref[...] = slab[:, :MLA_ROPE_DIM]
    if rope:
        slab = slab * tk_ref[...]
    c_kv = c_kv.astype(BF16)
    okm_ref[...] = (_dot(c_kv, wuk_ref[...]) + _dot(slab.astype(BF16), e_ref[...])).astype(BF16)
    ovm_ref[...] = _dot(c_kv, wuv_ref[...]).astype(BF16)

    og_ref[...] = _sigmoid(_dot(h, w_ref[:, C_GATE:C_END])).astype(BF16)


def _proj_call(x, ada, lw, layer, rope_tabs, kv_dtype, emit_cache, tm):
    bsz, seq, _ = x.shape
    nj = seq // tm
    rope = rope_tabs is not None

    def tok(width):
        return pl.BlockSpec((None, tm, width), lambda b, j: (b, j, 0))

    def full(arr):
        return _layer_spec(arr, layer)

    def tab(width):
        return pl.BlockSpec((tm, width), lambda b, j: (j, 0))

    args = [x, ada[0], lw["w_in"], lw["q_norm"], lw["kv_norm"], lw["w_uq"]]
    specs = [tok(D_MODEL), _ada_spec(ada, layer, lambda b, j: b),
             full(lw["w_in"]), full(lw["q_norm"]), full(lw["kv_norm"]), full(lw["w_uq"])]
    if rope:
        args.append(lw["w_uq_sw"])
        specs.append(full(lw["w_uq_sw"]))
    e_mat = lw["e_rope"] if rope else lw["e_plain"]
    args += [lw["w_uk"], lw["w_uv"], e_mat]
    specs += [full(lw["w_uk"]), full(lw["w_uv"]), _const_spec(e_mat)]
    if rope:
        args += list(rope_tabs)
        specs += [tab(MLA_HEAD_PAD), tab(MLA_HEAD_PAD), tab(LANES)]

    def shp(width, dt):
        return jax.ShapeDtypeStruct((bsz, seq, width), dt)

    out_shape = [shp(NA_WIDTH, BF16), shp(NA_WIDTH, kv_dtype), shp(NA_WIDTH, kv_dtype),
                 shp(MLA_PAD_WIDTH, BF16), shp(MLA_PAD_WIDTH, BF16), shp(NA_WIDTH, BF16),
                 shp(2 * D_MODEL, BF16)]
    out_specs = [tok(NA_WIDTH), tok(NA_WIDTH), tok(NA_WIDTH), tok(MLA_PAD_WIDTH), tok(MLA_PAD_WIDTH),
                 tok(NA_WIDTH), tok(2 * D_MODEL)]
    if emit_cache:
        out_shape += [shp(MLA_KV_LORA, F32), shp(MLA_ROPE_DIM, F32)]
        out_specs += [tok(MLA_KV_LORA), tok(MLA_ROPE_DIM)]
    return pl.pallas_call(
        functools.partial(_proj_body, rope=rope, emit_cache=emit_cache),
        grid=(bsz, nj),
        in_specs=specs,
        out_specs=out_specs,
        out_shape=out_shape,
        compiler_params=_params(("arbitrary", "arbitrary")),
        name="proj_lat" if rope else "proj_ctx",
    )(*args)


def _ctxkv_body(ckv_ref, kr_ref, wuk_ref, wuv_ref, e_ref, okm_ref, ovm_ref):
    c_kv = ckv_ref[...].astype(BF16)
    okm_ref[...] = (_dot(c_kv, wuk_ref[...]) + _dot(kr_ref[...].astype(BF16), e_ref[...])).astype(BF16)
    ovm_ref[...] = _dot(c_kv, wuv_ref[...]).astype(BF16)


def _ctxkv_call(cache_ckv, cache_kr_pad, layer, lw):
    bsz, _, past, _ = cache_ckv.shape

    def full(arr):
        return _layer_spec(arr, layer)

    return pl.pallas_call(
        _ctxkv_body,
        grid=(bsz,),
        in_specs=[
            pl.BlockSpec((None, None, past, MLA_KV_LORA), lambda b: (b, layer, 0, 0)),
            pl.BlockSpec((None, None, past, LANES), lambda b: (b, layer, 0, 0)),
            full(lw["w_uk"]), full(lw["w_uv"]), _const_spec(lw["e_plain"]),
        ],
        out_specs=[pl.BlockSpec((None, past, MLA_PAD_WIDTH), lambda b: (b, 0, 0)),
                   pl.BlockSpec((None, past, NA_WIDTH), lambda b: (b, 0, 0))],
        out_shape=[jax.ShapeDtypeStruct((bsz, past, MLA_PAD_WIDTH), BF16),
                   jax.ShapeDtypeStruct((bsz, past, NA_WIDTH), BF16)],
        compiler_params=_params(("arbitrary",)),
        name="ctxkv",
    )(cache_ckv, cache_kr_pad, lw["w_uk"], lw["w_uv"], lw["e_plain"])


def _local_bias(pb_ref, rm_ref, hh, q0, k0, q_rows, k_rows, grid_rows):
    rows = []
    for qr in range(q_rows):
        blocks = []
        for p in range(k_rows // 2):
            pair = k0 // 2 + p
            e = 2 * pair - (q0 + qr) + grid_rows
            blocks.append(pb_ref[hh, e] + rm_ref[pl.ds((q0 + qr) * (grid_rows // 2) + pair, 1), :])
        rows.append(jnp.concatenate(blocks, axis=-1))
    return jnp.concatenate(rows, axis=0)


def _attn_body(*refs, n_seg, dq, local, heads):
    q_ref = refs[0]
    seg_refs = [(refs[1 + 2 * i], refs[2 + 2 * i]) for i in range(n_seg)]
    o_ref = refs[-1]
    if local is not None:
        pb_ref, rm_ref = refs[1 + 2 * n_seg], refs[2 + 2 * n_seg]
        q_rows, k_rows, grid_rows = local
        q0 = pl.program_id(1) * q_rows
        k0 = jnp.minimum(jnp.clip(q0 - NA_KR // 2, 0, grid_rows - NA_KR), grid_rows - k_rows)
        keys = pl.ds(pl.multiple_of(k0 * GRID_W, GRID_W), k_rows * GRID_W)
    head_major = [tuple(jnp.swapaxes(r[...], 0, 1).astype(BF16) for r in seg) if len(seg[0].shape) == 3 else None
                  for seg in seg_refs]
    outs = []
    for hh in range(heads):
        qh = q_ref[:, hh * dq:(hh + 1) * dq]
        scores = []
        values = []
        for i, (k_ref, v_ref) in enumerate(seg_refs):
            windowed = local is not None and i == n_seg - 1
            rows = keys if windowed else slice(None)
            if head_major[i] is not None:
                kh, vh = head_major[i][0][hh], head_major[i][1][hh]
            else:
                kh = k_ref[rows, hh * dq:(hh + 1) * dq].astype(BF16)
                vh = v_ref[rows, hh * MLA_V_DIM:(hh + 1) * MLA_V_DIM].astype(BF16)
            values.append(vh)
            s = lax.dot_general(qh, kh, (((1,), (1,)), ((), ())), preferred_element_type=F32)
            if windowed:
                s = s + _local_bias(pb_ref, rm_ref, hh, q0, k0, q_rows, k_rows, grid_rows)
            scores.append(s)
        m = functools.reduce(jnp.maximum, [jnp.max(s, axis=-1, keepdims=True) for s in scores])
        den = None
        acc = None
        for s, vh in zip(scores, values):
            e = jnp.exp2(s - m)
            d = jnp.sum(e, axis=-1, keepdims=True)
            a = _dot(e.astype(BF16), vh)
            den = d if den is None else den + d
            acc = a if acc is None else acc + a
        outs.append(acc / den)
    o_ref[...] = jnp.concatenate(outs, axis=-1).astype(o_ref.dtype)


def _attn_call(q, segs, local_bias, dq, tq, heads, name, layer=0):
    bsz, seq, _ = q.shape
    args = [q]
    specs = [pl.BlockSpec((None, tq, heads * dq), lambda hp, qt, b: (b, qt, hp))]
    for k, v, ks, vs in segs:
        args += [k, v]
        specs += [ks, vs]
    local = None
    if local_bias is not None:
        pair_blocks, row_masks, k_rows = local_bias
        args += [pair_blocks, row_masks]
        specs += [pl.BlockSpec((None, heads) + pair_blocks.shape[2:], lambda hp, qt, b: (layer, hp, 0, 0, 0),
                               pipeline_mode=pl.Buffered(1)),
                  _const_spec(row_masks)]
        local = (tq // GRID_W, k_rows, seq // GRID_W)
    return pl.pallas_call(
        functools.partial(_attn_body, n_seg=len(segs), dq=dq, local=local, heads=heads),
        grid=(NA_HEADS // heads, seq // tq, bsz),
        in_specs=specs,
        out_specs=pl.BlockSpec((None, tq, heads * MLA_V_DIM), lambda hp, qt, b: (b, qt, hp)),
        out_shape=jax.ShapeDtypeStruct((bsz, seq, NA_WIDTH), BF16),
        compiler_params=_params(("arbitrary", "arbitrary", "arbitrary")),
        name=name,
    )(*args)


def _seg3(arr, width):
    n = arr.shape[1]
    return pl.BlockSpec((None, n, width), lambda hp, qt, b: (b, 0, hp))


def _seg_cache(arr, layer):
    return pl.BlockSpec((None, None) + arr.shape[2:], lambda hp, qt, b: (b, layer, 0, 0, 0))


def _route(logits):
    lane = lax.broadcasted_iota(jnp.int32, logits.shape, 1)
    lane_f = lane.astype(F32)
    far = float(LANES)
    is_g = lane < N_GROUPS
    gl = jnp.where(is_g, logits, NEG_INF)
    gmax = jnp.max(gl, axis=-1, keepdims=True)
    gsum = jnp.sum(jnp.exp(gl - gmax), axis=-1, keepdims=True)
    g_w = 1.0 / gsum
    g_idx = jnp.min(jnp.where(gl == gmax, lane_f, far), axis=-1, keepdims=True)
    e_group = ((lane - R_EXP0) >> 3).astype(F32)
    is_e = (lane >= R_EXP0) & (lane < R_EXP0 + N_EXPERTS) & (e_group == g_idx)
    el = jnp.where(is_e, logits, NEG_INF)
    m1 = jnp.max(el, axis=-1, keepdims=True)
    i1 = jnp.min(jnp.where(el == m1, lane_f, far), axis=-1, keepdims=True)
    el2 = jnp.where(lane_f == i1, NEG_INF, el)
    m2 = jnp.max(el2, axis=-1, keepdims=True)
    i2 = jnp.min(jnp.where(el2 == m2, lane_f, far), axis=-1, keepdims=True)
    r = jnp.exp(m2 - m1)
    w1 = g_w / (1.0 + r)
    w2 = g_w * r / (1.0 + r)
    return lane_f, i1, i2, w1, w2


def _merge_body(ona_ref, omla_ref, g_ref, x_ref, ada_ref, wna_ref, wmla_ref, wout_ref, lng_ref, lnb_ref,
                wr_ref, wrl_ref, br_ref, ox_ref, oh_ref, or_ref, ocnt_ref, carry_ref):
    first = (pl.program_id(0) == 0) & (pl.program_id(1) == 0)

    @pl.when(first)
    def _():
        carry_ref[...] = jnp.zeros_like(carry_ref)

    y_na = _dot(ona_ref[...], wna_ref[...])
    y_mla = _dot(omla_ref[...], wmla_ref[...])
    g_na = g_ref[:, :D_MODEL].astype(F32)
    g_mla = g_ref[:, D_MODEL:].astype(F32)
    y = _dot((g_na * y_na + g_mla * y_mla).astype(BF16), wout_ref[...])
    g1 = ada_ref[2:3, :]
    x1 = _ln_plain(DEEPNORM_ALPHA * x_ref[...] + g1 * y) * lng_ref[...] + lnb_ref[...]
    ox_ref[...] = x1
    h2 = _ln_plain(x1) * (1.0 + ada_ref[4:5, :]) + ada_ref[3:4, :]
    _store_token_tiles(oh_ref, h2)
    h_hi = h2.astype(BF16)
    h_lo = (h2 - h_hi.astype(F32)).astype(BF16)
    logits = (_dot(h_hi, wr_ref[...]) + (_dot(h_lo, wr_ref[...]) + _dot(h_hi, wrl_ref[...]))) + br_ref[...]
    lane_f, i1, i2, w1, w2 = _route(logits)

    tm = logits.shape[0]
    sel1 = lane_f == i1
    sel2 = lane_f == i2
    used = (jnp.where(sel1, 1.0, 0.0) + jnp.where(sel2, 1.0, 0.0)).astype(BF16)
    row = lax.broadcasted_iota(jnp.int32, (tm, tm), 0)
    col = lax.broadcasted_iota(jnp.int32, (tm, tm), 1)
    earlier = jnp.where(row > col, 1.0, 0.0).astype(BF16)
    before = _dot(earlier, used) + carry_ref[0:1, :]
    rank1 = jnp.sum(jnp.where(sel1, before, 0.0), axis=-1, keepdims=True)
    rank2 = jnp.sum(jnp.where(sel2, before, 0.0), axis=-1, keepdims=True)
    carry_ref[...] += _dot(jnp.ones((8, tm), BF16), used)
    ocnt_ref[...] = carry_ref[...]

    fields = (i1 - R_EXP0, i2 - R_EXP0, w1, w2, rank1, rank2)
    route = jnp.zeros_like(logits)
    for k, val in enumerate(fields):
        route = jnp.where(lane_f == float(k), val, route)
    or_ref[...] = route


def _merge_call(o_na, o_mla, gates, x, ada, lw, layer, tm, name):
    bsz, seq, _ = x.shape

    def tok(width):
        return pl.BlockSpec((None, tm, width), lambda b, j: (b, j, 0))

    def full(arr):
        return _layer_spec(arr, layer)

    ws = [lw["w_o_na"], lw["w_o_mla"], lw["w_out"], lw["ln1_g"], lw["ln1_b"], lw["w_r"], lw["w_r_lo"], lw["b_r"]]
    return pl.pallas_call(
        _merge_body,
        grid=(bsz, seq // tm),
        in_specs=[tok(NA_WIDTH), tok(NA_WIDTH), tok(2 * D_MODEL), tok(D_MODEL),
                  _ada_spec(ada, layer, lambda b, j: b)] + [full(w) for w in ws],
        out_specs=[tok(D_MODEL), pl.BlockSpec((None, tm) + TOKEN_TILE, lambda b, j: (b, j, 0, 0)), tok(LANES),
                   pl.BlockSpec((8, LANES), lambda b, j: (0, 0))],
        out_shape=[jax.ShapeDtypeStruct((bsz, seq, D_MODEL), F32),
                   jax.ShapeDtypeStruct((bsz, seq) + TOKEN_TILE, F32),
                   jax.ShapeDtypeStruct((bsz, seq, LANES), F32),
                   jax.ShapeDtypeStruct((8, LANES), F32)],
        scratch_shapes=[pltpu.VMEM((8, LANES), F32)],
        compiler_params=_params(("arbitrary", "arbitrary")),
        name=name,
    )(o_na, o_mla, gates, x, ada[0], *ws)


POS_BITS = 16
POS_MASK = (1 << POS_BITS) - 1


def _plan(route, counts, n_tok, tr):
    r = route.reshape(n_tok, LANES)
    e1, e2 = r[:, 0].astype(jnp.int32), r[:, 1].astype(jnp.int32)
    rk1, rk2 = r[:, 4].astype(jnp.int32), r[:, 5].astype(jnp.int32)
    cnt = counts[0, R_EXP0:R_EXP0 + N_EXPERTS].astype(jnp.int32)
    tiles_e = (cnt + tr - 1) // tr
    tile_end = jnp.cumsum(tiles_e)
    row_off = (tile_end - tiles_e) * tr
    eid = jnp.arange(N_EXPERTS, dtype=jnp.int32)
    p1 = jnp.sum(jnp.where(e1[:, None] == eid, row_off, 0), axis=-1) + rk1
    p2 = jnp.sum(jnp.where(e2[:, None] == eid, row_off, 0), axis=-1) + rk2
    n_tiles = 2 * n_tok // tr + N_EXPERTS
    tile_off = jnp.concatenate([jnp.zeros((1,), jnp.int32), tile_end.astype(jnp.int32)])
    return p1 | (p2 << POS_BITS), tile_off, cnt, n_tiles


ROW_GROUP = 8


def _expert_body(pos_ref, toff_ref, cnt_ref, h_hbm, wg_ref, wu_ref, wd_ref, y_hbm,
                 src_ref, h_ref, hsem, hbuf, obuf, sem, wg_b, wu_b, wd_b, *, n_tok, tr, n_tiles):
    e = pl.program_id(0)
    t0 = toff_ref[e]
    n_mine = toff_ref[e + 1] - t0
    n_used = toff_ref[N_EXPERTS]

    def out_copy(slot, tile):
        return pltpu.make_async_copy(obuf.at[slot], y_hbm.at[pl.ds(tile * tr, tr)], sem.at[slot])

    @pl.when(e == 0)
    def _():
        hbuf[...] = jnp.zeros_like(hbuf)
        fetch = pltpu.make_async_copy(h_hbm, h_ref, hsem.at[0])
        fetch.start()

        def place(t, carry):
            packed = pos_ref[t]
            src_ref[packed & POS_MASK] = t
            src_ref[packed >> POS_BITS] = t
            return carry
        lax.fori_loop(0, n_tok, place, 0, unroll=8)
        fetch.wait()

    @pl.when(n_mine > 0)
    def _():
        wg_b[...] = wg_ref[...].astype(BF16)
        wu_b[...] = wu_ref[...].astype(BF16)
        wd_b[...] = wd_ref[...].astype(BF16)

    def tile_body(j, carry):
        tile = t0 + j
        slot = tile & 1
        base = tile * tr
        last = jnp.minimum(cnt_ref[e] - j * tr, tr) - 1

        def gather(g, inner):
            for k in range(ROW_GROUP):
                r = g * ROW_GROUP + k
                hbuf[r] = h_ref[src_ref[base + jnp.minimum(r, last)]]
            return inner
        lax.fori_loop(0, (last + ROW_GROUP) // ROW_GROUP, gather, 0)

        @pl.when(tile >= 2)
        def _():
            out_copy(slot, tile - 2).wait()

        h = _load_token_tiles(hbuf).astype(BF16)
        gate = _dot(h, wg_b[...])
        up = _dot(h, wu_b[...])
        hid = (gate * _sigmoid(gate) * up).astype(BF16)
        _store_token_tiles(obuf.at[slot], _dot(hid, wd_b[...]))
        out_copy(slot, tile).start()
        return carry
    lax.fori_loop(0, n_mine, tile_body, 0)

    @pl.when(e == pl.num_programs(0) - 1)
    def _():
        @pl.when(n_used >= 2)
        def _():
            out_copy(n_used & 1, n_used - 2).wait()
        out_copy((n_used - 1) & 1, n_used - 1).wait()
        obuf[0] = jnp.zeros(obuf.shape[1:], obuf.dtype)

        def zero_start(tile, carry):
            out_copy(0, tile).start()
            return carry

        def zero_wait(tile, carry):
            out_copy(0, tile).wait()
            return carry
        lax.fori_loop(n_used, n_tiles, zero_start, 0)
        lax.fori_loop(n_used, n_tiles, zero_wait, 0)


def _expert_call(h2, plan, w_gate, w_up, w_down, layer, tr, name):
    packed, tile_off, cnt, n_tiles = plan
    n_tok = h2.shape[0]
    n_rows = n_tiles * tr

    def weight_spec(rows, cols):
        return pl.BlockSpec((None, None, rows, cols), lambda e, pos, toff, cnt: (layer, e, 0, 0))

    return pl.pallas_call(
        functools.partial(_expert_body, n_tok=n_tok, tr=tr, n_tiles=n_tiles),
        grid_spec=pltpu.PrefetchScalarGridSpec(
            num_scalar_prefetch=3,
            grid=(N_EXPERTS,),
            in_specs=[pl.BlockSpec(memory_space=pl.ANY), weight_spec(D_MODEL, EXPERT_FF),
                      weight_spec(D_MODEL, EXPERT_FF),
                      weight_spec(EXPERT_FF, D_MODEL)],
            out_specs=pl.BlockSpec(memory_space=pl.ANY),
            scratch_shapes=[
                pltpu.SMEM((n_rows,), jnp.int32),
                pltpu.VMEM(h2.shape, F32),
                pltpu.SemaphoreType.DMA((1,)),
                pltpu.VMEM((tr,) + TOKEN_TILE, F32),
                pltpu.VMEM((2, tr) + TOKEN_TILE, F32),
                pltpu.SemaphoreType.DMA((2,)),
                pltpu.VMEM((D_MODEL, EXPERT_FF), BF16),
                pltpu.VMEM((D_MODEL, EXPERT_FF), BF16),
                pltpu.VMEM((EXPERT_FF, D_MODEL), BF16),
            ],
        ),
        out_shape=jax.ShapeDtypeStruct((n_rows,) + TOKEN_TILE, F32),
        compiler_params=_params(("arbitrary",)),
        name=name,
    )(packed, tile_off, cnt, h2, w_gate, w_up, w_down)


def _combine_body(pos_ref, y_hbm, r_ref, x_ref, ada_ref, lng_ref, lnb_ref, o_ref, ybuf, sem, *, tm):
    i = pl.program_id(0)
    n = pl.num_programs(0)

    def for_tile_rows(tile, slot, fn):
        base = tile * tm
        dst1 = ybuf.at[slot, 0]
        dst2 = ybuf.at[slot, 1]
        dsem = sem.at[slot]

        def body(g, carry):
            for k in range(ROW_GROUP):
                r = g * ROW_GROUP + k
                packed = pos_ref[base + r]
                fn(pltpu.make_async_copy(y_hbm.at[packed & POS_MASK], dst1.at[r], dsem))
                fn(pltpu.make_async_copy(y_hbm.at[packed >> POS_BITS], dst2.at[r], dsem))
            return carry
        lax.fori_loop(0, tm // ROW_GROUP, body, 0)

    @pl.when(i == 0)
    def _():
        for_tile_rows(0, 0, lambda cp: cp.start())

    @pl.when(i + 1 < n)
    def _():
        for_tile_rows(i + 1, (i + 1) & 1, lambda cp: cp.start())

    slot = i & 1
    for_tile_rows(i, slot, lambda cp: cp.wait())

    route = r_ref[...]
    lane = lax.broadcasted_iota(jnp.int32, route.shape, 1)
    w1 = jnp.sum(jnp.where(lane == 2, route, 0.0), axis=-1, keepdims=True)
    w2 = jnp.sum(jnp.where(lane == 3, route, 0.0), axis=-1, keepdims=True)
    y = w1 * _load_token_tiles(ybuf.at[slot, 0]) + w2 * _load_token_tiles(ybuf.at[slot, 1])
    g2 = ada_ref[5:6, :]
    o_ref[...] = _ln_plain(DEEPNORM_ALPHA * x_ref[...] + g2 * y) * lng_ref[...] + lnb_ref[...]


def _combine_call(ys, packed, route, x1, ada, lw, layer, seq, tm, name):
    n_tok = x1.shape[0]
    per_batch = seq // tm

    def tok(width):
        return pl.BlockSpec((tm, width), lambda i, pos: (i, 0))

    def full(arr):
        return _layer_spec(arr, layer)

    return pl.pallas_call(
        functools.partial(_combine_body, tm=tm),
        grid_spec=pltpu.PrefetchScalarGridSpec(
            num_scalar_prefetch=1,
            grid=(n_tok // tm,),
            in_specs=[pl.BlockSpec(memory_space=pl.ANY), tok(LANES), tok(D_MODEL),
                      _ada_spec(ada, layer, lambda i, pos: i // per_batch),
                      full(lw["ln2_g"]), full(lw["ln2_b"])],
            out_specs=tok(D_MODEL),
            scratch_shapes=[pltpu.VMEM((2, 2, tm) + TOKEN_TILE, F32), pltpu.SemaphoreType.DMA((2,))],
        ),
        out_shape=jax.ShapeDtypeStruct((n_tok, D_MODEL), F32),
        compiler_params=_params(("arbitrary",)),
        name=name,
    )(packed, ys, route, x1, ada[0], lw["ln2_g"], lw["ln2_b"])


def _moe(h2, route, counts, x1, ada, lw, experts, layer, tr, tm, name):
    bsz, seq, _ = x1.shape
    n_tok = bsz * seq
    plan = _plan(route, counts, n_tok, tr)
    ys = _expert_call(h2.reshape((n_tok,) + TOKEN_TILE), plan, *experts, layer, tr, "expert_" + name)
    out = _combine_call(ys, plan[0], route.reshape(n_tok, LANES), x1.reshape(n_tok, D_MODEL), ada, lw, layer,
                        seq, tm, "combine_" + name)
    return out.reshape(bsz, seq, D_MODEL)


def _placement(rows_used):
    e = np.zeros((LANES, MLA_PAD_WIDTH), np.float32)
    for r in range(rows_used):
        for hd in range(MLA_HEADS):
            e[r, hd * MLA_HEAD_PAD + MLA_NOPE_DIM + (r % MLA_ROPE_DIM)] = 1.0
    return jnp.asarray(e, BF16)


def _prepare_weights(p):
    partner, _ = _rope_partner()
    n_layers = p["w_in"].shape[0]
    w = p["w_in"]
    k_r = w[:, :, 2176:2208]
    pieces = [w[:, :, 0:512] * NA_SCALE, w[:, :, 512:2176], k_r, k_r[:, :, partner],
              jnp.zeros((n_layers, D_MODEL, LANES - 2 * MLA_ROPE_DIM), F32), w[:, :, 2208:]]
    w_in = jnp.concatenate([piece.astype(BF16) for piece in pieces], axis=2)
    uq = p["w_uq"].reshape(n_layers, MLA_Q_LORA, MLA_HEADS, MLA_QK_DIM)
    pad = MLA_HEAD_PAD - MLA_QK_DIM
    no_pad = ((0, 0), (0, 0), (0, 0))
    w_uq = jnp.pad(uq, no_pad + ((0, pad),)).reshape(n_layers, MLA_Q_LORA, MLA_PAD_WIDTH).astype(BF16)
    uq_sw = jnp.pad(uq[..., MLA_NOPE_DIM + partner], no_pad + ((MLA_NOPE_DIM, pad),))
    w_uq_sw = uq_sw.reshape(n_layers, MLA_Q_LORA, MLA_PAD_WIDTH).astype(BF16)
    uk = p["w_uk"].reshape(n_layers, MLA_KV_LORA, MLA_HEADS, MLA_NOPE_DIM)
    w_uk = jnp.pad(uk, no_pad + ((0, MLA_HEAD_PAD - MLA_NOPE_DIM),)).reshape(
        n_layers, MLA_KV_LORA, MLA_PAD_WIDTH).astype(BF16)
    n_pad = LANES - N_GROUPS - N_EXPERTS
    w_r = jnp.concatenate([p["router_group_w"], p["router_expert_w"],
                           jnp.zeros((n_layers, D_MODEL, n_pad), F32)], axis=2)
    b_r = jnp.concatenate([p["router_group_b"], p["router_expert_b"], jnp.zeros((n_layers, n_pad), F32)], axis=1)
    w_r_hi = w_r.astype(BF16)

    def row(name):
        return p[name][:, None, :]

    return dict(
        w_in=w_in, w_uq=w_uq, w_uq_sw=w_uq_sw, w_uk=w_uk, w_uv=p["w_uv"].astype(BF16),
        q_norm=row("mla_q_norm"), kv_norm=row("mla_kv_norm"),
        e_plain=_placement(MLA_ROPE_DIM), e_rope=_placement(2 * MLA_ROPE_DIM),
        w_o_na=p["w_o_na"].astype(BF16), w_o_mla=p["w_o_mla"].astype(BF16), w_out=p["w_out"].astype(BF16),
        ln1_g=row("ln1_g"), ln1_b=row("ln1_b"), ln2_g=row("ln2_g"), ln2_b=row("ln2_b"),
        w_r=w_r_hi, w_r_lo=(w_r - w_r_hi.astype(F32)).astype(BF16), b_r=b_r[:, None, :],
    )


def _window_rows(rows, q_rows):
    firsts = np.arange(0, rows, q_rows)
    lo = np.clip(firsts - NA_KR // 2, 0, rows - NA_KR)
    hi = np.clip(firsts + q_rows - 1 - NA_KR // 2, 0, rows - NA_KR) + NA_KR
    k_rows = min(rows, int(np.max(hi - lo)) + int(np.max(hi - lo)) % 2)
    k0 = np.minimum(lo, rows - k_rows)
    assert np.all(k0 % 2 == 0) and np.all(k0 + k_rows >= hi), (k0, k_rows)
    return k_rows, k0


def _na_bias_tables(rpb, rows):
    assert rows >= NA_KR and rows % 2 == 0
    r = np.arange(rows)
    rs = np.clip(r - NA_KR // 2, 0, rows - NA_KR)
    vrow = (r[None, :] >= rs[:, None]) & (r[None, :] < rs[:, None] + NA_KR)
    c = np.arange(GRID_W)
    cs = np.clip(c - NA_KC // 2, 0, GRID_W - NA_KC)
    vcol = (c[None, :] >= cs[:, None]) & (c[None, :] < cs[:, None] + NA_KC)
    coff = c[None, :] - c[:, None] + NA_KC - 1
    onehot = ((coff[None] == np.arange(2 * NA_KC - 1)[:, None, None]) & vcol[None]).astype(np.float32)
    zero = np.zeros_like(onehot)
    reach = NA_KR - 1
    halves = []
    for placed in (np.concatenate([onehot, zero], axis=-1), np.concatenate([zero, onehot], axis=-1)):
        vals = jnp.einsum("lhdj,jqk->lhdqk", rpb * LOG2E, jnp.asarray(placed), precision=lax.Precision.HIGHEST)
        halves.append(jnp.pad(vals, ((0, 0), (0, 0), (rows - reach, rows - reach - 1), (0, 0), (0, 0))))
    valid_d = np.abs(np.arange(-rows, rows)) <= reach
    valid = valid_d[:, None, None] & vcol[None]
    valid_pair = np.concatenate([valid[:-1], valid[1:]], axis=-1)
    neg_mask = jnp.asarray(np.where(valid_pair, 0.0, NEG_INF).astype(np.float32))
    pair_blocks = halves[0][:, :, :-1] + halves[1][:, :, 1:] + neg_mask
    in_window = np.repeat(vrow.reshape(rows, rows // 2, 2), GRID_W, axis=-1)
    row_masks = np.where(in_window, 0.0, NEG_INF).astype(np.float32).reshape(rows * rows // 2, 2 * GRID_W)
    return pair_blocks, jnp.asarray(row_masks)


def kernel(x_prompt, x_sample, cache_na_k, cache_na_v, cache_mla_ckv, cache_mla_kr, c, c_ctx, w_in, mla_q_norm, mla_kv_norm, w_uq, w_uk, w_uv, na_rpb, w_o_na, w_o_mla, w_out, w_ada, b_ada, ln1_g, ln1_b, ln2_g, ln2_b, router_group_w, router_group_b, router_expert_w, router_expert_b, expert_w_gate, expert_w_up, expert_w_down):
    p = dict(w_in=w_in, mla_q_norm=mla_q_norm, mla_kv_norm=mla_kv_norm, w_uq=w_uq, w_uk=w_uk, w_uv=w_uv,
             w_o_na=w_o_na, w_o_mla=w_o_mla, w_out=w_out, ln1_g=ln1_g, ln1_b=ln1_b, ln2_g=ln2_g, ln2_b=ln2_b,
             router_group_w=router_group_w, router_group_b=router_group_b,
             router_expert_w=router_expert_w, router_expert_b=router_expert_b,
             expert_w_gate=expert_w_gate, expert_w_up=expert_w_up, expert_w_down=expert_w_down)
    pb, ps, _ = x_prompt.shape
    sb, ss, _ = x_sample.shape

    n_cond = 16
    cond = jnp.zeros((n_cond, D_MODEL), F32).at[0].set(c_ctx).at[1:1 + sb].set(c)
    ada_all = _ada_call(cond, w_ada, b_ada[:, None, :]).reshape(DEPTH, n_cond, 6, D_MODEL)
    ada_all = jnp.pad(ada_all, ((0, 0), (0, 0), (0, 2), (0, 0)))
    ada_p = (ada_all, 0, 0)
    ada_s = (ada_all, 1, 1)

    cos32, sin32 = _rope_tables(ss)
    head_c = np.concatenate([np.ones((ss, MLA_NOPE_DIM)), cos32, np.zeros((ss, MLA_HEAD_PAD - MLA_QK_DIM))], -1)
    head_s = np.concatenate([np.zeros((ss, MLA_NOPE_DIM)), sin32, np.zeros((ss, MLA_HEAD_PAD - MLA_QK_DIM))], -1)
    rope_tabs = tuple(jnp.asarray(t, F32) for t in (
        head_c * (MLA_SCALE * LOG2E), head_s * (MLA_SCALE * LOG2E),
        np.concatenate([cos32, sin32, np.zeros((ss, LANES - 2 * MLA_ROPE_DIM))], -1)))

    ckr_pad = jnp.pad(cache_mla_kr, ((0, 0), (0, 0), (0, 0), (0, LANES - MLA_ROPE_DIM)))

    experts = (expert_w_gate, expert_w_up, expert_w_down)
    lw = _prepare_weights(p)
    tq = 512
    grid_rows = ss // GRID_W
    local_bias = _na_bias_tables(na_rpb, grid_rows) + (_window_rows(grid_rows, tq // GRID_W)[0],)
    xp, xs = x_prompt, x_sample
    new_k, new_v, new_ckv, new_kr = [], [], [], []
    for l in range(DEPTH):

        qn, kn, vn, qm, km, vm, gates, ckv, k_r = _proj_call(xp, ada_p, lw, l, None, F32, True, ps)
        o_na = _attn_call(qn, [(kn, vn, _seg3(kn, NA_WIDTH), _seg3(vn, NA_WIDTH))], None, NA_HEAD_DIM, ps,
                          NA_HEADS, "attn_ctx_na")
        o_mla = _attn_call(qm, [(km, vm, _seg3(km, MLA_PAD_WIDTH), _seg3(vm, NA_WIDTH))], None, MLA_HEAD_PAD, ps,
                           MLA_HEADS, "attn_ctx_mla")
        pair = lambda a: a.reshape((pb // 2, 2 * ps) + a.shape[2:])
        unpair = lambda a: a.reshape((pb, ps) + a.shape[2:])
        x1, h2, route, counts = _merge_call(pair(o_na), pair(o_mla), pair(gates), pair(xp), ada_p, lw, l, 2 * ps,
                                            "merge_ctx")
        xp = _moe(unpair(h2), unpair(route), counts, unpair(x1), ada_p, lw, experts, l, 128, ps, "ctx")
        new_k.append(kn)
        new_v.append(vn)
        new_ckv.append(ckv)
        new_kr.append(k_r)

        qn, kn, vn, qm, km, vm, gates = _proj_call(xs, ada_s, lw, l, rope_tabs, BF16, False, 512)
        km_ctx, vm_ctx = _ctxkv_call(cache_mla_ckv, ckr_pad, l, lw)
        o_na = _attn_call(
            qn, [(cache_na_k, cache_na_v, _seg_cache(cache_na_k, l), _seg_cache(cache_na_v, l)),
                 (kn, vn, _seg3(kn, NA_WIDTH), _seg3(vn, NA_WIDTH))], local_bias, NA_HEAD_DIM, tq, NA_HEADS,
            "attn_lat_na", l)
        o_mla = _attn_call(
            qm, [(km_ctx, vm_ctx, _seg3(km_ctx, MLA_PAD_WIDTH), _seg3(vm_ctx, NA_WIDTH)),
                 (km, vm, _seg3(km, MLA_PAD_WIDTH), _seg3(vm, NA_WIDTH))], None, MLA_HEAD_PAD, 512, MLA_HEADS,
            "attn_lat_mla")
        x1, h2, route, counts = _merge_call(o_na, o_mla, gates, xs, ada_s, lw, l, 512, "merge_lat")
        xs = _moe(h2, route, counts, x1, ada_s, lw, experts, l, 256, 256, "lat")

    def stack(parts, tail):
        return jnp.stack(parts, axis=1).reshape((pb, DEPTH, ps) + tail)

    return (xp, xs, stack(new_k, (NA_HEADS, NA_HEAD_DIM)), stack(new_v, (NA_HEADS, NA_HEAD_DIM)),
            stack(new_ckv, (MLA_KV_LORA,)), stack(new_kr, (MLA_ROPE_DIM,)))
```

```python
import functools

import numpy as np
import jax
import jax.numpy as jnp
from jax import lax
from jax.experimental import pallas as pl
from jax.experimental.pallas import tpu as pltpu

F32 = jnp.float32
BF16 = jnp.bfloat16

D_MODEL = 1024
DEPTH = 2
GRID_W = 64
NA_HEADS = 8
NA_HEAD_DIM = 64
NA_KR = 8
NA_KC = 16
NA_WIDTH = NA_HEADS * NA_HEAD_DIM
NA_SCALE = NA_HEAD_DIM ** -0.5
MLA_HEADS = 8
MLA_NOPE_DIM = 64
MLA_ROPE_DIM = 32
MLA_V_DIM = 64
MLA_Q_LORA = 384
MLA_KV_LORA = 256
MLA_QK_DIM = MLA_NOPE_DIM + MLA_ROPE_DIM
MLA_SCALE = MLA_QK_DIM ** -0.5
ROPE_BASE = 10000.0
N_GROUPS = 4
EXPERTS_PER_GROUP = 8
N_EXPERTS = N_GROUPS * EXPERTS_PER_GROUP
EXPERT_FF = 256
LN_EPS = 1e-5
RMS_EPS = 1e-6
NEG_INF = -1e30
LOG2E = 1.4426950408889634
DEEPNORM_ALPHA = (2 * DEPTH) ** 0.25

LANES = 128
MLA_HEAD_PAD = LANES
MLA_PAD_WIDTH = MLA_HEADS * MLA_HEAD_PAD
C_Q = 0
C_K = C_Q + NA_WIDTH
C_V = C_K + NA_WIDTH
C_CQ = C_V + NA_WIDTH
C_CKV = C_CQ + MLA_Q_LORA
C_KR = C_CKV + MLA_KV_LORA
C_GATE = C_KR + LANES
C_END = C_GATE + 2 * D_MODEL
R_EXP0 = N_GROUPS

VMEM_LIMIT = 56 * 1024 * 1024


def _rope_partner():
    j = np.arange(MLA_ROPE_DIM)
    n_freq = MLA_ROPE_DIM // 4
    axis, within = j // (2 * n_freq), j % (2 * n_freq)
    half, f = within // n_freq, within % n_freq
    return axis * 2 * n_freq + (1 - half) * n_freq + f, half


def _rope_tables(n_tokens):
    n_freq = MLA_ROPE_DIM // 4
    inv_freq = ROPE_BASE ** (-np.arange(n_freq, dtype=np.float64) / n_freq)
    t = np.arange(n_tokens)
    ang = np.concatenate([(t // GRID_W)[:, None] * inv_freq, (t % GRID_W)[:, None] * inv_freq], axis=-1)
    cos, sin = np.cos(ang), np.sin(ang)
    cos32 = np.concatenate([cos[:, :n_freq], cos[:, :n_freq], cos[:, n_freq:], cos[:, n_freq:]], axis=-1)
    sin32 = np.concatenate([-sin[:, :n_freq], sin[:, :n_freq], -sin[:, n_freq:], sin[:, n_freq:]], axis=-1)
    return cos32, sin32


def _ln_plain(x):
    mu = jnp.mean(x, axis=-1, keepdims=True)
    xc = x - mu
    var = jnp.mean(xc * xc, axis=-1, keepdims=True)
    return xc * lax.rsqrt(var + LN_EPS)


def _rms(x, g):
    return x * lax.rsqrt(jnp.mean(x * x, axis=-1, keepdims=True) + RMS_EPS) * g


def _sigmoid(x):
    return 0.5 * jnp.tanh(0.5 * x) + 0.5


def _dot(a, b):
    return jnp.dot(a, b, preferred_element_type=F32)


def _params(sem):
    return pltpu.CompilerParams(dimension_semantics=sem, vmem_limit_bytes=VMEM_LIMIT)


def _layer_spec(arr, layer):
    zeros = (0,) * (arr.ndim - 1)
    return pl.BlockSpec((None,) + arr.shape[1:], lambda *_: (layer,) + zeros, pipeline_mode=pl.Buffered(1))


def _ada_spec(ada, layer, batch_of):
    _, first, step = ada
    return pl.BlockSpec((None, None, 8, D_MODEL), lambda *idx: (layer, first + step * batch_of(*idx), 0, 0))


def _const_spec(arr):
    zeros = (0,) * arr.ndim
    return pl.BlockSpec(arr.shape, lambda *_: zeros, pipeline_mode=pl.Buffered(1))


TOKEN_TILE = (D_MODEL // LANES, LANES)


def _store_token_tiles(ref, val):
    chunks = jnp.stack([val[:, k * LANES:(k + 1) * LANES] for k in range(TOKEN_TILE[0])], axis=0)
    ref[...] = jnp.swapaxes(chunks, 0, 1)


def _load_token_tiles(ref):
    chunks = jnp.swapaxes(ref[...], 0, 1)
    return jnp.concatenate([chunks[k] for k in range(TOKEN_TILE[0])], axis=-1)


def _ada_body(c_ref, w_ref, b_ref, o_ref):
    c = c_ref[...]
    s = (c * _sigmoid(c)).astype(BF16)
    o_ref[0] = _dot(s, w_ref[0].astype(BF16)) + b_ref[0]


def _ada_call(cond, w_ada, b_ada):
    n_rows = cond.shape[0]
    tn = 1024
    return pl.pallas_call(
        _ada_body,
        grid=(DEPTH, 6 * D_MODEL // tn),
        in_specs=[
            pl.BlockSpec((n_rows, D_MODEL), lambda l, j: (0, 0)),
            pl.BlockSpec((1, D_MODEL, tn), lambda l, j: (l, 0, j)),
            pl.BlockSpec((1, 1, tn), lambda l, j: (l, 0, j)),
        ],
        out_specs=pl.BlockSpec((1, n_rows, tn), lambda l, j: (l, 0, j)),
        out_shape=jax.ShapeDtypeStruct((DEPTH, n_rows, 6 * D_MODEL), F32),
        compiler_params=_params(("arbitrary", "arbitrary")),
        name="ada",
    )(cond, w_ada, b_ada)


def _proj_body(*refs, rope, emit_cache):
    it = iter(refs)
    x_ref, ada_ref, w_ref, qn_ref, kvn_ref, wuq_ref = (next(it) for _ in range(6))
    wuqs_ref = next(it) if rope else None
    wuk_ref, wuv_ref, e_ref = next(it), next(it), next(it)
    if rope:
        cq_ref, sq_ref, tk_ref = next(it), next(it), next(it)
    oq_ref, ok_ref, ov_ref, oqm_ref, okm_ref, ovm_ref, og_ref = (next(it) for _ in range(7))
    if emit_cache:
        ockv_ref, okr_ref = next(it), next(it)

    x = x_ref[...]
    sh = ada_ref[0:1, :]
    sc = ada_ref[1:2, :]
    h = (_ln_plain(x) * (1.0 + sc) + sh).astype(BF16)

    oq_ref[...] = (_dot(h, w_ref[:, C_Q:C_K]) * LOG2E).astype(oq_ref.dtype)
    ok_ref[...] = _dot(h, w_ref[:, C_K:C_V]).astype(ok_ref.dtype)
    ov_ref[...] = _dot(h, w_ref[:, C_V:C_CQ]).astype(ov_ref.dtype)

    c_q = _rms(_dot(h, w_ref[:, C_CQ:C_CKV]), qn_ref[...]).astype(BF16)
    q = _dot(c_q, wuq_ref[...])
    if rope:
        q = (q * jnp.tile(cq_ref[...], (1, MLA_HEADS))
             + _dot(c_q, wuqs_ref[...]) * jnp.tile(sq_ref[...], (1, MLA_HEADS)))
    else:
        q = q * (MLA_SCALE * LOG2E)
    oqm_ref[...] = q.astype(BF16)

    c_kv = _rms(_dot(h, w_ref[:, C_CKV:C_KR]), kvn_ref[...])
    slab = _dot(h, w_ref[:, C_KR:C_GATE])
    if emit_cache:
        ockv_ref[...] = c_kv
        okr_ref[...] = slab[:, :MLA_ROPE_DIM]
    if rope:
        slab = slab * tk_ref[...]
    c_kv = c_kv.astype(BF16)
    okm_ref[...] = (_dot(c_kv, wuk_ref[...]) + _dot(slab.astype(BF16), e_ref[...])).astype(BF16)
    ovm_ref[...] = _dot(c_kv, wuv_ref[...]).astype(BF16)

    og_ref[...] = _sigmoid(_dot(h, w_ref[:, C_GATE:C_END])).astype(BF16)


def _proj_call(x, ada, lw, layer, rope_tabs, kv_dtype, emit_cache, tm):
    bsz, seq, _ = x.shape
    nj = seq // tm
    rope = rope_tabs is not None

    def tok(width):
        return pl.BlockSpec((None, tm, width), lambda b, j: (b, j, 0))

    def full(arr):
        return _layer_spec(arr, layer)

    def tab(width):
        return pl.BlockSpec((tm, width), lambda b, j: (j, 0))

    args = [x, ada[0], lw["w_in"], lw["q_norm"], lw["kv_norm"], lw["w_uq"]]
    specs = [tok(D_MODEL), _ada_spec(ada, layer, lambda b, j: b),
             full(lw["w_in"]), full(lw["q_norm"]), full(lw["kv_norm"]), full(lw["w_uq"])]
    if rope:
        args.append(lw["w_uq_sw"])
        specs.append(full(lw["w_uq_sw"]))
    e_mat = lw["e_rope"] if rope else lw["e_plain"]
    args += [lw["w_uk"], lw["w_uv"], e_mat]
    specs += [full(lw["w_uk"]), full(lw["w_uv"]), _const_spec(e_mat)]
    if rope:
        args += list(rope_tabs)
        specs += [tab(MLA_HEAD_PAD), tab(MLA_HEAD_PAD), tab(LANES)]

    def shp(width, dt):
        return jax.ShapeDtypeStruct((bsz, seq, width), dt)

    out_shape = [shp(NA_WIDTH, BF16), shp(NA_WIDTH, kv_dtype), shp(NA_WIDTH, kv_dtype),
                 shp(MLA_PAD_WIDTH, BF16), shp(MLA_PAD_WIDTH, BF16), shp(NA_WIDTH, BF16),
                 shp(2 * D_MODEL, BF16)]
    out_specs = [tok(NA_WIDTH), tok(NA_WIDTH), tok(NA_WIDTH), tok(MLA_PAD_WIDTH), tok(MLA_PAD_WIDTH),
                 tok(NA_WIDTH), tok(2 * D_MODEL)]
    if emit_cache:
        out_shape += [shp(MLA_KV_LORA, F32), shp(MLA_ROPE_DIM, F32)]
        out_specs += [tok(MLA_KV_LORA), tok(MLA_ROPE_DIM)]
    return pl.pallas_call(
        functools.partial(_proj_body, rope=rope, emit_cache=emit_cache),
        grid=(bsz, nj),
        in_specs=specs,
        out_specs=out_specs,
        out_shape=out_shape,
        compiler_params=_params(("arbitrary", "arbitrary")),
        name="proj_lat" if rope else "proj_ctx",
    )(*args)


def _ctxkv_body(ckv_ref, kr_ref, wuk_ref, wuv_ref, e_ref, okm_ref, ovm_ref):
    c_kv = ckv_ref[...].astype(BF16)
    okm_ref[...] = (_dot(c_kv, wuk_ref[...]) + _dot(kr_ref[...].astype(BF16), e_ref[...])).astype(BF16)
    ovm_ref[...] = _dot(c_kv, wuv_ref[...]).astype(BF16)


def _ctxkv_call(cache_ckv, cache_kr_pad, layer, lw):
    bsz, _, past, _ = cache_ckv.shape

    def full(arr):
        return _layer_spec(arr, layer)

    return pl.pallas_call(
        _ctxkv_body,
        grid=(bsz,),
        in_specs=[
            pl.BlockSpec((None, None, past, MLA_KV_LORA), lambda b: (b, layer, 0, 0)),
            pl.BlockSpec((None, None, past, LANES), lambda b: (b, layer, 0, 0)),
            full(lw["w_uk"]), full(lw["w_uv"]), _const_spec(lw["e_plain"]),
        ],
        out_specs=[pl.BlockSpec((None, past, MLA_PAD_WIDTH), lambda b: (b, 0, 0)),
                   pl.BlockSpec((None, past, NA_WIDTH), lambda b: (b, 0, 0))],
        out_shape=[jax.ShapeDtypeStruct((bsz, past, MLA_PAD_WIDTH), BF16),
                   jax.ShapeDtypeStruct((bsz, past, NA_WIDTH), BF16)],
        compiler_params=_params(("arbitrary",)),
        name="ctxkv",
    )(cache_ckv, cache_kr_pad, lw["w_uk"], lw["w_uv"], lw["e_plain"])


def _local_bias(pb_ref, rm_ref, hh, q0, k0, q_rows, k_rows, grid_rows):
    rows = []
    for qr in range(q_rows):
        blocks = []
        for p in range(k_rows // 2):
            pair = k0 // 2 + p
            e = 2 * pair - (q0 + qr) + grid_rows
            blocks.append(pb_ref[hh, e] + rm_ref[pl.ds((q0 + qr) * (grid_rows // 2) + pair, 1), :])
        rows.append(jnp.concatenate(blocks, axis=-1))
    return jnp.concatenate(rows, axis=0)


def _attn_body(*refs, n_seg, dq, local, heads):
    q_ref = refs[0]
    seg_refs = [(refs[1 + 2 * i], refs[2 + 2 * i]) for i in range(n_seg)]
    o_ref = refs[-1]
    if local is not None:
        pb_ref, rm_ref = refs[1 + 2 * n_seg], refs[2 + 2 * n_seg]
        q_rows, k_rows, grid_rows = local
        q0 = pl.program_id(1) * q_rows
        k0 = jnp.minimum(jnp.clip(q0 - NA_KR // 2, 0, grid_rows - NA_KR), grid_rows - k_rows)
        keys = pl.ds(pl.multiple_of(k0 * GRID_W, GRID_W), k_rows * GRID_W)
    head_major = [tuple(jnp.swapaxes(r[...], 0, 1).astype(BF16) for r in seg) if len(seg[0].shape) == 3 else None
                  for seg in seg_refs]
    outs = []
    for hh in range(heads):
        qh = q_ref[:, hh * dq:(hh + 1) * dq]
        scores = []
        values = []
        for i, (k_ref, v_ref) in enumerate(seg_refs):
            windowed = local is not None and i == n_seg - 1
            rows = keys if windowed else slice(None)
            if head_major[i] is not None:
                kh, vh = head_major[i][0][hh], head_major[i][1][hh]
            else:
                kh = k_ref[rows, hh * dq:(hh + 1) * dq].astype(BF16)
                vh = v_ref[rows, hh * MLA_V_DIM:(hh + 1) * MLA_V_DIM].astype(BF16)
            values.append(vh)
            s = lax.dot_general(qh, kh, (((1,), (1,)), ((), ())), preferred_element_type=F32)
            if windowed:
                s = s + _local_bias(pb_ref, rm_ref, hh, q0, k0, q_rows, k_rows, grid_rows)
            scores.append(s)
        m = functools.reduce(jnp.maximum, [jnp.max(s, axis=-1, keepdims=True) for s in scores])
        den = None
        acc = None
        for s, vh in zip(scores, values):
            e = jnp.exp2(s - m)
            d = jnp.sum(e, axis=-1, keepdims=True)
            a = _dot(e.astype(BF16), vh)
            den = d if den is None else den + d
            acc = a if acc is None else acc + a
        outs.append(acc / den)
    o_ref[...] = jnp.concatenate(outs, axis=-1).astype(o_ref.dtype)


def _attn_call(q, segs, local_bias, dq, tq, heads, name, layer=0):
    bsz, seq, _ = q.shape
    args = [q]
    specs = [pl.BlockSpec((None, tq, heads * dq), lambda hp, qt, b: (b, qt, hp))]
    for k, v, ks, vs in segs:
        args += [k, v]
        specs += [ks, vs]
    local = None
    if local_bias is not None:
        pair_blocks, row_masks, k_rows = local_bias
        args += [pair_blocks, row_masks]
        specs += [pl.BlockSpec((None, heads) + pair_blocks.shape[2:], lambda hp, qt, b: (layer, hp, 0, 0, 0),
                               pipeline_mode=pl.Buffered(1)),
                  _const_spec(row_masks)]
        local = (tq // GRID_W, k_rows, seq // GRID_W)
    return pl.pallas_call(
        functools.partial(_attn_body, n_seg=len(segs), dq=dq, local=local, heads=heads),
        grid=(NA_HEADS // heads, seq // tq, bsz),
        in_specs=specs,
        out_specs=pl.BlockSpec((None, tq, heads * MLA_V_DIM), lambda hp, qt, b: (b, qt, hp)),
        out_shape=jax.ShapeDtypeStruct((bsz, seq, NA_WIDTH), BF16),
        compiler_params=_params(("arbitrary", "arbitrary", "arbitrary")),
        name=name,
    )(*args)


def _seg3(arr, width):
    n = arr.shape[1]
    return pl.BlockSpec((None, n, width), lambda hp, qt, b: (b, 0, hp))


def _seg_cache(arr, layer):
    return pl.BlockSpec((None, None) + arr.shape[2:], lambda hp, qt, b: (b, layer, 0, 0, 0))


def _route(logits):
    lane = lax.broadcasted_iota(jnp.int32, logits.shape, 1)
    lane_f = lane.astype(F32)
    far = float(LANES)
    is_g = lane < N_GROUPS
    gl = jnp.where(is_g, logits, NEG_INF)
    gmax = jnp.max(gl, axis=-1, keepdims=True)
    gsum = jnp.sum(jnp.exp(gl - gmax), axis=-1, keepdims=True)
    g_w = 1.0 / gsum
    g_idx = jnp.min(jnp.where(gl == gmax, lane_f, far), axis=-1, keepdims=True)
    e_group = ((lane - R_EXP0) >> 3).astype(F32)
    is_e = (lane >= R_EXP0) & (lane < R_EXP0 + N_EXPERTS) & (e_group == g_idx)
    el = jnp.where(is_e, logits, NEG_INF)
    m1 = jnp.max(el, axis=-1, keepdims=True)
    i1 = jnp.min(jnp.where(el == m1, lane_f, far), axis=-1, keepdims=True)
    el2 = jnp.where(lane_f == i1, NEG_INF, el)
    m2 = jnp.max(el2, axis=-1, keepdims=True)
    i2 = jnp.min(jnp.where(el2 == m2, lane_f, far), axis=-1, keepdims=True)
    r = jnp.exp(m2 - m1)
    w1 = g_w / (1.0 + r)
    w2 = g_w * r / (1.0 + r)
    return lane_f, i1, i2, w1, w2


def _merge_body(ona_ref, omla_ref, g_ref, x_ref, ada_ref, wna_ref, wmla_ref, wout_ref, lng_ref, lnb_ref,
                wr_ref, wrl_ref, br_ref, ox_ref, oh_ref, or_ref, ocnt_ref, carry_ref):
    first = (pl.program_id(0) == 0) & (pl.program_id(1) == 0)

    @pl.when(first)
    def _():
        carry_ref[...] = jnp.zeros_like(carry_ref)

    y_na = _dot(ona_ref[...], wna_ref[...])
    y_mla = _dot(omla_ref[...], wmla_ref[...])
    g_na = g_ref[:, :D_MODEL].astype(F32)
    g_mla = g_ref[:, D_MODEL:].astype(F32)
    y = _dot((g_na * y_na + g_mla * y_mla).astype(BF16), wout_ref[...])
    g1 = ada_ref[2:3, :]
    x1 = _ln_plain(DEEPNORM_ALPHA * x_ref[...] + g1 * y) * lng_ref[...] + lnb_ref[...]
    ox_ref[...] = x1
    h2 = _ln_plain(x1) * (1.0 + ada_ref[4:5, :]) + ada_ref[3:4, :]
    _store_token_tiles(oh_ref, h2)
    h_hi = h2.astype(BF16)
    h_lo = (h2 - h_hi.astype(F32)).astype(BF16)
    logits = (_dot(h_hi, wr_ref[...]) + (_dot(h_lo, wr_ref[...]) + _dot(h_hi, wrl_ref[...]))) + br_ref[...]
    lane_f, i1, i2, w1, w2 = _route(logits)

    tm = logits.shape[0]
    sel1 = lane_f == i1
    sel2 = lane_f == i2
    used = (jnp.where(sel1, 1.0, 0.0) + jnp.where(sel2, 1.0, 0.0)).astype(BF16)
    row = lax.broadcasted_iota(jnp.int32, (tm, tm), 0)
    col = lax.broadcasted_iota(jnp.int32, (tm, tm), 1)
    earlier = jnp.where(row > col, 1.0, 0.0).astype(BF16)
    before = _dot(earlier, used) + carry_ref[0:1, :]
    rank1 = jnp.sum(jnp.where(sel1, before, 0.0), axis=-1, keepdims=True)
    rank2 = jnp.sum(jnp.where(sel2, before, 0.0), axis=-1, keepdims=True)
    carry_ref[...] += _dot(jnp.ones((8, tm), BF16), used)
    ocnt_ref[...] = carry_ref[...]

    fields = (i1 - R_EXP0, i2 - R_EXP0, w1, w2, rank1, rank2)
    route = jnp.zeros_like(logits)
    for k, val in enumerate(fields):
        route = jnp.where(lane_f == float(k), val, route)
    or_ref[...] = route


def _merge_call(o_na, o_mla, gates, x, ada, lw, layer, tm, name):
    bsz, seq, _ = x.shape

    def tok(width):
        return pl.BlockSpec((None, tm, width), lambda b, j: (b, j, 0))

    def full(arr):
        return _layer_spec(arr, layer)

    ws = [lw["w_o_na"], lw["w_o_mla"], lw["w_out"], lw["ln1_g"], lw["ln1_b"], lw["w_r"], lw["w_r_lo"], lw["b_r"]]
    return pl.pallas_call(
        _merge_body,
        grid=(bsz, seq // tm),
        in_specs=[tok(NA_WIDTH), tok(NA_WIDTH), tok(2 * D_MODEL), tok(D_MODEL),
                  _ada_spec(ada, layer, lambda b, j: b)] + [full(w) for w in ws],
        out_specs=[tok(D_MODEL), pl.BlockSpec((None, tm) + TOKEN_TILE, lambda b, j: (b, j, 0, 0)), tok(LANES),
                   pl.BlockSpec((8, LANES), lambda b, j: (0, 0))],
        out_shape=[jax.ShapeDtypeStruct((bsz, seq, D_MODEL), F32),
                   jax.ShapeDtypeStruct((bsz, seq) + TOKEN_TILE, F32),
                   jax.ShapeDtypeStruct((bsz, seq, LANES), F32),
                   jax.ShapeDtypeStruct((8, LANES), F32)],
        scratch_shapes=[pltpu.VMEM((8, LANES), F32)],
        compiler_params=_params(("arbitrary", "arbitrary")),
        name=name,
    )(o_na, o_mla, gates, x, ada[0], *ws)


POS_BITS = 16
POS_MASK = (1 << POS_BITS) - 1


def _plan(route, counts, n_tok, tr):
    r = route.reshape(n_tok, LANES)
    e1, e2 = r[:, 0].astype(jnp.int32), r[:, 1].astype(jnp.int32)
    rk1, rk2 = r[:, 4].astype(jnp.int32), r[:, 5].astype(jnp.int32)
    cnt = counts[0, R_EXP0:R_EXP0 + N_EXPERTS].astype(jnp.int32)
    tiles_e = (cnt + tr - 1) // tr
    tile_end = jnp.cumsum(tiles_e)
    row_off = (tile_end - tiles_e) * tr
    eid = jnp.arange(N_EXPERTS, dtype=jnp.int32)
    p1 = jnp.sum(jnp.where(e1[:, None] == eid, row_off, 0), axis=-1) + rk1
    p2 = jnp.sum(jnp.where(e2[:, None] == eid, row_off, 0), axis=-1) + rk2
    n_tiles = 2 * n_tok // tr + N_EXPERTS
    tile_off = jnp.concatenate([jnp.zeros((1,), jnp.int32), tile_end.astype(jnp.int32)])
    return p1 | (p2 << POS_BITS), tile_off, cnt, n_tiles


ROW_GROUP = 8


def _expert_body(pos_ref, toff_ref, cnt_ref, h_hbm, wg_ref, wu_ref, wd_ref, y_hbm,
                 src_ref, h_ref, hsem, hbuf, obuf, sem, wg_b, wu_b, wd_b, *, n_tok, tr, n_tiles):
    e = pl.program_id(0)
    t0 = toff_ref[e]
    n_mine = toff_ref[e + 1] - t0
    n_used = toff_ref[N_EXPERTS]

    def out_copy(slot, tile):
        return pltpu.make_async_copy(obuf.at[slot], y_hbm.at[pl.ds(tile * tr, tr)], sem.at[slot])

    @pl.when(e == 0)
    def _():
        hbuf[...] = jnp.zeros_like(hbuf)
        fetch = pltpu.make_async_copy(h_hbm, h_ref, hsem.at[0])
        fetch.start()

        def place(t, carry):
            packed = pos_ref[t]
            src_ref[packed & POS_MASK] = t
            src_ref[packed >> POS_BITS] = t
            return carry
        lax.fori_loop(0, n_tok, place, 0, unroll=8)
        fetch.wait()

    @pl.when(n_mine > 0)
    def _():
        wg_b[...] = wg_ref[...].astype(BF16)
        wu_b[...] = wu_ref[...].astype(BF16)
        wd_b[...] = wd_ref[...].astype(BF16)

    def tile_body(j, carry):
        tile = t0 + j
        slot = tile & 1
        base = tile * tr
        last = jnp.minimum(cnt_ref[e] - j * tr, tr) - 1

        def gather(g, inner):
            for k in range(ROW_GROUP):
                r = g * ROW_GROUP + k
                hbuf[r] = h_ref[src_ref[base + jnp.minimum(r, last)]]
            return inner
        lax.fori_loop(0, (last + ROW_GROUP) // ROW_GROUP, gather, 0)

        @pl.when(tile >= 2)
        def _():
            out_copy(slot, tile - 2).wait()

        h = _load_token_tiles(hbuf).astype(BF16)
        gate = _dot(h, wg_b[...])
        up = _dot(h, wu_b[...])
        hid = (gate * _sigmoid(gate) * up).astype(BF16)
        _store_token_tiles(obuf.at[slot], _dot(hid, wd_b[...]))
        out_copy(slot, tile).start()
        return carry
    lax.fori_loop(0, n_mine, tile_body, 0)

    @pl.when(e == pl.num_programs(0) - 1)
    def _():
        @pl.when(n_used >= 2)
        def _():
            out_copy(n_used & 1, n_used - 2).wait()
        out_copy((n_used - 1) & 1, n_used - 1).wait()
        obuf[0] = jnp.zeros(obuf.shape[1:], obuf.dtype)

        def zero_start(tile, carry):
            out_copy(0, tile).start()
            return carry

        def zero_wait(tile, carry):
            out_copy(0, tile).wait()
            return carry
        lax.fori_loop(n_used, n_tiles, zero_start, 0)
        lax.fori_loop(n_used, n_tiles, zero_wait, 0)


def _expert_call(h2, plan, w_gate, w_up, w_down, layer, tr, name):
    packed, tile_off, cnt, n_tiles = plan
    n_tok = h2.shape[0]
    n_rows = n_tiles * tr

    def weight_spec(rows, cols):
        return pl.BlockSpec((None, None, rows, cols), lambda e, pos, toff, cnt: (layer, e, 0, 0))

    return pl.pallas_call(
        functools.partial(_expert_body, n_tok=n_tok, tr=tr, n_tiles=n_tiles),
        grid_spec=pltpu.PrefetchScalarGridSpec(
            num_scalar_prefetch=3,
            grid=(N_EXPERTS,),
            in_specs=[pl.BlockSpec(memory_space=pl.ANY), weight_spec(D_MODEL, EXPERT_FF),
                      weight_spec(D_MODEL, EXPERT_FF),
                      weight_spec(EXPERT_FF, D_MODEL)],
            out_specs=pl.BlockSpec(memory_space=pl.ANY),
            scratch_shapes=[
                pltpu.SMEM((n_rows,), jnp.int32),
                pltpu.VMEM(h2.shape, F32),
                pltpu.SemaphoreType.DMA((1,)),
                pltpu.VMEM((tr,) + TOKEN_TILE, F32),
                pltpu.VMEM((2, tr) + TOKEN_TILE, F32),
                pltpu.SemaphoreType.DMA((2,)),
                pltpu.VMEM((D_MODEL, EXPERT_FF), BF16),
                pltpu.VMEM((D_MODEL, EXPERT_FF), BF16),
                pltpu.VMEM((EXPERT_FF, D_MODEL), BF16),
            ],
        ),
        out_shape=jax.ShapeDtypeStruct((n_rows,) + TOKEN_TILE, F32),
        compiler_params=_params(("arbitrary",)),
        name=name,
    )(packed, tile_off, cnt, h2, w_gate, w_up, w_down)


def _combine_body(pos_ref, y_hbm, r_ref, x_ref, ada_ref, lng_ref, lnb_ref, o_ref, ybuf, sem, *, tm):
    i = pl.program_id(0)
    n = pl.num_programs(0)

    def for_tile_rows(tile, slot, fn):
        base = tile * tm
        dst1 = ybuf.at[slot, 0]
        dst2 = ybuf.at[slot, 1]
        dsem = sem.at[slot]

        def body(g, carry):
            for k in range(ROW_GROUP):
                r = g * ROW_GROUP + k
                packed = pos_ref[base + r]
                fn(pltpu.make_async_copy(y_hbm.at[packed & POS_MASK], dst1.at[r], dsem))
                fn(pltpu.make_async_copy(y_hbm.at[packed >> POS_BITS], dst2.at[r], dsem))
            return carry
        lax.fori_loop(0, tm // ROW_GROUP, body, 0)

    @pl.when(i == 0)
    def _():
        for_tile_rows(0, 0, lambda cp: cp.start())

    @pl.when(i + 1 < n)
    def _():
        for_tile_rows(i + 1, (i + 1) & 1, lambda cp: cp.start())

    slot = i & 1
    for_tile_rows(i, slot, lambda cp: cp.wait())

    route = r_ref[...]
    lane = lax.broadcasted_iota(jnp.int32, route.shape, 1)
    w1 = jnp.sum(jnp.where(lane == 2, route, 0.0), axis=-1, keepdims=True)
    w2 = jnp.sum(jnp.where(lane == 3, route, 0.0), axis=-1, keepdims=True)
    y = w1 * _load_token_tiles(ybuf.at[slot, 0]) + w2 * _load_token_tiles(ybuf.at[slot, 1])
    g2 = ada_ref[5:6, :]
    o_ref[...] = _ln_plain(DEEPNORM_ALPHA * x_ref[...] + g2 * y) * lng_ref[...] + lnb_ref[...]


def _combine_call(ys, packed, route, x1, ada, lw, layer, seq, tm, name):
    n_tok = x1.shape[0]
    per_batch = seq // tm

    def tok(width):
        return pl.BlockSpec((tm, width), lambda i, pos: (i, 0))

    def full(arr):
        return _layer_spec(arr, layer)

    return pl.pallas_call(
        functools.partial(_combine_body, tm=tm),
        grid_spec=pltpu.PrefetchScalarGridSpec(
            num_scalar_prefetch=1,
            grid=(n_tok // tm,),
            in_specs=[pl.BlockSpec(memory_space=pl.ANY), tok(LANES), tok(D_MODEL),
                      _ada_spec(ada, layer, lambda i, pos: i // per_batch),
                      full(lw["ln2_g"]), full(lw["ln2_b"])],
            out_specs=tok(D_MODEL),
            scratch_shapes=[pltpu.VMEM((2, 2, tm) + TOKEN_TILE, F32), pltpu.SemaphoreType.DMA((2,))],
        ),
        out_shape=jax.ShapeDtypeStruct((n_tok, D_MODEL), F32),
        compiler_params=_params(("arbitrary",)),
        name=name,
    )(packed, ys, route, x1, ada[0], lw["ln2_g"], lw["ln2_b"])


def _moe(h2, route, counts, x1, ada, lw, experts, layer, tr, tm, name):
    bsz, seq, _ = x1.shape
    n_tok = bsz * seq
    plan = _plan(route, counts, n_tok, tr)
    ys = _expert_call(h2.reshape((n_tok,) + TOKEN_TILE), plan, *experts, layer, tr, "expert_" + name)
    out = _combine_call(ys, plan[0], route.reshape(n_tok, LANES), x1.reshape(n_tok, D_MODEL), ada, lw, layer,
                        seq, tm, "combine_" + name)
    return out.reshape(bsz, seq, D_MODEL)


def _placement(rows_used):
    e = np.zeros((LANES, MLA_PAD_WIDTH), np.float32)
    for r in range(rows_used):
        for hd in range(MLA_HEADS):
            e[r, hd * MLA_HEAD_PAD + MLA_NOPE_DIM + (r % MLA_ROPE_DIM)] = 1.0
    return jnp.asarray(e, BF16)


def _prepare_weights(p):
    partner, _ = _rope_partner()
    n_layers = p["w_in"].shape[0]
    w = p["w_in"]
    k_r = w[:, :, 2176:2208]
    pieces = [w[:, :, 0:512] * NA_SCALE, w[:, :, 512:2176], k_r, k_r[:, :, partner],
              jnp.zeros((n_layers, D_MODEL, LANES - 2 * MLA_ROPE_DIM), F32), w[:, :, 2208:]]
    w_in = jnp.concatenate([piece.astype(BF16) for piece in pieces], axis=2)
    uq = p["w_uq"].reshape(n_layers, MLA_Q_LORA, MLA_HEADS, MLA_QK_DIM)
    pad = MLA_HEAD_PAD - MLA_QK_DIM
    no_pad = ((0, 0), (0, 0), (0, 0))
    w_uq = jnp.pad(uq, no_pad + ((0, pad),)).reshape(n_layers, MLA_Q_LORA, MLA_PAD_WIDTH).astype(BF16)
    uq_sw = jnp.pad(uq[..., MLA_NOPE_DIM + partner], no_pad + ((MLA_NOPE_DIM, pad),))
    w_uq_sw = uq_sw.reshape(n_layers, MLA_Q_LORA, MLA_PAD_WIDTH).astype(BF16)
    uk = p["w_uk"].reshape(n_layers, MLA_KV_LORA, MLA_HEADS, MLA_NOPE_DIM)
    w_uk = jnp.pad(uk, no_pad + ((0, MLA_HEAD_PAD - MLA_NOPE_DIM),)).reshape(
        n_layers, MLA_KV_LORA, MLA_PAD_WIDTH).astype(BF16)
    n_pad = LANES - N_GROUPS - N_EXPERTS
    w_r = jnp.concatenate([p["router_group_w"], p["router_expert_w"],
                           jnp.zeros((n_layers, D_MODEL, n_pad), F32)], axis=2)
    b_r = jnp.concatenate([p["router_group_b"], p["router_expert_b"], jnp.zeros((n_layers, n_pad), F32)], axis=1)
    w_r_hi = w_r.astype(BF16)

    def row(name):
        return p[name][:, None, :]

    return dict(
        w_in=w_in, w_uq=w_uq, w_uq_sw=w_uq_sw, w_uk=w_uk, w_uv=p["w_uv"].astype(BF16),
        q_norm=row("mla_q_norm"), kv_norm=row("mla_kv_norm"),
        e_plain=_placement(MLA_ROPE_DIM), e_rope=_placement(2 * MLA_ROPE_DIM),
        w_o_na=p["w_o_na"].astype(BF16), w_o_mla=p["w_o_mla"].astype(BF16), w_out=p["w_out"].astype(BF16),
        ln1_g=row("ln1_g"), ln1_b=row("ln1_b"), ln2_g=row("ln2_g"), ln2_b=row("ln2_b"),
        w_r=w_r_hi, w_r_lo=(w_r - w_r_hi.astype(F32)).astype(BF16), b_r=b_r[:, None, :],
    )


def _window_rows(rows, q_rows):
    firsts = np.arange(0, rows, q_rows)
    lo = np.clip(firsts - NA_KR // 2, 0, rows - NA_KR)
    hi = np.clip(firsts + q_rows - 1 - NA_KR // 2, 0, rows - NA_KR) + NA_KR
    k_rows = min(rows, int(np.max(hi - lo)) + int(np.max(hi - lo)) % 2)
    k0 = np.minimum(lo, rows - k_rows)
    assert np.all(k0 % 2 == 0) and np.all(k0 + k_rows >= hi), (k0, k_rows)
    return k_rows, k0


def _na_bias_tables(rpb, rows):
    assert rows >= NA_KR and rows % 2 == 0
    r = np.arange(rows)
    rs = np.clip(r - NA_KR // 2, 0, rows - NA_KR)
    vrow = (r[None, :] >= rs[:, None]) & (r[None, :] < rs[:, None] + NA_KR)
    c = np.arange(GRID_W)
    cs = np.clip(c - NA_KC // 2, 0, GRID_W - NA_KC)
    vcol = (c[None, :] >= cs[:, None]) & (c[None, :] < cs[:, None] + NA_KC)
    coff = c[None, :] - c[:, None] + NA_KC - 1
    onehot = ((coff[None] == np.arange(2 * NA_KC - 1)[:, None, None]) & vcol[None]).astype(np.float32)
    zero = np.zeros_like(onehot)
    reach = NA_KR - 1
    halves = []
    for placed in (np.concatenate([onehot, zero], axis=-1), np.concatenate([zero, onehot], axis=-1)):
        vals = jnp.einsum("lhdj,jqk->lhdqk", rpb * LOG2E, jnp.asarray(placed), precision=lax.Precision.HIGHEST)
        halves.append(jnp.pad(vals, ((0, 0), (0, 0), (rows - reach, rows - reach - 1), (0, 0), (0, 0))))
    valid_d = np.abs(np.arange(-rows, rows)) <= reach
    valid = valid_d[:, None, None] & vcol[None]
    valid_pair = np.concatenate([valid[:-1], valid[1:]], axis=-1)
    neg_mask = jnp.asarray(np.where(valid_pair, 0.0, NEG_INF).astype(np.float32))
    pair_blocks = halves[0][:, :, :-1] + halves[1][:, :, 1:] + neg_mask
    in_window = np.repeat(vrow.reshape(rows, rows // 2, 2), GRID_W, axis=-1)
    row_masks = np.where(in_window, 0.0, NEG_INF).astype(np.float32).reshape(rows * rows // 2, 2 * GRID_W)
    return pair_blocks, jnp.asarray(row_masks)


def kernel(x_prompt, x_sample, cache_na_k, cache_na_v, cache_mla_ckv, cache_mla_kr, c, c_ctx, w_in, mla_q_norm, mla_kv_norm, w_uq, w_uk, w_uv, na_rpb, w_o_na, w_o_mla, w_out, w_ada, b_ada, ln1_g, ln1_b, ln2_g, ln2_b, router_group_w, router_group_b, router_expert_w, router_expert_b, expert_w_gate, expert_w_up, expert_w_down):
    p = dict(w_in=w_in, mla_q_norm=mla_q_norm, mla_kv_norm=mla_kv_norm, w_uq=w_uq, w_uk=w_uk, w_uv=w_uv,
             w_o_na=w_o_na, w_o_mla=w_o_mla, w_out=w_out, ln1_g=ln1_g, ln1_b=ln1_b, ln2_g=ln2_g, ln2_b=ln2_b,
             router_group_w=router_group_w, router_group_b=router_group_b,
             router_expert_w=router_expert_w, router_expert_b=router_expert_b,
             expert_w_gate=expert_w_gate, expert_w_up=expert_w_up, expert_w_down=expert_w_down)
    pb, ps, _ = x_prompt.shape
    sb, ss, _ = x_sample.shape

    n_cond = 16
    cond = jnp.zeros((n_cond, D_MODEL), F32).at[0].set(c_ctx).at[1:1 + sb].set(c)
    ada_all = _ada_call(cond, w_ada, b_ada[:, None, :]).reshape(DEPTH, n_cond, 6, D_MODEL)
    ada_all = jnp.pad(ada_all, ((0, 0), (0, 0), (0, 2), (0, 0)))
    ada_p = (ada_all, 0, 0)
    ada_s = (ada_all, 1, 1)

    cos32, sin32 = _rope_tables(ss)
    head_c = np.concatenate([np.ones((ss, MLA_NOPE_DIM)), cos32, np.zeros((ss, MLA_HEAD_PAD - MLA_QK_DIM))], -1)
    head_s = np.concatenate([np.zeros((ss, MLA_NOPE_DIM)), sin32, np.zeros((ss, MLA_HEAD_PAD - MLA_QK_DIM))], -1)
    rope_tabs = tuple(jnp.asarray(t, F32) for t in (
        head_c * (MLA_SCALE * LOG2E), head_s * (MLA_SCALE * LOG2E),
        np.concatenate([cos32, sin32, np.zeros((ss, LANES - 2 * MLA_ROPE_DIM))], -1)))

    ckr_pad = jnp.pad(cache_mla_kr, ((0, 0), (0, 0), (0, 0), (0, LANES - MLA_ROPE_DIM)))

    experts = (expert_w_gate, expert_w_up, expert_w_down)
    lw = _prepare_weights(p)
    tq = 512
    grid_rows = ss // GRID_W
    local_bias = _na_bias_tables(na_rpb, grid_rows) + (_window_rows(grid_rows, tq // GRID_W)[0],)
    xp, xs = x_prompt, x_sample
    new_k, new_v, new_ckv, new_kr = [], [], [], []
    for l in range(DEPTH):

        qn, kn, vn, qm, km, vm, gates, ckv, k_r = _proj_call(xp, ada_p, lw, l, None, F32, True, ps)
        o_na = _attn_call(qn, [(kn, vn, _seg3(kn, NA_WIDTH), _seg3(vn, NA_WIDTH))], None, NA_HEAD_DIM, ps,
                          NA_HEADS, "attn_ctx_na")
        o_mla = _attn_call(qm, [(km, vm, _seg3(km, MLA_PAD_WIDTH), _seg3(vm, NA_WIDTH))], None, MLA_HEAD_PAD, ps,
                           MLA_HEADS, "attn_ctx_mla")
        pair = lambda a: a.reshape((pb // 2, 2 * ps) + a.shape[2:])
        unpair = lambda a: a.reshape((pb, ps) + a.shape[2:])
        x1, h2, route, counts = _merge_call(pair(o_na), pair(o_mla), pair(gates), pair(xp), ada_p, lw, l, 2 * ps,
                                            "merge_ctx")
        xp = _moe(unpair(h2), unpair(route), counts, unpair(x1), ada_p, lw, experts, l, 128, ps, "ctx")
        new_k.append(kn)
        new_v.append(vn)
        new_ckv.append(ckv)
        new_kr.append(k_r)

        qn, kn, vn, qm, km, vm, gates = _proj_call(xs, ada_s, lw, l, rope_tabs, BF16, False, 512)
        km_ctx, vm_ctx = _ctxkv_call(cache_mla_ckv, ckr_pad, l, lw)
        o_na = _attn_call(
            qn, [(cache_na_k, cache_na_v, _seg_cache(cache_na_k, l), _seg_cache(cache_na_v, l)),
                 (kn, vn, _seg3(kn, NA_WIDTH), _seg3(vn, NA_WIDTH))], local_bias, NA_HEAD_DIM, tq, NA_HEADS,
            "attn_lat_na", l)
        o_mla = _attn_call(
            qm, [(km_ctx, vm_ctx, _seg3(km_ctx, MLA_PAD_WIDTH), _seg3(vm_ctx, NA_WIDTH)),
                 (km, vm, _seg3(km, MLA_PAD_WIDTH), _seg3(vm, NA_WIDTH))], None, MLA_HEAD_PAD, 512, MLA_HEADS,
            "attn_lat_mla")
        x1, h2, route, counts = _merge_call(o_na, o_mla, gates, xs, ada_s, lw, l, 512, "merge_lat")
        xs = _moe(h2, route, counts, x1, ada_s, lw, experts, l, 256, 256, "lat")

    def stack(parts, tail):
        return jnp.stack(parts, axis=1).reshape((pb, DEPTH, ps) + tail)

    return (xp, xs, stack(new_k, (NA_HEADS, NA_HEAD_DIM)), stack(new_v, (NA_HEADS, NA_HEAD_DIM)),
            stack(new_ckv, (MLA_KV_LORA,)), stack(new_kr, (MLA_ROPE_DIM,)))
```
